```python
import math
import jax, jax.numpy as jnp
from jax import lax
import numpy as np

D_MODEL = 1024
BATCH = 8
SEQ = 2048
DEPTH = 1
DEC_BATCH = 128
DEC_SEQ = 4
PAST_LEN = 16384
PAGE_SIZE = 128

D_CONV = D_MODEL // 2
CONV_WIDTH = 31
D_SSM = D_MODEL // 2
SSM_GROUP = 16
SSM_GROUPS = D_SSM // SSM_GROUP
SSM_STATE = 64
N_EXPERTS = 32
TOP_K = 4
D_FF = D_MODEL
SWIGLU_LIMIT = 7.0
SWIGLU_ALPHA = 1.702
DT_MIN = 1e-3
DT_MAX = 1e-1
LN_EPS = 1e-5
DN_ALPHA = (2 * DEPTH) ** 0.25
DN_BETA = (8 * DEPTH) ** -0.25
IN_COLS = 2 * D_CONV + D_SSM + 2 * D_MODEL

kernel_name = "gated_conv_s5_moe_decoder_step"


def layer_norm(x, g, b):
    xf = x.astype(jnp.float32)
    mu = jnp.mean(xf, -1, keepdims=True)
    var = jnp.mean(jnp.square(xf - mu), -1, keepdims=True)
    y = (xf - mu) * lax.rsqrt(var + LN_EPS) * g.astype(jnp.float32) + b.astype(jnp.float32)
    return y.astype(x.dtype)


def conformer_conv(z, buf, w_dw, b_dw, ln_g, ln_b, w_pw, b_pw):
    a, gt = jnp.split(z, 2, axis=-1)
    u = a * jax.nn.sigmoid(gt)
    full = jnp.concatenate([buf.astype(u.dtype), u], axis=1)
    v = lax.conv_general_dilated(
        full, w_dw[:, None, :].astype(u.dtype), window_strides=(1,), padding="VALID",
        dimension_numbers=("NWC", "WIO", "NWC"), feature_group_count=D_CONV) + b_dw
    v = jax.nn.silu(layer_norm(v, ln_g, ln_b))
    return v @ w_pw + b_pw, full[:, -(CONV_WIDTH - 1):]


def _scan_combine(e1, e2):
    a1, b1 = e1
    a2, b2 = e2
    return a2 * a1, a2 * b1 + b2


def s5_layer(u, h0_re, h0_im, lam_re, lam_im, log_dt, b_re, b_im, c_re, c_im, d_skip):
    f32 = jnp.float32
    bsz, L, _ = u.shape
    uf = u.astype(f32).reshape(bsz, L, SSM_GROUPS, SSM_GROUP)
    lam = lax.complex(lam_re.astype(f32), lam_im.astype(f32))
    dt = jnp.exp(log_dt.astype(f32))[:, None]
    lam_bar = jnp.exp(lam * dt)
    b_bar = ((lam_bar - 1.0) / lam)[..., None] * lax.complex(b_re.astype(f32), b_im.astype(f32))
    bu = jnp.einsum('blgh,gph->blgp', uf.astype(jnp.complex64), b_bar)
    h0 = lax.complex(h0_re.astype(f32), h0_im.astype(f32))
    bu = bu.at[:, 0].add(lam_bar * h0)
    a = jnp.broadcast_to(lam_bar, bu.shape)
    _, h = lax.associative_scan(_scan_combine, (a, bu), axis=1)
    c = lax.complex(c_re.astype(f32), c_im.astype(f32))
    y = jnp.real(jnp.einsum('blgp,ghp->blgh', h, c)) + d_skip.astype(f32) * uf
    y = y.reshape(bsz, L, D_SSM).astype(u.dtype)
    h_last = h[:, -1]
    return y, jnp.real(h_last).astype(h0_re.dtype), jnp.imag(h_last).astype(h0_im.dtype)


def mixer(h, conv_buf, h0_re, h0_im, p):
    z = h @ p['w_in'] + p['b_in']
    z_conv = z[..., :2 * D_CONV]
    z_ssm = z[..., 2 * D_CONV:2 * D_CONV + D_SSM]
    g_conv = jax.nn.sigmoid(z[..., 2 * D_CONV + D_SSM:2 * D_CONV + D_SSM + D_MODEL])
    g_ssm = jax.nn.sigmoid(z[..., 2 * D_CONV + D_SSM + D_MODEL:])
    conv_out, new_buf = conformer_conv(z_conv, conv_buf, p['w_dw'], p['b_dw'], p['conv_ln_g'],
                                       p['conv_ln_b'], p['w_pw'], p['b_pw'])
    y, new_re, new_im = s5_layer(z_ssm, h0_re, h0_im, p['lam_re'], p['lam_im'], p['log_dt'],
                                 p['b_re'], p['b_im'], p['c_re'], p['c_im'], p['d_skip'])
    yg = jax.nn.gelu(y)
    ssm_out = (yg @ p['w_sv'] + p['b_sv']) * jax.nn.sigmoid(yg @ p['w_sg'] + p['b_sg'])
    merged = g_conv * conv_out + g_ssm * ssm_out
    return merged @ p['w_out'] + p['b_out'], new_buf, new_re, new_im


def moe(h, w_r, b_r, w1, b1, w2, b2):
    shp = h.shape
    t = h.reshape(-1, D_MODEL)
    logits = (t @ w_r + b_r).astype(jnp.float32)
    top_val, top_idx = lax.top_k(logits, TOP_K)
    gates = jax.nn.softmax(top_val, axis=-1)
    comb = jnp.einsum('tk,tke->te', gates,
                      jax.nn.one_hot(top_idx, N_EXPERTS, dtype=jnp.float32)).astype(h.dtype)
    out = jnp.zeros_like(t)
    for e in range(N_EXPERTS):
        gu = t @ w1[e] + b1[e]
        g = jnp.minimum(gu[:, :D_FF], SWIGLU_LIMIT)
        up = jnp.clip(gu[:, D_FF:], -SWIGLU_LIMIT, SWIGLU_LIMIT)
        act = g * jax.nn.sigmoid(SWIGLU_ALPHA * g) * (up + 1.0)
        out = out + comb[:, e:e + 1] * (act @ w2[e] + b2[e])
    return out.reshape(shp)


def block(x, c, conv_buf, h0_re, h0_im, p):
    mod = (jax.nn.silu(c) @ p['w_ada'] + p['b_ada'])[:, None, :]
    sh1, sc1, g1, sh2, sc2, g2 = jnp.split(mod, 6, axis=-1)
    h = x * (1.0 + sc1) + sh1
    m, new_buf, new_re, new_im = mixer(h, conv_buf, h0_re, h0_im, p)
    x = layer_norm(DN_ALPHA * x + g1 * m, p['ln1_g'], p['ln1_b'])
    h = x * (1.0 + sc2) + sh2
    f = moe(h, p['w_router'], p['b_router'], p['w1'], p['b1'], p['w2'], p['b2'])
    x = layer_norm(DN_ALPHA * x + g2 * f, p['ln2_g'], p['ln2_b'])
    return x, new_buf, new_re, new_im


def setup_inputs(seed: int = 0) -> dict:
    key = jax.random.key(seed)
    ks = iter(jax.random.split(key, 64))
    f32 = jnp.float32

    def nrm(shape, scale):
        return scale * jax.random.normal(next(ks), shape, f32)

    L = DEPTH
    G, P, H = SSM_GROUPS, SSM_STATE, SSM_GROUP
    inputs = {
        'x_prompt': nrm((BATCH, SEQ, D_MODEL), 1.0),
        'x_sample': nrm((DEC_BATCH, DEC_SEQ, D_MODEL), 1.0),
        'state_conv': nrm((L, DEC_BATCH, CONV_WIDTH - 1, D_CONV), 1.0),
        'state_ssm_re': nrm((L, DEC_BATCH, G, P), 1.0),
        'state_ssm_im': nrm((L, DEC_BATCH, G, P), 1.0),
        'c_prompt': nrm((BATCH, D_MODEL), 1.0),
        'c_sample': nrm((DEC_BATCH, D_MODEL), 1.0),
        'w_ada': nrm((L, D_MODEL, 6 * D_MODEL), 0.5 * D_MODEL ** -0.5),
        'b_ada': nrm((L, 6 * D_MODEL), 0.02),
        'w_in': nrm((L, D_MODEL, IN_COLS), D_MODEL ** -0.5),
        'b_in': nrm((L, IN_COLS), 0.02),
        'w_dw': nrm((L, CONV_WIDTH, D_CONV), CONV_WIDTH ** -0.5),
        'b_dw': nrm((L, D_CONV), 0.02),
        'conv_ln_g': 1.0 + nrm((L, D_CONV), 0.02),
        'conv_ln_b': nrm((L, D_CONV), 0.02),
        'w_pw': nrm((L, D_CONV, D_MODEL), D_CONV ** -0.5),
        'b_pw': nrm((L, D_MODEL), 0.02),
        'lam_re': -0.5 + nrm((L, G, P), 0.01),
        'lam_im': jnp.broadcast_to(jnp.pi * jnp.arange(P, dtype=f32), (L, G, P)) + nrm((L, G, P), 0.01),
        'log_dt': jax.random.uniform(next(ks), (L, G), f32, math.log(DT_MIN), math.log(DT_MAX)),
        'b_re': nrm((L, G, P, H), (2.0 * H) ** -0.5),
        'b_im': nrm((L, G, P, H), (2.0 * H) ** -0.5),
        'c_re': nrm((L, G, H, P), (2.0 * P) ** -0.5),
        'c_im': nrm((L, G, H, P), (2.0 * P) ** -0.5),
        'd_skip': nrm((L, G, H), 1.0),
        'w_sv': nrm((L, D_SSM, D_MODEL), D_SSM ** -0.5),
        'b_sv': nrm((L, D_MODEL), 0.02),
        'w_sg': nrm((L, D_SSM, D_MODEL), D_SSM ** -0.5),
        'b_sg': nrm((L, D_MODEL), 0.02),
        'w_out': nrm((L, D_MODEL, D_MODEL), DN_BETA * D_MODEL ** -0.5),
        'b_out': nrm((L, D_MODEL), 0.02),
        'ln1_g': 1.0 + nrm((L, D_MODEL), 0.02),
        'ln1_b': nrm((L, D_MODEL), 0.02),
        'w_router': nrm((L, D_MODEL, N_EXPERTS), D_MODEL ** -0.5),
        'b_router': nrm((L, N_EXPERTS), 0.01),
        'w1': nrm((L, N_EXPERTS, D_MODEL, 2 * D_FF), D_MODEL ** -0.5),
        'b1': nrm((L, N_EXPERTS, 2 * D_FF), 0.02),
        'w2': nrm((L, N_EXPERTS, D_FF, D_MODEL), DN_BETA * D_FF ** -0.5),
        'b2': nrm((L, N_EXPERTS, D_MODEL), 0.02),
        'ln2_g': 1.0 + nrm((L, D_MODEL), 0.02),
        'ln2_b': nrm((L, D_MODEL), 0.02),
    }
    return inputs


def reference(x_prompt, x_sample, state_conv, state_ssm_re, state_ssm_im, c_prompt, c_sample,
              w_ada, b_ada, w_in, b_in, w_dw, b_dw, conv_ln_g, conv_ln_b, w_pw, b_pw,
              lam_re, lam_im, log_dt, b_re, b_im, c_re, c_im, d_skip,
              w_sv, b_sv, w_sg, b_sg, w_out, b_out, ln1_g, ln1_b,
              w_router, b_router, w1, b1, w2, b2, ln2_g, ln2_b):
    xp, xs = x_prompt, x_sample
    conv_p, re_p, im_p, conv_s, re_s, im_s = [], [], [], [], [], []
    for l in range(DEPTH):
        p = {
            'w_ada': w_ada[l], 'b_ada': b_ada[l], 'w_in': w_in[l], 'b_in': b_in[l],
            'w_dw': w_dw[l], 'b_dw': b_dw[l], 'conv_ln_g': conv_ln_g[l], 'conv_ln_b': conv_ln_b[l],
            'w_pw': w_pw[l], 'b_pw': b_pw[l], 'lam_re': lam_re[l], 'lam_im': lam_im[l],
            'log_dt': log_dt[l], 'b_re': b_re[l], 'b_im': b_im[l], 'c_re': c_re[l], 'c_im': c_im[l],
            'd_skip': d_skip[l], 'w_sv': w_sv[l], 'b_sv': b_sv[l], 'w_sg': w_sg[l], 'b_sg': b_sg[l],
            'w_out': w_out[l], 'b_out': b_out[l], 'ln1_g': ln1_g[l], 'ln1_b': ln1_b[l],
            'w_router': w_router[l], 'b_router': b_router[l], 'w1': w1[l], 'b1': b1[l],
            'w2': w2[l], 'b2': b2[l], 'ln2_g': ln2_g[l], 'ln2_b': ln2_b[l],
        }
        zbuf = jnp.zeros((xp.shape[0], CONV_WIDTH - 1, D_CONV), xp.dtype)
        zre = jnp.zeros((xp.shape[0], SSM_GROUPS, SSM_STATE), state_ssm_re.dtype)
        xp, nb, nr, ni = block(xp, c_prompt, zbuf, zre, zre, p)
        conv_p.append(nb.astype(state_conv.dtype)); re_p.append(nr); im_p.append(ni)
        xs, nb, nr, ni = block(xs, c_sample, state_conv[l], state_ssm_re[l], state_ssm_im[l], p)
        conv_s.append(nb.astype(state_conv.dtype)); re_s.append(nr); im_s.append(ni)
    return (xp, xs, jnp.stack(conv_p), jnp.stack(re_p), jnp.stack(im_p),
            jnp.stack(conv_s), jnp.stack(re_s), jnp.stack(im_s))
```

```python
import functools

import jax
import jax.numpy as jnp
from jax import lax
from jax.experimental import pallas as pl
from jax.experimental.pallas import tpu as pltpu

CONV_WIDTH = 31
SSM_GROUP = 16
SSM_STATE = 64
N_EXPERTS = 32
TOP_K = 4
SWIGLU_LIMIT = 7.0
SWIGLU_ALPHA = 1.702
LN_EPS = 1e-5

LANES = 128
SUBLANES = 8
TOKEN_BLOCK = 256
EXPERT_TILE = 256
SEG_PAD = 16
GROUPS_PER_BLOCK = LANES // SSM_GROUP
CONV_HIST = 32
CONV_ROWS = 32
ROUTE_IDX = N_EXPERTS
ROUTE_GATE = N_EXPERTS + TOP_K
VMEM_LIMIT = 56 * 1024 * 1024

_F32 = jnp.float32
_BF16 = jnp.bfloat16


def _bf(x):
    return x.astype(_BF16)


def _dot(a, b):
    return jnp.dot(a, b, preferred_element_type=_F32)


def _dot_nt(a, b):
    return lax.dot_general(a, b, (((1,), (1,)), ((), ())), preferred_element_type=_F32)


def _split(x):
    hi = _bf(x)
    lo = _bf(x - hi.astype(_F32))
    return hi, lo


def _sigmoid(x):
    return 1.0 / (1.0 + jnp.exp(-x))


def _gelu_tanh(x):
    return 0.5 * x * (1.0 + jnp.tanh(0.7978845608028654 * (x + 0.044715 * (x * x * x))))


def _layer_norm(x, g, b):
    mu = jnp.mean(x, axis=-1, keepdims=True)
    xc = x - mu
    var = jnp.mean(xc * xc, axis=-1, keepdims=True)
    return xc * lax.rsqrt(var + LN_EPS) * g + b


def _const_spec(shape):
    nd = len(shape)
    return pl.BlockSpec(shape, lambda *_: (0,) * nd)


def _params(sem):
    return pltpu.CompilerParams(dimension_semantics=sem, vmem_limit_bytes=VMEM_LIMIT)


def _ada_kernel(c_ref, w_ref, b_ref, o_ref):
    c = c_ref[...]
    s_hi, s_lo = _split(c * _sigmoid(c))
    w_hi, w_lo = _split(w_ref[...])
    o_ref[...] = _dot(s_hi, w_hi) + _dot(s_lo, w_hi) + _dot(s_hi, w_lo) + b_ref[...]


def _ada(c, w, b):
    n, d = c.shape
    cols = w.shape[1]
    tn = d
    return pl.pallas_call(
        _ada_kernel,
        grid=(cols // tn,),
        in_specs=[pl.BlockSpec((n, d), lambda j: (0, 0)),
                  pl.BlockSpec((d, tn), lambda j: (0, j)),
                  pl.BlockSpec((1, tn), lambda j: (0, j))],
        out_specs=pl.BlockSpec((n, tn), lambda j: (0, j)),
        out_shape=jax.ShapeDtypeStruct((n, cols), _F32),
        compiler_params=_params(("arbitrary",)),
        name="ada",
    )(c, w, b.reshape(1, cols))


def _in_proj(hb, w_in, b_in, lo, hi):
    return _dot(hb, w_in[:, lo:hi]) + b_in[:, lo:hi]


def _conv_tail(v, cln_g, cln_b, w_pw, b_pw):
    v = _layer_norm(v, cln_g[...], cln_b[...])
    v = v * _sigmoid(v)
    return _dot(_bf(v), w_pw[...]) + b_pw[...]


def _ssm_tail(y, w_sv, b_sv, w_sg, b_sg):
    yg = _bf(_gelu_tanh(y))
    return (_dot(yg, w_sv[...]) + b_sv[...]) * _sigmoid(_dot(yg, w_sg[...]) + b_sg[...])


def _route(h2, wr_hi, wr_lo, b_r):
    n = h2.shape[0]
    h_hi, h_lo = _split(h2)
    logits = _dot(h_hi, wr_hi[...]) + _dot(h_lo, wr_hi[...]) + _dot(h_hi, wr_lo[...]) + b_r[...]
    lane = lax.broadcasted_iota(jnp.int32, (n, LANES), 1).astype(_F32)
    neg = jnp.float32(-jnp.inf)
    cur = jnp.where(lane < N_EXPERTS, logits, neg)
    vals, idxs = [], []
    for _ in range(TOP_K):
        m = jnp.max(cur, axis=-1, keepdims=True)
        idx = jnp.min(jnp.where(cur == m, lane, float(LANES)), axis=-1, keepdims=True)
        vals.append(m)
        idxs.append(idx)
        cur = jnp.where(lane == idx, neg, cur)
    es = [jnp.exp(v - vals[0]) for v in vals]
    tot = es[0]
    for e in es[1:]:
        tot = tot + e
    inv = 1.0 / tot
    route = jnp.zeros((n, LANES), _F32)
    for k in range(TOP_K):
        gate = es[k] * inv
        route = route + jnp.where(lane == idxs[k], gate, 0.0)
        route = route + jnp.where(lane == float(ROUTE_IDX + k), idxs[k], 0.0)
        route = route + jnp.where(lane == float(ROUTE_GATE + k), gate, 0.0)
    return route


def _block_counts(route):
    lane = lax.broadcasted_iota(jnp.int32, route.shape, 1)
    member = jnp.where((route > 0.0) & (lane < N_EXPERTS), 1.0, 0.0)
    return jnp.sum(member, axis=0, keepdims=True)


def _mixer_tail(x, hb, conv_out, ssm_out, g1, sh2, sc2, alpha, dims, w_in, b_in, w_out, b_out,
                ln1_g, ln1_b, wr_hi, wr_lo, b_r):
    d, dc, dsm = dims
    o = 2 * dc + dsm
    g_conv = _sigmoid(_in_proj(hb, w_in, b_in, o, o + d))
    g_ssm = _sigmoid(_in_proj(hb, w_in, b_in, o + d, o + 2 * d))
    merged = g_conv * conv_out + g_ssm * ssm_out
    m = _dot(_bf(merged), w_out[...]) + b_out[...]
    x1 = _layer_norm(alpha * x + g1 * m, ln1_g[...], ln1_b[...])
    h2 = x1 * (1.0 + sc2) + sh2
    return x1, h2, _route(h2, wr_hi, wr_lo, b_r)


def _mixer_prompt_kernel(x_ref, mod_ref, w_in, b_in, w_dw, b_dw, cln_g, cln_b, w_pw, b_pw,
                         bre, bim, cc, einv_r, einv_i, epow_r, epow_i, lamb_r, lamb_i, dskip,
                         w_sv, b_sv, w_sg, b_sg, w_out, b_out, ln1_g, ln1_b, wr_hi, wr_lo, b_r,
                         x1_ref, h2_ref, route_ref, cnt_ref, cstate_ref, sre_ref, sim_ref,
                         ubuf, ush, vbuf, car_r, car_i, *, alpha):
    c = pl.program_id(1)
    last = pl.num_programs(1) - 1
    tl, d = x_ref.shape[1], x_ref.shape[2]
    dc = w_pw.shape[0]
    dsm = dskip.shape[1]
    nj = bre.shape[0]
    rows = bre.shape[1]

    @pl.when(c == 0)
    def _():
        ubuf[0:CONV_HIST, :] = jnp.zeros((CONV_HIST, dc), _F32)
        car_r[...] = jnp.zeros(car_r.shape, _F32)
        car_i[...] = jnp.zeros(car_i.shape, _F32)

    x = x_ref[0]
    mod = mod_ref[0]
    sh1, sc1, g1 = mod[:, 0:d], mod[:, d:2 * d], mod[:, 2 * d:3 * d]
    sh2, sc2 = mod[:, 3 * d:4 * d], mod[:, 4 * d:5 * d]
    hb = _bf(x * (1.0 + sc1) + sh1)

    u = _in_proj(hb, w_in, b_in, 0, dc) * _sigmoid(_in_proj(hb, w_in, b_in, dc, 2 * dc))
    ubuf[CONV_HIST:CONV_HIST + tl, :] = u
    span = ush.shape[1]
    for r in range(1, SUBLANES):
        ush[r - 1] = ubuf[r:r + span, :]

    def conv_rows(rb, carry):
        r0 = pl.multiple_of(rb * CONV_ROWS, CONV_ROWS)
        acc = jnp.broadcast_to(b_dw[...], (CONV_ROWS, dc))
        for k in range(CONV_WIDTH):
            q, r = divmod(CONV_HIST - (CONV_WIDTH - 1) + k, SUBLANES)
            rows_k = pl.ds(r0 + q * SUBLANES, CONV_ROWS)
            tap = ubuf[rows_k, :] if r == 0 else ush[r - 1, rows_k, :]
            acc = acc + w_dw[k:k + 1, :] * tap
        vbuf[pl.ds(r0, CONV_ROWS), :] = acc
        return carry

    lax.fori_loop(0, tl // CONV_ROWS, conv_rows, 0)

    @pl.when(c == last)
    def _():
        cstate_ref[0] = ubuf[CONV_HIST + tl - (CONV_WIDTH - 1):CONV_HIST + tl, :]

    ubuf[0:CONV_HIST, :] = ubuf[tl:tl + CONV_HIST, :]
    conv_out = _conv_tail(vbuf[...], cln_g, cln_b, w_pw, b_pw)

    zs = _in_proj(hb, w_in, b_in, 2 * dc, 2 * dc + dsm)
    zst = _bf(zs.T)
    tri = jnp.where(lax.broadcasted_iota(jnp.int32, (tl, tl), 0)
                    <= lax.broadcasted_iota(jnp.int32, (tl, tl), 1), 1.0, 0.0).astype(_BF16)
    yts = []
    for j in range(nj):
        r = slice(j * rows, (j + 1) * rows)
        zj = zst[j * LANES:(j + 1) * LANES, :]
        bu_r = _dot(bre[j], zj)
        bu_i = _dot(bim[j], zj)
        eir, eii = einv_r[r, :], einv_i[r, :]
        v_r = bu_r * eir - bu_i * eii
        v_i = bu_r * eii + bu_i * eir
        vr_hi, vr_lo = _split(v_r)
        vi_hi, vi_lo = _split(v_i)
        cr, ci = car_r[r, :], car_i[r, :]
        lr, li = lamb_r[r, :], lamb_i[r, :]
        corr_r = jnp.concatenate([lr * cr - li * ci] * (tl // LANES), axis=1)
        corr_i = jnp.concatenate([lr * ci + li * cr] * (tl // LANES), axis=1)
        cum_r = _dot(vr_hi, tri) + _dot(vr_lo, tri) + corr_r
        cum_i = _dot(vi_hi, tri) + _dot(vi_lo, tri) + corr_i
        epr, epi = epow_r[r, :], epow_i[r, :]
        h_r = cum_r * epr - cum_i * epi
        h_i = cum_r * epi + cum_i * epr
        car_r[r, :] = jnp.broadcast_to(h_r[:, tl - 1:tl], (rows, LANES))
        car_i[r, :] = jnp.broadcast_to(h_i[:, tl - 1:tl], (rows, LANES))
        yts.append(_dot(cc[j], jnp.concatenate([_bf(h_r), _bf(h_i)], axis=0)))
    y = jnp.concatenate(yts, axis=0).T + dskip[...] * zs
    ssm_out = _ssm_tail(y, w_sv, b_sv, w_sg, b_sg)

    @pl.when(c == last)
    def _():
        sre_ref[0] = car_r[...].T[0:1, :]
        sim_ref[0] = car_i[...].T[0:1, :]

    x1, h2, route = _mixer_tail(x, hb, conv_out, ssm_out, g1, sh2, sc2, alpha, (d, dc, dsm),
                                w_in, b_in, w_out, b_out, ln1_g, ln1_b, wr_hi, wr_lo, b_r)
    x1_ref[0] = x1
    h2_ref[0] = _bf(h2)
    route_ref[0] = route
    cnt_ref[0, 0] = _block_counts(route)


def _mixer_prompt(x, mod, wts, alpha):
    b, l, d = x.shape
    tl = TOKEN_BLOCK
    nc = l // tl
    dc = wts["w_pw"].shape[0]
    nstate = wts["lamb_r"].shape[0]
    names = ["w_in", "b_in", "w_dw", "b_dw", "cln_g", "cln_b", "w_pw", "b_pw", "bre", "bim", "cc",
             "einv_r", "einv_i", "epow_r", "epow_i", "lamb_r", "lamb_i", "dskip",
             "w_sv", "b_sv", "w_sg", "b_sg", "w_out", "b_out", "ln1_g", "ln1_b",
             "wr_hi", "wr_lo", "b_r"]
    consts = [wts[n] for n in names]
    tok = lambda bi, ci: (bi, ci, 0)
    seq = lambda bi, ci: (bi, 0, 0)
    return pl.pallas_call(
        functools.partial(_mixer_prompt_kernel, alpha=alpha),
        grid=(b, nc),
        in_specs=[pl.BlockSpec((1, tl, d), tok), pl.BlockSpec((1, 1, mod.shape[-1]), seq)]
                 + [_const_spec(a.shape) for a in consts],
        out_specs=[pl.BlockSpec((1, tl, d), tok), pl.BlockSpec((1, tl, d), tok),
                   pl.BlockSpec((1, tl, LANES), tok),
                   pl.BlockSpec((1, 1, 1, LANES), lambda bi, ci: (bi, ci, 0, 0)),
                   pl.BlockSpec((1, CONV_WIDTH - 1, dc), seq),
                   pl.BlockSpec((1, 1, nstate), seq), pl.BlockSpec((1, 1, nstate), seq)],
        out_shape=[jax.ShapeDtypeStruct((b, l, d), _F32), jax.ShapeDtypeStruct((b, l, d), _BF16),
                   jax.ShapeDtypeStruct((b, l, LANES), _F32),
                   jax.ShapeDtypeStruct((b, nc, 1, LANES), _F32),
                   jax.ShapeDtypeStruct((b, CONV_WIDTH - 1, dc), _F32),
                   jax.ShapeDtypeStruct((b, 1, nstate), _F32),
                   jax.ShapeDtypeStruct((b, 1, nstate), _F32)],
        scratch_shapes=[pltpu.VMEM((CONV_HIST + tl, dc), _F32),
                        pltpu.VMEM((SUBLANES - 1, tl + CONV_HIST - SUBLANES, dc), _F32),
                        pltpu.VMEM((tl, dc), _F32),
                        pltpu.VMEM((nstate, LANES), _F32), pltpu.VMEM((nstate, LANES), _F32)],
        compiler_params=_params(("arbitrary", "arbitrary")),
        name="mixer_prompt",
    )(x, mod, *consts)


def _mixer_sample_kernel(x_ref, mod_ref, hist_ref, h0r_ref, h0i_ref,
                         w_in, b_in, w_dw, b_dw, cln_g, cln_b, w_pw, b_pw,
                         bre, bim, cc, lrow_r, lrow_i, dskip,
                         w_sv, b_sv, w_sg, b_sg, w_out, b_out, ln1_g, ln1_b, wr_hi, wr_lo, b_r,
                         x1_ref, h2_ref, route_ref, cnt_ref, cstate_ref, sre_ref, sim_ref, *, alpha):
    steps, nb, d = x_ref.shape
    dc = w_pw.shape[0]
    dsm = dskip.shape[1]
    nj = bre.shape[0]
    rows = bre.shape[1]
    nhist = CONV_WIDTH - 1
    x = x_ref[...].reshape(steps * nb, d)
    mod = jnp.concatenate([mod_ref[...]] * steps, axis=0)
    sh1, sc1, g1 = mod[:, 0:d], mod[:, d:2 * d], mod[:, 2 * d:3 * d]
    sh2, sc2 = mod[:, 3 * d:4 * d], mod[:, 4 * d:5 * d]
    hb = _bf(x * (1.0 + sc1) + sh1)

    u = _in_proj(hb, w_in, b_in, 0, dc) * _sigmoid(_in_proj(hb, w_in, b_in, dc, 2 * dc))
    us = [u[t * nb:(t + 1) * nb, :] for t in range(steps)]
    vs = []
    for t in range(steps):
        acc = jnp.broadcast_to(b_dw[...], (nb, dc))
        for i in range(t, nhist):
            acc = acc + w_dw[i - t:i - t + 1, :] * hist_ref[i]
        for s in range(t + 1):
            acc = acc + w_dw[nhist + s - t:nhist + s - t + 1, :] * us[s]
        vs.append(acc)
    for i in range(nhist):
        src = i + steps
        cstate_ref[i] = hist_ref[src] if src < nhist else us[src - nhist]
    conv_out = _conv_tail(jnp.concatenate(vs, axis=0), cln_g, cln_b, w_pw, b_pw)

    zs = _in_proj(hb, w_in, b_in, 2 * dc, 2 * dc + dsm)
    zsb = _bf(zs)
    h_r, h_i = h0r_ref[...], h0i_ref[...]
    lr, li = lrow_r[...], lrow_i[...]
    ys = []
    for t in range(steps):
        zt = zsb[t * nb:(t + 1) * nb, :]
        bu_r = jnp.concatenate([_dot_nt(zt[:, j * LANES:(j + 1) * LANES], bre[j]) for j in range(nj)], axis=1)
        bu_i = jnp.concatenate([_dot_nt(zt[:, j * LANES:(j + 1) * LANES], bim[j]) for j in range(nj)], axis=1)
        h_r, h_i = lr * h_r - li * h_i + bu_r, lr * h_i + li * h_r + bu_i
        hrb, hib = _bf(h_r), _bf(h_i)
        ys.append(jnp.concatenate(
            [_dot_nt(jnp.concatenate([hrb[:, j * rows:(j + 1) * rows], hib[:, j * rows:(j + 1) * rows]], axis=1),
                     cc[j]) for j in range(nj)], axis=1))
    sre_ref[...] = h_r
    sim_ref[...] = h_i
    y = jnp.concatenate(ys, axis=0) + dskip[...] * zs
    ssm_out = _ssm_tail(y, w_sv, b_sv, w_sg, b_sg)

    x1, h2, route = _mixer_tail(x, hb, conv_out, ssm_out, g1, sh2, sc2, alpha, (d, dc, dsm),
                                w_in, b_in, w_out, b_out, ln1_g, ln1_b, wr_hi, wr_lo, b_r)
    x1_ref[...] = x1
    h2_ref[...] = _bf(h2)
    route_ref[...] = route
    for blk in range(cnt_ref.shape[0]):
        cnt_ref[blk] = _block_counts(route[blk * TOKEN_BLOCK:(blk + 1) * TOKEN_BLOCK, :])


def _mixer_sample(x_tm, mod, hist_tm, h0r, h0i, wts, alpha):
    steps, nb, d = x_tm.shape
    n = steps * nb
    dc = wts["w_pw"].shape[0]
    nstate = h0r.shape[1]
    names = ["w_in", "b_in", "w_dw", "b_dw", "cln_g", "cln_b", "w_pw", "b_pw", "bre", "bim", "cc",
             "lrow_r", "lrow_i", "dskip", "w_sv", "b_sv", "w_sg", "b_sg", "w_out", "b_out",
             "ln1_g", "ln1_b", "wr_hi", "wr_lo", "b_r"]
    args = [x_tm, mod, hist_tm, h0r, h0i] + [wts[k] for k in names]
    return pl.pallas_call(
        functools.partial(_mixer_sample_kernel, alpha=alpha),
        grid=(1,),
        in_specs=[_const_spec(a.shape) for a in args],
        out_specs=[_const_spec((n, d)), _const_spec((n, d)), _const_spec((n, LANES)),
                   _const_spec((n // TOKEN_BLOCK, 1, LANES)),
                   _const_spec((CONV_WIDTH - 1, nb, dc)),
                   _const_spec((nb, nstate)), _const_spec((nb, nstate))],
        out_shape=[jax.ShapeDtypeStruct((n, d), _F32), jax.ShapeDtypeStruct((n, d), _BF16),
                   jax.ShapeDtypeStruct((n, LANES), _F32),
                   jax.ShapeDtypeStruct((n // TOKEN_BLOCK, 1, LANES), _F32),
                   jax.ShapeDtypeStruct((CONV_WIDTH - 1, nb, dc), _F32),
                   jax.ShapeDtypeStruct((nb, nstate), _F32), jax.ShapeDtypeStruct((nb, nstate), _F32)],
        compiler_params=_params(("arbitrary",)),
        name="mixer_sample",
    )(*args)


def _round_up(x, m):
    return (x + m - 1) // m * m


def _plan_sizes(n_tokens):
    nb = n_tokens // TOKEN_BLOCK
    r_max = _round_up(TOP_K * TOKEN_BLOCK + N_EXPERTS * (SEG_PAD - 1), EXPERT_TILE)
    rows_max = nb * (TOP_K * TOKEN_BLOCK + N_EXPERTS * (SEG_PAD - 1)) + N_EXPERTS * (EXPERT_TILE - 1)
    nt_max = -(-rows_max // EXPERT_TILE)
    return nb, r_max, nt_max


def _routing_plan(cnt, r_max, nt_max):
    nb, ne = cnt.shape
    seg = _round_up(cnt, SEG_PAD)
    loc_end = jnp.cumsum(seg, axis=1)
    loc = loc_end - seg
    used = loc_end[:, -1]
    tot = jnp.sum(seg, axis=0)
    totpad = _round_up(tot, EXPERT_TILE)
    eend = jnp.cumsum(totpad)
    estart = eend - totpad
    goff = estart[None, :] + jnp.cumsum(seg, axis=0) - seg
    n_tiles = (eend[-1] // EXPERT_TILE).astype(jnp.int32)
    tiles = jnp.arange(nt_max, dtype=jnp.int32)
    te = jnp.searchsorted(eend // EXPERT_TILE, jnp.minimum(tiles, n_tiles - 1), side="right")
    te = jnp.clip(te, 0, ne - 1).astype(jnp.int32)
    crow = jnp.arange(r_max // SEG_PAD, dtype=jnp.int32) * SEG_PAD
    ce = jax.vmap(lambda le: jnp.searchsorted(le, crow, side="right"))(loc_end)
    ce = jnp.clip(ce, 0, ne - 1)
    cdst = jnp.take_along_axis(goff, ce, axis=1) + crow[None, :] - jnp.take_along_axis(loc, ce, axis=1)
    return dict(
        loc=loc, nct=(used // SEG_PAD).astype(jnp.int32), cdst=cdst.astype(jnp.int32).reshape(-1),
        n_tiles=n_tiles.reshape(1), te=te,
        npad=((totpad - tot) // SEG_PAD).astype(jnp.int32), pad_base=(estart + tot).astype(jnp.int32))


def _dest_columns(route, loc_row):
    n = route.shape[0]
    lane_i = lax.broadcasted_iota(jnp.int32, (n, LANES), 1)
    lane = lane_i.astype(_F32)
    member = jnp.where((route > 0.0) & (lane_i < N_EXPERTS), 1.0, 0.0).astype(_BF16)
    strict = jnp.where(lax.broadcasted_iota(jnp.int32, (n, n), 1)
                       < lax.broadcasted_iota(jnp.int32, (n, n), 0), 1.0, 0.0).astype(_BF16)
    dest = _dot(strict, member) + loc_row
    d4 = jnp.zeros((n, LANES), _F32)
    for k in range(TOP_K):
        idx = route[:, ROUTE_IDX + k:ROUTE_IDX + k + 1]
        dk = jnp.sum(jnp.where(lane == idx, dest, 0.0), axis=-1, keepdims=True)
        d4 = d4 + jnp.where(lane_i == k, dk, 0.0)
    return d4


def _dispatch_kernel(nct, cdst, npad, pad_base, hp_ref, hs_ref, rp_ref, rs_ref, loc_ref,
                     xs_hbm, d4_ref, buf, zbuf, sem, zsem, *, nbp, n_chunks):
    i = pl.program_id(0)
    nblk = pl.num_programs(0)
    slot = lax.rem(i, 2)
    tb = hp_ref.shape[0]
    r_max = buf.shape[1]
    is_p = i < nbp
    h = jnp.where(is_p, hp_ref[...], hs_ref[...])
    route = jnp.where(is_p, rp_ref[...], rs_ref[...])
    d4 = _dest_columns(route, loc_ref[0])
    d4_ref[...] = d4
    d4t = d4.T
    row = lax.broadcasted_iota(jnp.int32, (r_max, tb), 0).astype(_F32)
    p = jnp.zeros((r_max, tb), _F32)
    for k in range(TOP_K):
        p = p + jnp.where(row == d4t[k:k + 1, :], 1.0, 0.0)
    buf[slot] = _bf(_dot(_bf(p), h))

    def chunk_copy(blk, s, c):
        dst = pl.multiple_of(cdst[blk * n_chunks + c], SEG_PAD)
        src = pl.multiple_of(c * SEG_PAD, SEG_PAD)
        return pltpu.make_async_copy(buf.at[s, pl.ds(src, SEG_PAD), :],
                                     xs_hbm.at[pl.ds(dst, SEG_PAD), :], sem.at[s])

    def pad_copy(e, q):
        dst = pl.multiple_of(pad_base[e] + q * SEG_PAD, SEG_PAD)
        return pltpu.make_async_copy(zbuf, xs_hbm.at[pl.ds(dst, SEG_PAD), :], zsem.at[0])

    def start_chunks(c, carry):
        chunk_copy(i, slot, c).start()
        return carry

    lax.fori_loop(0, nct[i], start_chunks, 0)

    @pl.when(i > 0)
    def _():
        def wait_prev(c, carry):
            chunk_copy(i - 1, 1 - slot, c).wait()
            return carry
        lax.fori_loop(0, nct[i - 1], wait_prev, 0)

    @pl.when(i == nblk - 1)
    def _():
        zbuf[...] = jnp.zeros(zbuf.shape, zbuf.dtype)
        for e in range(N_EXPERTS):
            def start_pad(q, carry, e=e):
                pad_copy(e, q).start()
                return carry
            lax.fori_loop(0, npad[e], start_pad, 0)
        for e in range(N_EXPERTS):
            def wait_pad(q, carry, e=e):
                pad_copy(e, q).wait()
                return carry
            lax.fori_loop(0, npad[e], wait_pad, 0)

        def wait_own(c, carry):
            chunk_copy(i, slot, c).wait()
            return carry
        lax.fori_loop(0, nct[i], wait_own, 0)


def _dispatch(plan, h2p, h2s, rp, rs, r_max, nt_max):
    tb = TOKEN_BLOCK
    d = h2p.shape[1]
    nbp, nbs = h2p.shape[0] // tb, h2s.shape[0] // tb
    nb = nbp + nbs
    n_chunks = r_max // SEG_PAD
    loc = jnp.pad(plan["loc"].astype(_F32), ((0, 0), (0, LANES - N_EXPERTS))).reshape(nb, 1, LANES)
    pidx = lambda i, *_: (jnp.minimum(i, nbp - 1), 0)
    sidx = lambda i, *_: (jnp.maximum(i - nbp, 0), 0)
    grid_spec = pltpu.PrefetchScalarGridSpec(
        num_scalar_prefetch=4,
        grid=(nb,),
        in_specs=[pl.BlockSpec((tb, d), pidx), pl.BlockSpec((tb, d), sidx),
                  pl.BlockSpec((tb, LANES), pidx), pl.BlockSpec((tb, LANES), sidx),
                  pl.BlockSpec((1, 1, LANES), lambda i, *_: (i, 0, 0))],
        out_specs=[pl.BlockSpec(memory_space=pl.ANY), pl.BlockSpec((tb, LANES), lambda i, *_: (i, 0))],
        scratch_shapes=[pltpu.VMEM((2, r_max, d), _BF16), pltpu.VMEM((SEG_PAD, d), _BF16),
                        pltpu.SemaphoreType.DMA((2,)), pltpu.SemaphoreType.DMA((1,))])
    return pl.pallas_call(
        functools.partial(_dispatch_kernel, nbp=nbp, n_chunks=n_chunks),
        grid_spec=grid_spec,
        out_shape=[jax.ShapeDtypeStruct((nt_max * EXPERT_TILE, d), _BF16),
                   jax.ShapeDtypeStruct((nb * tb, LANES), _F32)],
        compiler_params=_params(("arbitrary",)),
        name="dispatch",
    )(plan["nct"], plan["cdst"], plan["npad"], plan["pad_base"], h2p, h2s, rp, rs, loc)


def _expert_kernel(te, n_tiles, x_ref, w1_ref, b1_ref, w2_ref, b2_ref, y_ref, w1b, w2b):
    i = pl.program_id(0)
    dff = w2_ref.shape[1]

    @pl.when(i < n_tiles[0])
    def _():
        prev = te[jnp.maximum(i - 1, 0)]

        @pl.when((i == 0) | (te[i] != prev))
        def _():
            w1b[...] = _bf(w1_ref[0])
            w2b[...] = _bf(w2_ref[0])

        gu = _dot(x_ref[...], w1b[...]) + b1_ref[0]
        g = jnp.minimum(gu[:, :dff], SWIGLU_LIMIT)
        up = jnp.clip(gu[:, dff:], -SWIGLU_LIMIT, SWIGLU_LIMIT)
        act = g * _sigmoid(SWIGLU_ALPHA * g) * (up + 1.0)
        y_ref[...] = _bf(_dot(_bf(act), w2b[...]) + b2_ref[0])


def _experts(plan, xs, w1, b1, w2, b2, nt_max):
    ne, d, dff2 = w1.shape
    dff = w2.shape[1]
    tm = EXPERT_TILE
    tile = lambda i, te, nt: (jnp.minimum(i, nt[0] - 1), 0)
    wsel = lambda i, te, nt: (te[i], 0, 0)
    grid_spec = pltpu.PrefetchScalarGridSpec(
        num_scalar_prefetch=2,
        grid=(nt_max,),
        in_specs=[pl.BlockSpec((tm, d), tile),
                  pl.BlockSpec((1, d, dff2), wsel), pl.BlockSpec((1, 1, dff2), wsel),
                  pl.BlockSpec((1, dff, d), wsel), pl.BlockSpec((1, 1, d), wsel)],
        out_specs=pl.BlockSpec((tm, d), tile),
        scratch_shapes=[pltpu.VMEM((d, dff2), _BF16), pltpu.VMEM((dff, d), _BF16)])
    return pl.pallas_call(
        _expert_kernel,
        grid_spec=grid_spec,
        out_shape=jax.ShapeDtypeStruct(xs.shape, _BF16),
        compiler_params=_params(("arbitrary",)),
        name="experts",
    )(plan["te"], plan["n_tiles"], xs, w1, b1.reshape(ne, 1, dff2), w2, b2.reshape(ne, 1, d))


def _combine_kernel(nct, cdst, rp_ref, rs_ref, d4_ref, x1p_ref, x1s_ref, g2p_ref, g2s_ref,
                    ln2_g, ln2_b, ys_hbm, yp_ref, ysm_ref, buf, sem, *, nbp, n_chunks, alpha):
    i = pl.program_id(0)
    nblk = pl.num_programs(0)
    slot = lax.rem(i, 2)
    tb = rp_ref.shape[0]
    r_max = buf.shape[1]

    def chunk_copy(blk, s, c):
        src = pl.multiple_of(cdst[blk * n_chunks + c], SEG_PAD)
        dst = pl.multiple_of(c * SEG_PAD, SEG_PAD)
        return pltpu.make_async_copy(ys_hbm.at[pl.ds(src, SEG_PAD), :],
                                     buf.at[s, pl.ds(dst, SEG_PAD), :], sem.at[s])

    def fetch(blk, s):
        def start(c, carry):
            chunk_copy(blk, s, c).start()
            return carry
        lax.fori_loop(0, nct[blk], start, 0)

    @pl.when(i == 0)
    def _():
        buf[...] = jnp.zeros(buf.shape, buf.dtype)
        fetch(0, 0)

    @pl.when(i + 1 < nblk)
    def _():
        fetch(i + 1, 1 - slot)

    def wait_cur(c, carry):
        chunk_copy(i, slot, c).wait()
        return carry
    lax.fori_loop(0, nct[i], wait_cur, 0)

    is_p = i < nbp
    route = jnp.where(is_p, rp_ref[...], rs_ref[...])
    d4 = d4_ref[...]
    col = lax.broadcasted_iota(jnp.int32, (tb, r_max), 1).astype(_F32)
    pg = jnp.zeros((tb, r_max), _F32)
    for k in range(TOP_K):
        gate = route[:, ROUTE_GATE + k:ROUTE_GATE + k + 1]
        pg = pg + jnp.where(col == d4[:, k:k + 1], gate, 0.0)
    f = _dot(_bf(pg), buf[slot])

    x1 = jnp.where(is_p, x1p_ref[...], x1s_ref[...])
    g2s = jnp.concatenate([g2s_ref[...]] * (tb // g2s_ref.shape[0]), axis=0)
    g2 = jnp.where(is_p, jnp.broadcast_to(g2p_ref[0], g2s.shape), g2s)
    y = _layer_norm(alpha * x1 + g2 * f, ln2_g[...], ln2_b[...])

    @pl.when(is_p)
    def _():
        yp_ref[...] = y

    @pl.when(jnp.logical_not(is_p))
    def _():
        ysm_ref[...] = y


def _combine(plan, ys, rp, rs, d4, x1p, x1s, modp, mods, ln2_g, ln2_b, r_max, alpha, blocks_per_seq):
    tb = TOKEN_BLOCK
    d = x1p.shape[1]
    nbp, nbs = x1p.shape[0] // tb, x1s.shape[0] // tb
    nb = nbp + nbs
    n_chunks = r_max // SEG_PAD
    pidx = lambda i, *_: (jnp.minimum(i, nbp - 1), 0)
    sidx = lambda i, *_: (jnp.maximum(i - nbp, 0), 0)
    g2_lane_block = 5
    grid_spec = pltpu.PrefetchScalarGridSpec(
        num_scalar_prefetch=2,
        grid=(nb,),
        in_specs=[pl.BlockSpec((tb, LANES), pidx), pl.BlockSpec((tb, LANES), sidx),
                  pl.BlockSpec((tb, LANES), lambda i, *_: (i, 0)),
                  pl.BlockSpec((tb, d), pidx), pl.BlockSpec((tb, d), sidx),
                  pl.BlockSpec((1, 1, d), lambda i, *_: (jnp.minimum(i, nbp - 1) // blocks_per_seq, 0,
                                                         g2_lane_block)),
                  pl.BlockSpec((mods.shape[0], d), lambda i, *_: (0, g2_lane_block)),
                  pl.BlockSpec((1, d), lambda i, *_: (0, 0)), pl.BlockSpec((1, d), lambda i, *_: (0, 0)),
                  pl.BlockSpec(memory_space=pl.ANY)],
        out_specs=[pl.BlockSpec((tb, d), pidx), pl.BlockSpec((tb, d), sidx)],
        scratch_shapes=[pltpu.VMEM((2, r_max, d), _BF16), pltpu.SemaphoreType.DMA((2,))])
    return pl.pallas_call(
        functools.partial(_combine_kernel, nbp=nbp, n_chunks=n_chunks, alpha=alpha),
        grid_spec=grid_spec,
        out_shape=[jax.ShapeDtypeStruct(x1p.shape, _F32), jax.ShapeDtypeStruct(x1s.shape, _F32)],
        compiler_params=_params(("arbitrary",)),
        name="combine",
    )(plan["nct"], plan["cdst"], rp, rs, d4, x1p, x1s, modp, mods, ln2_g, ln2_b, ys)


def _complex_powers(zr, zi, n):
    pr, pi = jnp.ones_like(zr)[None], jnp.zeros_like(zi)[None]
    sr, si = zr, zi
    while pr.shape[0] < n:
        pr, pi = (jnp.concatenate([pr, pr * sr - pi * si]), jnp.concatenate([pi, pr * si + pi * sr]))
        sr, si = sr * sr - si * si, 2.0 * sr * si
    return pr[:n], pi[:n]


def _block_diag(m, transpose_inner=False):
    g, a, b = m.shape
    gb = GROUPS_PER_BLOCK
    m = m.reshape(g // gb, gb, a, b)
    eye = jnp.eye(gb, dtype=m.dtype)
    return jnp.einsum("jgab,gk->jgakb", m, eye).reshape(g // gb, gb * a, gb * b)


def _layer_weights(p, q):
    d = p["w_in"].shape[0]
    row = lambda v: v.reshape(1, -1).astype(_F32)
    lam_re, lam_im = p["lam_re"].astype(_F32), p["lam_im"].astype(_F32)
    dt = jnp.exp(p["log_dt"].astype(_F32))[:, None]
    mag = jnp.exp(lam_re * dt)
    lbr, lbi = mag * jnp.cos(lam_im * dt), mag * jnp.sin(lam_im * dt)
    den = lam_re * lam_re + lam_im * lam_im
    nr, ni = lbr - 1.0, lbi
    fr, fi = (nr * lam_re + ni * lam_im) / den, (ni * lam_re - nr * lam_im) / den
    b_re, b_im = p["b_re"].astype(_F32), p["b_im"].astype(_F32)
    bbr = fr[..., None] * b_re - fi[..., None] * b_im
    bbi = fr[..., None] * b_im + fi[..., None] * b_re
    nstate = lam_re.size
    mod2 = lbr * lbr + lbi * lbi
    epr, epi = _complex_powers(lbr, lbi, q)
    eir, eii = _complex_powers(lbr / mod2, -lbi / mod2, q)
    table = lambda t: t.reshape(q, nstate).T
    bcast = lambda v: jnp.broadcast_to(v.reshape(nstate, 1), (nstate, LANES))
    c_re, c_im = p["c_re"].astype(_F32), p["c_im"].astype(_F32)
    w_r = p["w_router"].astype(_F32)
    w_r = jnp.pad(w_r, ((0, 0), (0, LANES - w_r.shape[1])))
    wr_hi = _bf(w_r)
    return dict(
        w_in=_bf(p["w_in"]), b_in=row(p["b_in"]), w_dw=p["w_dw"].astype(_F32), b_dw=row(p["b_dw"]),
        cln_g=row(p["conv_ln_g"]), cln_b=row(p["conv_ln_b"]), w_pw=_bf(p["w_pw"]), b_pw=row(p["b_pw"]),
        bre=_bf(_block_diag(bbr)), bim=_bf(_block_diag(bbi)),
        cc=_bf(jnp.concatenate([_block_diag(c_re), _block_diag(-c_im)], axis=2)),
        einv_r=table(eir), einv_i=table(eii), epow_r=table(epr), epow_i=table(epi),
        lamb_r=bcast(lbr), lamb_i=bcast(lbi), lrow_r=lbr.reshape(1, nstate), lrow_i=lbi.reshape(1, nstate),
        dskip=row(p["d_skip"]),
        w_sv=_bf(p["w_sv"]), b_sv=row(p["b_sv"]), w_sg=_bf(p["w_sg"]), b_sg=row(p["b_sg"]),
        w_out=_bf(p["w_out"]), b_out=row(p["b_out"]), ln1_g=row(p["ln1_g"]), ln1_b=row(p["ln1_b"]),
        wr_hi=wr_hi, wr_lo=_bf(w_r - wr_hi.astype(_F32)),
        b_r=jnp.pad(p["b_router"].astype(_F32), (0, LANES - p["b_router"].shape[0])).reshape(1, LANES),
        ln2_g=row(p["ln2_g"]), ln2_b=row(p["ln2_b"]))


def _layer(xp, xs_tm, c_all, hist_tm, h0r, h0i, p, alpha):
    b, l, d = xp.shape
    steps, nbs, _ = xs_tm.shape
    tb = TOKEN_BLOCK
    assert l % tb == 0 and l >= CONV_WIDTH - 1 and (steps * nbs) % tb == 0 and tb % nbs == 0
    wts = _layer_weights(p, tb)
    mod = _ada(c_all, p["w_ada"].astype(_F32), p["b_ada"].astype(_F32))
    modp, mods = mod[:b].reshape(b, 1, -1), mod[b:]

    x1p, h2p, rp, cntp, conv_p, sre_p, sim_p = _mixer_prompt(xp, modp, wts, alpha)
    x1s, h2s, rs, cnts, conv_s, sre_s, sim_s = _mixer_sample(xs_tm, mods, hist_tm, h0r, h0i, wts, alpha)

    n_tok = b * l + steps * nbs
    nb, r_max, nt_max = _plan_sizes(n_tok)
    cnt = jnp.concatenate([cntp.reshape(-1, LANES), cnts.reshape(-1, LANES)], axis=0)[:, :N_EXPERTS]
    plan = _routing_plan(cnt.astype(jnp.int32), r_max, nt_max)

    flat = lambda a: a.reshape(b * l, a.shape[-1])
    xs_sorted, d4 = _dispatch(plan, flat(h2p), h2s, flat(rp), rs, r_max, nt_max)
    ys_sorted = _experts(plan, xs_sorted, p["w1"], p["b1"], p["w2"], p["b2"], nt_max)
    yp, ysm = _combine(plan, ys_sorted, flat(rp), rs, d4, flat(x1p), x1s, modp, mods,
                       wts["ln2_g"], wts["ln2_b"], r_max, alpha, l // tb)
    return (yp.reshape(b, l, d), ysm.reshape(steps, nbs, d), conv_p, sre_p[:, 0], sim_p[:, 0],
            conv_s, sre_s, sim_s)


def kernel(x_prompt, x_sample, state_conv, state_ssm_re, state_ssm_im, c_prompt, c_sample, w_ada, b_ada, w_in, b_in, w_dw, b_dw, conv_ln_g, conv_ln_b, w_pw, b_pw, lam_re, lam_im, log_dt, b_re, b_im, c_re, c_im, d_skip, w_sv, b_sv, w_sg, b_sg, w_out, b_out, ln1_g, ln1_b, w_router, b_router, w1, b1, w2, b2, ln2_g, ln2_b):
    stacked = dict(w_ada=w_ada, b_ada=b_ada, w_in=w_in, b_in=b_in, w_dw=w_dw, b_dw=b_dw,
                   conv_ln_g=conv_ln_g, conv_ln_b=conv_ln_b, w_pw=w_pw, b_pw=b_pw, lam_re=lam_re,
                   lam_im=lam_im, log_dt=log_dt, b_re=b_re, b_im=b_im, c_re=c_re, c_im=c_im,
                   d_skip=d_skip, w_sv=w_sv, b_sv=b_sv, w_sg=w_sg, b_sg=b_sg, w_out=w_out, b_out=b_out,
                   ln1_g=ln1_g, ln1_b=ln1_b, w_router=w_router, b_router=b_router, w1=w1, b1=b1,
                   w2=w2, b2=b2, ln2_g=ln2_g, ln2_b=ln2_b)
    depth = w_ada.shape[0]
    alpha = (2 * depth) ** 0.25
    b = x_prompt.shape[0]
    nbs = x_sample.shape[0]
    g, s = state_ssm_re.shape[2], state_ssm_re.shape[3]
    xp = x_prompt
    xs_tm = jnp.transpose(x_sample, (1, 0, 2))
    c_all = jnp.concatenate([c_prompt, c_sample], axis=0)
    conv_ps, re_ps, im_ps, conv_ss, re_ss, im_ss = [], [], [], [], [], []
    for layer in range(depth):
        p = {k: v[layer] for k, v in stacked.items()}
        hist_tm = jnp.transpose(state_conv[layer], (1, 0, 2))
        h0r = state_ssm_re[layer].reshape(nbs, g * s)
        h0i = state_ssm_im[layer].reshape(nbs, g * s)
        xp, xs_tm, conv_p, sre_p, sim_p, conv_s, sre_s, sim_s = _layer(
            xp, xs_tm, c_all, hist_tm, h0r, h0i, p, alpha)
        conv_ps.append(conv_p.astype(state_conv.dtype))
        re_ps.append(sre_p.reshape(b, g, s))
        im_ps.append(sim_p.reshape(b, g, s))
        conv_ss.append(jnp.transpose(conv_s, (1, 0, 2)).astype(state_conv.dtype))
        re_ss.append(sre_s.reshape(nbs, g, s))
        im_ss.append(sim_s.reshape(nbs, g, s))
    return (xp, jnp.transpose(xs_tm, (1, 0, 2)), jnp.stack(conv_ps), jnp.stack(re_ps), jnp.stack(im_ps),
            jnp.stack(conv_ss), jnp.stack(re_ss), jnp.stack(im_ss))
```

```python
import functools

import jax
import jax.numpy as jnp
from jax import lax
from jax.experimental import pallas as pl
from jax.experimental.pallas import tpu as pltpu

CONV_WIDTH = 31
SSM_GROUP = 16
SSM_STATE = 64
N_EXPERTS = 32
TOP_K = 4
SWIGLU_LIMIT = 7.0
SWIGLU_ALPHA = 1.702
LN_EPS = 1e-5

LANES = 128
SUBLANES = 8
TOKEN_BLOCK = 256
EXPERT_TILE = 512
SEG_PAD = 16
GROUPS_PER_BLOCK = LANES // SSM_GROUP
CONV_HIST = 32
CONV_ROWS = 32
ROUTE_IDX = N_EXPERTS
ROUTE_GATE = N_EXPERTS + TOP_K
VMEM_LIMIT = 56 * 1024 * 1024

_F32 = jnp.float32
_BF16 = jnp.bfloat16


def _bf(x):
    return x.astype(_BF16)


def _dot(a, b):
    return jnp.dot(a, b, preferred_element_type=_F32)


def _dot_nt(a, b):
    return lax.dot_general(a, b, (((1,), (1,)), ((), ())), preferred_element_type=_F32)


def _split(x):
    hi = _bf(x)
    lo = _bf(x - hi.astype(_F32))
    return hi, lo


def _sigmoid(x):
    return 1.0 / (1.0 + jnp.exp(-x))


def _gelu_tanh(x):
    return 0.5 * x * (1.0 + jnp.tanh(0.7978845608028654 * (x + 0.044715 * (x * x * x))))


def _layer_norm(x, g, b):
    mu = jnp.mean(x, axis=-1, keepdims=True)
    xc = x - mu
    var = jnp.mean(xc * xc, axis=-1, keepdims=True)
    return xc * lax.rsqrt(var + LN_EPS) * g + b


def _const_spec(shape):
    nd = len(shape)
    return pl.BlockSpec(shape, lambda *_: (0,) * nd)


def _params(sem):
    return pltpu.CompilerParams(dimension_semantics=sem, vmem_limit_bytes=VMEM_LIMIT)


def _ada_kernel(c_ref, w_ref, b_ref, o_ref):
    c = c_ref[...]
    s_hi, s_lo = _split(c * _sigmoid(c))
    w_hi, w_lo = _split(w_ref[...])
    o_ref[...] = _dot(s_hi, w_hi) + _dot(s_lo, w_hi) + _dot(s_hi, w_lo) + b_ref[...]


def _ada(c, w, b):
    n, d = c.shape
    cols = w.shape[1]
    tn = d
    return pl.pallas_call(
        _ada_kernel,
        grid=(cols // tn,),
        in_specs=[pl.BlockSpec((n, d), lambda j: (0, 0)),
                  pl.BlockSpec((d, tn), lambda j: (0, j)),
                  pl.BlockSpec((1, tn), lambda j: (0, j))],
        out_specs=pl.BlockSpec((n, tn), lambda j: (0, j)),
        out_shape=jax.ShapeDtypeStruct((n, cols), _F32),
        compiler_params=_params(("arbitrary",)),
        name="ada",
    )(c, w, b.reshape(1, cols))


def _in_proj(hb, w_in, b_in, lo, hi):
    return _dot(hb, w_in[:, lo:hi]) + b_in[:, lo:hi]


def _conv_tail(v, cln_g, cln_b, w_pw, b_pw):
    v = _layer_norm(v, cln_g[...], cln_b[...])
    v = v * _sigmoid(v)
    return _dot(_bf(v), w_pw[...]) + b_pw[...]


def _ssm_tail(y, w_sv, b_sv, w_sg, b_sg):
    yg = _bf(_gelu_tanh(y))
    return (_dot(yg, w_sv[...]) + b_sv[...]) * _sigmoid(_dot(yg, w_sg[...]) + b_sg[...])


def _route(h2, wr_hi, wr_lo, b_r):
    n = h2.shape[0]
    h_hi, h_lo = _split(h2)
    logits = _dot(h_hi, wr_hi[...]) + _dot(h_lo, wr_hi[...]) + _dot(h_hi, wr_lo[...]) + b_r[...]
    lane = lax.broadcasted_iota(jnp.int32, (n, LANES), 1).astype(_F32)
    neg = jnp.float32(-jnp.inf)
    cur = jnp.where(lane < N_EXPERTS, logits, neg)
    vals, idxs = [], []
    for _ in range(TOP_K):
        m = jnp.max(cur, axis=-1, keepdims=True)
        idx = jnp.min(jnp.where(cur == m, lane, float(LANES)), axis=-1, keepdims=True)
        vals.append(m)
        idxs.append(idx)
        cur = jnp.where(lane == idx, neg, cur)
    es = [jnp.exp(v - vals[0]) for v in vals]
    tot = es[0]
    for e in es[1:]:
        tot = tot + e
    inv = 1.0 / tot
    route = jnp.zeros((n, LANES), _F32)
    for k in range(TOP_K):
        gate = es[k] * inv
        route = route + jnp.where(lane == idxs[k], gate, 0.0)
        route = route + jnp.where(lane == float(ROUTE_IDX + k), idxs[k], 0.0)
        route = route + jnp.where(lane == float(ROUTE_GATE + k), gate, 0.0)
    return route


def _block_counts(route):
    lane = lax.broadcasted_iota(jnp.int32, route.shape, 1)
    member = jnp.where((route > 0.0) & (lane < N_EXPERTS), 1.0, 0.0)
    return jnp.sum(member, axis=0, keepdims=True)


def _mixer_tail(x, hb, conv_out, ssm_out, g1, sh2, sc2, alpha, dims, w_in, b_in, w_out, b_out,
                ln1_g, ln1_b, wr_hi, wr_lo, b_r):
    d, dc, dsm = dims
    o = 2 * dc + dsm
    g_conv = _sigmoid(_in_proj(hb, w_in, b_in, o, o + d))
    g_ssm = _sigmoid(_in_proj(hb, w_in, b_in, o + d, o + 2 * d))
    merged = g_conv * conv_out + g_ssm * ssm_out
    m = _dot(_bf(merged), w_out[...]) + b_out[...]
    x1 = _layer_norm(alpha * x + g1 * m, ln1_g[...], ln1_b[...])
    h2 = x1 * (1.0 + sc2) + sh2
    return x1, h2, _route(h2, wr_hi, wr_lo, b_r)


def _mixer_prompt_kernel(x_ref, mod_ref, w_in, b_in, w_dw, b_dw, cln_g, cln_b, w_pw, b_pw,
                         bre, bim, cc, einv_r, einv_i, epow_r, epow_i, lamb_r, lamb_i, dskip,
                         w_sv, b_sv, w_sg, b_sg, w_out, b_out, ln1_g, ln1_b, wr_hi, wr_lo, b_r,
                         x1_ref, h2_ref, route_ref, cnt_ref, cstate_ref, sre_ref, sim_ref,
                         ubuf, ush, vbuf, car_r, car_i, *, alpha):
    c = pl.program_id(1)
    last = pl.num_programs(1) - 1
    tl, d = x_ref.shape[1], x_ref.shape[2]
    dc = w_pw.shape[0]
    dsm = dskip.shape[1]
    nj = bre.shape[0]
    rows = bre.shape[1]

    @pl.when(c == 0)
    def _():
        ubuf[0:CONV_HIST, :] = jnp.zeros((CONV_HIST, dc), _F32)
        car_r[...] = jnp.zeros(car_r.shape, _F32)
        car_i[...] = jnp.zeros(car_i.shape, _F32)

    x = x_ref[0]
    mod = mod_ref[0]
    sh1, sc1, g1 = mod[:, 0:d], mod[:, d:2 * d], mod[:, 2 * d:3 * d]
    sh2, sc2 = mod[:, 3 * d:4 * d], mod[:, 4 * d:5 * d]
    hb = _bf(x * (1.0 + sc1) + sh1)

    u = _in_proj(hb, w_in, b_in, 0, dc) * _sigmoid(_in_proj(hb, w_in, b_in, dc, 2 * dc))
    ubuf[CONV_HIST:CONV_HIST + tl, :] = u
    span = ush.shape[1]
    for r in range(1, SUBLANES):
        ush[r - 1] = ubuf[r:r + span, :]

    def conv_rows(rb, carry):
        r0 = pl.multiple_of(rb * CONV_ROWS, CONV_ROWS)
        acc = jnp.broadcast_to(b_dw[...], (CONV_ROWS, dc))
        for k in range(CONV_WIDTH):
            q, r = divmod(CONV_HIST - (CONV_WIDTH - 1) + k, SUBLANES)
            rows_k = pl.ds(r0 + q * SUBLANES, CONV_ROWS)
            tap = ubuf[rows_k, :] if r == 0 else ush[r - 1, rows_k, :]
            acc = acc + w_dw[k:k + 1, :] * tap
        vbuf[pl.ds(r0, CONV_ROWS), :] = acc
        return carry

    lax.fori_loop(0, tl // CONV_ROWS, conv_rows, 0)

    @pl.when(c == last)
    def _():
        cstate_ref[0] = ubuf[CONV_HIST + tl - (CONV_WIDTH - 1):CONV_HIST + tl, :]

    ubuf[0:CONV_HIST, :] = ubuf[tl:tl + CONV_HIST, :]
    conv_out = _conv_tail(vbuf[...], cln_g, cln_b, w_pw, b_pw)

    zs = _in_proj(hb, w_in, b_in, 2 * dc, 2 * dc + dsm)
    zst = _bf(zs.T)
    tri = jnp.where(lax.broadcasted_iota(jnp.int32, (tl, tl), 0)
                    <= lax.broadcasted_iota(jnp.int32, (tl, tl), 1), 1.0, 0.0).astype(_BF16)
    yts = []
    for j in range(nj):
        r = slice(j * rows, (j + 1) * rows)
        zj = zst[j * LANES:(j + 1) * LANES, :]
        bu_r = _dot(bre[j], zj)
        bu_i = _dot(bim[j], zj)
        eir, eii = einv_r[r, :], einv_i[r, :]
        v_r = bu_r * eir - bu_i * eii
        v_i = bu_r * eii + bu_i * eir
        vr_hi, vr_lo = _split(v_r)
        vi_hi, vi_lo = _split(v_i)
        cr, ci = car_r[r, :], car_i[r, :]
        lr, li = lamb_r[r, :], lamb_i[r, :]
        corr_r = jnp.concatenate([lr * cr - li * ci] * (tl // LANES), axis=1)
        corr_i = jnp.concatenate([lr * ci + li * cr] * (tl // LANES), axis=1)
        cum_r = _dot(vr_hi, tri) + _dot(vr_lo, tri) + corr_r
        cum_i = _dot(vi_hi, tri) + _dot(vi_lo, tri) + corr_i
        epr, epi = epow_r[r, :], epow_i[r, :]
        h_r = cum_r * epr - cum_i * epi
        h_i = cum_r * epi + cum_i * epr
        car_r[r, :] = jnp.broadcast_to(h_r[:, tl - 1:tl], (rows, LANES))
        car_i[r, :] = jnp.broadcast_to(h_i[:, tl - 1:tl], (rows, LANES))
        yts.append(_dot(cc[j], jnp.concatenate([_bf(h_r), _bf(h_i)], axis=0)))
    y = jnp.concatenate(yts, axis=0).T + dskip[...] * zs
    ssm_out = _ssm_tail(y, w_sv, b_sv, w_sg, b_sg)

    @pl.when(c == last)
    def _():
        sre_ref[0] = car_r[...].T[0:1, :]
        sim_ref[0] = car_i[...].T[0:1, :]

    x1, h2, route = _mixer_tail(x, hb, conv_out, ssm_out, g1, sh2, sc2, alpha, (d, dc, dsm),
                                w_in, b_in, w_out, b_out, ln1_g, ln1_b, wr_hi, wr_lo, b_r)
    x1_ref[0] = x1
    h2_ref[0] = _bf(h2)
    route_ref[0] = route
    cnt_ref[0, 0] = _block_counts(route)


def _mixer_prompt(x, mod, wts, alpha):
    b, l, d = x.shape
    tl = TOKEN_BLOCK
    nc = l // tl
    dc = wts["w_pw"].shape[0]
    nstate = wts["lamb_r"].shape[0]
    names = ["w_in", "b_in", "w_dw", "b_dw", "cln_g", "cln_b", "w_pw", "b_pw", "bre", "bim", "cc",
             "einv_r", "einv_i", "epow_r", "epow_i", "lamb_r", "lamb_i", "dskip",
             "w_sv", "b_sv", "w_sg", "b_sg", "w_out", "b_out", "ln1_g", "ln1_b",
             "wr_hi", "wr_lo", "b_r"]
    consts = [wts[n] for n in names]
    tok = lambda bi, ci: (bi, ci, 0)
    seq = lambda bi, ci: (bi, 0, 0)
    return pl.pallas_call(
        functools.partial(_mixer_prompt_kernel, alpha=alpha),
        grid=(b, nc),
        in_specs=[pl.BlockSpec((1, tl, d), tok), pl.BlockSpec((1, 1, mod.shape[-1]), seq)]
                 + [_const_spec(a.shape) for a in consts],
        out_specs=[pl.BlockSpec((1, tl, d), tok), pl.BlockSpec((1, tl, d), tok),
                   pl.BlockSpec((1, tl, LANES), tok),
                   pl.BlockSpec((1, 1, 1, LANES), lambda bi, ci: (bi, ci, 0, 0)),
                   pl.BlockSpec((1, CONV_WIDTH - 1, dc), seq),
                   pl.BlockSpec((1, 1, nstate), seq), pl.BlockSpec((1, 1, nstate), seq)],
        out_shape=[jax.ShapeDtypeStruct((b, l, d), _F32), jax.ShapeDtypeStruct((b, l, d), _BF16),
                   jax.ShapeDtypeStruct((b, l, LANES), _F32),
                   jax.ShapeDtypeStruct((b, nc, 1, LANES), _F32),
                   jax.ShapeDtypeStruct((b, CONV_WIDTH - 1, dc), _F32),
                   jax.ShapeDtypeStruct((b, 1, nstate), _F32),
                   jax.ShapeDtypeStruct((b, 1, nstate), _F32)],
        scratch_shapes=[pltpu.VMEM((CONV_HIST + tl, dc), _F32),
                        pltpu.VMEM((SUBLANES - 1, tl + CONV_HIST - SUBLANES, dc), _F32),
                        pltpu.VMEM((tl, dc), _F32),
                        pltpu.VMEM((nstate, LANES), _F32), pltpu.VMEM((nstate, LANES), _F32)],
        compiler_params=_params(("arbitrary", "arbitrary")),
        name="mixer_prompt",
    )(x, mod, *consts)


def _mixer_sample_kernel(x_ref, mod_ref, hist_ref, h0r_ref, h0i_ref,
                         w_in, b_in, w_dw, b_dw, cln_g, cln_b, w_pw, b_pw,
                         bre, bim, cc, lrow_r, lrow_i, dskip,
                         w_sv, b_sv, w_sg, b_sg, w_out, b_out, ln1_g, ln1_b, wr_hi, wr_lo, b_r,
                         x1_ref, h2_ref, route_ref, cnt_ref, cstate_ref, sre_ref, sim_ref, *, alpha):
    steps, nb, d = x_ref.shape
    dc = w_pw.shape[0]
    dsm = dskip.shape[1]
    nj = bre.shape[0]
    rows = bre.shape[1]
    nhist = CONV_WIDTH - 1
    x = x_ref[...].reshape(steps * nb, d)
    mod = jnp.concatenate([mod_ref[...]] * steps, axis=0)
    sh1, sc1, g1 = mod[:, 0:d], mod[:, d:2 * d], mod[:, 2 * d:3 * d]
    sh2, sc2 = mod[:, 3 * d:4 * d], mod[:, 4 * d:5 * d]
    hb = _bf(x * (1.0 + sc1) + sh1)

    u = _in_proj(hb, w_in, b_in, 0, dc) * _sigmoid(_in_proj(hb, w_in, b_in, dc, 2 * dc))
    us = [u[t * nb:(t + 1) * nb, :] for t in range(steps)]
    vs = []
    for t in range(steps):
        acc = jnp.broadcast_to(b_dw[...], (nb, dc))
        for i in range(t, nhist):
            acc = acc + w_dw[i - t:i - t + 1, :] * hist_ref[i]
        for s in range(t + 1):
            acc = acc + w_dw[nhist + s - t:nhist + s - t + 1, :] * us[s]
        vs.append(acc)
    for i in range(nhist):
        src = i + steps
        cstate_ref[i] = hist_ref[src] if src < nhist else us[src - nhist]
    conv_out = _conv_tail(jnp.concatenate(vs, axis=0), cln_g, cln_b, w_pw, b_pw)

    zs = _in_proj(hb, w_in, b_in, 2 * dc, 2 * dc + dsm)
    zsb = _bf(zs)
    h_r, h_i = h0r_ref[...], h0i_ref[...]
    lr, li = lrow_r[...], lrow_i[...]
    ys = []
    for t in range(steps):
        zt = zsb[t * nb:(t + 1) * nb, :]
        bu_r = jnp.concatenate([_dot_nt(zt[:, j * LANES:(j + 1) * LANES], bre[j]) for j in range(nj)], axis=1)
        bu_i = jnp.concatenate([_dot_nt(zt[:, j * LANES:(j + 1) * LANES], bim[j]) for j in range(nj)], axis=1)
        h_r, h_i = lr * h_r - li * h_i + bu_r, lr * h_i + li * h_r + bu_i
        hrb, hib = _bf(h_r), _bf(h_i)
        ys.append(jnp.concatenate(
            [_dot_nt(jnp.concatenate([hrb[:, j * rows:(j + 1) * rows], hib[:, j * rows:(j + 1) * rows]], axis=1),
                     cc[j]) for j in range(nj)], axis=1))
    sre_ref[...] = h_r
    sim_ref[...] = h_i
    y = jnp.concatenate(ys, axis=0) + dskip[...] * zs
    ssm_out = _ssm_tail(y, w_sv, b_sv, w_sg, b_sg)

    x1, h2, route = _mixer_tail(x, hb, conv_out, ssm_out, g1, sh2, sc2, alpha, (d, dc, dsm),
                                w_in, b_in, w_out, b_out, ln1_g, ln1_b, wr_hi, wr_lo, b_r)
    x1_ref[...] = x1
    h2_ref[...] = _bf(h2)
    route_ref[...] = route
    for blk in range(cnt_ref.shape[0]):
        cnt_ref[blk] = _block_counts(route[blk * TOKEN_BLOCK:(blk + 1) * TOKEN_BLOCK, :])


def _mixer_sample(x_tm, mod, hist_tm, h0r, h0i, wts, alpha):
    steps, nb, d = x_tm.shape
    n = steps * nb
    dc = wts["w_pw"].shape[0]
    nstate = h0r.shape[1]
    names = ["w_in", "b_in", "w_dw", "b_dw", "cln_g", "cln_b", "w_pw", "b_pw", "bre", "bim", "cc",
             "lrow_r", "lrow_i", "dskip", "w_sv", "b_sv", "w_sg", "b_sg", "w_out", "b_out",
             "ln1_g", "ln1_b", "wr_hi", "wr_lo", "b_r"]
    args = [x_tm, mod, hist_tm, h0r, h0i] + [wts[k] for k in names]
    return pl.pallas_call(
        functools.partial(_mixer_sample_kernel, alpha=alpha),
        grid=(1,),
        in_specs=[_const_spec(a.shape) for a in args],
        out_specs=[_const_spec((n, d)), _const_spec((n, d)), _const_spec((n, LANES)),
                   _const_spec((n // TOKEN_BLOCK, 1, LANES)),
                   _const_spec((CONV_WIDTH - 1, nb, dc)),
                   _const_spec((nb, nstate)), _const_spec((nb, nstate))],
        out_shape=[jax.ShapeDtypeStruct((n, d), _F32), jax.ShapeDtypeStruct((n, d), _BF16),
                   jax.ShapeDtypeStruct((n, LANES), _F32),
                   jax.ShapeDtypeStruct((n // TOKEN_BLOCK, 1, LANES), _F32),
                   jax.ShapeDtypeStruct((CONV_WIDTH - 1, nb, dc), _F32),
                   jax.ShapeDtypeStruct((nb, nstate), _F32), jax.ShapeDtypeStruct((nb, nstate), _F32)],
        compiler_params=_params(("arbitrary",)),
        name="mixer_sample",
    )(*args)


def _round_up(x, m):
    return (x + m - 1) // m * m


def _plan_sizes(n_tokens):
    nb = n_tokens // TOKEN_BLOCK
    r_max = _round_up(TOP_K * TOKEN_BLOCK + N_EXPERTS * (SEG_PAD - 1), EXPERT_TILE)
    rows_max = nb * (TOP_K * TOKEN_BLOCK + N_EXPERTS * (SEG_PAD - 1)) + N_EXPERTS * (EXPERT_TILE - 1)
    nt_max = -(-rows_max // EXPERT_TILE)
    return nb, r_max, nt_max


def _routing_plan(cnt, r_max, nt_max):
    nb, ne = cnt.shape
    seg = _round_up(cnt, SEG_PAD)
    loc_end = jnp.cumsum(seg, axis=1)
    loc = loc_end - seg
    used = loc_end[:, -1]
    tot = jnp.sum(seg, axis=0)
    totpad = _round_up(tot, EXPERT_TILE)
    eend = jnp.cumsum(totpad)
    estart = eend - totpad
    goff = estart[None, :] + jnp.cumsum(seg, axis=0) - seg
    n_tiles = (eend[-1] // EXPERT_TILE).astype(jnp.int32)
    tiles = jnp.minimum(jnp.arange(nt_max, dtype=jnp.int32), n_tiles - 1)
    te = jnp.sum(((eend // EXPERT_TILE)[None, :] <= tiles[:, None]).astype(jnp.int32), axis=1)
    te = jnp.minimum(te, ne - 1)
    crow = jnp.arange(r_max // SEG_PAD, dtype=jnp.int32) * SEG_PAD
    ce = jnp.sum((loc_end[:, None, :] <= crow[None, :, None]).astype(jnp.int32), axis=2)
    onehot = ce[:, :, None] == jnp.arange(ne, dtype=jnp.int32)[None, None, :]
    cdst = jnp.sum(jnp.where(onehot, (goff - loc)[:, None, :], 0), axis=2) + crow[None, :]
    return dict(
        loc=loc, nct=(used // SEG_PAD).astype(jnp.int32), cdst=cdst.astype(jnp.int32).reshape(-1),
        n_tiles=n_tiles.reshape(1), te=te,
        npad=((totpad - tot) // SEG_PAD).astype(jnp.int32), pad_base=(estart + tot).astype(jnp.int32))


def _dest_columns(route, loc_row):
    n = route.shape[0]
    lane_i = lax.broadcasted_iota(jnp.int32, (n, LANES), 1)
    lane = lane_i.astype(_F32)
    member = jnp.where((route > 0.0) & (lane_i < N_EXPERTS), 1.0, 0.0).astype(_BF16)
    strict = jnp.where(lax.broadcasted_iota(jnp.int32, (n, n), 1)
                       < lax.broadcasted_iota(jnp.int32, (n, n), 0), 1.0, 0.0).astype(_BF16)
    dest = _dot(strict, member) + loc_row
    d4 = jnp.zeros((n, LANES), _F32)
    for k in range(TOP_K):
        idx = route[:, ROUTE_IDX + k:ROUTE_IDX + k + 1]
        dk = jnp.sum(jnp.where(lane == idx, dest, 0.0), axis=-1, keepdims=True)
        d4 = d4 + jnp.where(lane_i == k, dk, 0.0)
    return d4


def _dispatch_kernel(nct, cdst, npad, pad_base, hp_ref, hs_ref, rp_ref, rs_ref, loc_ref,
                     xs_hbm, d4_ref, buf, zbuf, sem, zsem, *, nbp, n_chunks):
    i = pl.program_id(0)
    nblk = pl.num_programs(0)
    slot = lax.rem(i, 2)
    tb = hp_ref.shape[0]
    r_max = buf.shape[1]
    is_p = i < nbp
    h = jnp.where(is_p, hp_ref[...], hs_ref[...])
    route = jnp.where(is_p, rp_ref[...], rs_ref[...])
    d4 = _dest_columns(route, loc_ref[0])
    d4_ref[...] = d4
    d4t = d4.T
    row = lax.broadcasted_iota(jnp.int32, (r_max, tb), 0).astype(_F32)
    p = jnp.zeros((r_max, tb), _F32)
    for k in range(TOP_K):
        p = p + jnp.where(row == d4t[k:k + 1, :], 1.0, 0.0)
    buf[slot] = _bf(_dot(_bf(p), h))

    def chunk_copy(blk, s, c):
        dst = pl.multiple_of(cdst[blk * n_chunks + c], SEG_PAD)
        src = pl.multiple_of(c * SEG_PAD, SEG_PAD)
        return pltpu.make_async_copy(buf.at[s, pl.ds(src, SEG_PAD), :],
                                     xs_hbm.at[pl.ds(dst, SEG_PAD), :], sem.at[s])

    def pad_copy(e, q):
        dst = pl.multiple_of(pad_base[e] + q * SEG_PAD, SEG_PAD)
        return pltpu.make_async_copy(zbuf, xs_hbm.at[pl.ds(dst, SEG_PAD), :], zsem.at[0])

    def start_chunks(c, carry):
        chunk_copy(i, slot, c).start()
        return carry

    lax.fori_loop(0, nct[i], start_chunks, 0)

    @pl.when(i > 0)
    def _():
        def wait_prev(c, carry):
            chunk_copy(i - 1, 1 - slot, c).wait()
            return carry
        lax.fori_loop(0, nct[i - 1], wait_prev, 0)

    @pl.when(i == nblk - 1)
    def _():
        zbuf[...] = jnp.zeros(zbuf.shape, zbuf.dtype)
        for e in range(N_EXPERTS):
            def start_pad(q, carry, e=e):
                pad_copy(e, q).start()
                return carry
            lax.fori_loop(0, npad[e], start_pad, 0)
        for e in range(N_EXPERTS):
            def wait_pad(q, carry, e=e):
                pad_copy(e, q).wait()
                return carry
            lax.fori_loop(0, npad[e], wait_pad, 0)

        def wait_own(c, carry):
            chunk_copy(i, slot, c).wait()
            return carry
        lax.fori_loop(0, nct[i], wait_own, 0)


def _dispatch(plan, h2p, h2s, rp, rs, r_max, nt_max):
    tb = TOKEN_BLOCK
    d = h2p.shape[1]
    nbp, nbs = h2p.shape[0] // tb, h2s.shape[0] // tb
    nb = nbp + nbs
    n_chunks = r_max // SEG_PAD
    loc = jnp.pad(plan["loc"].astype(_F32), ((0, 0), (0, LANES - N_EXPERTS))).reshape(nb, 1, LANES)
    pidx = lambda i, *_: (jnp.minimum(i, nbp - 1), 0)
    sidx = lambda i, *_: (jnp.maximum(i - nbp, 0), 0)
    grid_spec = pltpu.PrefetchScalarGridSpec(
        num_scalar_prefetch=4,
        grid=(nb,),
        in_specs=[pl.BlockSpec((tb, d), pidx), pl.BlockSpec((tb, d), sidx),
                  pl.BlockSpec((tb, LANES), pidx), pl.BlockSpec((tb, LANES), sidx),
                  pl.BlockSpec((1, 1, LANES), lambda i, *_: (i, 0, 0))],
        out_specs=[pl.BlockSpec(memory_space=pl.ANY), pl.BlockSpec((tb, LANES), lambda i, *_: (i, 0))],
        scratch_shapes=[pltpu.VMEM((2, r_max, d), _BF16), pltpu.VMEM((SEG_PAD, d), _BF16),
                        pltpu.SemaphoreType.DMA((2,)), pltpu.SemaphoreType.DMA((1,))])
    return pl.pallas_call(
        functools.partial(_dispatch_kernel, nbp=nbp, n_chunks=n_chunks),
        grid_spec=grid_spec,
        out_shape=[jax.ShapeDtypeStruct((nt_max * EXPERT_TILE, d), _BF16),
                   jax.ShapeDtypeStruct((nb * tb, LANES), _F32)],
        compiler_params=_params(("arbitrary",)),
        name="dispatch",
    )(plan["nct"], plan["cdst"], plan["npad"], plan["pad_base"], h2p, h2s, rp, rs, loc)


def _expert_kernel(te, n_tiles, x_ref, w1_ref, b1_ref, w2_ref, b2_ref, y_ref, w1b, w2b):
    i = pl.program_id(0)
    dff = w2_ref.shape[1]

    @pl.when(i < n_tiles[0])
    def _():
        prev = te[jnp.maximum(i - 1, 0)]

        @pl.when((i == 0) | (te[i] != prev))
        def _():
            w1b[...] = _bf(w1_ref[0])
            w2b[...] = _bf(w2_ref[0])

        gu = _dot(x_ref[...], w1b[...]) + b1_ref[0]
        g = jnp.minimum(gu[:, :dff], SWIGLU_LIMIT)
        up = jnp.clip(gu[:, dff:], -SWIGLU_LIMIT, SWIGLU_LIMIT)
        act = g * _sigmoid(SWIGLU_ALPHA * g) * (up + 1.0)
        y_ref[...] = _bf(_dot(_bf(act), w2b[...]) + b2_ref[0])


def _experts(plan, xs, w1, b1, w2, b2, nt_max):
    ne, d, dff2 = w1.shape
    dff = w2.shape[1]
    tm = EXPERT_TILE
    tile = lambda i, te, nt: (jnp.minimum(i, nt[0] - 1), 0)
    wsel = lambda i, te, nt: (te[i], 0, 0)
    grid_spec = pltpu.PrefetchScalarGridSpec(
        num_scalar_prefetch=2,
        grid=(nt_max,),
        in_specs=[pl.BlockSpec((tm, d), tile),
                  pl.BlockSpec((1, d, dff2), wsel), pl.BlockSpec((1, 1, dff2), wsel),
                  pl.BlockSpec((1, dff, d), wsel), pl.BlockSpec((1, 1, d), wsel)],
        out_specs=pl.BlockSpec((tm, d), tile),
        scratch_shapes=[pltpu.VMEM((d, dff2), _BF16), pltpu.VMEM((dff, d), _BF16)])
    return pl.pallas_call(
        _expert_kernel,
        grid_spec=grid_spec,
        out_shape=jax.ShapeDtypeStruct(xs.shape, _BF16),
        compiler_params=_params(("arbitrary",)),
        name="experts",
    )(plan["te"], plan["n_tiles"], xs, w1, b1.reshape(ne, 1, dff2), w2, b2.reshape(ne, 1, d))


def _combine_kernel(nct, cdst, rp_ref, rs_ref, d4_ref, x1p_ref, x1s_ref, g2p_ref, g2s_ref,
                    ln2_g, ln2_b, ys_hbm, yp_ref, ysm_ref, buf, sem, *, nbp, n_chunks, alpha):
    i = pl.program_id(0)
    nblk = pl.num_programs(0)
    slot = lax.rem(i, 2)
    tb = rp_ref.shape[0]
    r_max = buf.shape[1]

    def chunk_copy(blk, s, c):
        src = pl.multiple_of(cdst[blk * n_chunks + c], SEG_PAD)
        dst = pl.multiple_of(c * SEG_PAD, SEG_PAD)
        return pltpu.make_async_copy(ys_hbm.at[pl.ds(src, SEG_PAD), :],
                                     buf.at[s, pl.ds(dst, SEG_PAD), :], sem.at[s])

    def fetch(blk, s):
        def start(c, carry):
            chunk_copy(blk, s, c).start()
            return carry
        lax.fori_loop(0, nct[blk], start, 0)

    @pl.when(i == 0)
    def _():
        buf[...] = jnp.zeros(buf.shape, buf.dtype)
        fetch(0, 0)

    @pl.when(i + 1 < nblk)
    def _():
        fetch(i + 1, 1 - slot)

    def wait_cur(c, carry):
        chunk_copy(i, slot, c).wait()
        return carry
    lax.fori_loop(0, nct[i], wait_cur, 0)

    is_p = i < nbp
    route = jnp.where(is_p, rp_ref[...], rs_ref[...])
    d4 = d4_ref[...]
    col = lax.broadcasted_iota(jnp.int32, (tb, r_max), 1).astype(_F32)
    pg = jnp.zeros((tb, r_max), _F32)
    for k in range(TOP_K):
        gate = route[:, ROUTE_GATE + k:ROUTE_GATE + k + 1]
        pg = pg + jnp.where(col == d4[:, k:k + 1], gate, 0.0)
    f = _dot(_bf(pg), buf[slot])

    x1 = jnp.where(is_p, x1p_ref[...], x1s_ref[...])
    g2s = jnp.concatenate([g2s_ref[...]] * (tb // g2s_ref.shape[0]), axis=0)
    g2 = jnp.where(is_p, jnp.broadcast_to(g2p_ref[0], g2s.shape), g2s)
    y = _layer_norm(alpha * x1 + g2 * f, ln2_g[...], ln2_b[...])

    @pl.when(is_p)
    def _():
        yp_ref[...] = y

    @pl.when(jnp.logical_not(is_p))
    def _():
        ysm_ref[...] = y


def _combine(plan, ys, rp, rs, d4, x1p, x1s, modp, mods, ln2_g, ln2_b, r_max, alpha, blocks_per_seq):
    tb = TOKEN_BLOCK
    d = x1p.shape[1]
    nbp, nbs = x1p.shape[0] // tb, x1s.shape[0] // tb
    nb = nbp + nbs
    n_chunks = r_max // SEG_PAD
    pidx = lambda i, *_: (jnp.minimum(i, nbp - 1), 0)
    sidx = lambda i, *_: (jnp.maximum(i - nbp, 0), 0)
    g2_lane_block = 5
    grid_spec = pltpu.PrefetchScalarGridSpec(
        num_scalar_prefetch=2,
        grid=(nb,),
        in_specs=[pl.BlockSpec((tb, LANES), pidx), pl.BlockSpec((tb, LANES), sidx),
                  pl.BlockSpec((tb, LANES), lambda i, *_: (i, 0)),
                  pl.BlockSpec((tb, d), pidx), pl.BlockSpec((tb, d), sidx),
                  pl.BlockSpec((1, 1, d), lambda i, *_: (jnp.minimum(i, nbp - 1) // blocks_per_seq, 0,
                                                         g2_lane_block)),
                  pl.BlockSpec((mods.shape[0], d), lambda i, *_: (0, g2_lane_block)),
                  pl.BlockSpec((1, d), lambda i, *_: (0, 0)), pl.BlockSpec((1, d), lambda i, *_: (0, 0)),
                  pl.BlockSpec(memory_space=pl.ANY)],
        out_specs=[pl.BlockSpec((tb, d), pidx), pl.BlockSpec((tb, d), sidx)],
        scratch_shapes=[pltpu.VMEM((2, r_max, d), _BF16), pltpu.SemaphoreType.DMA((2,))])
    return pl.pallas_call(
        functools.partial(_combine_kernel, nbp=nbp, n_chunks=n_chunks, alpha=alpha),
        grid_spec=grid_spec,
        out_shape=[jax.ShapeDtypeStruct(x1p.shape, _F32), jax.ShapeDtypeStruct(x1s.shape, _F32)],
        compiler_params=_params(("arbitrary",)),
        name="combine",
    )(plan["nct"], plan["cdst"], rp, rs, d4, x1p, x1s, modp, mods, ln2_g, ln2_b, ys)


def _complex_powers(zr, zi, n):
    k = jnp.arange(n, dtype=jnp.int32)[None, :]
    pr = jnp.ones((zr.shape[0], n), _F32)
    pi = jnp.zeros((zr.shape[0], n), _F32)
    sr, si = zr[:, None], zi[:, None]
    bit = 1
    while bit < n:
        on = (k & bit) != 0
        mr, mi = jnp.where(on, sr, 1.0), jnp.where(on, si, 0.0)
        pr, pi = pr * mr - pi * mi, pr * mi + pi * mr
        sr, si = sr * sr - si * si, 2.0 * sr * si
        bit *= 2
    return pr, pi


def _block_diag(m):
    g, a, b = m.shape
    gb = GROUPS_PER_BLOCK
    m = m.reshape(g // gb, gb, a, 1, b)
    eye = jnp.eye(gb, dtype=m.dtype).reshape(1, gb, 1, gb, 1)
    return (m * eye).reshape(g // gb, gb * a, gb * b)


def _layer_weights(p, q):
    d = p["w_in"].shape[0]
    row = lambda v: v.reshape(1, -1).astype(_F32)
    lam_re, lam_im = p["lam_re"].astype(_F32), p["lam_im"].astype(_F32)
    dt = jnp.exp(p["log_dt"].astype(_F32))[:, None]
    mag = jnp.exp(lam_re * dt)
    lbr, lbi = mag * jnp.cos(lam_im * dt), mag * jnp.sin(lam_im * dt)
    den = lam_re * lam_re + lam_im * lam_im
    nr, ni = lbr - 1.0, lbi
    fr, fi = (nr * lam_re + ni * lam_im) / den, (ni * lam_re - nr * lam_im) / den
    b_re, b_im = p["b_re"].astype(_F32), p["b_im"].astype(_F32)
    bbr = fr[..., None] * b_re - fi[..., None] * b_im
    bbi = fr[..., None] * b_im + fi[..., None] * b_re
    nstate = lam_re.size
    mod2 = lbr * lbr + lbi * lbi
    flat = lambda v: v.reshape(nstate)
    epr, epi = _complex_powers(flat(lbr), flat(lbi), q)
    eir, eii = _complex_powers(flat(lbr / mod2), flat(-lbi / mod2), q)
    table = lambda t: t
    bcast = lambda v: jnp.broadcast_to(v.reshape(nstate, 1), (nstate, LANES))
    c_re, c_im = p["c_re"].astype(_F32), p["c_im"].astype(_F32)
    w_r = p["w_router"].astype(_F32)
    w_r = jnp.pad(w_r, ((0, 0), (0, LANES - w_r.shape[1])))
    wr_hi = _bf(w_r)
    return dict(
        w_in=_bf(p["w_in"]), b_in=row(p["b_in"]), w_dw=p["w_dw"].astype(_F32), b_dw=row(p["b_dw"]),
        cln_g=row(p["conv_ln_g"]), cln_b=row(p["conv_ln_b"]), w_pw=_bf(p["w_pw"]), b_pw=row(p["b_pw"]),
        bre=_bf(_block_diag(bbr)), bim=_bf(_block_diag(bbi)),
        cc=_bf(jnp.concatenate([_block_diag(c_re), _block_diag(-c_im)], axis=2)),
        einv_r=table(eir), einv_i=table(eii), epow_r=table(epr), epow_i=table(epi),
        lamb_r=bcast(lbr), lamb_i=bcast(lbi), lrow_r=lbr.reshape(1, nstate), lrow_i=lbi.reshape(1, nstate),
        dskip=row(p["d_skip"]),
        w_sv=_bf(p["w_sv"]), b_sv=row(p["b_sv"]), w_sg=_bf(p["w_sg"]), b_sg=row(p["b_sg"]),
        w_out=_bf(p["w_out"]), b_out=row(p["b_out"]), ln1_g=row(p["ln1_g"]), ln1_b=row(p["ln1_b"]),
        wr_hi=wr_hi, wr_lo=_bf(w_r - wr_hi.astype(_F32)),
        b_r=jnp.pad(p["b_router"].astype(_F32), (0, LANES - p["b_router"].shape[0])).reshape(1, LANES),
        ln2_g=row(p["ln2_g"]), ln2_b=row(p["ln2_b"]))


def _layer(xp, xs_tm, c_all, hist_tm, h0r, h0i, p, alpha):
    b, l, d = xp.shape
    steps, nbs, _ = xs_tm.shape
    tb = TOKEN_BLOCK
    assert l % tb == 0 and l >= CONV_WIDTH - 1 and (steps * nbs) % tb == 0 and tb % nbs == 0
    wts = _layer_weights(p, tb)
    mod = _ada(c_all, p["w_ada"].astype(_F32), p["b_ada"].astype(_F32))
    modp, mods = mod[:b].reshape(b, 1, -1), mod[b:]

    x1p, h2p, rp, cntp, conv_p, sre_p, sim_p = _mixer_prompt(xp, modp, wts, alpha)
    x1s, h2s, rs, cnts, conv_s, sre_s, sim_s = _mixer_sample(xs_tm, mods, hist_tm, h0r, h0i, wts, alpha)

    n_tok = b * l + steps * nbs
    nb, r_max, nt_max = _plan_sizes(n_tok)
    cnt = jnp.concatenate([cntp.reshape(-1, LANES), cnts.reshape(-1, LANES)], axis=0)[:, :N_EXPERTS]
    plan = _routing_plan(cnt.astype(jnp.int32), r_max, nt_max)

    flat = lambda a: a.reshape(b * l, a.shape[-1])
    xs_sorted, d4 = _dispatch(plan, flat(h2p), h2s, flat(rp), rs, r_max, nt_max)
    ys_sorted = _experts(plan, xs_sorted, p["w1"], p["b1"], p["w2"], p["b2"], nt_max)
    yp, ysm = _combine(plan, ys_sorted, flat(rp), rs, d4, flat(x1p), x1s, modp, mods,
                       wts["ln2_g"], wts["ln2_b"], r_max, alpha, l // tb)
    return (yp.reshape(b, l, d), ysm.reshape(steps, nbs, d), conv_p, sre_p[:, 0], sim_p[:, 0],
            conv_s, sre_s, sim_s)


def kernel(x_prompt, x_sample, state_conv, state_ssm_re, state_ssm_im, c_prompt, c_sample, w_ada, b_ada, w_in, b_in, w_dw, b_dw, conv_ln_g, conv_ln_b, w_pw, b_pw, lam_re, lam_im, log_dt, b_re, b_im, c_re, c_im, d_skip, w_sv, b_sv, w_sg, b_sg, w_out, b_out, ln1_g, ln1_b, w_router, b_router, w1, b1, w2, b2, ln2_g, ln2_b):
    stacked = dict(w_ada=w_ada, b_ada=b_ada, w_in=w_in, b_in=b_in, w_dw=w_dw, b_dw=b_dw,
                   conv_ln_g=conv_ln_g, conv_ln_b=conv_ln_b, w_pw=w_pw, b_pw=b_pw, lam_re=lam_re,
                   lam_im=lam_im, log_dt=log_dt, b_re=b_re, b_im=b_im, c_re=c_re, c_im=c_im,
                   d_skip=d_skip, w_sv=w_sv, b_sv=b_sv, w_sg=w_sg, b_sg=b_sg, w_out=w_out, b_out=b_out,
                   ln1_g=ln1_g, ln1_b=ln1_b, w_router=w_router, b_router=b_router, w1=w1, b1=b1,
                   w2=w2, b2=b2, ln2_g=ln2_g, ln2_b=ln2_b)
    depth = w_ada.shape[0]
    alpha = (2 * depth) ** 0.25
    b = x_prompt.shape[0]
    nbs = x_sample.shape[0]
    g, s = state_ssm_re.shape[2], state_ssm_re.shape[3]
    xp = x_prompt
    xs_tm = jnp.transpose(x_sample, (1, 0, 2))
    c_all = jnp.concatenate([c_prompt, c_sample], axis=0)
    conv_ps, re_ps, im_ps, conv_ss, re_ss, im_ss = [], [], [], [], [], []
    for layer in range(depth):
        p = {k: v[layer] for k, v in stacked.items()}
        hist_tm = jnp.transpose(state_conv[layer], (1, 0, 2))
        h0r = state_ssm_re[layer].reshape(nbs, g * s)
        h0i = state_ssm_im[layer].reshape(nbs, g * s)
        xp, xs_tm, conv_p, sre_p, sim_p, conv_s, sre_s, sim_s = _layer(
            xp, xs_tm, c_all, hist_tm, h0r, h0i, p, alpha)
        conv_ps.append(conv_p.astype(state_conv.dtype))
        re_ps.append(sre_p.reshape(b, g, s))
        im_ps.append(sim_p.reshape(b, g, s))
        conv_ss.append(jnp.transpose(conv_s, (1, 0, 2)).astype(state_conv.dtype))
        re_ss.append(sre_s.reshape(nbs, g, s))
        im_ss.append(sim_s.reshape(nbs, g, s))
    return (xp, jnp.transpose(xs_tm, (1, 0, 2)), jnp.stack(conv_ps), jnp.stack(re_ps), jnp.stack(im_ps),
            jnp.stack(conv_ss), jnp.stack(re_ss), jnp.stack(im_ss))
```

```python
import functools

import jax
import jax.numpy as jnp
from jax import lax
from jax.experimental import pallas as pl
from jax.experimental.pallas import tpu as pltpu

CONV_WIDTH = 31
SSM_GROUP = 16
SSM_STATE = 64
N_EXPERTS = 32
TOP_K = 4
SWIGLU_LIMIT = 7.0
SWIGLU_ALPHA = 1.702
LN_EPS = 1e-5

LANES = 128
SUBLANES = 8
TOKEN_BLOCK = 256
EXPERT_TILE = 512
SEG_PAD = 16
GROUPS_PER_BLOCK = LANES // SSM_GROUP
CONV_HIST = 32
CONV_ROWS = 32
ROUTE_IDX = N_EXPERTS
ROUTE_GATE = N_EXPERTS + TOP_K
VMEM_LIMIT = 56 * 1024 * 1024

_F32 = jnp.float32
_BF16 = jnp.bfloat16


def _bf(x):
    return x.astype(_BF16)


def _dot(a, b):
    return jnp.dot(a, b, preferred_element_type=_F32)


def _dot_nt(a, b):
    return lax.dot_general(a, b, (((1,), (1,)), ((), ())), preferred_element_type=_F32)


def _split(x):
    hi = _bf(x)
    lo = _bf(x - hi.astype(_F32))
    return hi, lo


def _sigmoid(x):
    return 1.0 / (1.0 + jnp.exp(-x))


def _gelu_tanh(x):
    return 0.5 * x * (1.0 + jnp.tanh(0.7978845608028654 * (x + 0.044715 * (x * x * x))))


def _layer_norm(x, g, b):
    mu = jnp.mean(x, axis=-1, keepdims=True)
    xc = x - mu
    var = jnp.mean(xc * xc, axis=-1, keepdims=True)
    return xc * lax.rsqrt(var + LN_EPS) * g + b


def _const_spec(shape):
    nd = len(shape)
    return pl.BlockSpec(shape, lambda *_: (0,) * nd)


def _params(sem):
    return pltpu.CompilerParams(dimension_semantics=sem, vmem_limit_bytes=VMEM_LIMIT)


def _ada_kernel(c_ref, w_ref, b_ref, o_ref):
    c = c_ref[...]
    s_hi, s_lo = _split(c * _sigmoid(c))
    w_hi, w_lo = _split(w_ref[...])
    o_ref[...] = _dot(s_hi, w_hi) + _dot(s_lo, w_hi) + _dot(s_hi, w_lo) + b_ref[...]


def _ada(c, w, b):
    n, d = c.shape
    cols = w.shape[1]
    tn = d
    return pl.pallas_call(
        _ada_kernel,
        grid=(cols // tn,),
        in_specs=[pl.BlockSpec((n, d), lambda j: (0, 0)),
                  pl.BlockSpec((d, tn), lambda j: (0, j)),
                  pl.BlockSpec((1, tn), lambda j: (0, j))],
        out_specs=pl.BlockSpec((n, tn), lambda j: (0, j)),
        out_shape=jax.ShapeDtypeStruct((n, cols), _F32),
        compiler_params=_params(("arbitrary",)),
        name="ada",
    )(c, w, b.reshape(1, cols))


def _in_proj(hb, w_in, b_in, lo, hi):
    return _dot(hb, w_in[:, lo:hi]) + b_in[:, lo:hi]


def _conv_tail(v, cln_g, cln_b, w_pw, b_pw):
    v = _layer_norm(v, cln_g[...], cln_b[...])
    v = v * _sigmoid(v)
    return _dot(_bf(v), w_pw[...]) + b_pw[...]


def _ssm_tail(y, w_sv, b_sv, w_sg, b_sg):
    yg = _bf(_gelu_tanh(y))
    return (_dot(yg, w_sv[...]) + b_sv[...]) * _sigmoid(_dot(yg, w_sg[...]) + b_sg[...])


def _route(h2, wr_hi, wr_lo, b_r):
    n = h2.shape[0]
    ne = wr_hi.shape[0]
    h_hi, h_lo = _split(h2)
    logits = _dot_nt(wr_hi[...], h_hi) + _dot_nt(wr_hi[...], h_lo) + _dot_nt(wr_lo[...], h_hi) + b_r[...]
    row = lax.broadcasted_iota(jnp.int32, (ne, n), 0).astype(_F32)
    neg = jnp.float32(-jnp.inf)
    cur = logits
    vals, idxs = [], []
    for _ in range(TOP_K):
        m = jnp.max(cur, axis=0, keepdims=True)
        idx = jnp.min(jnp.where(cur == m, row, float(ne)), axis=0, keepdims=True)
        vals.append(m)
        idxs.append(idx)
        cur = jnp.where(row == idx, neg, cur)
    es = [jnp.exp(v - vals[0]) for v in vals]
    tot = es[0]
    for e in es[1:]:
        tot = tot + e
    inv = 1.0 / tot
    comb = jnp.zeros((ne, n), _F32)
    row8 = lax.broadcasted_iota(jnp.int32, (2 * TOP_K, n), 0)
    extra = jnp.zeros((2 * TOP_K, n), _F32)
    for k in range(TOP_K):
        gate = es[k] * inv
        comb = comb + jnp.where(row == idxs[k], gate, 0.0)
        extra = extra + jnp.where(row8 == k, idxs[k], 0.0) + jnp.where(row8 == TOP_K + k, gate, 0.0)
    rest = jnp.zeros((LANES - ne - 2 * TOP_K, n), _F32)
    return jnp.concatenate([comb, extra, rest], axis=0).T


def _block_counts(route):
    lane = lax.broadcasted_iota(jnp.int32, route.shape, 1)
    member = jnp.where((route > 0.0) & (lane < N_EXPERTS), 1.0, 0.0)
    return jnp.sum(member, axis=0, keepdims=True)


def _mixer_tail(x, zg, conv_out, ssm_out, g1, sh2, sc2, alpha, w_out, b_out,
                ln1_g, ln1_b, wr_hi, wr_lo, b_r):
    d = x.shape[1]
    merged = _sigmoid(zg[:, :d]) * conv_out + _sigmoid(zg[:, d:]) * ssm_out
    m = _dot(_bf(merged), w_out[...]) + b_out[...]
    x1 = _layer_norm(alpha * x + g1 * m, ln1_g[...], ln1_b[...])
    h2 = x1 * (1.0 + sc2) + sh2
    return x1, h2, _route(h2, wr_hi, wr_lo, b_r)


def _mixer_prompt_kernel(x_ref, mod_ref, w_in, b_in, w_dw, b_dw, cln_g, cln_b, w_pw, b_pw,
                         bre, bim, cc, einv_r, einv_i, epow_r, epow_i, lamb_r, lamb_i, dskip,
                         w_sv, b_sv, w_sg, b_sg, w_out, b_out, ln1_g, ln1_b, wr_hi, wr_lo, b_r,
                         x1_ref, h2_ref, route_ref, cnt_ref, cstate_ref, sre_ref, sim_ref,
                         ubuf, ush, car_r, car_i, *, alpha):
    c = pl.program_id(1)
    last = pl.num_programs(1) - 1
    tl, d = x_ref.shape[1], x_ref.shape[2]
    dc = w_pw.shape[0]
    dsm = dskip.shape[1]
    nj = bre.shape[0]
    rows = bre.shape[1]

    @pl.when(c == 0)
    def _():
        ubuf[0:CONV_HIST, :] = jnp.zeros((CONV_HIST, dc), _F32)
        car_r[...] = jnp.zeros(car_r.shape, _F32)
        car_i[...] = jnp.zeros(car_i.shape, _F32)

    x = x_ref[0]
    mod = mod_ref[0]
    sh1, sc1, g1 = mod[:, 0:d], mod[:, d:2 * d], mod[:, 2 * d:3 * d]
    sh2, sc2 = mod[:, 3 * d:4 * d], mod[:, 4 * d:5 * d]
    hb = _bf(x * (1.0 + sc1) + sh1)

    u = _in_proj(hb, w_in, b_in, 0, dc) * _sigmoid(_in_proj(hb, w_in, b_in, dc, 2 * dc))
    ubuf[CONV_HIST:CONV_HIST + tl, :] = u
    span = ush.shape[1]
    for r in range(1, SUBLANES):
        ush[r - 1] = ubuf[r:r + span, :]

    n_blocks = tl // CONV_ROWS
    gate_lo = 2 * dc + dsm
    gate_w = 2 * d // n_blocks
    vblocks, zg_cols = [], []
    for blk in range(n_blocks):
        r0 = blk * CONV_ROWS
        acc = jnp.broadcast_to(b_dw[...], (CONV_ROWS, dc))
        for k in range(CONV_WIDTH):
            q, r = divmod(CONV_HIST - (CONV_WIDTH - 1) + k, SUBLANES)
            rows_k = slice(r0 + q * SUBLANES, r0 + q * SUBLANES + CONV_ROWS)
            tap = ubuf[rows_k, :] if r == 0 else ush[r - 1, rows_k, :]
            acc = acc + w_dw[k:k + 1, :] * tap
        vblocks.append(acc)
        zg_cols.append(_in_proj(hb, w_in, b_in, gate_lo + blk * gate_w, gate_lo + (blk + 1) * gate_w))
    ubuf[0:CONV_HIST, :] = ubuf[tl:tl + CONV_HIST, :]
    conv_out = _conv_tail(jnp.concatenate(vblocks, axis=0), cln_g, cln_b, w_pw, b_pw)

    zs = _in_proj(hb, w_in, b_in, 2 * dc, 2 * dc + dsm)
    zst = _bf(zs.T)
    tri = jnp.where(lax.broadcasted_iota(jnp.int32, (tl, tl), 0)
                    <= lax.broadcasted_iota(jnp.int32, (tl, tl), 1), 1.0, 0.0).astype(_BF16)
    yts = []
    for j in range(nj):
        r = slice(j * rows, (j + 1) * rows)
        zj = zst[j * LANES:(j + 1) * LANES, :]
        bu_r = _dot(bre[j], zj)
        bu_i = _dot(bim[j], zj)
        eir, eii = einv_r[r, :], einv_i[r, :]
        v_r = bu_r * eir - bu_i * eii
        v_i = bu_r * eii + bu_i * eir
        cr, ci = car_r[r, :], car_i[r, :]
        lr, li = lamb_r[r, :], lamb_i[r, :]
        corr_r = jnp.concatenate([lr * cr - li * ci] * (tl // LANES), axis=1)
        corr_i = jnp.concatenate([lr * ci + li * cr] * (tl // LANES), axis=1)
        cum_r = _dot(_bf(v_r), tri) + corr_r
        cum_i = _dot(_bf(v_i), tri) + corr_i
        epr, epi = epow_r[r, :], epow_i[r, :]
        h_r = cum_r * epr - cum_i * epi
        h_i = cum_r * epi + cum_i * epr
        car_r[r, :] = jnp.broadcast_to(h_r[:, tl - 1:tl], (rows, LANES))
        car_i[r, :] = jnp.broadcast_to(h_i[:, tl - 1:tl], (rows, LANES))
        yts.append(_dot(cc[j], jnp.concatenate([_bf(h_r), _bf(h_i)], axis=0)))
    y = jnp.concatenate(yts, axis=0).T + dskip[...] * zs
    ssm_out = _ssm_tail(y, w_sv, b_sv, w_sg, b_sg)

    x1, h2, route = _mixer_tail(x, jnp.concatenate(zg_cols, axis=1), conv_out, ssm_out, g1, sh2, sc2, alpha,
                                w_out, b_out, ln1_g, ln1_b, wr_hi, wr_lo, b_r)
    x1_ref[0] = x1
    h2_ref[0] = _bf(h2)
    route_ref[0] = route
    cnt_ref[0, 0] = _block_counts(route)

    @pl.when(c == last)
    def _():
        cstate_ref[0] = ubuf[CONV_HIST + tl - (CONV_WIDTH - 1):CONV_HIST + tl, :]
        sre_ref[0] = car_r[...].T[0:1, :]
        sim_ref[0] = car_i[...].T[0:1, :]


def _mixer_prompt(x, mod, wts, alpha):
    b, l, d = x.shape
    tl = TOKEN_BLOCK
    nc = l // tl
    dc = wts["w_pw"].shape[0]
    nstate = wts["lamb_r"].shape[0]
    names = ["w_in", "b_in", "w_dw", "b_dw", "cln_g", "cln_b", "w_pw", "b_pw", "bre", "bim", "cc",
             "einv_r", "einv_i", "epow_r", "epow_i", "lamb_r", "lamb_i", "dskip",
             "w_sv", "b_sv", "w_sg", "b_sg", "w_out", "b_out", "ln1_g", "ln1_b",
             "wr_hi", "wr_lo", "b_r"]
    consts = [wts[n] for n in names]
    tok = lambda bi, ci: (bi, ci, 0)
    seq = lambda bi, ci: (bi, 0, 0)
    return pl.pallas_call(
        functools.partial(_mixer_prompt_kernel, alpha=alpha),
        grid=(b, nc),
        in_specs=[pl.BlockSpec((1, tl, d), tok), pl.BlockSpec((1, 1, mod.shape[-1]), seq)]
                 + [_const_spec(a.shape) for a in consts],
        out_specs=[pl.BlockSpec((1, tl, d), tok), pl.BlockSpec((1, tl, d), tok),
                   pl.BlockSpec((1, tl, LANES), tok),
                   pl.BlockSpec((1, 1, 1, LANES), lambda bi, ci: (bi, ci, 0, 0)),
                   pl.BlockSpec((1, CONV_WIDTH - 1, dc), seq),
                   pl.BlockSpec((1, 1, nstate), seq), pl.BlockSpec((1, 1, nstate), seq)],
        out_shape=[jax.ShapeDtypeStruct((b, l, d), _F32), jax.ShapeDtypeStruct((b, l, d), _BF16),
                   jax.ShapeDtypeStruct((b, l, LANES), _F32),
                   jax.ShapeDtypeStruct((b, nc, 1, LANES), _F32),
                   jax.ShapeDtypeStruct((b, CONV_WIDTH - 1, dc), _F32),
                   jax.ShapeDtypeStruct((b, 1, nstate), _F32),
                   jax.ShapeDtypeStruct((b, 1, nstate), _F32)],
        scratch_shapes=[pltpu.VMEM((CONV_HIST + tl, dc), _F32),
                        pltpu.VMEM((SUBLANES - 1, tl + CONV_HIST - SUBLANES, dc), _F32),
                        pltpu.VMEM((nstate, LANES), _F32), pltpu.VMEM((nstate, LANES), _F32)],
        compiler_params=_params(("arbitrary", "arbitrary")),
        name="mixer_prompt",
    )(x, mod, *consts)


def _mixer_sample_kernel(x_ref, mod_ref, hist_ref, h0r_ref, h0i_ref,
                         w_in, b_in, w_dw, b_dw, cln_g, cln_b, w_pw, b_pw,
                         bre, bim, cc, lrow_r, lrow_i, dskip,
                         w_sv, b_sv, w_sg, b_sg, w_out, b_out, ln1_g, ln1_b, wr_hi, wr_lo, b_r,
                         x1_ref, h2_ref, route_ref, cnt_ref, cstate_ref, sre_ref, sim_ref, *, alpha):
    steps, nb, d = x_ref.shape
    dc = w_pw.shape[0]
    dsm = dskip.shape[1]
    nj = bre.shape[0]
    rows = bre.shape[1]
    nhist = CONV_WIDTH - 1
    x = x_ref[...].reshape(steps * nb, d)
    mod = jnp.concatenate([mod_ref[...]] * steps, axis=0)
    sh1, sc1, g1 = mod[:, 0:d], mod[:, d:2 * d], mod[:, 2 * d:3 * d]
    sh2, sc2 = mod[:, 3 * d:4 * d], mod[:, 4 * d:5 * d]
    hb = _bf(x * (1.0 + sc1) + sh1)

    u = _in_proj(hb, w_in, b_in, 0, dc) * _sigmoid(_in_proj(hb, w_in, b_in, dc, 2 * dc))
    us = [u[t * nb:(t + 1) * nb, :] for t in range(steps)]
    vs = []
    for t in range(steps):
        acc = jnp.broadcast_to(b_dw[...], (nb, dc))
        for i in range(t, nhist):
            acc = acc + w_dw[i - t:i - t + 1, :] * hist_ref[i]
        for s in range(t + 1):
            acc = acc + w_dw[nhist + s - t:nhist + s - t + 1, :] * us[s]
        vs.append(acc)
    for i in range(nhist):
        src = i + steps
        cstate_ref[i] = hist_ref[src] if src < nhist else us[src - nhist]
    conv_out = _conv_tail(jnp.concatenate(vs, axis=0), cln_g, cln_b, w_pw, b_pw)

    zs = _in_proj(hb, w_in, b_in, 2 * dc, 2 * dc + dsm)
    zsb = _bf(zs)
    h_r, h_i = h0r_ref[...], h0i_ref[...]
    lr, li = lrow_r[...], lrow_i[...]
    ys = []
    for t in range(steps):
        zt = zsb[t * nb:(t + 1) * nb, :]
        bu_r = jnp.concatenate([_dot_nt(zt[:, j * LANES:(j + 1) * LANES], bre[j]) for j in range(nj)], axis=1)
        bu_i = jnp.concatenate([_dot_nt(zt[:, j * LANES:(j + 1) * LANES], bim[j]) for j in range(nj)], axis=1)
        h_r, h_i = lr * h_r - li * h_i + bu_r, lr * h_i + li * h_r + bu_i
        hrb, hib = _bf(h_r), _bf(h_i)
        ys.append(jnp.concatenate(
            [_dot_nt(jnp.concatenate([hrb[:, j * rows:(j + 1) * rows], hib[:, j * rows:(j + 1) * rows]], axis=1),
                     cc[j]) for j in range(nj)], axis=1))
    sre_ref[...] = h_r
    sim_ref[...] = h_i
    y = jnp.concatenate(ys, axis=0) + dskip[...] * zs
    ssm_out = _ssm_tail(y, w_sv, b_sv, w_sg, b_sg)

    zg = _in_proj(hb, w_in, b_in, 2 * dc + dsm, 2 * dc + dsm + 2 * d)
    x1, h2, route = _mixer_tail(x, zg, conv_out, ssm_out, g1, sh2, sc2, alpha,
                                w_out, b_out, ln1_g, ln1_b, wr_hi, wr_lo, b_r)
    x1_ref[...] = x1
    h2_ref[...] = _bf(h2)
    route_ref[...] = route
    for blk in range(cnt_ref.shape[0]):
        cnt_ref[blk] = _block_counts(route[blk * TOKEN_BLOCK:(blk + 1) * TOKEN_BLOCK, :])


def _mixer_sample(x_tm, mod, hist_tm, h0r, h0i, wts, alpha):
    steps, nb, d = x_tm.shape
    n = steps * nb
    dc = wts["w_pw"].shape[0]
    nstate = h0r.shape[1]
    names = ["w_in", "b_in", "w_dw", "b_dw", "cln_g", "cln_b", "w_pw", "b_pw", "bre", "bim", "cc",
             "lrow_r", "lrow_i", "dskip", "w_sv", "b_sv", "w_sg", "b_sg", "w_out", "b_out",
             "ln1_g", "ln1_b", "wr_hi", "wr_lo", "b_r"]
    args = [x_tm, mod, hist_tm, h0r, h0i] + [wts[k] for k in names]
    return pl.pallas_call(
        functools.partial(_mixer_sample_kernel, alpha=alpha),
        grid=(1,),
        in_specs=[_const_spec(a.shape) for a in args],
        out_specs=[_const_spec((n, d)), _const_spec((n, d)), _const_spec((n, LANES)),
                   _const_spec((n // TOKEN_BLOCK, 1, LANES)),
                   _const_spec((CONV_WIDTH - 1, nb, dc)),
                   _const_spec((nb, nstate)), _const_spec((nb, nstate))],
        out_shape=[jax.ShapeDtypeStruct((n, d), _F32), jax.ShapeDtypeStruct((n, d), _BF16),
                   jax.ShapeDtypeStruct((n, LANES), _F32),
                   jax.ShapeDtypeStruct((n // TOKEN_BLOCK, 1, LANES), _F32),
                   jax.ShapeDtypeStruct((CONV_WIDTH - 1, nb, dc), _F32),
                   jax.ShapeDtypeStruct((nb, nstate), _F32), jax.ShapeDtypeStruct((nb, nstate), _F32)],
        compiler_params=_params(("arbitrary",)),
        name="mixer_sample",
    )(*args)


def _round_up(x, m):
    return (x + m - 1) // m * m


def _plan_sizes(n_tokens):
    nb = n_tokens // TOKEN_BLOCK
    r_max = _round_up(TOP_K * TOKEN_BLOCK + N_EXPERTS * (SEG_PAD - 1), EXPERT_TILE)
    rows_max = nb * (TOP_K * TOKEN_BLOCK + N_EXPERTS * (SEG_PAD - 1)) + N_EXPERTS * (EXPERT_TILE - 1)
    nt_max = -(-rows_max // EXPERT_TILE)
    return nb, r_max, nt_max


def _routing_plan(cnt, r_max, nt_max):
    nb, ne = cnt.shape
    seg = _round_up(cnt, SEG_PAD)
    loc_end = jnp.cumsum(seg, axis=1)
    loc = loc_end - seg
    used = loc_end[:, -1]
    tot = jnp.sum(seg, axis=0)
    totpad = _round_up(tot, EXPERT_TILE)
    eend = jnp.cumsum(totpad)
    estart = eend - totpad
    goff = estart[None, :] + jnp.cumsum(seg, axis=0) - seg
    n_tiles = (eend[-1] // EXPERT_TILE).astype(jnp.int32)
    tiles = jnp.minimum(jnp.arange(nt_max, dtype=jnp.int32), n_tiles - 1)
    te = jnp.sum(((eend // EXPERT_TILE)[None, :] <= tiles[:, None]).astype(jnp.int32), axis=1)
    te = jnp.minimum(te, ne - 1)
    return dict(seg=seg, loc=loc, goff=goff, n_tiles=n_tiles.reshape(1), te=te,
                npad=(totpad - tot).astype(jnp.int32), pad_base=(estart + tot).astype(jnp.int32))


def _dest_columns(route, loc_row):
    n = route.shape[0]
    lane_i = lax.broadcasted_iota(jnp.int32, (n, LANES), 1)
    lane = lane_i.astype(_F32)
    member = jnp.where((route > 0.0) & (lane_i < N_EXPERTS), 1.0, 0.0).astype(_BF16)
    strict = jnp.where(lax.broadcasted_iota(jnp.int32, (n, n), 1)
                       < lax.broadcasted_iota(jnp.int32, (n, n), 0), 1.0, 0.0).astype(_BF16)
    dest = _dot(strict, member) + loc_row
    d4 = jnp.zeros((n, LANES), _F32)
    for k in range(TOP_K):
        idx = route[:, ROUTE_IDX + k:ROUTE_IDX + k + 1]
        dk = jnp.sum(jnp.where(lane == idx, dest, 0.0), axis=-1, keepdims=True)
        d4 = d4 + jnp.where(lane_i == k, dk, 0.0)
    return d4


def _dispatch_kernel(seg, sloc, goff, npad, pad_base, hp_ref, hs_ref, rp_ref, rs_ref, loc_ref,
                     xs_hbm, d4_ref, buf, zbuf, sem, zsem, *, nbp):
    i = pl.program_id(0)
    nblk = pl.num_programs(0)
    slot = lax.rem(i, 2)
    tb = hp_ref.shape[0]
    r_max = buf.shape[1]
    is_p = i < nbp
    h = jnp.where(is_p, hp_ref[...], hs_ref[...])
    route = jnp.where(is_p, rp_ref[...], rs_ref[...])
    d4 = _dest_columns(route, loc_ref[0])
    d4_ref[...] = d4
    d4t = d4.T
    row = lax.broadcasted_iota(jnp.int32, (r_max, tb), 0).astype(_F32)
    p = jnp.zeros((r_max, tb), _F32)
    for k in range(TOP_K):
        p = p + jnp.where(row == d4t[k:k + 1, :], 1.0, 0.0)
    buf[slot] = _bf(_dot(_bf(p), h))

    def seg_copy(blk, s, e):
        n = pl.multiple_of(seg[blk * N_EXPERTS + e], SEG_PAD)
        src = pl.multiple_of(sloc[blk * N_EXPERTS + e], SEG_PAD)
        dst = pl.multiple_of(goff[blk * N_EXPERTS + e], SEG_PAD)
        return pltpu.make_async_copy(buf.at[s, pl.ds(src, n), :], xs_hbm.at[pl.ds(dst, n), :], sem.at[s])

    def pad_copy(e):
        n = pl.multiple_of(npad[e], SEG_PAD)
        dst = pl.multiple_of(pad_base[e], SEG_PAD)
        return pltpu.make_async_copy(zbuf.at[pl.ds(0, n), :], xs_hbm.at[pl.ds(dst, n), :], zsem.at[0])

    def for_segments(blk, fn):
        for e in range(N_EXPERTS):
            @pl.when(seg[blk * N_EXPERTS + e] > 0)
            def _(e=e):
                fn(e)

    for_segments(i, lambda e: seg_copy(i, slot, e).start())

    @pl.when(i > 0)
    def _():
        for_segments(i - 1, lambda e: seg_copy(i - 1, 1 - slot, e).wait())

    @pl.when(i == nblk - 1)
    def _():
        zbuf[...] = jnp.zeros(zbuf.shape, zbuf.dtype)
        for e in range(N_EXPERTS):
            @pl.when(npad[e] > 0)
            def _(e=e):
                pad_copy(e).start()
        for e in range(N_EXPERTS):
            @pl.when(npad[e] > 0)
            def _(e=e):
                pad_copy(e).wait()
        for_segments(i, lambda e: seg_copy(i, slot, e).wait())


def _dispatch(plan, h2p, h2s, rp, rs, r_max, nt_max):
    tb = TOKEN_BLOCK
    d = h2p.shape[1]
    nbp, nbs = h2p.shape[0] // tb, h2s.shape[0] // tb
    nb = nbp + nbs
    loc = jnp.pad(plan["loc"].astype(_F32), ((0, 0), (0, LANES - N_EXPERTS))).reshape(nb, 1, LANES)
    pidx = lambda i, *_: (jnp.minimum(i, nbp - 1), 0)
    sidx = lambda i, *_: (jnp.maximum(i - nbp, 0), 0)
    grid_spec = pltpu.PrefetchScalarGridSpec(
        num_scalar_prefetch=5,
        grid=(nb,),
        in_specs=[pl.BlockSpec((tb, d), pidx), pl.BlockSpec((tb, d), sidx),
                  pl.BlockSpec((tb, LANES), pidx), pl.BlockSpec((tb, LANES), sidx),
                  pl.BlockSpec((1, 1, LANES), lambda i, *_: (i, 0, 0))],
        out_specs=[pl.BlockSpec(memory_space=pl.ANY), pl.BlockSpec((tb, LANES), lambda i, *_: (i, 0))],
        scratch_shapes=[pltpu.VMEM((2, r_max, d), _BF16), pltpu.VMEM((EXPERT_TILE, d), _BF16),
                        pltpu.SemaphoreType.DMA((2,)), pltpu.SemaphoreType.DMA((1,))])
    flat = lambda a: a.astype(jnp.int32).reshape(-1)
    return pl.pallas_call(
        functools.partial(_dispatch_kernel, nbp=nbp),
        grid_spec=grid_spec,
        out_shape=[jax.ShapeDtypeStruct((nt_max * EXPERT_TILE, d), _BF16),
                   jax.ShapeDtypeStruct((nb * tb, LANES), _F32)],
        compiler_params=_params(("arbitrary",)),
        name="dispatch",
    )(flat(plan["seg"]), flat(plan["loc"]), flat(plan["goff"]), plan["npad"], plan["pad_base"],
      h2p, h2s, rp, rs, loc)


def _expert_kernel(te, n_tiles, x_ref, w1_ref, b1_ref, w2_ref, b2_ref, y_ref, w1b, w2b):
    i = pl.program_id(0)
    dff = w2_ref.shape[1]

    @pl.when(i < n_tiles[0])
    def _():
        prev = te[jnp.maximum(i - 1, 0)]

        @pl.when((i == 0) | (te[i] != prev))
        def _():
            w1b[...] = _bf(w1_ref[0])
            w2b[...] = _bf(w2_ref[0])

        gu = _dot(x_ref[...], w1b[...]) + b1_ref[0]
        g = jnp.minimum(gu[:, :dff], SWIGLU_LIMIT)
        up = jnp.clip(gu[:, dff:], -SWIGLU_LIMIT, SWIGLU_LIMIT)
        act = g * _sigmoid(SWIGLU_ALPHA * g) * (up + 1.0)
        y_ref[...] = _bf(_dot(_bf(act), w2b[...]) + b2_ref[0])


def _experts(plan, xs, w1, b1, w2, b2, nt_max):
    ne, d, dff2 = w1.shape
    dff = w2.shape[1]
    tm = EXPERT_TILE
    tile = lambda i, te, nt: (jnp.minimum(i, nt[0] - 1), 0)
    wsel = lambda i, te, nt: (te[i], 0, 0)
    grid_spec = pltpu.PrefetchScalarGridSpec(
        num_scalar_prefetch=2,
        grid=(nt_max,),
        in_specs=[pl.BlockSpec((tm, d), tile),
                  pl.BlockSpec((1, d, dff2), wsel), pl.BlockSpec((1, 1, dff2), wsel),
                  pl.BlockSpec((1, dff, d), wsel), pl.BlockSpec((1, 1, d), wsel)],
        out_specs=pl.BlockSpec((tm, d), tile),
        scratch_shapes=[pltpu.VMEM((d, dff2), _BF16), pltpu.VMEM((dff, d), _BF16)])
    return pl.pallas_call(
        _expert_kernel,
        grid_spec=grid_spec,
        out_shape=jax.ShapeDtypeStruct(xs.shape, _BF16),
        compiler_params=_params(("arbitrary",)),
        name="experts",
    )(plan["te"], plan["n_tiles"], xs, w1, b1.reshape(ne, 1, dff2), w2, b2.reshape(ne, 1, d))


def _combine_kernel(seg, sloc, goff, rp_ref, rs_ref, d4_ref, x1p_ref, x1s_ref, g2p_ref, g2s_ref,
                    ln2_g, ln2_b, ys_hbm, yp_ref, ysm_ref, buf, sem, *, nbp, alpha):
    i = pl.program_id(0)
    nblk = pl.num_programs(0)
    slot = lax.rem(i, 2)
    tb = rp_ref.shape[0]
    r_max = buf.shape[1]

    def seg_copy(blk, s, e):
        n = pl.multiple_of(seg[blk * N_EXPERTS + e], SEG_PAD)
        src = pl.multiple_of(goff[blk * N_EXPERTS + e], SEG_PAD)
        dst = pl.multiple_of(sloc[blk * N_EXPERTS + e], SEG_PAD)
        return pltpu.make_async_copy(ys_hbm.at[pl.ds(src, n), :], buf.at[s, pl.ds(dst, n), :], sem.at[s])

    def for_segments(blk, fn):
        for e in range(N_EXPERTS):
            @pl.when(seg[blk * N_EXPERTS + e] > 0)
            def _(e=e):
                fn(e)

    @pl.when(i == 0)
    def _():
        buf[...] = jnp.zeros(buf.shape, buf.dtype)
        for_segments(0, lambda e: seg_copy(0, 0, e).start())

    @pl.when(i + 1 < nblk)
    def _():
        for_segments(i + 1, lambda e: seg_copy(i + 1, 1 - slot, e).start())

    for_segments(i, lambda e: seg_copy(i, slot, e).wait())

    is_p = i < nbp
    route = jnp.where(is_p, rp_ref[...], rs_ref[...])
    d4 = d4_ref[...]
    col = lax.broadcasted_iota(jnp.int32, (tb, r_max), 1).astype(_F32)
    pg = jnp.zeros((tb, r_max), _F32)
    for k in range(TOP_K):
        gate = route[:, ROUTE_GATE + k:ROUTE_GATE + k + 1]
        pg = pg + jnp.where(col == d4[:, k:k + 1], gate, 0.0)
    f = _dot(_bf(pg), buf[slot])

    x1 = jnp.where(is_p, x1p_ref[...], x1s_ref[...])
    g2s = jnp.concatenate([g2s_ref[...]] * (tb // g2s_ref.shape[0]), axis=0)
    g2 = jnp.where(is_p, jnp.broadcast_to(g2p_ref[0], g2s.shape), g2s)
    y = _layer_norm(alpha * x1 + g2 * f, ln2_g[...], ln2_b[...])

    @pl.when(is_p)
    def _():
        yp_ref[...] = y

    @pl.when(jnp.logical_not(is_p))
    def _():
        ysm_ref[...] = y


def _combine(plan, ys, rp, rs, d4, x1p, x1s, modp, mods, ln2_g, ln2_b, r_max, alpha, blocks_per_seq):
    tb = TOKEN_BLOCK
    d = x1p.shape[1]
    nbp, nbs = x1p.shape[0] // tb, x1s.shape[0] // tb
    nb = nbp + nbs
    pidx = lambda i, *_: (jnp.minimum(i, nbp - 1), 0)
    sidx = lambda i, *_: (jnp.maximum(i - nbp, 0), 0)
    g2_lane_block = 5
    flat = lambda a: a.astype(jnp.int32).reshape(-1)
    grid_spec = pltpu.PrefetchScalarGridSpec(
        num_scalar_prefetch=3,
        grid=(nb,),
        in_specs=[pl.BlockSpec((tb, LANES), pidx), pl.BlockSpec((tb, LANES), sidx),
                  pl.BlockSpec((tb, LANES), lambda i, *_: (i, 0)),
                  pl.BlockSpec((tb, d), pidx), pl.BlockSpec((tb, d), sidx),
                  pl.BlockSpec((1, 1, d), lambda i, *_: (jnp.minimum(i, nbp - 1) // blocks_per_seq, 0,
                                                         g2_lane_block)),
                  pl.BlockSpec((mods.shape[0], d), lambda i, *_: (0, g2_lane_block)),
                  pl.BlockSpec((1, d), lambda i, *_: (0, 0)), pl.BlockSpec((1, d), lambda i, *_: (0, 0)),
                  pl.BlockSpec(memory_space=pl.ANY)],
        out_specs=[pl.BlockSpec((tb, d), pidx), pl.BlockSpec((tb, d), sidx)],
        scratch_shapes=[pltpu.VMEM((2, r_max, d), _BF16), pltpu.SemaphoreType.DMA((2,))])
    return pl.pallas_call(
        functools.partial(_combine_kernel, nbp=nbp, alpha=alpha),
        grid_spec=grid_spec,
        out_shape=[jax.ShapeDtypeStruct(x1p.shape, _F32), jax.ShapeDtypeStruct(x1s.shape, _F32)],
        compiler_params=_params(("arbitrary",)),
        name="combine",
    )(flat(plan["seg"]), flat(plan["loc"]), flat(plan["goff"]),
      rp, rs, d4, x1p, x1s, modp, mods, ln2_g, ln2_b, ys)


def _complex_powers(zr, zi, n):
    k = jnp.arange(n, dtype=jnp.int32)[None, :]
    pr = jnp.ones((zr.shape[0], n), _F32)
    pi = jnp.zeros((zr.shape[0], n), _F32)
    sr, si = zr[:, None], zi[:, None]
    bit = 1
    while bit < n:
        on = (k & bit) != 0
        mr, mi = jnp.where(on, sr, 1.0), jnp.where(on, si, 0.0)
        pr, pi = pr * mr - pi * mi, pr * mi + pi * mr
        sr, si = sr * sr - si * si, 2.0 * sr * si
        bit *= 2
    return pr, pi


def _block_diag(m):
    g, a, b = m.shape
    gb = GROUPS_PER_BLOCK
    m = m.reshape(g // gb, gb, a, 1, b)
    eye = jnp.eye(gb, dtype=m.dtype).reshape(1, gb, 1, gb, 1)
    return (m * eye).reshape(g // gb, gb * a, gb * b)


def _layer_weights(p, q):
    d = p["w_in"].shape[0]
    row = lambda v: v.reshape(1, -1).astype(_F32)
    lam_re, lam_im = p["lam_re"].astype(_F32), p["lam_im"].astype(_F32)
    dt = jnp.exp(p["log_dt"].astype(_F32))[:, None]
    mag = jnp.exp(lam_re * dt)
    lbr, lbi = mag * jnp.cos(lam_im * dt), mag * jnp.sin(lam_im * dt)
    den = lam_re * lam_re + lam_im * lam_im
    nr, ni = lbr - 1.0, lbi
    fr, fi = (nr * lam_re + ni * lam_im) / den, (ni * lam_re - nr * lam_im) / den
    b_re, b_im = p["b_re"].astype(_F32), p["b_im"].astype(_F32)
    bbr = fr[..., None] * b_re - fi[..., None] * b_im
    bbi = fr[..., None] * b_im + fi[..., None] * b_re
    nstate = lam_re.size
    mod2 = lbr * lbr + lbi * lbi
    flat = lambda v: v.reshape(nstate)
    epr, epi = _complex_powers(flat(lbr), flat(lbi), q)
    eir, eii = _complex_powers(flat(lbr / mod2), flat(-lbi / mod2), q)
    table = lambda t: t
    bcast = lambda v: jnp.broadcast_to(v.reshape(nstate, 1), (nstate, LANES))
    c_re, c_im = p["c_re"].astype(_F32), p["c_im"].astype(_F32)
    w_r = p["w_router"].astype(_F32).T
    wr_hi = _bf(w_r)
    return dict(
        w_in=_bf(p["w_in"]), b_in=row(p["b_in"]), w_dw=p["w_dw"].astype(_F32), b_dw=row(p["b_dw"]),
        cln_g=row(p["conv_ln_g"]), cln_b=row(p["conv_ln_b"]), w_pw=_bf(p["w_pw"]), b_pw=row(p["b_pw"]),
        bre=_bf(_block_diag(bbr)), bim=_bf(_block_diag(bbi)),
        cc=_bf(jnp.concatenate([_block_diag(c_re), _block_diag(-c_im)], axis=2)),
        einv_r=table(eir), einv_i=table(eii), epow_r=table(epr), epow_i=table(epi),
        lamb_r=bcast(lbr), lamb_i=bcast(lbi), lrow_r=lbr.reshape(1, nstate), lrow_i=lbi.reshape(1, nstate),
        dskip=row(p["d_skip"]),
        w_sv=_bf(p["w_sv"]), b_sv=row(p["b_sv"]), w_sg=_bf(p["w_sg"]), b_sg=row(p["b_sg"]),
        w_out=_bf(p["w_out"]), b_out=row(p["b_out"]), ln1_g=row(p["ln1_g"]), ln1_b=row(p["ln1_b"]),
        wr_hi=wr_hi, wr_lo=_bf(w_r - wr_hi.astype(_F32)),
        b_r=p["b_router"].astype(_F32).reshape(-1, 1),
        ln2_g=row(p["ln2_g"]), ln2_b=row(p["ln2_b"]))


def _layer(xp, xs_tm, c_all, hist_tm, h0r, h0i, p, alpha):
    b, l, d = xp.shape
    steps, nbs, _ = xs_tm.shape
    tb = TOKEN_BLOCK
    assert l % tb == 0 and l >= CONV_WIDTH - 1 and (steps * nbs) % tb == 0 and tb % nbs == 0
    wts = _layer_weights(p, tb)
    mod = _ada(c_all, p["w_ada"].astype(_F32), p["b_ada"].astype(_F32))
    modp, mods = mod[:b].reshape(b, 1, -1), mod[b:]

    x1p, h2p, rp, cntp, conv_p, sre_p, sim_p = _mixer_prompt(xp, modp, wts, alpha)
    x1s, h2s, rs, cnts, conv_s, sre_s, sim_s = _mixer_sample(xs_tm, mods, hist_tm, h0r, h0i, wts, alpha)

    n_tok = b * l + steps * nbs
    nb, r_max, nt_max = _plan_sizes(n_tok)
    cnt = jnp.concatenate([cntp.reshape(-1, LANES), cnts.reshape(-1, LANES)], axis=0)[:, :N_EXPERTS]
    plan = _routing_plan(cnt.astype(jnp.int32), r_max, nt_max)

    flat = lambda a: a.reshape(b * l, a.shape[-1])
    xs_sorted, d4 = _dispatch(plan, flat(h2p), h2s, flat(rp), rs, r_max, nt_max)
    ys_sorted = _experts(plan, xs_sorted, p["w1"], p["b1"], p["w2"], p["b2"], nt_max)
    yp, ysm = _combine(plan, ys_sorted, flat(rp), rs, d4, flat(x1p), x1s, modp, mods,
                       wts["ln2_g"], wts["ln2_b"], r_max, alpha, l // tb)
    return (yp.reshape(b, l, d), ysm.reshape(steps, nbs, d), conv_p, sre_p[:, 0], sim_p[:, 0],
            conv_s, sre_s, sim_s)


def kernel(x_prompt, x_sample, state_conv, state_ssm_re, state_ssm_im, c_prompt, c_sample, w_ada, b_ada, w_in, b_in, w_dw, b_dw, conv_ln_g, conv_ln_b, w_pw, b_pw, lam_re, lam_im, log_dt, b_re, b_im, c_re, c_im, d_skip, w_sv, b_sv, w_sg, b_sg, w_out, b_out, ln1_g, ln1_b, w_router, b_router, w1, b1, w2, b2, ln2_g, ln2_b):
    stacked = dict(w_ada=w_ada, b_ada=b_ada, w_in=w_in, b_in=b_in, w_dw=w_dw, b_dw=b_dw,
                   conv_ln_g=conv_ln_g, conv_ln_b=conv_ln_b, w_pw=w_pw, b_pw=b_pw, lam_re=lam_re,
                   lam_im=lam_im, log_dt=log_dt, b_re=b_re, b_im=b_im, c_re=c_re, c_im=c_im,
                   d_skip=d_skip, w_sv=w_sv, b_sv=b_sv, w_sg=w_sg, b_sg=b_sg, w_out=w_out, b_out=b_out,
                   ln1_g=ln1_g, ln1_b=ln1_b, w_router=w_router, b_router=b_router, w1=w1, b1=b1,
                   w2=w2, b2=b2, ln2_g=ln2_g, ln2_b=ln2_b)
    depth = w_ada.shape[0]
    alpha = (2 * depth) ** 0.25
    b = x_prompt.shape[0]
    nbs = x_sample.shape[0]
    g, s = state_ssm_re.shape[2], state_ssm_re.shape[3]
    xp = x_prompt
    xs_tm = jnp.transpose(x_sample, (1, 0, 2))
    c_all = jnp.concatenate([c_prompt, c_sample], axis=0)
    conv_ps, re_ps, im_ps, conv_ss, re_ss, im_ss = [], [], [], [], [], []
    for layer in range(depth):
        p = {k: v[layer] for k, v in stacked.items()}
        hist_tm = jnp.transpose(state_conv[layer], (1, 0, 2))
        h0r = state_ssm_re[layer].reshape(nbs, g * s)
        h0i = state_ssm_im[layer].reshape(nbs, g * s)
        xp, xs_tm, conv_p, sre_p, sim_p, conv_s, sre_s, sim_s = _layer(
            xp, xs_tm, c_all, hist_tm, h0r, h0i, p, alpha)
        conv_ps.append(conv_p.astype(state_conv.dtype))
        re_ps.append(sre_p.reshape(b, g, s))
        im_ps.append(sim_p.reshape(b, g, s))
        conv_ss.append(jnp.transpose(conv_s, (1, 0, 2)).astype(state_conv.dtype))
        re_ss.append(sre_s.reshape(nbs, g, s))
        im_ss.append(sim_s.reshape(nbs, g, s))
    return (xp, jnp.transpose(xs_tm, (1, 0, 2)), jnp.stack(conv_ps), jnp.stack(re_ps), jnp.stack(im_ps),
            jnp.stack(conv_ss), jnp.stack(re_ss), jnp.stack(im_ss))
```

```python
import functools

import jax
import jax.numpy as jnp
from jax import lax
from jax.experimental import pallas as pl
from jax.experimental.pallas import tpu as pltpu

CONV_WIDTH = 31
SSM_GROUP = 16
SSM_STATE = 64
N_EXPERTS = 32
TOP_K = 4
SWIGLU_LIMIT = 7.0
SWIGLU_ALPHA = 1.702
LN_EPS = 1e-5

LANES = 128
SUBLANES = 8
TOKEN_BLOCK = 256
EXPERT_TILE = 512
SEG_PAD = 16
GROUPS_PER_BLOCK = LANES // SSM_GROUP
CONV_HIST = 32
CONV_ROWS = 32
ROUTE_IDX = N_EXPERTS
ROUTE_GATE = N_EXPERTS + TOP_K
VMEM_LIMIT = 56 * 1024 * 1024

_F32 = jnp.float32
_BF16 = jnp.bfloat16


def _bf(x):
    return x.astype(_BF16)


def _dot(a, b):
    return jnp.dot(a, b, preferred_element_type=_F32)


def _dot_nt(a, b):
    return lax.dot_general(a, b, (((1,), (1,)), ((), ())), preferred_element_type=_F32)


def _split(x):
    hi = _bf(x)
    lo = _bf(x - hi.astype(_F32))
    return hi, lo


def _sigmoid(x):
    return 1.0 / (1.0 + jnp.exp(-x))


def _gelu_tanh(x):
    return 0.5 * x * (1.0 + jnp.tanh(0.7978845608028654 * (x + 0.044715 * (x * x * x))))


def _layer_norm(x, g, b):
    mu = jnp.mean(x, axis=-1, keepdims=True)
    xc = x - mu
    var = jnp.mean(xc * xc, axis=-1, keepdims=True)
    return xc * lax.rsqrt(var + LN_EPS) * g + b


def _const_spec(shape):
    nd = len(shape)
    return pl.BlockSpec(shape, lambda *_: (0,) * nd)


def _params(sem):
    return pltpu.CompilerParams(dimension_semantics=sem, vmem_limit_bytes=VMEM_LIMIT)


def _ada_kernel(c_ref, w_ref, b_ref, o_ref):
    c = c_ref[...]
    s_hi, s_lo = _split(c * _sigmoid(c))
    w_hi, w_lo = _split(w_ref[...])
    o_ref[...] = _dot(s_hi, w_hi) + _dot(s_lo, w_hi) + _dot(s_hi, w_lo) + b_ref[...]


def _ada(c, w, b):
    n, d = c.shape
    cols = w.shape[1]
    tn = d
    return pl.pallas_call(
        _ada_kernel,
        grid=(cols // tn,),
        in_specs=[pl.BlockSpec((n, d), lambda j: (0, 0)),
                  pl.BlockSpec((d, tn), lambda j: (0, j)),
                  pl.BlockSpec((1, tn), lambda j: (0, j))],
        out_specs=pl.BlockSpec((n, tn), lambda j: (0, j)),
        out_shape=jax.ShapeDtypeStruct((n, cols), _F32),
        compiler_params=_params(("arbitrary",)),
        name="ada",
    )(c, w, b.reshape(1, cols))


def _tiled(fn, tile, *args, vecs=()):
    n, m = args[0].shape
    tr, tc = min(tile[0], n), min(tile[1], m)
    rows_out = None
    for r0 in range(0, n, tr):
        cols_out = None
        for c0 in range(0, m, tc):
            res = fn(*[a[r0:r0 + tr, c0:c0 + tc] for a in args], *[v[:, c0:c0 + tc] for v in vecs])
            res = res if isinstance(res, tuple) else (res,)
            if cols_out is None:
                cols_out = [[] for _ in res]
            for acc, v in zip(cols_out, res):
                acc.append(v)
        row_vals = [c[0] if len(c) == 1 else jnp.concatenate(c, axis=1) for c in cols_out]
        if rows_out is None:
            rows_out = [[] for _ in row_vals]
        for acc, v in zip(rows_out, row_vals):
            acc.append(v)
    outs = [r[0] if len(r) == 1 else jnp.concatenate(r, axis=0) for r in rows_out]
    return outs[0] if len(outs) == 1 else tuple(outs)


def _in_proj(hb, w_in, b_in, lo, hi):
    return _dot(hb, w_in[:, lo:hi]) + b_in[:, lo:hi]


def _conv_tail(v, cln_g, cln_b, w_pw, b_pw):
    g, b = cln_g[...], cln_b[...]

    def norm_swish(vb):
        vb = _layer_norm(vb, g, b)
        return vb * _sigmoid(vb)
    v = _tiled(norm_swish, (SUBLANES, v.shape[1]), v)
    return _dot(_bf(v), w_pw[...]) + b_pw[...]


def _ssm_tail(y, zs, dskip, w_sv, b_sv, w_sg, b_sg):
    yg = _tiled(lambda yb, zb, db: _bf(_gelu_tanh(yb + db * zb)), (2 * SUBLANES, 4 * LANES), y, zs, vecs=(dskip,))
    sv, sg = _dot(yg, w_sv[...]), _dot(yg, w_sg[...])
    return _tiled(lambda a, b, bv, bg: (a + bv) * _sigmoid(b + bg), (SUBLANES, 4 * LANES),
                  sv, sg, vecs=(b_sv, b_sg))


def _route(h2, wr_hi, wr_lo, b_r):
    n = h2.shape[0]
    ne = b_r.shape[0]
    h_hi, h_lo = _split(h2)
    logits = (_dot(h_hi, wr_hi[...]) + _dot(h_lo, wr_hi[...]) + _dot(h_hi, wr_lo[...])).T[:ne, :] + b_r[...]
    row = lax.broadcasted_iota(jnp.int32, (ne, n), 0).astype(_F32)
    neg = jnp.float32(-jnp.inf)
    cur = logits
    vals, idxs = [], []
    for _ in range(TOP_K):
        m = jnp.max(cur, axis=0, keepdims=True)
        idx = jnp.min(jnp.where(cur == m, row, float(ne)), axis=0, keepdims=True)
        vals.append(m)
        idxs.append(idx)
        cur = jnp.where(row == idx, neg, cur)
    es = [jnp.exp(v - vals[0]) for v in vals]
    tot = es[0]
    for e in es[1:]:
        tot = tot + e
    inv = 1.0 / tot
    comb = jnp.zeros((ne, n), _F32)
    row8 = lax.broadcasted_iota(jnp.int32, (2 * TOP_K, n), 0)
    extra = jnp.zeros((2 * TOP_K, n), _F32)
    for k in range(TOP_K):
        gate = es[k] * inv
        comb = comb + jnp.where(row == idxs[k], gate, 0.0)
        extra = extra + jnp.where(row8 == k, idxs[k], 0.0) + jnp.where(row8 == TOP_K + k, gate, 0.0)
    rest = jnp.zeros((LANES - ne - 2 * TOP_K, n), _F32)
    return jnp.concatenate([comb, extra, rest], axis=0).T


def _block_counts(route):
    lane = lax.broadcasted_iota(jnp.int32, route.shape, 1)
    member = jnp.where((route > 0.0) & (lane < N_EXPERTS), 1.0, 0.0)
    return jnp.sum(member, axis=0, keepdims=True)


def _mixer_tail(x, zg, conv_out, ssm_out, g1, sh2, sc2, alpha, w_out, b_out,
                ln1_g, ln1_b, wr_hi, wr_lo, b_r):
    d = x.shape[1]
    merged = _tiled(lambda gc, gs, co, so: _bf(_sigmoid(gc) * co + _sigmoid(gs) * so),
                    (2 * SUBLANES, 4 * LANES), zg[:, :d], zg[:, d:], conv_out, ssm_out)
    m = _dot(merged, w_out[...])

    def norm_mod(xb, mb, gb, scb, shb, bo, lg, lb):
        x1b = _layer_norm(alpha * xb + gb * (mb + bo), lg, lb)
        return x1b, x1b * (1.0 + scb) + shb
    mods = (g1, sc2, sh2)
    per_seq = g1.shape[0] == 1
    x1, h2 = _tiled(norm_mod, (SUBLANES, d), x, m, *(() if per_seq else mods),
                    vecs=(mods if per_seq else ()) + (b_out, ln1_g, ln1_b))
    return x1, h2, _route(h2, wr_hi, wr_lo, b_r)


def _mixer_prompt_kernel(x_ref, mod_ref, w_in, b_in, w_dw, b_dw, cln_g, cln_b, w_pw, b_pw,
                         bre, bim, cc, einv_r, einv_i, epow_r, epow_i, lamb_r, lamb_i, dskip,
                         w_sv, b_sv, w_sg, b_sg, w_out, b_out, ln1_g, ln1_b, wr_hi, wr_lo, b_r,
                         x1_ref, h2_ref, route_ref, cnt_ref, cstate_ref, sre_ref, sim_ref,
                         ubuf, ush, car_r, car_i, *, alpha):
    c = pl.program_id(1)
    last = pl.num_programs(1) - 1
    tl, d = x_ref.shape[1], x_ref.shape[2]
    dc = w_pw.shape[0]
    dsm = dskip.shape[1]
    nj = bre.shape[0]
    rows = bre.shape[1]

    @pl.when(c == 0)
    def _():
        ubuf[0:CONV_HIST, :] = jnp.zeros((CONV_HIST, dc), _F32)
        car_r[...] = jnp.zeros(car_r.shape, _F32)
        car_i[...] = jnp.zeros(car_i.shape, _F32)

    x = x_ref[0]
    mod = mod_ref[0]
    sh1, sc1, g1 = mod[:, 0:d], mod[:, d:2 * d], mod[:, 2 * d:3 * d]
    sh2, sc2 = mod[:, 3 * d:4 * d], mod[:, 4 * d:5 * d]
    hb = _bf(x * (1.0 + sc1) + sh1)

    u = _in_proj(hb, w_in, b_in, 0, dc) * _sigmoid(_in_proj(hb, w_in, b_in, dc, 2 * dc))
    ubuf[CONV_HIST:CONV_HIST + tl, :] = u
    span = ush.shape[1]
    for r in range(1, SUBLANES):
        ush[r - 1] = ubuf[r:r + span, :]

    n_blocks = tl // CONV_ROWS
    gate_lo = 2 * dc + dsm
    gate_w = 2 * d // n_blocks
    vblocks, zg_cols = [], []
    for blk in range(n_blocks):
        r0 = blk * CONV_ROWS
        acc = jnp.broadcast_to(b_dw[...], (CONV_ROWS, dc))
        for k in range(CONV_WIDTH):
            q, r = divmod(CONV_HIST - (CONV_WIDTH - 1) + k, SUBLANES)
            rows_k = slice(r0 + q * SUBLANES, r0 + q * SUBLANES + CONV_ROWS)
            tap = ubuf[rows_k, :] if r == 0 else ush[r - 1, rows_k, :]
            acc = acc + w_dw[k:k + 1, :] * tap
        vblocks.append(acc)
        zg_cols.append(_in_proj(hb, w_in, b_in, gate_lo + blk * gate_w, gate_lo + (blk + 1) * gate_w))
    ubuf[0:CONV_HIST, :] = ubuf[tl:tl + CONV_HIST, :]
    conv_out = _conv_tail(jnp.concatenate(vblocks, axis=0), cln_g, cln_b, w_pw, b_pw)

    zs = _in_proj(hb, w_in, b_in, 2 * dc, 2 * dc + dsm)
    zst = _bf(zs.T)
    tri = jnp.where(lax.broadcasted_iota(jnp.int32, (tl, tl), 0)
                    <= lax.broadcasted_iota(jnp.int32, (tl, tl), 1), 1.0, 0.0).astype(_BF16)
    def project(j):
        zj = zst[j * LANES:(j + 1) * LANES, :]
        return _dot(bre[j], zj), _dot(bim[j], zj)

    def scale_in(j, bu):
        r = slice(j * rows, (j + 1) * rows)
        eir, eii = einv_r[r, :], einv_i[r, :]
        return _bf(bu[0] * eir - bu[1] * eii), _bf(bu[0] * eii + bu[1] * eir)

    def prefix(v):
        return _dot(v[0], tri), _dot(v[1], tri)

    def scale_out(j, cum):
        r = slice(j * rows, (j + 1) * rows)
        cr = jnp.broadcast_to(car_r[r, LANES - 1:LANES], (rows, LANES))
        ci = jnp.broadcast_to(car_i[r, LANES - 1:LANES], (rows, LANES))
        lr, li = lamb_r[r, :], lamb_i[r, :]
        cum_r = cum[0] + jnp.concatenate([lr * cr - li * ci] * (tl // LANES), axis=1)
        cum_i = cum[1] + jnp.concatenate([lr * ci + li * cr] * (tl // LANES), axis=1)
        epr, epi = epow_r[r, :], epow_i[r, :]
        h_r = cum_r * epr - cum_i * epi
        h_i = cum_r * epi + cum_i * epr
        car_r[r, :] = h_r[:, tl - LANES:tl]
        car_i[r, :] = h_i[:, tl - LANES:tl]
        return jnp.concatenate([_bf(h_r), _bf(h_i)], axis=0)

    bu, v, cum, hcat, yts = {}, {}, {}, {}, {}
    for t in range(nj + 4):
        if t < nj:
            bu[t] = project(t)
        if 0 <= t - 2 < nj:
            cum[t - 2] = prefix(v.pop(t - 2))
        if 0 <= t - 4 < nj:
            yts[t - 4] = _dot(cc[t - 4], hcat.pop(t - 4))
        if 0 <= t - 1 < nj:
            v[t - 1] = scale_in(t - 1, bu.pop(t - 1))
        if 0 <= t - 3 < nj:
            hcat[t - 3] = scale_out(t - 3, cum.pop(t - 3))
    ssm_out = _ssm_tail(jnp.concatenate([yts[j] for j in range(nj)], axis=0).T, zs, dskip,
                        w_sv, b_sv, w_sg, b_sg)

    x1, h2, route = _mixer_tail(x, jnp.concatenate(zg_cols, axis=1), conv_out, ssm_out, g1, sh2, sc2, alpha,
                                w_out, b_out, ln1_g, ln1_b, wr_hi, wr_lo, b_r)
    x1_ref[0] = x1
    h2_ref[0] = _bf(h2)
    route_ref[0] = route
    cnt_ref[0, 0] = _block_counts(route)

    @pl.when(c == last)
    def _():
        cstate_ref[0] = ubuf[CONV_HIST + tl - (CONV_WIDTH - 1):CONV_HIST + tl, :]
        sre_ref[0] = car_r[...].T[LANES - 1:LANES, :]
        sim_ref[0] = car_i[...].T[LANES - 1:LANES, :]


def _mixer_prompt(x, mod, wts, alpha):
    b, l, d = x.shape
    tl = TOKEN_BLOCK
    nc = l // tl
    dc = wts["w_pw"].shape[0]
    nstate = wts["lamb_r"].shape[0]
    names = ["w_in", "b_in", "w_dw", "b_dw", "cln_g", "cln_b", "w_pw", "b_pw", "bre", "bim", "cc",
             "einv_r", "einv_i", "epow_r", "epow_i", "lamb_r", "lamb_i", "dskip",
             "w_sv", "b_sv", "w_sg", "b_sg", "w_out", "b_out", "ln1_g", "ln1_b",
             "wr_hi", "wr_lo", "b_r"]
    consts = [wts[n] for n in names]
    tok = lambda bi, ci: (bi, ci, 0)
    seq = lambda bi, ci: (bi, 0, 0)
    return pl.pallas_call(
        functools.partial(_mixer_prompt_kernel, alpha=alpha),
        grid=(b, nc),
        in_specs=[pl.BlockSpec((1, tl, d), tok), pl.BlockSpec((1, 1, mod.shape[-1]), seq)]
                 + [_const_spec(a.shape) for a in consts],
        out_specs=[pl.BlockSpec((1, tl, d), tok), pl.BlockSpec((1, tl, d), tok),
                   pl.BlockSpec((1, tl, LANES), tok),
                   pl.BlockSpec((1, 1, 1, LANES), lambda bi, ci: (bi, ci, 0, 0)),
                   pl.BlockSpec((1, CONV_WIDTH - 1, dc), seq),
                   pl.BlockSpec((1, 1, nstate), seq), pl.BlockSpec((1, 1, nstate), seq)],
        out_shape=[jax.ShapeDtypeStruct((b, l, d), _F32), jax.ShapeDtypeStruct((b, l, d), _BF16),
                   jax.ShapeDtypeStruct((b, l, LANES), _F32),
                   jax.ShapeDtypeStruct((b, nc, 1, LANES), _F32),
                   jax.ShapeDtypeStruct((b, CONV_WIDTH - 1, dc), _F32),
                   jax.ShapeDtypeStruct((b, 1, nstate), _F32),
                   jax.ShapeDtypeStruct((b, 1, nstate), _F32)],
        scratch_shapes=[pltpu.VMEM((CONV_HIST + tl, dc), _F32),
                        pltpu.VMEM((SUBLANES - 1, tl + CONV_HIST - SUBLANES, dc), _F32),
                        pltpu.VMEM((nstate, LANES), _F32), pltpu.VMEM((nstate, LANES), _F32)],
        compiler_params=_params(("arbitrary", "arbitrary")),
        name="mixer_prompt",
    )(x, mod, *consts)


def _mixer_sample_kernel(x_ref, mod_ref, hist_ref, h0r_ref, h0i_ref,
                         w_in, b_in, w_dw, b_dw, cln_g, cln_b, w_pw, b_pw,
                         bre, bim, cc, lrow_r, lrow_i, dskip,
                         w_sv, b_sv, w_sg, b_sg, w_out, b_out, ln1_g, ln1_b, wr_hi, wr_lo, b_r,
                         x1_ref, h2_ref, route_ref, cnt_ref, cstate_ref, sre_ref, sim_ref, *, alpha):
    steps, nb, d = x_ref.shape
    dc = w_pw.shape[0]
    dsm = dskip.shape[1]
    nj = bre.shape[0]
    rows = bre.shape[1]
    nhist = CONV_WIDTH - 1
    x = x_ref[...].reshape(steps * nb, d)
    mod = jnp.concatenate([mod_ref[...]] * steps, axis=0)
    sh1, sc1, g1 = mod[:, 0:d], mod[:, d:2 * d], mod[:, 2 * d:3 * d]
    sh2, sc2 = mod[:, 3 * d:4 * d], mod[:, 4 * d:5 * d]
    hb = _bf(x * (1.0 + sc1) + sh1)

    u = _in_proj(hb, w_in, b_in, 0, dc) * _sigmoid(_in_proj(hb, w_in, b_in, dc, 2 * dc))
    us = [u[t * nb:(t + 1) * nb, :] for t in range(steps)]
    vs = []
    for t in range(steps):
        acc = jnp.broadcast_to(b_dw[...], (nb, dc))
        for i in range(t, nhist):
            acc = acc + w_dw[i - t:i - t + 1, :] * hist_ref[i]
        for s in range(t + 1):
            acc = acc + w_dw[nhist + s - t:nhist + s - t + 1, :] * us[s]
        vs.append(acc)
    for i in range(nhist):
        src = i + steps
        cstate_ref[i] = hist_ref[src] if src < nhist else us[src - nhist]
    conv_out = _conv_tail(jnp.concatenate(vs, axis=0), cln_g, cln_b, w_pw, b_pw)

    zs = _in_proj(hb, w_in, b_in, 2 * dc, 2 * dc + dsm)
    zsb = _bf(zs)
    h_r, h_i = h0r_ref[...], h0i_ref[...]
    lr, li = lrow_r[...], lrow_i[...]
    ys = []
    for t in range(steps):
        zt = zsb[t * nb:(t + 1) * nb, :]
        bu_r = jnp.concatenate([_dot_nt(zt[:, j * LANES:(j + 1) * LANES], bre[j]) for j in range(nj)], axis=1)
        bu_i = jnp.concatenate([_dot_nt(zt[:, j * LANES:(j + 1) * LANES], bim[j]) for j in range(nj)], axis=1)
        h_r, h_i = lr * h_r - li * h_i + bu_r, lr * h_i + li * h_r + bu_i
        hrb, hib = _bf(h_r), _bf(h_i)
        ys.append(jnp.concatenate(
            [_dot_nt(jnp.concatenate([hrb[:, j * rows:(j + 1) * rows], hib[:, j * rows:(j + 1) * rows]], axis=1),
                     cc[j]) for j in range(nj)], axis=1))
    sre_ref[...] = h_r
    sim_ref[...] = h_i
    ssm_out = _ssm_tail(jnp.concatenate(ys, axis=0), zs, dskip, w_sv, b_sv, w_sg, b_sg)

    zg = _in_proj(hb, w_in, b_in, 2 * dc + dsm, 2 * dc + dsm + 2 * d)
    x1, h2, route = _mixer_tail(x, zg, conv_out, ssm_out, g1, sh2, sc2, alpha,
                                w_out, b_out, ln1_g, ln1_b, wr_hi, wr_lo, b_r)
    x1_ref[...] = x1
    h2_ref[...] = _bf(h2)
    route_ref[...] = route
    for blk in range(cnt_ref.shape[0]):
        cnt_ref[blk] = _block_counts(route[blk * TOKEN_BLOCK:(blk + 1) * TOKEN_BLOCK, :])


def _mixer_sample(x_tm, mod, hist_tm, h0r, h0i, wts, alpha):
    steps, nb, d = x_tm.shape
    n = steps * nb
    dc = wts["w_pw"].shape[0]
    nstate = h0r.shape[1]
    names = ["w_in", "b_in", "w_dw", "b_dw", "cln_g", "cln_b", "w_pw", "b_pw", "bre", "bim", "cc",
             "lrow_r", "lrow_i", "dskip", "w_sv", "b_sv", "w_sg", "b_sg", "w_out", "b_out",
             "ln1_g", "ln1_b", "wr_hi", "wr_lo", "b_r"]
    args = [x_tm, mod, hist_tm, h0r, h0i] + [wts[k] for k in names]
    return pl.pallas_call(
        functools.partial(_mixer_sample_kernel, alpha=alpha),
        grid=(1,),
        in_specs=[_const_spec(a.shape) for a in args],
        out_specs=[_const_spec((n, d)), _const_spec((n, d)), _const_spec((n, LANES)),
                   _const_spec((n // TOKEN_BLOCK, 1, LANES)),
                   _const_spec((CONV_WIDTH - 1, nb, dc)),
                   _const_spec((nb, nstate)), _const_spec((nb, nstate))],
        out_shape=[jax.ShapeDtypeStruct((n, d), _F32), jax.ShapeDtypeStruct((n, d), _BF16),
                   jax.ShapeDtypeStruct((n, LANES), _F32),
                   jax.ShapeDtypeStruct((n // TOKEN_BLOCK, 1, LANES), _F32),
                   jax.ShapeDtypeStruct((CONV_WIDTH - 1, nb, dc), _F32),
                   jax.ShapeDtypeStruct((nb, nstate), _F32), jax.ShapeDtypeStruct((nb, nstate), _F32)],
        compiler_params=_params(("arbitrary",)),
        name="mixer_sample",
    )(*args)


def _round_up(x, m):
    return (x + m - 1) // m * m


def _plan_sizes(n_tokens):
    nb = n_tokens // TOKEN_BLOCK
    r_max = _round_up(TOP_K * TOKEN_BLOCK + N_EXPERTS * (SEG_PAD - 1), EXPERT_TILE)
    rows_max = nb * (TOP_K * TOKEN_BLOCK + N_EXPERTS * (SEG_PAD - 1)) + N_EXPERTS * (EXPERT_TILE - 1)
    nt_max = -(-rows_max // EXPERT_TILE)
    return nb, r_max, nt_max


def _routing_plan(cnt, r_max, nt_max):
    nb, ne = cnt.shape
    seg = _round_up(cnt, SEG_PAD)
    loc_end = jnp.cumsum(seg, axis=1)
    loc = loc_end - seg
    used = loc_end[:, -1]
    tot = jnp.sum(seg, axis=0)
    totpad = _round_up(tot, EXPERT_TILE)
    eend = jnp.cumsum(totpad)
    estart = eend - totpad
    goff = estart[None, :] + jnp.cumsum(seg, axis=0) - seg
    n_tiles = (eend[-1] // EXPERT_TILE).astype(jnp.int32)
    tiles = jnp.minimum(jnp.arange(nt_max, dtype=jnp.int32), n_tiles - 1)
    te = jnp.sum(((eend // EXPERT_TILE)[None, :] <= tiles[:, None]).astype(jnp.int32), axis=1)
    te = jnp.minimum(te, ne - 1)
    return dict(seg=seg, loc=loc, goff=goff, n_tiles=n_tiles.reshape(1), te=te,
                npad=(totpad - tot).astype(jnp.int32), pad_base=(estart + tot).astype(jnp.int32))


def _dest_columns(route, loc_row):
    n = route.shape[0]
    lane_i = lax.broadcasted_iota(jnp.int32, (n, LANES), 1)
    lane = lane_i.astype(_F32)
    member = jnp.where((route > 0.0) & (lane_i < N_EXPERTS), 1.0, 0.0).astype(_BF16)
    strict = jnp.where(lax.broadcasted_iota(jnp.int32, (n, n), 1)
                       < lax.broadcasted_iota(jnp.int32, (n, n), 0), 1.0, 0.0).astype(_BF16)
    dest = _dot(strict, member) + loc_row
    d4 = jnp.zeros((n, LANES), _F32)
    for k in range(TOP_K):
        idx = route[:, ROUTE_IDX + k:ROUTE_IDX + k + 1]
        dk = jnp.sum(jnp.where(lane == idx, dest, 0.0), axis=-1, keepdims=True)
        d4 = d4 + jnp.where(lane_i == k, dk, 0.0)
    return d4


def _dispatch_kernel(seg, sloc, goff, npad, pad_base, hp_ref, hs_ref, rp_ref, rs_ref, loc_ref,
                     xs_hbm, d4_ref, buf, zbuf, sem, zsem, *, nbp):
    i = pl.program_id(0)
    nblk = pl.num_programs(0)
    slot = lax.rem(i, 2)
    tb = hp_ref.shape[0]
    r_max = buf.shape[1]
    is_p = i < nbp
    h = jnp.where(is_p, hp_ref[...], hs_ref[...])
    route = jnp.where(is_p, rp_ref[...], rs_ref[...])
    d4 = _dest_columns(route, loc_ref[0])
    d4_ref[...] = d4
    d4t = d4.T
    row = lax.broadcasted_iota(jnp.int32, (r_max, tb), 0).astype(_F32)
    p = jnp.zeros((r_max, tb), _F32)
    for k in range(TOP_K):
        p = p + jnp.where(row == d4t[k:k + 1, :], 1.0, 0.0)
    buf[slot] = _bf(_dot(_bf(p), h))

    def seg_copy(blk, s, e):
        n = pl.multiple_of(seg[blk * N_EXPERTS + e], SEG_PAD)
        src = pl.multiple_of(sloc[blk * N_EXPERTS + e], SEG_PAD)
        dst = pl.multiple_of(goff[blk * N_EXPERTS + e], SEG_PAD)
        return pltpu.make_async_copy(buf.at[s, pl.ds(src, n), :], xs_hbm.at[pl.ds(dst, n), :], sem.at[s])

    def pad_copy(e):
        n = pl.multiple_of(npad[e], SEG_PAD)
        dst = pl.multiple_of(pad_base[e], SEG_PAD)
        return pltpu.make_async_copy(zbuf.at[pl.ds(0, n), :], xs_hbm.at[pl.ds(dst, n), :], zsem.at[0])

    def for_segments(blk, fn):
        for e in range(N_EXPERTS):
            @pl.when(seg[blk * N_EXPERTS + e] > 0)
            def _(e=e):
                fn(e)

    for_segments(i, lambda e: seg_copy(i, slot, e).start())

    @pl.when(i > 0)
    def _():
        for_segments(i - 1, lambda e: seg_copy(i - 1, 1 - slot, e).wait())

    @pl.when(i == nblk - 1)
    def _():
        zbuf[...] = jnp.zeros(zbuf.shape, zbuf.dtype)
        for e in range(N_EXPERTS):
            @pl.when(npad[e] > 0)
            def _(e=e):
                pad_copy(e).start()
        for e in range(N_EXPERTS):
            @pl.when(npad[e] > 0)
            def _(e=e):
                pad_copy(e).wait()
        for_segments(i, lambda e: seg_copy(i, slot, e).wait())


def _dispatch(plan, h2p, h2s, rp, rs, r_max, nt_max):
    tb = TOKEN_BLOCK
    d = h2p.shape[1]
    nbp, nbs = h2p.shape[0] // tb, h2s.shape[0] // tb
    nb = nbp + nbs
    loc = jnp.pad(plan["loc"].astype(_F32), ((0, 0), (0, LANES - N_EXPERTS))).reshape(nb, 1, LANES)
    pidx = lambda i, *_: (jnp.minimum(i, nbp - 1), 0)
    sidx = lambda i, *_: (jnp.maximum(i - nbp, 0), 0)
    grid_spec = pltpu.PrefetchScalarGridSpec(
        num_scalar_prefetch=5,
        grid=(nb,),
        in_specs=[pl.BlockSpec((tb, d), pidx), pl.BlockSpec((tb, d), sidx),
                  pl.BlockSpec((tb, LANES), pidx), pl.BlockSpec((tb, LANES), sidx),
                  pl.BlockSpec((1, 1, LANES), lambda i, *_: (i, 0, 0))],
        out_specs=[pl.BlockSpec(memory_space=pl.ANY), pl.BlockSpec((tb, LANES), lambda i, *_: (i, 0))],
        scratch_shapes=[pltpu.VMEM((2, r_max, d), _BF16), pltpu.VMEM((EXPERT_TILE, d), _BF16),
                        pltpu.SemaphoreType.DMA((2,)), pltpu.SemaphoreType.DMA((1,))])
    flat = lambda a: a.astype(jnp.int32).reshape(-1)
    return pl.pallas_call(
        functools.partial(_dispatch_kernel, nbp=nbp),
        grid_spec=grid_spec,
        out_shape=[jax.ShapeDtypeStruct((nt_max * EXPERT_TILE, d), _BF16),
                   jax.ShapeDtypeStruct((nb * tb, LANES), _F32)],
        compiler_params=_params(("arbitrary",)),
        name="dispatch",
    )(flat(plan["seg"]), flat(plan["loc"]), flat(plan["goff"]), plan["npad"], plan["pad_base"],
      h2p, h2s, rp, rs, loc)


def _expert_kernel(te, n_tiles, x_ref, w1_ref, b1_ref, w2_ref, b2_ref, y_ref, w1b, w2b):
    i = pl.program_id(0)
    dff = w2_ref.shape[1]

    @pl.when(i < n_tiles[0])
    def _():
        prev = te[jnp.maximum(i - 1, 0)]

        @pl.when((i == 0) | (te[i] != prev))
        def _():
            w1b[...] = _bf(w1_ref[0])
            w2b[...] = _bf(w2_ref[0])

        gu = _dot(x_ref[...], w1b[...]) + b1_ref[0]
        g = jnp.minimum(gu[:, :dff], SWIGLU_LIMIT)
        up = jnp.clip(gu[:, dff:], -SWIGLU_LIMIT, SWIGLU_LIMIT)
        act = g * _sigmoid(SWIGLU_ALPHA * g) * (up + 1.0)
        y_ref[...] = _bf(_dot(_bf(act), w2b[...]) + b2_ref[0])


def _experts(plan, xs, w1, b1, w2, b2, nt_max):
    ne, d, dff2 = w1.shape
    dff = w2.shape[1]
    tm = EXPERT_TILE
    tile = lambda i, te, nt: (jnp.minimum(i, nt[0] - 1), 0)
    wsel = lambda i, te, nt: (te[i], 0, 0)
    grid_spec = pltpu.PrefetchScalarGridSpec(
        num_scalar_prefetch=2,
        grid=(nt_max,),
        in_specs=[pl.BlockSpec((tm, d), tile),
                  pl.BlockSpec((1, d, dff2), wsel), pl.BlockSpec((1, 1, dff2), wsel),
                  pl.BlockSpec((1, dff, d), wsel), pl.BlockSpec((1, 1, d), wsel)],
        out_specs=pl.BlockSpec((tm, d), tile),
        scratch_shapes=[pltpu.VMEM((d, dff2), _BF16), pltpu.VMEM((dff, d), _BF16)])
    return pl.pallas_call(
        _expert_kernel,
        grid_spec=grid_spec,
        out_shape=jax.ShapeDtypeStruct(xs.shape, _BF16),
        compiler_params=_params(("arbitrary",)),
        name="experts",
    )(plan["te"], plan["n_tiles"], xs, w1, b1.reshape(ne, 1, dff2), w2, b2.reshape(ne, 1, d))


def _combine_kernel(seg, sloc, goff, rp_ref, rs_ref, d4_ref, x1p_ref, x1s_ref, g2p_ref, g2s_ref,
                    ln2_g, ln2_b, ys_hbm, yp_ref, ysm_ref, buf, sem, *, nbp, alpha):
    i = pl.program_id(0)
    nblk = pl.num_programs(0)
    slot = lax.rem(i, 2)
    tb = rp_ref.shape[0]
    r_max = buf.shape[1]

    def seg_copy(blk, s, e):
        n = pl.multiple_of(seg[blk * N_EXPERTS + e], SEG_PAD)
        src = pl.multiple_of(goff[blk * N_EXPERTS + e], SEG_PAD)
        dst = pl.multiple_of(sloc[blk * N_EXPERTS + e], SEG_PAD)
        return pltpu.make_async_copy(ys_hbm.at[pl.ds(src, n), :], buf.at[s, pl.ds(dst, n), :], sem.at[s])

    def for_segments(blk, fn):
        for e in range(N_EXPERTS):
            @pl.when(seg[blk * N_EXPERTS + e] > 0)
            def _(e=e):
                fn(e)

    @pl.when(i == 0)
    def _():
        buf[...] = jnp.zeros(buf.shape, buf.dtype)
        for_segments(0, lambda e: seg_copy(0, 0, e).start())

    @pl.when(i + 1 < nblk)
    def _():
        for_segments(i + 1, lambda e: seg_copy(i + 1, 1 - slot, e).start())

    for_segments(i, lambda e: seg_copy(i, slot, e).wait())

    is_p = i < nbp
    route = jnp.where(is_p, rp_ref[...], rs_ref[...])
    d4 = d4_ref[...]
    col = lax.broadcasted_iota(jnp.int32, (tb, r_max), 1).astype(_F32)
    pg = jnp.zeros((tb, r_max), _F32)
    for k in range(TOP_K):
        gate = route[:, ROUTE_GATE + k:ROUTE_GATE + k + 1]
        pg = pg + jnp.where(col == d4[:, k:k + 1], gate, 0.0)
    f = _dot(_bf(pg), buf[slot])

    x1 = jnp.where(is_p, x1p_ref[...], x1s_ref[...])
    g2s = jnp.concatenate([g2s_ref[...]] * (tb // g2s_ref.shape[0]), axis=0)
    g2 = jnp.where(is_p, jnp.broadcast_to(g2p_ref[0], g2s.shape), g2s)
    y = _layer_norm(alpha * x1 + g2 * f, ln2_g[...], ln2_b[...])

    @pl.when(is_p)
    def _():
        yp_ref[...] = y

    @pl.when(jnp.logical_not(is_p))
    def _():
        ysm_ref[...] = y


def _combine(plan, ys, rp, rs, d4, x1p, x1s, modp, mods, ln2_g, ln2_b, r_max, alpha, blocks_per_seq):
    tb = TOKEN_BLOCK
    d = x1p.shape[1]
    nbp, nbs = x1p.shape[0] // tb, x1s.shape[0] // tb
    nb = nbp + nbs
    pidx = lambda i, *_: (jnp.minimum(i, nbp - 1), 0)
    sidx = lambda i, *_: (jnp.maximum(i - nbp, 0), 0)
    g2_lane_block = 5
    flat = lambda a: a.astype(jnp.int32).reshape(-1)
    grid_spec = pltpu.PrefetchScalarGridSpec(
        num_scalar_prefetch=3,
        grid=(nb,),
        in_specs=[pl.BlockSpec((tb, LANES), pidx), pl.BlockSpec((tb, LANES), sidx),
                  pl.BlockSpec((tb, LANES), lambda i, *_: (i, 0)),
                  pl.BlockSpec((tb, d), pidx), pl.BlockSpec((tb, d), sidx),
                  pl.BlockSpec((1, 1, d), lambda i, *_: (jnp.minimum(i, nbp - 1) // blocks_per_seq, 0,
                                                         g2_lane_block)),
                  pl.BlockSpec((mods.shape[0], d), lambda i, *_: (0, g2_lane_block)),
                  pl.BlockSpec((1, d), lambda i, *_: (0, 0)), pl.BlockSpec((1, d), lambda i, *_: (0, 0)),
                  pl.BlockSpec(memory_space=pl.ANY)],
        out_specs=[pl.BlockSpec((tb, d), pidx), pl.BlockSpec((tb, d), sidx)],
        scratch_shapes=[pltpu.VMEM((2, r_max, d), _BF16), pltpu.SemaphoreType.DMA((2,))])
    return pl.pallas_call(
        functools.partial(_combine_kernel, nbp=nbp, alpha=alpha),
        grid_spec=grid_spec,
        out_shape=[jax.ShapeDtypeStruct(x1p.shape, _F32), jax.ShapeDtypeStruct(x1s.shape, _F32)],
        compiler_params=_params(("arbitrary",)),
        name="combine",
    )(flat(plan["seg"]), flat(plan["loc"]), flat(plan["goff"]),
      rp, rs, d4, x1p, x1s, modp, mods, ln2_g, ln2_b, ys)


def _complex_powers(zr, zi, n):
    k = jnp.arange(n, dtype=jnp.int32)[None, :]
    pr = jnp.ones((zr.shape[0], n), _F32)
    pi = jnp.zeros((zr.shape[0], n), _F32)
    sr, si = zr[:, None], zi[:, None]
    bit = 1
    while bit < n:
        on = (k & bit) != 0
        mr, mi = jnp.where(on, sr, 1.0), jnp.where(on, si, 0.0)
        pr, pi = pr * mr - pi * mi, pr * mi + pi * mr
        sr, si = sr * sr - si * si, 2.0 * sr * si
        bit *= 2
    return pr, pi


def _block_diag(m):
    g, a, b = m.shape
    gb = GROUPS_PER_BLOCK
    m = m.reshape(g // gb, gb, a, 1, b)
    eye = jnp.eye(gb, dtype=m.dtype).reshape(1, gb, 1, gb, 1)
    return (m * eye).reshape(g // gb, gb * a, gb * b)


def _layer_weights(p, q):
    d = p["w_in"].shape[0]
    row = lambda v: v.reshape(1, -1).astype(_F32)
    lam_re, lam_im = p["lam_re"].astype(_F32), p["lam_im"].astype(_F32)
    dt = jnp.exp(p["log_dt"].astype(_F32))[:, None]
    mag = jnp.exp(lam_re * dt)
    lbr, lbi = mag * jnp.cos(lam_im * dt), mag * jnp.sin(lam_im * dt)
    den = lam_re * lam_re + lam_im * lam_im
    nr, ni = lbr - 1.0, lbi
    fr, fi = (nr * lam_re + ni * lam_im) / den, (ni * lam_re - nr * lam_im) / den
    b_re, b_im = p["b_re"].astype(_F32), p["b_im"].astype(_F32)
    bbr = fr[..., None] * b_re - fi[..., None] * b_im
    bbi = fr[..., None] * b_im + fi[..., None] * b_re
    nstate = lam_re.size
    mod2 = lbr * lbr + lbi * lbi
    flat = lambda v: v.reshape(nstate)
    epr, epi = _complex_powers(flat(lbr), flat(lbi), q)
    eir, eii = _complex_powers(flat(lbr / mod2), flat(-lbi / mod2), q)
    table = lambda t: t
    bcast = lambda v: jnp.broadcast_to(v.reshape(nstate, 1), (nstate, LANES))
    c_re, c_im = p["c_re"].astype(_F32), p["c_im"].astype(_F32)
    w_r = p["w_router"].astype(_F32)
    w_r = jnp.pad(w_r, ((0, 0), (0, LANES - w_r.shape[1])))
    wr_hi = _bf(w_r)
    return dict(
        w_in=_bf(p["w_in"]), b_in=row(p["b_in"]), w_dw=p["w_dw"].astype(_F32), b_dw=row(p["b_dw"]),
        cln_g=row(p["conv_ln_g"]), cln_b=row(p["conv_ln_b"]), w_pw=_bf(p["w_pw"]), b_pw=row(p["b_pw"]),
        bre=_bf(_block_diag(bbr)), bim=_bf(_block_diag(bbi)),
        cc=_bf(jnp.concatenate([_block_diag(c_re), _block_diag(-c_im)], axis=2)),
        einv_r=table(eir), einv_i=table(eii), epow_r=table(epr), epow_i=table(epi),
        lamb_r=bcast(lbr), lamb_i=bcast(lbi), lrow_r=lbr.reshape(1, nstate), lrow_i=lbi.reshape(1, nstate),
        dskip=row(p["d_skip"]),
        w_sv=_bf(p["w_sv"]), b_sv=row(p["b_sv"]), w_sg=_bf(p["w_sg"]), b_sg=row(p["b_sg"]),
        w_out=_bf(p["w_out"]), b_out=row(p["b_out"]), ln1_g=row(p["ln1_g"]), ln1_b=row(p["ln1_b"]),
        wr_hi=wr_hi, wr_lo=_bf(w_r - wr_hi.astype(_F32)),
        b_r=p["b_router"].astype(_F32).reshape(-1, 1),
        ln2_g=row(p["ln2_g"]), ln2_b=row(p["ln2_b"]))


def _layer(xp, xs_tm, c_all, hist_tm, h0r, h0i, p, alpha):
    b, l, d = xp.shape
    steps, nbs, _ = xs_tm.shape
    tb = TOKEN_BLOCK
    assert l % tb == 0 and l >= CONV_WIDTH - 1 and (steps * nbs) % tb == 0 and tb % nbs == 0
    wts = _layer_weights(p, tb)
    mod = _ada(c_all, p["w_ada"].astype(_F32), p["b_ada"].astype(_F32))
    modp, mods = mod[:b].reshape(b, 1, -1), mod[b:]

    x1p, h2p, rp, cntp, conv_p, sre_p, sim_p = _mixer_prompt(xp, modp, wts, alpha)
    x1s, h2s, rs, cnts, conv_s, sre_s, sim_s = _mixer_sample(xs_tm, mods, hist_tm, h0r, h0i, wts, alpha)

    n_tok = b * l + steps * nbs
    nb, r_max, nt_max = _plan_sizes(n_tok)
    cnt = jnp.concatenate([cntp.reshape(-1, LANES), cnts.reshape(-1, LANES)], axis=0)[:, :N_EXPERTS]
    plan = _routing_plan(cnt.astype(jnp.int32), r_max, nt_max)

    flat = lambda a: a.reshape(b * l, a.shape[-1])
    xs_sorted, d4 = _dispatch(plan, flat(h2p), h2s, flat(rp), rs, r_max, nt_max)
    ys_sorted = _experts(plan, xs_sorted, p["w1"], p["b1"], p["w2"], p["b2"], nt_max)
    yp, ysm = _combine(plan, ys_sorted, flat(rp), rs, d4, flat(x1p), x1s, modp, mods,
                       wts["ln2_g"], wts["ln2_b"], r_max, alpha, l // tb)
    return (yp.reshape(b, l, d), ysm.reshape(steps, nbs, d), conv_p, sre_p[:, 0], sim_p[:, 0],
            conv_s, sre_s, sim_s)


def kernel(x_prompt, x_sample, state_conv, state_ssm_re, state_ssm_im, c_prompt, c_sample, w_ada, b_ada, w_in, b_in, w_dw, b_dw, conv_ln_g, conv_ln_b, w_pw, b_pw, lam_re, lam_im, log_dt, b_re, b_im, c_re, c_im, d_skip, w_sv, b_sv, w_sg, b_sg, w_out, b_out, ln1_g, ln1_b, w_router, b_router, w1, b1, w2, b2, ln2_g, ln2_b):
    stacked = dict(w_ada=w_ada, b_ada=b_ada, w_in=w_in, b_in=b_in, w_dw=w_dw, b_dw=b_dw,
                   conv_ln_g=conv_ln_g, conv_ln_b=conv_ln_b, w_pw=w_pw, b_pw=b_pw, lam_re=lam_re,
                   lam_im=lam_im, log_dt=log_dt, b_re=b_re, b_im=b_im, c_re=c_re, c_im=c_im,
                   d_skip=d_skip, w_sv=w_sv, b_sv=b_sv, w_sg=w_sg, b_sg=b_sg, w_out=w_out, b_out=b_out,
                   ln1_g=ln1_g, ln1_b=ln1_b, w_router=w_router, b_router=b_router, w1=w1, b1=b1,
                   w2=w2, b2=b2, ln2_g=ln2_g, ln2_b=ln2_b)
    depth = w_ada.shape[0]
    alpha = (2 * depth) ** 0.25
    b = x_prompt.shape[0]
    nbs = x_sample.shape[0]
    g, s = state_ssm_re.shape[2], state_ssm_re.shape[3]
    xp = x_prompt
    xs_tm = jnp.transpose(x_sample, (1, 0, 2))
    c_all = jnp.concatenate([c_prompt, c_sample], axis=0)
    conv_ps, re_ps, im_ps, conv_ss, re_ss, im_ss = [], [], [], [], [], []
    for layer in range(depth):
        p = {k: v[layer] for k, v in stacked.items()}
        hist_tm = jnp.transpose(state_conv[layer], (1, 0, 2))
        h0r = state_ssm_re[layer].reshape(nbs, g * s)
        h0i = state_ssm_im[layer].reshape(nbs, g * s)
        xp, xs_tm, conv_p, sre_p, sim_p, conv_s, sre_s, sim_s = _layer(
            xp, xs_tm, c_all, hist_tm, h0r, h0i, p, alpha)
        conv_ps.append(conv_p.astype(state_conv.dtype))
        re_ps.append(sre_p.reshape(b, g, s))
        im_ps.append(sim_p.reshape(b, g, s))
        conv_ss.append(jnp.transpose(conv_s, (1, 0, 2)).astype(state_conv.dtype))
        re_ss.append(sre_s.reshape(nbs, g, s))
        im_ss.append(sim_s.reshape(nbs, g, s))
    return (xp, jnp.transpose(xs_tm, (1, 0, 2)), jnp.stack(conv_ps), jnp.stack(re_ps), jnp.stack(im_ps),
            jnp.stack(conv_ss), jnp.stack(re_ss), jnp.stack(im_ss))
```

```python
import functools

import jax
import jax.numpy as jnp
from jax import lax
from jax.experimental import pallas as pl
from jax.experimental.pallas import tpu as pltpu

CONV_WIDTH = 31
SSM_GROUP = 16
SSM_STATE = 64
N_EXPERTS = 32
TOP_K = 4
SWIGLU_LIMIT = 7.0
SWIGLU_ALPHA = 1.702
LN_EPS = 1e-5

LANES = 128
SUBLANES = 8
TOKEN_BLOCK = 256
EXPERT_TILE = 512
SEG_PAD = 16
GROUPS_PER_BLOCK = LANES // SSM_GROUP
CONV_HIST = 32
CONV_ROWS = 32
ROUTE_IDX = N_EXPERTS
ROUTE_GATE = N_EXPERTS + TOP_K
VMEM_LIMIT = 56 * 1024 * 1024

_F32 = jnp.float32
_BF16 = jnp.bfloat16


def _bf(x):
    return x.astype(_BF16)


def _dot(a, b):
    return jnp.dot(a, b, preferred_element_type=_F32)


def _dot_nt(a, b):
    return lax.dot_general(a, b, (((1,), (1,)), ((), ())), preferred_element_type=_F32)


def _split(x):
    hi = _bf(x)
    lo = _bf(x - hi.astype(_F32))
    return hi, lo


def _sigmoid(x):
    return 1.0 / (1.0 + jnp.exp(-x))


def _gelu_tanh(x):
    return 0.5 * x * (1.0 + jnp.tanh(0.7978845608028654 * (x + 0.044715 * (x * x * x))))


def _layer_norm(x, g, b):
    mu = jnp.mean(x, axis=-1, keepdims=True)
    xc = x - mu
    var = jnp.mean(xc * xc, axis=-1, keepdims=True)
    return xc * lax.rsqrt(var + LN_EPS) * g + b


def _const_spec(shape):
    nd = len(shape)
    return pl.BlockSpec(shape, lambda *_: (0,) * nd)


def _params(sem):
    return pltpu.CompilerParams(dimension_semantics=sem, vmem_limit_bytes=VMEM_LIMIT)


def _ada_kernel(c_ref, w_ref, b_ref, o_ref):
    c = c_ref[...]
    s_hi, s_lo = _split(c * _sigmoid(c))
    w_hi, w_lo = _split(w_ref[...])
    o_ref[...] = _dot(s_hi, w_hi) + _dot(s_lo, w_hi) + _dot(s_hi, w_lo) + b_ref[...]


def _ada(c, w, b):
    n, d = c.shape
    cols = w.shape[1]
    tn = d
    return pl.pallas_call(
        _ada_kernel,
        grid=(cols // tn,),
        in_specs=[pl.BlockSpec((n, d), lambda j: (0, 0)),
                  pl.BlockSpec((d, tn), lambda j: (0, j)),
                  pl.BlockSpec((1, tn), lambda j: (0, j))],
        out_specs=pl.BlockSpec((n, tn), lambda j: (0, j)),
        out_shape=jax.ShapeDtypeStruct((n, cols), _F32),
        compiler_params=_params(("arbitrary",)),
        name="ada",
    )(c, w, b.reshape(1, cols))


def _tiled(fn, tile, *args, vecs=()):
    n, m = args[0].shape
    tr, tc = min(tile[0], n), min(tile[1], m)
    rows_out = None
    for r0 in range(0, n, tr):
        cols_out = None
        for c0 in range(0, m, tc):
            res = fn(*[a[r0:r0 + tr, c0:c0 + tc] for a in args], *[v[:, c0:c0 + tc] for v in vecs])
            res = res if isinstance(res, tuple) else (res,)
            if cols_out is None:
                cols_out = [[] for _ in res]
            for acc, v in zip(cols_out, res):
                acc.append(v)
        row_vals = [c[0] if len(c) == 1 else jnp.concatenate(c, axis=1) for c in cols_out]
        if rows_out is None:
            rows_out = [[] for _ in row_vals]
        for acc, v in zip(rows_out, row_vals):
            acc.append(v)
    outs = [r[0] if len(r) == 1 else jnp.concatenate(r, axis=0) for r in rows_out]
    return outs[0] if len(outs) == 1 else tuple(outs)


def _in_proj(hb, w_in, b_in, lo, hi):
    return _dot(hb, w_in[:, lo:hi]) + b_in[:, lo:hi]


def _conv_tail(v, cln_g, cln_b, w_pw, b_pw):
    g, b = cln_g[...], cln_b[...]

    def norm_swish(vb):
        vb = _layer_norm(vb, g, b)
        return vb * _sigmoid(vb)
    v = _tiled(norm_swish, (SUBLANES, v.shape[1]), v)
    return _dot(_bf(v), w_pw[...]) + b_pw[...]


def _ssm_tail(y, zs, dskip, w_sv, b_sv, w_sg, b_sg):
    yg = _tiled(lambda yb, zb, db: _bf(_gelu_tanh(yb + db * zb)), (2 * SUBLANES, 4 * LANES), y, zs, vecs=(dskip,))
    sv, sg = _dot(yg, w_sv[...]), _dot(yg, w_sg[...])
    return _tiled(lambda a, b, bv, bg: (a + bv) * _sigmoid(b + bg), (SUBLANES, 4 * LANES),
                  sv, sg, vecs=(b_sv, b_sg))


def _route(h2, wr_hi, wr_lo, b_r):
    n = h2.shape[0]
    ne = b_r.shape[0]
    h_hi, h_lo = _split(h2)
    logits = (_dot(h_hi, wr_hi[...]) + _dot(h_lo, wr_hi[...]) + _dot(h_hi, wr_lo[...])).T[:ne, :] + b_r[...]
    row = lax.broadcasted_iota(jnp.int32, (ne, n), 0).astype(_F32)
    neg = jnp.float32(-jnp.inf)
    cur = logits
    vals, idxs = [], []
    for _ in range(TOP_K):
        m = jnp.max(cur, axis=0, keepdims=True)
        idx = jnp.min(jnp.where(cur == m, row, float(ne)), axis=0, keepdims=True)
        vals.append(m)
        idxs.append(idx)
        cur = jnp.where(row == idx, neg, cur)
    es = [jnp.exp(v - vals[0]) for v in vals]
    tot = es[0]
    for e in es[1:]:
        tot = tot + e
    inv = 1.0 / tot
    comb = jnp.zeros((ne, n), _F32)
    row8 = lax.broadcasted_iota(jnp.int32, (2 * TOP_K, n), 0)
    extra = jnp.zeros((2 * TOP_K, n), _F32)
    for k in range(TOP_K):
        gate = es[k] * inv
        comb = comb + jnp.where(row == idxs[k], gate, 0.0)
        extra = extra + jnp.where(row8 == k, idxs[k], 0.0) + jnp.where(row8 == TOP_K + k, gate, 0.0)
    rest = jnp.zeros((LANES - ne - 2 * TOP_K, n), _F32)
    return jnp.concatenate([comb, extra, rest], axis=0).T


def _block_counts(route):
    lane = lax.broadcasted_iota(jnp.int32, route.shape, 1)
    member = jnp.where((route > 0.0) & (lane < N_EXPERTS), 1.0, 0.0)
    return jnp.sum(member, axis=0, keepdims=True)


def _mixer_tail(x, zg, conv_out, ssm_out, g1, sh2, sc2, alpha, w_out, b_out,
                ln1_g, ln1_b, wr_hi, wr_lo, b_r):
    d = x.shape[1]
    merged = _tiled(lambda gc, gs, co, so: _bf(_sigmoid(gc) * co + _sigmoid(gs) * so),
                    (2 * SUBLANES, 4 * LANES), zg[:, :d], zg[:, d:], conv_out, ssm_out)
    m = _dot(merged, w_out[...])

    def norm_mod(xb, mb, gb, scb, shb, bo, lg, lb):
        x1b = _layer_norm(alpha * xb + gb * (mb + bo), lg, lb)
        return x1b, x1b * (1.0 + scb) + shb
    mods = (g1, sc2, sh2)
    per_seq = g1.shape[0] == 1
    x1, h2 = _tiled(norm_mod, (SUBLANES, d), x, m, *(() if per_seq else mods),
                    vecs=(mods if per_seq else ()) + (b_out, ln1_g, ln1_b))
    return x1, h2, _route(h2, wr_hi, wr_lo, b_r)


def _mixer_prompt_kernel(x_ref, mod_ref, w_in, b_in, w_dw, b_dw, cln_g, cln_b, w_pw, b_pw,
                         bre, bim, cc, einv_r, einv_i, epow_r, epow_i, lamb_r, lamb_i, dskip,
                         w_sv, b_sv, w_sg, b_sg, w_out, b_out, ln1_g, ln1_b, wr_hi, wr_lo, b_r,
                         x1_ref, h2_ref, route_ref, cnt_ref, cstate_ref, sre_ref, sim_ref,
                         ubuf, ush, car_r, car_i, *, alpha):
    c = pl.program_id(1)
    last = pl.num_programs(1) - 1
    tl, d = x_ref.shape[1], x_ref.shape[2]
    dc = w_pw.shape[0]
    dsm = dskip.shape[1]
    nj = bre.shape[0]
    rows = bre.shape[1]

    @pl.when(c == 0)
    def _():
        ubuf[0:CONV_HIST, :] = jnp.zeros((CONV_HIST, dc), _F32)
        car_r[...] = jnp.zeros(car_r.shape, _F32)
        car_i[...] = jnp.zeros(car_i.shape, _F32)

    x = x_ref[0]
    mod = mod_ref[0]
    sh1, sc1, g1 = mod[:, 0:d], mod[:, d:2 * d], mod[:, 2 * d:3 * d]
    sh2, sc2 = mod[:, 3 * d:4 * d], mod[:, 4 * d:5 * d]
    hb = _bf(x * (1.0 + sc1) + sh1)

    u = _in_proj(hb, w_in, b_in, 0, dc) * _sigmoid(_in_proj(hb, w_in, b_in, dc, 2 * dc))
    ubuf[CONV_HIST:CONV_HIST + tl, :] = u
    span = ush.shape[1]
    for r in range(1, SUBLANES):
        ush[r - 1] = ubuf[r:r + span, :]

    n_blocks = tl // CONV_ROWS
    gate_lo = 2 * dc + dsm
    gate_w = 2 * d // n_blocks
    vblocks, zg_cols = [], []
    for blk in range(n_blocks):
        r0 = blk * CONV_ROWS
        acc = jnp.broadcast_to(b_dw[...], (CONV_ROWS, dc))
        for k in range(CONV_WIDTH):
            q, r = divmod(CONV_HIST - (CONV_WIDTH - 1) + k, SUBLANES)
            rows_k = slice(r0 + q * SUBLANES, r0 + q * SUBLANES + CONV_ROWS)
            tap = ubuf[rows_k, :] if r == 0 else ush[r - 1, rows_k, :]
            acc = acc + w_dw[k:k + 1, :] * tap
        vblocks.append(acc)
        zg_cols.append(_in_proj(hb, w_in, b_in, gate_lo + blk * gate_w, gate_lo + (blk + 1) * gate_w))
    ubuf[0:CONV_HIST, :] = ubuf[tl:tl + CONV_HIST, :]
    conv_out = _conv_tail(jnp.concatenate(vblocks, axis=0), cln_g, cln_b, w_pw, b_pw)

    zs = _in_proj(hb, w_in, b_in, 2 * dc, 2 * dc + dsm)
    zst = _bf(zs.T)
    tri = jnp.where(lax.broadcasted_iota(jnp.int32, (tl, tl), 0)
                    <= lax.broadcasted_iota(jnp.int32, (tl, tl), 1), 1.0, 0.0).astype(_BF16)
    def project(j):
        zj = zst[j * LANES:(j + 1) * LANES, :]
        return _dot(bre[j], zj), _dot(bim[j], zj)

    def scale_in(j, bu):
        r = slice(j * rows, (j + 1) * rows)
        eir, eii = einv_r[r, :], einv_i[r, :]
        return _bf(bu[0] * eir - bu[1] * eii), _bf(bu[0] * eii + bu[1] * eir)

    def prefix(v):
        return _dot(v[0], tri), _dot(v[1], tri)

    def scale_out(j, cum):
        r = slice(j * rows, (j + 1) * rows)
        cr = jnp.broadcast_to(car_r[r, LANES - 1:LANES], (rows, LANES))
        ci = jnp.broadcast_to(car_i[r, LANES - 1:LANES], (rows, LANES))
        lr, li = lamb_r[r, :], lamb_i[r, :]
        cum_r = cum[0] + jnp.concatenate([lr * cr - li * ci] * (tl // LANES), axis=1)
        cum_i = cum[1] + jnp.concatenate([lr * ci + li * cr] * (tl // LANES), axis=1)
        epr, epi = epow_r[r, :], epow_i[r, :]
        h_r = cum_r * epr - cum_i * epi
        h_i = cum_r * epi + cum_i * epr
        car_r[r, :] = h_r[:, tl - LANES:tl]
        car_i[r, :] = h_i[:, tl - LANES:tl]
        return jnp.concatenate([_bf(h_r), _bf(h_i)], axis=0)

    bu, v, cum, hcat, yts = {}, {}, {}, {}, {}
    for t in range(nj + 4):
        if t < nj:
            bu[t] = project(t)
        if 0 <= t - 2 < nj:
            cum[t - 2] = prefix(v.pop(t - 2))
        if 0 <= t - 4 < nj:
            yts[t - 4] = _dot(cc[t - 4], hcat.pop(t - 4))
        if 0 <= t - 1 < nj:
            v[t - 1] = scale_in(t - 1, bu.pop(t - 1))
        if 0 <= t - 3 < nj:
            hcat[t - 3] = scale_out(t - 3, cum.pop(t - 3))
    ssm_out = _ssm_tail(jnp.concatenate([yts[j] for j in range(nj)], axis=0).T, zs, dskip,
                        w_sv, b_sv, w_sg, b_sg)

    x1, h2, route = _mixer_tail(x, jnp.concatenate(zg_cols, axis=1), conv_out, ssm_out, g1, sh2, sc2, alpha,
                                w_out, b_out, ln1_g, ln1_b, wr_hi, wr_lo, b_r)
    x1_ref[0] = x1
    h2_ref[0] = _bf(h2)
    route_ref[0] = route
    cnt_ref[0, 0] = _block_counts(route)

    @pl.when(c == last)
    def _():
        cstate_ref[0] = ubuf[CONV_HIST + tl - (CONV_WIDTH - 1):CONV_HIST + tl, :]
        sre_ref[0] = car_r[...].T[LANES - 1:LANES, :]
        sim_ref[0] = car_i[...].T[LANES - 1:LANES, :]


def _mixer_prompt(x, mod, wts, alpha):
    b, l, d = x.shape
    tl = TOKEN_BLOCK
    nc = l // tl
    dc = wts["w_pw"].shape[0]
    nstate = wts["lamb_r"].shape[0]
    names = ["w_in", "b_in", "w_dw", "b_dw", "cln_g", "cln_b", "w_pw", "b_pw", "bre", "bim", "cc",
             "einv_r", "einv_i", "epow_r", "epow_i", "lamb_r", "lamb_i", "dskip",
             "w_sv", "b_sv", "w_sg", "b_sg", "w_out", "b_out", "ln1_g", "ln1_b",
             "wr_hi", "wr_lo", "b_r"]
    consts = [wts[n] for n in names]
    tok = lambda bi, ci: (bi, ci, 0)
    seq = lambda bi, ci: (bi, 0, 0)
    return pl.pallas_call(
        functools.partial(_mixer_prompt_kernel, alpha=alpha),
        grid=(b, nc),
        in_specs=[pl.BlockSpec((1, tl, d), tok), pl.BlockSpec((1, 1, mod.shape[-1]), seq)]
                 + [_const_spec(a.shape) for a in consts],
        out_specs=[pl.BlockSpec((1, tl, d), tok), pl.BlockSpec((1, tl, d), tok),
                   pl.BlockSpec((1, tl, LANES), tok),
                   pl.BlockSpec((1, 1, 1, LANES), lambda bi, ci: (bi, ci, 0, 0)),
                   pl.BlockSpec((1, CONV_WIDTH - 1, dc), seq),
                   pl.BlockSpec((1, 1, nstate), seq), pl.BlockSpec((1, 1, nstate), seq)],
        out_shape=[jax.ShapeDtypeStruct((b, l, d), _F32), jax.ShapeDtypeStruct((b, l, d), _BF16),
                   jax.ShapeDtypeStruct((b, l, LANES), _F32),
                   jax.ShapeDtypeStruct((b, nc, 1, LANES), _F32),
                   jax.ShapeDtypeStruct((b, CONV_WIDTH - 1, dc), _F32),
                   jax.ShapeDtypeStruct((b, 1, nstate), _F32),
                   jax.ShapeDtypeStruct((b, 1, nstate), _F32)],
        scratch_shapes=[pltpu.VMEM((CONV_HIST + tl, dc), _F32),
                        pltpu.VMEM((SUBLANES - 1, tl + CONV_HIST - SUBLANES, dc), _F32),
                        pltpu.VMEM((nstate, LANES), _F32), pltpu.VMEM((nstate, LANES), _F32)],
        compiler_params=_params(("arbitrary", "arbitrary")),
        name="mixer_prompt",
    )(x, mod, *consts)


def _mixer_sample_kernel(x_ref, mod_ref, hist_ref, h0r_ref, h0i_ref,
                         w_in, b_in, w_dw, b_dw, cln_g, cln_b, w_pw, b_pw,
                         bre, bim, cc, lrow_r, lrow_i, dskip,
                         w_sv, b_sv, w_sg, b_sg, w_out, b_out, ln1_g, ln1_b, wr_hi, wr_lo, b_r,
                         x1_ref, h2_ref, route_ref, cnt_ref, cstate_ref, sre_ref, sim_ref, *, alpha):
    steps, nb, d = x_ref.shape
    dc = w_pw.shape[0]
    dsm = dskip.shape[1]
    nj = bre.shape[0]
    rows = bre.shape[1]
    nhist = CONV_WIDTH - 1
    x = x_ref[...].reshape(steps * nb, d)
    mod = jnp.concatenate([mod_ref[...]] * steps, axis=0)
    sh1, sc1, g1 = mod[:, 0:d], mod[:, d:2 * d], mod[:, 2 * d:3 * d]
    sh2, sc2 = mod[:, 3 * d:4 * d], mod[:, 4 * d:5 * d]
    hb = _bf(x * (1.0 + sc1) + sh1)

    u = _in_proj(hb, w_in, b_in, 0, dc) * _sigmoid(_in_proj(hb, w_in, b_in, dc, 2 * dc))
    us = [u[t * nb:(t + 1) * nb, :] for t in range(steps)]
    vs = []
    for t in range(steps):
        acc = jnp.broadcast_to(b_dw[...], (nb, dc))
        for i in range(t, nhist):
            acc = acc + w_dw[i - t:i - t + 1, :] * hist_ref[i]
        for s in range(t + 1):
            acc = acc + w_dw[nhist + s - t:nhist + s - t + 1, :] * us[s]
        vs.append(acc)
    for i in range(nhist):
        src = i + steps
        cstate_ref[i] = hist_ref[src] if src < nhist else us[src - nhist]
    conv_out = _conv_tail(jnp.concatenate(vs, axis=0), cln_g, cln_b, w_pw, b_pw)

    zs = _in_proj(hb, w_in, b_in, 2 * dc, 2 * dc + dsm)
    zsb = _bf(zs)
    h_r, h_i = h0r_ref[...], h0i_ref[...]
    lr, li = lrow_r[...], lrow_i[...]
    ys = []
    for t in range(steps):
        zt = zsb[t * nb:(t + 1) * nb, :]
        bu_r = jnp.concatenate([_dot_nt(zt[:, j * LANES:(j + 1) * LANES], bre[j]) for j in range(nj)], axis=1)
        bu_i = jnp.concatenate([_dot_nt(zt[:, j * LANES:(j + 1) * LANES], bim[j]) for j in range(nj)], axis=1)
        h_r, h_i = lr * h_r - li * h_i + bu_r, lr * h_i + li * h_r + bu_i
        hrb, hib = _bf(h_r), _bf(h_i)
        ys.append(jnp.concatenate(
            [_dot_nt(jnp.concatenate([hrb[:, j * rows:(j + 1) * rows], hib[:, j * rows:(j + 1) * rows]], axis=1),
                     cc[j]) for j in range(nj)], axis=1))
    sre_ref[...] = h_r
    sim_ref[...] = h_i
    ssm_out = _ssm_tail(jnp.concatenate(ys, axis=0), zs, dskip, w_sv, b_sv, w_sg, b_sg)

    zg = _in_proj(hb, w_in, b_in, 2 * dc + dsm, 2 * dc + dsm + 2 * d)
    x1, h2, route = _mixer_tail(x, zg, conv_out, ssm_out, g1, sh2, sc2, alpha,
                                w_out, b_out, ln1_g, ln1_b, wr_hi, wr_lo, b_r)
    x1_ref[...] = x1
    h2_ref[...] = _bf(h2)
    route_ref[...] = route
    for blk in range(cnt_ref.shape[0]):
        cnt_ref[blk] = _block_counts(route[blk * TOKEN_BLOCK:(blk + 1) * TOKEN_BLOCK, :])


def _mixer_sample(x_tm, mod, hist_tm, h0r, h0i, wts, alpha):
    steps, nb, d = x_tm.shape
    n = steps * nb
    dc = wts["w_pw"].shape[0]
    nstate = h0r.shape[1]
    names = ["w_in", "b_in", "w_dw", "b_dw", "cln_g", "cln_b", "w_pw", "b_pw", "bre", "bim", "cc",
             "lrow_r", "lrow_i", "dskip", "w_sv", "b_sv", "w_sg", "b_sg", "w_out", "b_out",
             "ln1_g", "ln1_b", "wr_hi", "wr_lo", "b_r"]
    args = [x_tm, mod, hist_tm, h0r, h0i] + [wts[k] for k in names]
    return pl.pallas_call(
        functools.partial(_mixer_sample_kernel, alpha=alpha),
        grid=(1,),
        in_specs=[_const_spec(a.shape) for a in args],
        out_specs=[_const_spec((n, d)), _const_spec((n, d)), _const_spec((n, LANES)),
                   _const_spec((n // TOKEN_BLOCK, 1, LANES)),
                   _const_spec((CONV_WIDTH - 1, nb, dc)),
                   _const_spec((nb, nstate)), _const_spec((nb, nstate))],
        out_shape=[jax.ShapeDtypeStruct((n, d), _F32), jax.ShapeDtypeStruct((n, d), _BF16),
                   jax.ShapeDtypeStruct((n, LANES), _F32),
                   jax.ShapeDtypeStruct((n // TOKEN_BLOCK, 1, LANES), _F32),
                   jax.ShapeDtypeStruct((CONV_WIDTH - 1, nb, dc), _F32),
                   jax.ShapeDtypeStruct((nb, nstate), _F32), jax.ShapeDtypeStruct((nb, nstate), _F32)],
        compiler_params=_params(("arbitrary",)),
        name="mixer_sample",
    )(*args)


def _round_up(x, m):
    return (x + m - 1) // m * m


def _plan_sizes(n_tokens):
    nb = n_tokens // TOKEN_BLOCK
    r_max = _round_up(TOP_K * TOKEN_BLOCK + 2 * N_EXPERTS * (SEG_PAD - 1), 2 * LANES)
    rows_max = n_tokens * TOP_K + N_EXPERTS * (EXPERT_TILE - 1)
    nt_max = -(-rows_max // EXPERT_TILE)
    return nb, r_max, nt_max


def _routing_plan(cnt, nt_max):
    nb, ne = cnt.shape
    before = jnp.cumsum(cnt, axis=0) - cnt
    tot = jnp.sum(cnt, axis=0)
    cin = before % SEG_PAD
    last = (jnp.arange(nb, dtype=jnp.int32) == nb - 1)[:, None]
    active = (cnt > 0) | (last & (cin > 0))
    seg = jnp.where(active, _round_up(cin + cnt, SEG_PAD), 0)
    loc = jnp.cumsum(seg, axis=1) - seg
    totpad = _round_up(tot, EXPERT_TILE)
    eend = jnp.cumsum(totpad)
    estart = eend - totpad
    goff = estart[None, :] + before - cin
    nfull = jnp.where(last, seg, jnp.where(active, (cin + cnt) // SEG_PAD * SEG_PAD, 0))
    cout = jnp.where(last | ~active, 0, (cin + cnt) % SEG_PAD)
    n_tiles = (eend[-1] // EXPERT_TILE).astype(jnp.int32)
    tiles = jnp.minimum(jnp.arange(nt_max, dtype=jnp.int32), n_tiles - 1)
    te = jnp.sum(((eend // EXPERT_TILE)[None, :] <= tiles[:, None]).astype(jnp.int32), axis=1)
    te = jnp.minimum(te, ne - 1)
    tot16 = _round_up(tot, SEG_PAD)
    return dict(seg=seg, loc=loc, goff=goff, nfull=nfull, cin=jnp.where(active, cin, 0), cout=cout,
                n_tiles=n_tiles.reshape(1), te=te,
                npad=(totpad - tot16).astype(jnp.int32), pad_base=(estart + tot16).astype(jnp.int32))


def _dest_columns(route, loc_row):
    n = route.shape[0]
    lane_i = lax.broadcasted_iota(jnp.int32, (n, LANES), 1)
    lane = lane_i.astype(_F32)
    member = jnp.where((route > 0.0) & (lane_i < N_EXPERTS), 1.0, 0.0).astype(_BF16)
    strict = jnp.where(lax.broadcasted_iota(jnp.int32, (n, n), 1)
                       < lax.broadcasted_iota(jnp.int32, (n, n), 0), 1.0, 0.0).astype(_BF16)
    dest = _dot(strict, member) + loc_row
    d4 = jnp.zeros((n, LANES), _F32)
    for k in range(TOP_K):
        idx = route[:, ROUTE_IDX + k:ROUTE_IDX + k + 1]
        dk = jnp.sum(jnp.where(lane == idx, dest, 0.0), axis=-1, keepdims=True)
        d4 = d4 + jnp.where(lane_i == k, dk, 0.0)
    return d4


def _dispatch_kernel(nfull, sloc, goff, cin, cout, npad, pad_base, hp_ref, hs_ref, rp_ref, rs_ref, loc_ref,
                     xs_hbm, d4_ref, buf, stage, zbuf, sem, zsem, *, nbp):
    i = pl.program_id(0)
    nblk = pl.num_programs(0)
    slot = lax.rem(i, 2)
    tb = hp_ref.shape[0]
    r_max = buf.shape[1]
    is_p = i < nbp
    h = jnp.where(is_p, hp_ref[...], hs_ref[...])
    route = jnp.where(is_p, rp_ref[...], rs_ref[...])
    d4 = _dest_columns(route, loc_ref[0])
    d4_ref[...] = d4
    d4t = d4.T
    row = lax.broadcasted_iota(jnp.int32, (r_max, tb), 0).astype(_F32)
    p = jnp.zeros((r_max, tb), _F32)
    for k in range(TOP_K):
        p = p + jnp.where(row == d4t[k:k + 1, :], 1.0, 0.0)
    buf[slot] = _bf(_dot(_bf(p), h))

    def seg_copy(blk, s, e):
        n = pl.multiple_of(nfull[blk * N_EXPERTS + e], SEG_PAD)
        src = pl.multiple_of(sloc[blk * N_EXPERTS + e], SEG_PAD)
        dst = pl.multiple_of(goff[blk * N_EXPERTS + e], SEG_PAD)
        return pltpu.make_async_copy(buf.at[s, pl.ds(src, n), :], xs_hbm.at[pl.ds(dst, n), :], sem.at[s])

    def pad_copy(e):
        n = pl.multiple_of(npad[e], SEG_PAD)
        dst = pl.multiple_of(pad_base[e], SEG_PAD)
        return pltpu.make_async_copy(zbuf.at[pl.ds(0, n), :], xs_hbm.at[pl.ds(dst, n), :], zsem.at[0])

    def for_segments(blk, fn):
        for e in range(N_EXPERTS):
            @pl.when(nfull[blk * N_EXPERTS + e] > 0)
            def _(e=e):
                fn(e)

    for e in range(N_EXPERTS):
        @pl.when(cin[i * N_EXPERTS + e] > 0)
        def _(e=e):
            first = pl.ds(pl.multiple_of(sloc[i * N_EXPERTS + e], SEG_PAD), SEG_PAD)
            buf[slot, first, :] = buf[slot, first, :] + stage[e]

    for_segments(i, lambda e: seg_copy(i, slot, e).start())

    for e in range(N_EXPERTS):
        @pl.when(cout[i * N_EXPERTS + e] > 0)
        def _(e=e):
            rest = pl.multiple_of(sloc[i * N_EXPERTS + e] + nfull[i * N_EXPERTS + e], SEG_PAD)
            stage[e] = buf[slot, pl.ds(rest, SEG_PAD), :]

    @pl.when(i > 0)
    def _():
        for_segments(i - 1, lambda e: seg_copy(i - 1, 1 - slot, e).wait())

    @pl.when(i == nblk - 1)
    def _():
        zbuf[...] = jnp.zeros(zbuf.shape, zbuf.dtype)
        for e in range(N_EXPERTS):
            @pl.when(npad[e] > 0)
            def _(e=e):
                pad_copy(e).start()
        for e in range(N_EXPERTS):
            @pl.when(npad[e] > 0)
            def _(e=e):
                pad_copy(e).wait()
        for_segments(i, lambda e: seg_copy(i, slot, e).wait())


def _dispatch(plan, h2p, h2s, rp, rs, r_max, nt_max):
    tb = TOKEN_BLOCK
    d = h2p.shape[1]
    nbp, nbs = h2p.shape[0] // tb, h2s.shape[0] // tb
    nb = nbp + nbs
    first_row = (plan["loc"] + plan["cin"]).astype(_F32)
    loc = jnp.pad(first_row, ((0, 0), (0, LANES - N_EXPERTS))).reshape(nb, 1, LANES)
    pidx = lambda i, *_: (jnp.minimum(i, nbp - 1), 0)
    sidx = lambda i, *_: (jnp.maximum(i - nbp, 0), 0)
    grid_spec = pltpu.PrefetchScalarGridSpec(
        num_scalar_prefetch=7,
        grid=(nb,),
        in_specs=[pl.BlockSpec((tb, d), pidx), pl.BlockSpec((tb, d), sidx),
                  pl.BlockSpec((tb, LANES), pidx), pl.BlockSpec((tb, LANES), sidx),
                  pl.BlockSpec((1, 1, LANES), lambda i, *_: (i, 0, 0))],
        out_specs=[pl.BlockSpec(memory_space=pl.ANY), pl.BlockSpec((tb, LANES), lambda i, *_: (i, 0))],
        scratch_shapes=[pltpu.VMEM((2, r_max, d), _BF16), pltpu.VMEM((N_EXPERTS, SEG_PAD, d), _BF16),
                        pltpu.VMEM((EXPERT_TILE, d), _BF16),
                        pltpu.SemaphoreType.DMA((2,)), pltpu.SemaphoreType.DMA((1,))])
    flat = lambda a: a.astype(jnp.int32).reshape(-1)
    return pl.pallas_call(
        functools.partial(_dispatch_kernel, nbp=nbp),
        grid_spec=grid_spec,
        out_shape=[jax.ShapeDtypeStruct((nt_max * EXPERT_TILE, d), _BF16),
                   jax.ShapeDtypeStruct((nb * tb, LANES), _F32)],
        compiler_params=_params(("arbitrary",)),
        name="dispatch",
    )(flat(plan["nfull"]), flat(plan["loc"]), flat(plan["goff"]), flat(plan["cin"]), flat(plan["cout"]),
      plan["npad"], plan["pad_base"], h2p, h2s, rp, rs, loc)


def _expert_kernel(te, n_tiles, x_ref, w1_ref, b1_ref, w2_ref, b2_ref, y_ref, w1b, w2b):
    i = pl.program_id(0)
    dff = w2_ref.shape[1]

    @pl.when(i < n_tiles[0])
    def _():
        prev = te[jnp.maximum(i - 1, 0)]

        @pl.when((i == 0) | (te[i] != prev))
        def _():
            w1b[...] = _bf(w1_ref[0])
            w2b[...] = _bf(w2_ref[0])

        gu = _dot(x_ref[...], w1b[...]) + b1_ref[0]
        g = jnp.minimum(gu[:, :dff], SWIGLU_LIMIT)
        up = jnp.clip(gu[:, dff:], -SWIGLU_LIMIT, SWIGLU_LIMIT)
        act = g * _sigmoid(SWIGLU_ALPHA * g) * (up + 1.0)
        y_ref[...] = _bf(_dot(_bf(act), w2b[...]) + b2_ref[0])


def _experts(plan, xs, w1, b1, w2, b2, nt_max):
    ne, d, dff2 = w1.shape
    dff = w2.shape[1]
    tm = EXPERT_TILE
    tile = lambda i, te, nt: (jnp.minimum(i, nt[0] - 1), 0)
    wsel = lambda i, te, nt: (te[i], 0, 0)
    grid_spec = pltpu.PrefetchScalarGridSpec(
        num_scalar_prefetch=2,
        grid=(nt_max,),
        in_specs=[pl.BlockSpec((tm, d), tile),
                  pl.BlockSpec((1, d, dff2), wsel), pl.BlockSpec((1, 1, dff2), wsel),
                  pl.BlockSpec((1, dff, d), wsel), pl.BlockSpec((1, 1, d), wsel)],
        out_specs=pl.BlockSpec((tm, d), tile),
        scratch_shapes=[pltpu.VMEM((d, dff2), _BF16), pltpu.VMEM((dff, d), _BF16)])
    return pl.pallas_call(
        _expert_kernel,
        grid_spec=grid_spec,
        out_shape=jax.ShapeDtypeStruct(xs.shape, _BF16),
        compiler_params=_params(("arbitrary",)),
        name="experts",
    )(plan["te"], plan["n_tiles"], xs, w1, b1.reshape(ne, 1, dff2), w2, b2.reshape(ne, 1, d))


def _combine_kernel(seg, sloc, goff, rp_ref, rs_ref, d4_ref, x1p_ref, x1s_ref, g2p_ref, g2s_ref,
                    ln2_g, ln2_b, ys_hbm, yp_ref, ysm_ref, buf, sem, *, nbp, alpha):
    i = pl.program_id(0)
    nblk = pl.num_programs(0)
    slot = lax.rem(i, 2)
    tb = rp_ref.shape[0]
    r_max = buf.shape[1]

    def seg_copy(blk, s, e):
        n = pl.multiple_of(seg[blk * N_EXPERTS + e], SEG_PAD)
        src = pl.multiple_of(goff[blk * N_EXPERTS + e], SEG_PAD)
        dst = pl.multiple_of(sloc[blk * N_EXPERTS + e], SEG_PAD)
        return pltpu.make_async_copy(ys_hbm.at[pl.ds(src, n), :], buf.at[s, pl.ds(dst, n), :], sem.at[s])

    def for_segments(blk, fn):
        for e in range(N_EXPERTS):
            @pl.when(seg[blk * N_EXPERTS + e] > 0)
            def _(e=e):
                fn(e)

    @pl.when(i == 0)
    def _():
        buf[...] = jnp.zeros(buf.shape, buf.dtype)
        for_segments(0, lambda e: seg_copy(0, 0, e).start())

    @pl.when(i + 1 < nblk)
    def _():
        for_segments(i + 1, lambda e: seg_copy(i + 1, 1 - slot, e).start())

    for_segments(i, lambda e: seg_copy(i, slot, e).wait())

    is_p = i < nbp
    route = jnp.where(is_p, rp_ref[...], rs_ref[...])
    d4 = d4_ref[...]
    col = lax.broadcasted_iota(jnp.int32, (tb, r_max), 1).astype(_F32)
    pg = jnp.zeros((tb, r_max), _F32)
    for k in range(TOP_K):
        gate = route[:, ROUTE_GATE + k:ROUTE_GATE + k + 1]
        pg = pg + jnp.where(col == d4[:, k:k + 1], gate, 0.0)
    f = _dot(_bf(pg), buf[slot])

    x1 = jnp.where(is_p, x1p_ref[...], x1s_ref[...])
    g2s = jnp.concatenate([g2s_ref[...]] * (tb // g2s_ref.shape[0]), axis=0)
    g2 = jnp.where(is_p, jnp.broadcast_to(g2p_ref[0], g2s.shape), g2s)
    y = _layer_norm(alpha * x1 + g2 * f, ln2_g[...], ln2_b[...])

    @pl.when(is_p)
    def _():
        yp_ref[...] = y

    @pl.when(jnp.logical_not(is_p))
    def _():
        ysm_ref[...] = y


def _combine(plan, ys, rp, rs, d4, x1p, x1s, modp, mods, ln2_g, ln2_b, r_max, alpha, blocks_per_seq):
    tb = TOKEN_BLOCK
    d = x1p.shape[1]
    nbp, nbs = x1p.shape[0] // tb, x1s.shape[0] // tb
    nb = nbp + nbs
    pidx = lambda i, *_: (jnp.minimum(i, nbp - 1), 0)
    sidx = lambda i, *_: (jnp.maximum(i - nbp, 0), 0)
    g2_lane_block = 5
    flat = lambda a: a.astype(jnp.int32).reshape(-1)
    grid_spec = pltpu.PrefetchScalarGridSpec(
        num_scalar_prefetch=3,
        grid=(nb,),
        in_specs=[pl.BlockSpec((tb, LANES), pidx), pl.BlockSpec((tb, LANES), sidx),
                  pl.BlockSpec((tb, LANES), lambda i, *_: (i, 0)),
                  pl.BlockSpec((tb, d), pidx), pl.BlockSpec((tb, d), sidx),
                  pl.BlockSpec((1, 1, d), lambda i, *_: (jnp.minimum(i, nbp - 1) // blocks_per_seq, 0,
                                                         g2_lane_block)),
                  pl.BlockSpec((mods.shape[0], d), lambda i, *_: (0, g2_lane_block)),
                  pl.BlockSpec((1, d), lambda i, *_: (0, 0)), pl.BlockSpec((1, d), lambda i, *_: (0, 0)),
                  pl.BlockSpec(memory_space=pl.ANY)],
        out_specs=[pl.BlockSpec((tb, d), pidx), pl.BlockSpec((tb, d), sidx)],
        scratch_shapes=[pltpu.VMEM((2, r_max, d), _BF16), pltpu.SemaphoreType.DMA((2,))])
    return pl.pallas_call(
        functools.partial(_combine_kernel, nbp=nbp, alpha=alpha),
        grid_spec=grid_spec,
        out_shape=[jax.ShapeDtypeStruct(x1p.shape, _F32), jax.ShapeDtypeStruct(x1s.shape, _F32)],
        compiler_params=_params(("arbitrary",)),
        name="combine",
    )(flat(plan["seg"]), flat(plan["loc"]), flat(plan["goff"]),
      rp, rs, d4, x1p, x1s, modp, mods, ln2_g, ln2_b, ys)


def _complex_powers(zr, zi, n):
    k = jnp.arange(n, dtype=jnp.int32)[None, :]
    pr = jnp.ones((zr.shape[0], n), _F32)
    pi = jnp.zeros((zr.shape[0], n), _F32)
    sr, si = zr[:, None], zi[:, None]
    bit = 1
    while bit < n:
        on = (k & bit) != 0
        mr, mi = jnp.where(on, sr, 1.0), jnp.where(on, si, 0.0)
        pr, pi = pr * mr - pi * mi, pr * mi + pi * mr
        sr, si = sr * sr - si * si, 2.0 * sr * si
        bit *= 2
    return pr, pi


def _block_diag(m):
    g, a, b = m.shape
    gb = GROUPS_PER_BLOCK
    m = m.reshape(g // gb, gb, a, 1, b)
    eye = jnp.eye(gb, dtype=m.dtype).reshape(1, gb, 1, gb, 1)
    return (m * eye).reshape(g // gb, gb * a, gb * b)


def _layer_weights(p, q):
    d = p["w_in"].shape[0]
    row = lambda v: v.reshape(1, -1).astype(_F32)
    lam_re, lam_im = p["lam_re"].astype(_F32), p["lam_im"].astype(_F32)
    dt = jnp.exp(p["log_dt"].astype(_F32))[:, None]
    mag = jnp.exp(lam_re * dt)
    lbr, lbi = mag * jnp.cos(lam_im * dt), mag * jnp.sin(lam_im * dt)
    den = lam_re * lam_re + lam_im * lam_im
    nr, ni = lbr - 1.0, lbi
    fr, fi = (nr * lam_re + ni * lam_im) / den, (ni * lam_re - nr * lam_im) / den
    b_re, b_im = p["b_re"].astype(_F32), p["b_im"].astype(_F32)
    bbr = fr[..., None] * b_re - fi[..., None] * b_im
    bbi = fr[..., None] * b_im + fi[..., None] * b_re
    nstate = lam_re.size
    mod2 = lbr * lbr + lbi * lbi
    flat = lambda v: v.reshape(nstate)
    epr, epi = _complex_powers(flat(lbr), flat(lbi), q)
    eir, eii = _complex_powers(flat(lbr / mod2), flat(-lbi / mod2), q)
    table = lambda t: t
    bcast = lambda v: jnp.broadcast_to(v.reshape(nstate, 1), (nstate, LANES))
    c_re, c_im = p["c_re"].astype(_F32), p["c_im"].astype(_F32)
    w_r = p["w_router"].astype(_F32)
    w_r = jnp.pad(w_r, ((0, 0), (0, LANES - w_r.shape[1])))
    wr_hi = _bf(w_r)
    return dict(
        w_in=_bf(p["w_in"]), b_in=row(p["b_in"]), w_dw=p["w_dw"].astype(_F32), b_dw=row(p["b_dw"]),
        cln_g=row(p["conv_ln_g"]), cln_b=row(p["conv_ln_b"]), w_pw=_bf(p["w_pw"]), b_pw=row(p["b_pw"]),
        bre=_bf(_block_diag(bbr)), bim=_bf(_block_diag(bbi)),
        cc=_bf(jnp.concatenate([_block_diag(c_re), _block_diag(-c_im)], axis=2)),
        einv_r=table(eir), einv_i=table(eii), epow_r=table(epr), epow_i=table(epi),
        lamb_r=bcast(lbr), lamb_i=bcast(lbi), lrow_r=lbr.reshape(1, nstate), lrow_i=lbi.reshape(1, nstate),
        dskip=row(p["d_skip"]),
        w_sv=_bf(p["w_sv"]), b_sv=row(p["b_sv"]), w_sg=_bf(p["w_sg"]), b_sg=row(p["b_sg"]),
        w_out=_bf(p["w_out"]), b_out=row(p["b_out"]), ln1_g=row(p["ln1_g"]), ln1_b=row(p["ln1_b"]),
        wr_hi=wr_hi, wr_lo=_bf(w_r - wr_hi.astype(_F32)),
        b_r=p["b_router"].astype(_F32).reshape(-1, 1),
        ln2_g=row(p["ln2_g"]), ln2_b=row(p["ln2_b"]))


def _layer(xp, xs_tm, c_all, hist_tm, h0r, h0i, p, alpha):
    b, l, d = xp.shape
    steps, nbs, _ = xs_tm.shape
    tb = TOKEN_BLOCK
    assert l % tb == 0 and l >= CONV_WIDTH - 1 and (steps * nbs) % tb == 0 and tb % nbs == 0
    wts = _layer_weights(p, tb)
    mod = _ada(c_all, p["w_ada"].astype(_F32), p["b_ada"].astype(_F32))
    modp, mods = mod[:b].reshape(b, 1, -1), mod[b:]

    x1p, h2p, rp, cntp, conv_p, sre_p, sim_p = _mixer_prompt(xp, modp, wts, alpha)
    x1s, h2s, rs, cnts, conv_s, sre_s, sim_s = _mixer_sample(xs_tm, mods, hist_tm, h0r, h0i, wts, alpha)

    n_tok = b * l + steps * nbs
    nb, r_max, nt_max = _plan_sizes(n_tok)
    cnt = jnp.concatenate([cntp.reshape(-1, LANES), cnts.reshape(-1, LANES)], axis=0)[:, :N_EXPERTS]
    plan = _routing_plan(cnt.astype(jnp.int32), nt_max)

    flat = lambda a: a.reshape(b * l, a.shape[-1])
    xs_sorted, d4 = _dispatch(plan, flat(h2p), h2s, flat(rp), rs, r_max, nt_max)
    ys_sorted = _experts(plan, xs_sorted, p["w1"], p["b1"], p["w2"], p["b2"], nt_max)
    yp, ysm = _combine(plan, ys_sorted, flat(rp), rs, d4, flat(x1p), x1s, modp, mods,
                       wts["ln2_g"], wts["ln2_b"], r_max, alpha, l // tb)
    return (yp.reshape(b, l, d), ysm.reshape(steps, nbs, d), conv_p, sre_p[:, 0], sim_p[:, 0],
            conv_s, sre_s, sim_s)


def kernel(x_prompt, x_sample, state_conv, state_ssm_re, state_ssm_im, c_prompt, c_sample, w_ada, b_ada, w_in, b_in, w_dw, b_dw, conv_ln_g, conv_ln_b, w_pw, b_pw, lam_re, lam_im, log_dt, b_re, b_im, c_re, c_im, d_skip, w_sv, b_sv, w_sg, b_sg, w_out, b_out, ln1_g, ln1_b, w_router, b_router, w1, b1, w2, b2, ln2_g, ln2_b):
    stacked = dict(w_ada=w_ada, b_ada=b_ada, w_in=w_in, b_in=b_in, w_dw=w_dw, b_dw=b_dw,
                   conv_ln_g=conv_ln_g, conv_ln_b=conv_ln_b, w_pw=w_pw, b_pw=b_pw, lam_re=lam_re,
                   lam_im=lam_im, log_dt=log_dt, b_re=b_re, b_im=b_im, c_re=c_re, c_im=c_im,
                   d_skip=d_skip, w_sv=w_sv, b_sv=b_sv, w_sg=w_sg, b_sg=b_sg, w_out=w_out, b_out=b_out,
                   ln1_g=ln1_g, ln1_b=ln1_b, w_router=w_router, b_router=b_router, w1=w1, b1=b1,
                   w2=w2, b2=b2, ln2_g=ln2_g, ln2_b=ln2_b)
    depth = w_ada.shape[0]
    alpha = (2 * depth) ** 0.25
    b = x_prompt.shape[0]
    nbs = x_sample.shape[0]
    g, s = state_ssm_re.shape[2], state_ssm_re.shape[3]
    xp = x_prompt
    xs_tm = jnp.transpose(x_sample, (1, 0, 2))
    c_all = jnp.concatenate([c_prompt, c_sample], axis=0)
    conv_ps, re_ps, im_ps, conv_ss, re_ss, im_ss = [], [], [], [], [], []
    for layer in range(depth):
        p = {k: v[layer] for k, v in stacked.items()}
        hist_tm = jnp.transpose(state_conv[layer], (1, 0, 2))
        h0r = state_ssm_re[layer].reshape(nbs, g * s)
        h0i = state_ssm_im[layer].reshape(nbs, g * s)
        xp, xs_tm, conv_p, sre_p, sim_p, conv_s, sre_s, sim_s = _layer(
            xp, xs_tm, c_all, hist_tm, h0r, h0i, p, alpha)
        conv_ps.append(conv_p.astype(state_conv.dtype))
        re_ps.append(sre_p.reshape(b, g, s))
        im_ps.append(sim_p.reshape(b, g, s))
        conv_ss.append(jnp.transpose(conv_s, (1, 0, 2)).astype(state_conv.dtype))
        re_ss.append(sre_s.reshape(nbs, g, s))
        im_ss.append(sim_s.reshape(nbs, g, s))
    return (xp, jnp.transpose(xs_tm, (1, 0, 2)), jnp.stack(conv_ps), jnp.stack(re_ps), jnp.stack(im_ps),
            jnp.stack(conv_ss), jnp.stack(re_ss), jnp.stack(im_ss))
```

```python
import functools

import jax
import jax.numpy as jnp
from jax import lax
from jax.experimental import pallas as pl
from jax.experimental.pallas import tpu as pltpu

CONV_WIDTH = 31
SSM_GROUP = 16
SSM_STATE = 64
N_EXPERTS = 32
TOP_K = 4
SWIGLU_LIMIT = 7.0
SWIGLU_ALPHA = 1.702
LN_EPS = 1e-5

LANES = 128
SUBLANES = 8
TOKEN_BLOCK = 256
EXPERT_TILE = 512
SEG_PAD = 16
GROUPS_PER_BLOCK = LANES // SSM_GROUP
CONV_HIST = 32
CONV_ROWS = 32
ROUTE_IDX = N_EXPERTS
ROUTE_GATE = N_EXPERTS + TOP_K
VMEM_LIMIT = 56 * 1024 * 1024

_F32 = jnp.float32
_BF16 = jnp.bfloat16


def _bf(x):
    return x.astype(_BF16)


def _dot(a, b):
    return jnp.dot(a, b, preferred_element_type=_F32)


def _dot_nt(a, b):
    return lax.dot_general(a, b, (((1,), (1,)), ((), ())), preferred_element_type=_F32)


def _split(x):
    hi = _bf(x)
    lo = _bf(x - hi.astype(_F32))
    return hi, lo


def _sigmoid(x):
    return 1.0 / (1.0 + jnp.exp(-x))


def _gelu_tanh(x):
    return 0.5 * x * (1.0 + jnp.tanh(0.7978845608028654 * (x + 0.044715 * (x * x * x))))


def _layer_norm(x, g, b):
    mu = jnp.mean(x, axis=-1, keepdims=True)
    xc = x - mu
    var = jnp.mean(xc * xc, axis=-1, keepdims=True)
    return xc * lax.rsqrt(var + LN_EPS) * g + b


def _const_spec(shape):
    nd = len(shape)
    return pl.BlockSpec(shape, lambda *_: (0,) * nd)


def _params(sem):
    return pltpu.CompilerParams(dimension_semantics=sem, vmem_limit_bytes=VMEM_LIMIT)


def _ada_kernel(c_ref, w_ref, b_ref, o_ref):
    c = c_ref[...]
    s_hi, s_lo = _split(c * _sigmoid(c))
    w_hi, w_lo = _split(w_ref[...])
    o_ref[...] = _dot(s_hi, w_hi) + _dot(s_lo, w_hi) + _dot(s_hi, w_lo) + b_ref[...]


def _ada(c, w, b):
    n, d = c.shape
    cols = w.shape[1]
    tn = d
    return pl.pallas_call(
        _ada_kernel,
        grid=(cols // tn,),
        in_specs=[pl.BlockSpec((n, d), lambda j: (0, 0)),
                  pl.BlockSpec((d, tn), lambda j: (0, j)),
                  pl.BlockSpec((1, tn), lambda j: (0, j))],
        out_specs=pl.BlockSpec((n, tn), lambda j: (0, j)),
        out_shape=jax.ShapeDtypeStruct((n, cols), _F32),
        compiler_params=_params(("arbitrary",)),
        name="ada",
    )(c, w, b.reshape(1, cols))


def _tiled(fn, tile, *args, vecs=()):
    n, m = args[0].shape
    tr, tc = min(tile[0], n), min(tile[1], m)
    rows_out = None
    for r0 in range(0, n, tr):
        cols_out = None
        for c0 in range(0, m, tc):
            res = fn(*[a[r0:r0 + tr, c0:c0 + tc] for a in args], *[v[:, c0:c0 + tc] for v in vecs])
            res = res if isinstance(res, tuple) else (res,)
            if cols_out is None:
                cols_out = [[] for _ in res]
            for acc, v in zip(cols_out, res):
                acc.append(v)
        row_vals = [c[0] if len(c) == 1 else jnp.concatenate(c, axis=1) for c in cols_out]
        if rows_out is None:
            rows_out = [[] for _ in row_vals]
        for acc, v in zip(rows_out, row_vals):
            acc.append(v)
    outs = [r[0] if len(r) == 1 else jnp.concatenate(r, axis=0) for r in rows_out]
    return outs[0] if len(outs) == 1 else tuple(outs)


def _in_proj(hb, w_in, b_in, lo, hi):
    return _dot(hb, w_in[:, lo:hi]) + b_in[:, lo:hi]


def _conv_tail(v, cln_g, cln_b, w_pw, b_pw):
    g, b = cln_g[...], cln_b[...]

    def norm_swish(vb):
        vb = _layer_norm(vb, g, b)
        return vb * _sigmoid(vb)
    v = _tiled(norm_swish, (SUBLANES, v.shape[1]), v)
    return _dot(_bf(v), w_pw[...]) + b_pw[...]


def _ssm_tail(y, zs, dskip, w_sv, b_sv, w_sg, b_sg):
    yg = _tiled(lambda yb, zb, db: _bf(_gelu_tanh(yb + db * zb)), (2 * SUBLANES, 4 * LANES), y, zs, vecs=(dskip,))
    sv, sg = _dot(yg, w_sv[...]), _dot(yg, w_sg[...])
    return _tiled(lambda a, b, bv, bg: (a + bv) * _sigmoid(b + bg), (SUBLANES, 4 * LANES),
                  sv, sg, vecs=(b_sv, b_sg))


def _route(h2, wr_hi, wr_lo, b_r):
    n = h2.shape[0]
    ne = b_r.shape[0]
    h_hi, h_lo = _split(h2)
    logits = (_dot(h_hi, wr_hi[...]) + _dot(h_lo, wr_hi[...]) + _dot(h_hi, wr_lo[...])).T[:ne, :] + b_r[...]
    row = lax.broadcasted_iota(jnp.int32, (ne, n), 0).astype(_F32)
    neg = jnp.float32(-jnp.inf)
    cur = logits
    vals, idxs = [], []
    for _ in range(TOP_K):
        m = jnp.max(cur, axis=0, keepdims=True)
        idx = jnp.min(jnp.where(cur == m, row, float(ne)), axis=0, keepdims=True)
        vals.append(m)
        idxs.append(idx)
        cur = jnp.where(row == idx, neg, cur)
    es = [jnp.exp(v - vals[0]) for v in vals]
    tot = es[0]
    for e in es[1:]:
        tot = tot + e
    inv = 1.0 / tot
    comb = jnp.zeros((ne, n), _F32)
    row8 = lax.broadcasted_iota(jnp.int32, (2 * TOP_K, n), 0)
    extra = jnp.zeros((2 * TOP_K, n), _F32)
    for k in range(TOP_K):
        gate = es[k] * inv
        comb = comb + jnp.where(row == idxs[k], gate, 0.0)
        extra = extra + jnp.where(row8 == k, idxs[k], 0.0) + jnp.where(row8 == TOP_K + k, gate, 0.0)
    rest = jnp.zeros((LANES - ne - 2 * TOP_K, n), _F32)
    return jnp.concatenate([comb, extra, rest], axis=0).T


def _block_counts(route):
    lane = lax.broadcasted_iota(jnp.int32, route.shape, 1)
    member = jnp.where((route > 0.0) & (lane < N_EXPERTS), 1.0, 0.0)
    return jnp.sum(member, axis=0, keepdims=True)


def _mixer_tail(x, zg, conv_out, ssm_out, g1, sh2, sc2, alpha, w_out, b_out,
                ln1_g, ln1_b, wr_hi, wr_lo, b_r):
    d = x.shape[1]
    merged = _tiled(lambda gc, gs, co, so: _bf(_sigmoid(gc) * co + _sigmoid(gs) * so),
                    (2 * SUBLANES, 4 * LANES), zg[:, :d], zg[:, d:], conv_out, ssm_out)
    m = _dot(merged, w_out[...])

    def norm_mod(xb, mb, gb, scb, shb, bo, lg, lb):
        x1b = _layer_norm(alpha * xb + gb * (mb + bo), lg, lb)
        return x1b, x1b * (1.0 + scb) + shb
    mods = (g1, sc2, sh2)
    per_seq = g1.shape[0] == 1
    x1, h2 = _tiled(norm_mod, (SUBLANES, d), x, m, *(() if per_seq else mods),
                    vecs=(mods if per_seq else ()) + (b_out, ln1_g, ln1_b))
    return x1, h2, _route(h2, wr_hi, wr_lo, b_r)


def _mixer_prompt_kernel(x_ref, mod_ref, w_in, b_in, w_dw, b_dw, cln_g, cln_b, w_pw, b_pw,
                         bre, bim, cc, einv_r, einv_i, epow_r, epow_i, lamb_r, lamb_i, dskip,
                         w_sv, b_sv, w_sg, b_sg, w_out, b_out, ln1_g, ln1_b, wr_hi, wr_lo, b_r,
                         x1_ref, h2_ref, route_ref, cnt_ref, cstate_ref, sre_ref, sim_ref,
                         ubuf, ush, car_r, car_i, *, alpha):
    c = pl.program_id(1)
    last = pl.num_programs(1) - 1
    tl, d = x_ref.shape[1], x_ref.shape[2]
    dc = w_pw.shape[0]
    dsm = dskip.shape[1]
    nj = bre.shape[0]
    rows = bre.shape[1]

    @pl.when(c == 0)
    def _():
        ubuf[0:CONV_HIST, :] = jnp.zeros((CONV_HIST, dc), _F32)
        car_r[...] = jnp.zeros(car_r.shape, _F32)
        car_i[...] = jnp.zeros(car_i.shape, _F32)

    x = x_ref[0]
    mod = mod_ref[0]
    sh1, sc1, g1 = mod[:, 0:d], mod[:, d:2 * d], mod[:, 2 * d:3 * d]
    sh2, sc2 = mod[:, 3 * d:4 * d], mod[:, 4 * d:5 * d]
    hb = _bf(x * (1.0 + sc1) + sh1)

    u = _in_proj(hb, w_in, b_in, 0, dc) * _sigmoid(_in_proj(hb, w_in, b_in, dc, 2 * dc))
    ubuf[CONV_HIST:CONV_HIST + tl, :] = u
    span = ush.shape[1]
    for r in range(1, SUBLANES):
        ush[r - 1] = ubuf[r:r + span, :]

    n_blocks = tl // CONV_ROWS
    gate_lo = 2 * dc + dsm
    gate_w = 2 * d // n_blocks
    vblocks, zg_cols = [], []
    for blk in range(n_blocks):
        r0 = blk * CONV_ROWS
        acc = jnp.broadcast_to(b_dw[...], (CONV_ROWS, dc))
        for k in range(CONV_WIDTH):
            q, r = divmod(CONV_HIST - (CONV_WIDTH - 1) + k, SUBLANES)
            rows_k = slice(r0 + q * SUBLANES, r0 + q * SUBLANES + CONV_ROWS)
            tap = ubuf[rows_k, :] if r == 0 else ush[r - 1, rows_k, :]
            acc = acc + w_dw[k:k + 1, :] * tap
        vblocks.append(acc)
        zg_cols.append(_in_proj(hb, w_in, b_in, gate_lo + blk * gate_w, gate_lo + (blk + 1) * gate_w))
    ubuf[0:CONV_HIST, :] = ubuf[tl:tl + CONV_HIST, :]
    conv_out = _conv_tail(jnp.concatenate(vblocks, axis=0), cln_g, cln_b, w_pw, b_pw)

    zs = _in_proj(hb, w_in, b_in, 2 * dc, 2 * dc + dsm)
    zst = _bf(zs.T)
    tri = jnp.where(lax.broadcasted_iota(jnp.int32, (tl, tl), 0)
                    <= lax.broadcasted_iota(jnp.int32, (tl, tl), 1), 1.0, 0.0).astype(_BF16)
    def project(j):
        zj = zst[j * LANES:(j + 1) * LANES, :]
        return _dot(bre[j], zj), _dot(bim[j], zj)

    def scale_in(j, bu):
        r = slice(j * rows, (j + 1) * rows)
        eir, eii = einv_r[r, :], einv_i[r, :]
        return _bf(bu[0] * eir - bu[1] * eii), _bf(bu[0] * eii + bu[1] * eir)

    def prefix(v):
        return _dot(v[0], tri), _dot(v[1], tri)

    def scale_out(j, cum):
        r = slice(j * rows, (j + 1) * rows)
        cr = jnp.broadcast_to(car_r[r, LANES - 1:LANES], (rows, LANES))
        ci = jnp.broadcast_to(car_i[r, LANES - 1:LANES], (rows, LANES))
        lr, li = lamb_r[r, :], lamb_i[r, :]
        cum_r = cum[0] + jnp.concatenate([lr * cr - li * ci] * (tl // LANES), axis=1)
        cum_i = cum[1] + jnp.concatenate([lr * ci + li * cr] * (tl // LANES), axis=1)
        epr, epi = epow_r[r, :], epow_i[r, :]
        h_r = cum_r * epr - cum_i * epi
        h_i = cum_r * epi + cum_i * epr
        car_r[r, :] = h_r[:, tl - LANES:tl]
        car_i[r, :] = h_i[:, tl - LANES:tl]
        return jnp.concatenate([_bf(h_r), _bf(h_i)], axis=0)

    bu, v, cum, hcat, yts = {}, {}, {}, {}, {}
    for t in range(nj + 4):
        if t < nj:
            bu[t] = project(t)
        if 0 <= t - 2 < nj:
            cum[t - 2] = prefix(v.pop(t - 2))
        if 0 <= t - 4 < nj:
            yts[t - 4] = _dot(cc[t - 4], hcat.pop(t - 4))
        if 0 <= t - 1 < nj:
            v[t - 1] = scale_in(t - 1, bu.pop(t - 1))
        if 0 <= t - 3 < nj:
            hcat[t - 3] = scale_out(t - 3, cum.pop(t - 3))
    ssm_out = _ssm_tail(jnp.concatenate([yts[j] for j in range(nj)], axis=0).T, zs, dskip,
                        w_sv, b_sv, w_sg, b_sg)

    x1, h2, route = _mixer_tail(x, jnp.concatenate(zg_cols, axis=1), conv_out, ssm_out, g1, sh2, sc2, alpha,
                                w_out, b_out, ln1_g, ln1_b, wr_hi, wr_lo, b_r)
    x1_ref[0] = x1
    h2_ref[0] = _bf(h2)
    route_ref[0] = route
    cnt_ref[0, 0] = _block_counts(route)

    @pl.when(c == last)
    def _():
        cstate_ref[0] = ubuf[CONV_HIST + tl - (CONV_WIDTH - 1):CONV_HIST + tl, :]
        sre_ref[0] = car_r[...].T[LANES - 1:LANES, :]
        sim_ref[0] = car_i[...].T[LANES - 1:LANES, :]


def _mixer_prompt(x, mod, wts, alpha):
    b, l, d = x.shape
    tl = TOKEN_BLOCK
    nc = l // tl
    dc = wts["w_pw"].shape[0]
    nstate = wts["lamb_r"].shape[0]
    names = ["w_in", "b_in", "w_dw", "b_dw", "cln_g", "cln_b", "w_pw", "b_pw", "bre", "bim", "cc",
             "einv_r", "einv_i", "epow_r", "epow_i", "lamb_r", "lamb_i", "dskip",
             "w_sv", "b_sv", "w_sg", "b_sg", "w_out", "b_out", "ln1_g", "ln1_b",
             "wr_hi", "wr_lo", "b_r"]
    consts = [wts[n] for n in names]
    tok = lambda bi, ci: (bi, ci, 0)
    seq = lambda bi, ci: (bi, 0, 0)
    return pl.pallas_call(
        functools.partial(_mixer_prompt_kernel, alpha=alpha),
        grid=(b, nc),
        in_specs=[pl.BlockSpec((1, tl, d), tok), pl.BlockSpec((1, 1, mod.shape[-1]), seq)]
                 + [_const_spec(a.shape) for a in consts],
        out_specs=[pl.BlockSpec((1, tl, d), tok), pl.BlockSpec((1, tl, d), tok),
                   pl.BlockSpec((1, tl, LANES), tok),
                   pl.BlockSpec((1, 1, 1, LANES), lambda bi, ci: (bi, ci, 0, 0)),
                   pl.BlockSpec((1, CONV_WIDTH - 1, dc), seq),
                   pl.BlockSpec((1, 1, nstate), seq), pl.BlockSpec((1, 1, nstate), seq)],
        out_shape=[jax.ShapeDtypeStruct((b, l, d), _F32), jax.ShapeDtypeStruct((b, l, d), _BF16),
                   jax.ShapeDtypeStruct((b, l, LANES), _F32),
                   jax.ShapeDtypeStruct((b, nc, 1, LANES), _F32),
                   jax.ShapeDtypeStruct((b, CONV_WIDTH - 1, dc), _F32),
                   jax.ShapeDtypeStruct((b, 1, nstate), _F32),
                   jax.ShapeDtypeStruct((b, 1, nstate), _F32)],
        scratch_shapes=[pltpu.VMEM((CONV_HIST + tl, dc), _F32),
                        pltpu.VMEM((SUBLANES - 1, tl + CONV_HIST - SUBLANES, dc), _F32),
                        pltpu.VMEM((nstate, LANES), _F32), pltpu.VMEM((nstate, LANES), _F32)],
        compiler_params=_params(("arbitrary", "arbitrary")),
        name="mixer_prompt",
    )(x, mod, *consts)


def _mixer_sample_kernel(x_ref, mod_ref, hist_ref, h0r_ref, h0i_ref,
                         w_in, b_in, w_dw, b_dw, cln_g, cln_b, w_pw, b_pw,
                         bre, bim, cc, lrow_r, lrow_i, dskip,
                         w_sv, b_sv, w_sg, b_sg, w_out, b_out, ln1_g, ln1_b, wr_hi, wr_lo, b_r,
                         x1_ref, h2_ref, route_ref, cnt_ref, cstate_ref, sre_ref, sim_ref, *, alpha):
    steps, nb, d = x_ref.shape
    dc = w_pw.shape[0]
    dsm = dskip.shape[1]
    nj = bre.shape[0]
    rows = bre.shape[1]
    nhist = CONV_WIDTH - 1
    x = x_ref[...].reshape(steps * nb, d)
    mod = jnp.concatenate([mod_ref[...]] * steps, axis=0)
    sh1, sc1, g1 = mod[:, 0:d], mod[:, d:2 * d], mod[:, 2 * d:3 * d]
    sh2, sc2 = mod[:, 3 * d:4 * d], mod[:, 4 * d:5 * d]
    hb = _bf(x * (1.0 + sc1) + sh1)

    u = _in_proj(hb, w_in, b_in, 0, dc) * _sigmoid(_in_proj(hb, w_in, b_in, dc, 2 * dc))
    us = [u[t * nb:(t + 1) * nb, :] for t in range(steps)]
    vs = []
    for t in range(steps):
        acc = jnp.broadcast_to(b_dw[...], (nb, dc))
        for i in range(t, nhist):
            acc = acc + w_dw[i - t:i - t + 1, :] * hist_ref[i]
        for s in range(t + 1):
            acc = acc + w_dw[nhist + s - t:nhist + s - t + 1, :] * us[s]
        vs.append(acc)
    for i in range(nhist):
        src = i + steps
        cstate_ref[i] = hist_ref[src] if src < nhist else us[src - nhist]
    conv_out = _conv_tail(jnp.concatenate(vs, axis=0), cln_g, cln_b, w_pw, b_pw)

    zs = _in_proj(hb, w_in, b_in, 2 * dc, 2 * dc + dsm)
    zsb = _bf(zs)
    h_r, h_i = h0r_ref[...], h0i_ref[...]
    lr, li = lrow_r[...], lrow_i[...]
    ys = []
    for t in range(steps):
        zt = zsb[t * nb:(t + 1) * nb, :]
        bu_r = jnp.concatenate([_dot_nt(zt[:, j * LANES:(j + 1) * LANES], bre[j]) for j in range(nj)], axis=1)
        bu_i = jnp.concatenate([_dot_nt(zt[:, j * LANES:(j + 1) * LANES], bim[j]) for j in range(nj)], axis=1)
        h_r, h_i = lr * h_r - li * h_i + bu_r, lr * h_i + li * h_r + bu_i
        hrb, hib = _bf(h_r), _bf(h_i)
        ys.append(jnp.concatenate(
            [_dot_nt(jnp.concatenate([hrb[:, j * rows:(j + 1) * rows], hib[:, j * rows:(j + 1) * rows]], axis=1),
                     cc[j]) for j in range(nj)], axis=1))
    sre_ref[...] = h_r
    sim_ref[...] = h_i
    ssm_out = _ssm_tail(jnp.concatenate(ys, axis=0), zs, dskip, w_sv, b_sv, w_sg, b_sg)

    zg = _in_proj(hb, w_in, b_in, 2 * dc + dsm, 2 * dc + dsm + 2 * d)
    x1, h2, route = _mixer_tail(x, zg, conv_out, ssm_out, g1, sh2, sc2, alpha,
                                w_out, b_out, ln1_g, ln1_b, wr_hi, wr_lo, b_r)
    x1_ref[...] = x1
    h2_ref[...] = _bf(h2)
    route_ref[...] = route
    for blk in range(cnt_ref.shape[0]):
        cnt_ref[blk] = _block_counts(route[blk * TOKEN_BLOCK:(blk + 1) * TOKEN_BLOCK, :])


def _mixer_sample(x_tm, mod, hist_tm, h0r, h0i, wts, alpha):
    steps, nb, d = x_tm.shape
    n = steps * nb
    dc = wts["w_pw"].shape[0]
    nstate = h0r.shape[1]
    names = ["w_in", "b_in", "w_dw", "b_dw", "cln_g", "cln_b", "w_pw", "b_pw", "bre", "bim", "cc",
             "lrow_r", "lrow_i", "dskip", "w_sv", "b_sv", "w_sg", "b_sg", "w_out", "b_out",
             "ln1_g", "ln1_b", "wr_hi", "wr_lo", "b_r"]
    args = [x_tm, mod, hist_tm, h0r, h0i] + [wts[k] for k in names]
    return pl.pallas_call(
        functools.partial(_mixer_sample_kernel, alpha=alpha),
        grid=(1,),
        in_specs=[_const_spec(a.shape) for a in args],
        out_specs=[_const_spec((n, d)), _const_spec((n, d)), _const_spec((n, LANES)),
                   _const_spec((n // TOKEN_BLOCK, 1, LANES)),
                   _const_spec((CONV_WIDTH - 1, nb, dc)),
                   _const_spec((nb, nstate)), _const_spec((nb, nstate))],
        out_shape=[jax.ShapeDtypeStruct((n, d), _F32), jax.ShapeDtypeStruct((n, d), _BF16),
                   jax.ShapeDtypeStruct((n, LANES), _F32),
                   jax.ShapeDtypeStruct((n // TOKEN_BLOCK, 1, LANES), _F32),
                   jax.ShapeDtypeStruct((CONV_WIDTH - 1, nb, dc), _F32),
                   jax.ShapeDtypeStruct((nb, nstate), _F32), jax.ShapeDtypeStruct((nb, nstate), _F32)],
        compiler_params=_params(("arbitrary",)),
        name="mixer_sample",
    )(*args)


def _round_up(x, m):
    return (x + m - 1) // m * m


def _plan_sizes(n_tokens):
    nb = n_tokens // TOKEN_BLOCK
    r_max = _round_up(TOP_K * TOKEN_BLOCK + 2 * N_EXPERTS * (SEG_PAD - 1), 2 * LANES)
    rows_max = n_tokens * TOP_K + N_EXPERTS * (EXPERT_TILE - 1)
    nt_max = -(-rows_max // EXPERT_TILE)
    return nb, r_max, nt_max


def _routing_plan(cnt, nt_max):
    nb, ne = cnt.shape
    before = jnp.cumsum(cnt, axis=0) - cnt
    tot = jnp.sum(cnt, axis=0)
    cin = before % SEG_PAD
    last = (jnp.arange(nb, dtype=jnp.int32) == nb - 1)[:, None]
    active = (cnt > 0) | (last & (cin > 0))
    seg = jnp.where(active, _round_up(cin + cnt, SEG_PAD), 0)
    loc = jnp.cumsum(seg, axis=1) - seg
    totpad = _round_up(tot, EXPERT_TILE)
    eend = jnp.cumsum(totpad)
    estart = eend - totpad
    goff = estart[None, :] + before - cin
    nfull = jnp.where(last, seg, jnp.where(active, (cin + cnt) // SEG_PAD * SEG_PAD, 0))
    cout = jnp.where(last | ~active, 0, (cin + cnt) % SEG_PAD)
    n_tiles = (eend[-1] // EXPERT_TILE).astype(jnp.int32)
    tiles = jnp.minimum(jnp.arange(nt_max, dtype=jnp.int32), n_tiles - 1)
    te = jnp.sum(((eend // EXPERT_TILE)[None, :] <= tiles[:, None]).astype(jnp.int32), axis=1)
    te = jnp.minimum(te, ne - 1)
    tot16 = _round_up(tot, SEG_PAD)
    return dict(seg=seg, loc=loc, goff=goff, nfull=nfull, cin=jnp.where(active, cin, 0), cout=cout,
                n_tiles=n_tiles.reshape(1), te=te,
                npad=(totpad - tot16).astype(jnp.int32), pad_base=(estart + tot16).astype(jnp.int32))


def _dest_columns(route, loc_row):
    n = route.shape[0]
    lane_i = lax.broadcasted_iota(jnp.int32, (n, LANES), 1)
    lane = lane_i.astype(_F32)
    member = jnp.where((route > 0.0) & (lane_i < N_EXPERTS), 1.0, 0.0).astype(_BF16)
    strict = jnp.where(lax.broadcasted_iota(jnp.int32, (n, n), 1)
                       < lax.broadcasted_iota(jnp.int32, (n, n), 0), 1.0, 0.0).astype(_BF16)
    dest = _dot(strict, member) + loc_row
    d4 = jnp.zeros((n, LANES), _F32)
    for k in range(TOP_K):
        idx = route[:, ROUTE_IDX + k:ROUTE_IDX + k + 1]
        dk = jnp.sum(jnp.where(lane == idx, dest, 0.0), axis=-1, keepdims=True)
        d4 = d4 + jnp.where(lane_i == k, dk, 0.0)
    return d4


def _dispatch_kernel(nfull, sloc, goff, cin, cout, npad, pad_base, hp_ref, hs_ref, rp_ref, rs_ref, loc_ref,
                     xs_hbm, d4_ref, buf, stage, zbuf, sem, zsem, *, nbp):
    i = pl.program_id(0)
    nblk = pl.num_programs(0)
    slot = lax.rem(i, 2)
    tb = hp_ref.shape[0]
    r_max = buf.shape[1]
    is_p = i < nbp
    h = jnp.where(is_p, hp_ref[...], hs_ref[...])
    route = jnp.where(is_p, rp_ref[...], rs_ref[...])
    d4 = _dest_columns(route, loc_ref[0])
    d4_ref[...] = d4
    d4t = d4.T
    row = lax.broadcasted_iota(jnp.int32, (r_max, tb), 0).astype(_F32)
    p = jnp.zeros((r_max, tb), _F32)
    for k in range(TOP_K):
        p = p + jnp.where(row == d4t[k:k + 1, :], 1.0, 0.0)
    buf[slot] = _bf(_dot(_bf(p), h))

    def seg_copy(blk, s, e):
        n = pl.multiple_of(nfull[blk * N_EXPERTS + e], SEG_PAD)
        src = pl.multiple_of(sloc[blk * N_EXPERTS + e], SEG_PAD)
        dst = pl.multiple_of(goff[blk * N_EXPERTS + e], SEG_PAD)
        return pltpu.make_async_copy(buf.at[s, pl.ds(src, n), :], xs_hbm.at[pl.ds(dst, n), :], sem.at[s])

    def pad_copy(e):
        n = pl.multiple_of(npad[e], SEG_PAD)
        dst = pl.multiple_of(pad_base[e], SEG_PAD)
        return pltpu.make_async_copy(zbuf.at[pl.ds(0, n), :], xs_hbm.at[pl.ds(dst, n), :], zsem.at[0])

    def for_segments(blk, fn):
        for e in range(N_EXPERTS):
            @pl.when(nfull[blk * N_EXPERTS + e] > 0)
            def _(e=e):
                fn(e)

    @pl.when(i == 0)
    def _():
        stage[...] = jnp.zeros(stage.shape, stage.dtype)

    for e in range(N_EXPERTS):
        first = pl.ds(pl.multiple_of(sloc[i * N_EXPERTS + e], SEG_PAD), SEG_PAD)
        carried = jnp.where(cin[i * N_EXPERTS + e] > 0, stage[e], jnp.zeros_like(stage[e]))
        buf[slot, first, :] = buf[slot, first, :] + carried

    for_segments(i, lambda e: seg_copy(i, slot, e).start())

    for e in range(N_EXPERTS):
        rest = pl.multiple_of(sloc[i * N_EXPERTS + e] + nfull[i * N_EXPERTS + e], SEG_PAD)
        stage[e] = jnp.where(cout[i * N_EXPERTS + e] > 0, buf[slot, pl.ds(rest, SEG_PAD), :], stage[e])

    @pl.when(i > 0)
    def _():
        for_segments(i - 1, lambda e: seg_copy(i - 1, 1 - slot, e).wait())

    @pl.when(i == nblk - 1)
    def _():
        zbuf[...] = jnp.zeros(zbuf.shape, zbuf.dtype)
        for e in range(N_EXPERTS):
            @pl.when(npad[e] > 0)
            def _(e=e):
                pad_copy(e).start()
        for e in range(N_EXPERTS):
            @pl.when(npad[e] > 0)
            def _(e=e):
                pad_copy(e).wait()
        for_segments(i, lambda e: seg_copy(i, slot, e).wait())


def _dispatch(plan, h2p, h2s, rp, rs, r_max, nt_max):
    tb = TOKEN_BLOCK
    d = h2p.shape[1]
    nbp, nbs = h2p.shape[0] // tb, h2s.shape[0] // tb
    nb = nbp + nbs
    first_row = (plan["loc"] + plan["cin"]).astype(_F32)
    loc = jnp.pad(first_row, ((0, 0), (0, LANES - N_EXPERTS))).reshape(nb, 1, LANES)
    pidx = lambda i, *_: (jnp.minimum(i, nbp - 1), 0)
    sidx = lambda i, *_: (jnp.maximum(i - nbp, 0), 0)
    grid_spec = pltpu.PrefetchScalarGridSpec(
        num_scalar_prefetch=7,
        grid=(nb,),
        in_specs=[pl.BlockSpec((tb, d), pidx), pl.BlockSpec((tb, d), sidx),
                  pl.BlockSpec((tb, LANES), pidx), pl.BlockSpec((tb, LANES), sidx),
                  pl.BlockSpec((1, 1, LANES), lambda i, *_: (i, 0, 0))],
        out_specs=[pl.BlockSpec(memory_space=pl.ANY), pl.BlockSpec((tb, LANES), lambda i, *_: (i, 0))],
        scratch_shapes=[pltpu.VMEM((2, r_max, d), _BF16), pltpu.VMEM((N_EXPERTS, SEG_PAD, d), _BF16),
                        pltpu.VMEM((EXPERT_TILE, d), _BF16),
                        pltpu.SemaphoreType.DMA((2,)), pltpu.SemaphoreType.DMA((1,))])
    flat = lambda a: a.astype(jnp.int32).reshape(-1)
    return pl.pallas_call(
        functools.partial(_dispatch_kernel, nbp=nbp),
        grid_spec=grid_spec,
        out_shape=[jax.ShapeDtypeStruct((nt_max * EXPERT_TILE, d), _BF16),
                   jax.ShapeDtypeStruct((nb * tb, LANES), _F32)],
        compiler_params=_params(("arbitrary",)),
        name="dispatch",
    )(flat(plan["nfull"]), flat(plan["loc"]), flat(plan["goff"]), flat(plan["cin"]), flat(plan["cout"]),
      plan["npad"], plan["pad_base"], h2p, h2s, rp, rs, loc)


def _expert_kernel(te, n_tiles, x_ref, w1_ref, b1_ref, w2_ref, b2_ref, y_ref, w1b, w2b):
    i = pl.program_id(0)
    dff = w2_ref.shape[1]

    @pl.when(i < n_tiles[0])
    def _():
        prev = te[jnp.maximum(i - 1, 0)]

        @pl.when((i == 0) | (te[i] != prev))
        def _():
            w1b[...] = _bf(w1_ref[0])
            w2b[...] = _bf(w2_ref[0])

        gu = _dot(x_ref[...], w1b[...]) + b1_ref[0]
        g = jnp.minimum(gu[:, :dff], SWIGLU_LIMIT)
        up = jnp.clip(gu[:, dff:], -SWIGLU_LIMIT, SWIGLU_LIMIT)
        act = g * _sigmoid(SWIGLU_ALPHA * g) * (up + 1.0)
        y_ref[...] = _bf(_dot(_bf(act), w2b[...]) + b2_ref[0])


def _experts(plan, xs, w1, b1, w2, b2, nt_max):
    ne, d, dff2 = w1.shape
    dff = w2.shape[1]
    tm = EXPERT_TILE
    tile = lambda i, te, nt: (jnp.minimum(i, nt[0] - 1), 0)
    wsel = lambda i, te, nt: (te[i], 0, 0)
    grid_spec = pltpu.PrefetchScalarGridSpec(
        num_scalar_prefetch=2,
        grid=(nt_max,),
        in_specs=[pl.BlockSpec((tm, d), tile),
                  pl.BlockSpec((1, d, dff2), wsel), pl.BlockSpec((1, 1, dff2), wsel),
                  pl.BlockSpec((1, dff, d), wsel), pl.BlockSpec((1, 1, d), wsel)],
        out_specs=pl.BlockSpec((tm, d), tile),
        scratch_shapes=[pltpu.VMEM((d, dff2), _BF16), pltpu.VMEM((dff, d), _BF16)])
    return pl.pallas_call(
        _expert_kernel,
        grid_spec=grid_spec,
        out_shape=jax.ShapeDtypeStruct(xs.shape, _BF16),
        compiler_params=_params(("arbitrary",)),
        name="experts",
    )(plan["te"], plan["n_tiles"], xs, w1, b1.reshape(ne, 1, dff2), w2, b2.reshape(ne, 1, d))


def _combine_kernel(seg, sloc, goff, rp_ref, rs_ref, d4_ref, x1p_ref, x1s_ref, g2p_ref, g2s_ref,
                    ln2_g, ln2_b, ys_hbm, yp_ref, ysm_ref, buf, sem, *, nbp, alpha):
    i = pl.program_id(0)
    nblk = pl.num_programs(0)
    slot = lax.rem(i, 2)
    tb = rp_ref.shape[0]
    r_max = buf.shape[1]

    def seg_copy(blk, s, e):
        n = pl.multiple_of(seg[blk * N_EXPERTS + e], SEG_PAD)
        src = pl.multiple_of(goff[blk * N_EXPERTS + e], SEG_PAD)
        dst = pl.multiple_of(sloc[blk * N_EXPERTS + e], SEG_PAD)
        return pltpu.make_async_copy(ys_hbm.at[pl.ds(src, n), :], buf.at[s, pl.ds(dst, n), :], sem.at[s])

    def for_segments(blk, fn):
        for e in range(N_EXPERTS):
            @pl.when(seg[blk * N_EXPERTS + e] > 0)
            def _(e=e):
                fn(e)

    @pl.when(i == 0)
    def _():
        buf[...] = jnp.zeros(buf.shape, buf.dtype)
        for_segments(0, lambda e: seg_copy(0, 0, e).start())

    @pl.when(i + 1 < nblk)
    def _():
        for_segments(i + 1, lambda e: seg_copy(i + 1, 1 - slot, e).start())

    for_segments(i, lambda e: seg_copy(i, slot, e).wait())

    is_p = i < nbp
    route = jnp.where(is_p, rp_ref[...], rs_ref[...])
    d4 = d4_ref[...]
    col = lax.broadcasted_iota(jnp.int32, (tb, r_max), 1).astype(_F32)
    pg = jnp.zeros((tb, r_max), _F32)
    for k in range(TOP_K):
        gate = route[:, ROUTE_GATE + k:ROUTE_GATE + k + 1]
        pg = pg + jnp.where(col == d4[:, k:k + 1], gate, 0.0)
    f = _dot(_bf(pg), buf[slot])

    x1 = jnp.where(is_p, x1p_ref[...], x1s_ref[...])
    g2s = jnp.concatenate([g2s_ref[...]] * (tb // g2s_ref.shape[0]), axis=0)
    g2 = jnp.where(is_p, jnp.broadcast_to(g2p_ref[0], g2s.shape), g2s)
    y = _layer_norm(alpha * x1 + g2 * f, ln2_g[...], ln2_b[...])

    @pl.when(is_p)
    def _():
        yp_ref[...] = y

    @pl.when(jnp.logical_not(is_p))
    def _():
        ysm_ref[...] = y


def _combine(plan, ys, rp, rs, d4, x1p, x1s, modp, mods, ln2_g, ln2_b, r_max, alpha, blocks_per_seq):
    tb = TOKEN_BLOCK
    d = x1p.shape[1]
    nbp, nbs = x1p.shape[0] // tb, x1s.shape[0] // tb
    nb = nbp + nbs
    pidx = lambda i, *_: (jnp.minimum(i, nbp - 1), 0)
    sidx = lambda i, *_: (jnp.maximum(i - nbp, 0), 0)
    g2_lane_block = 5
    flat = lambda a: a.astype(jnp.int32).reshape(-1)
    grid_spec = pltpu.PrefetchScalarGridSpec(
        num_scalar_prefetch=3,
        grid=(nb,),
        in_specs=[pl.BlockSpec((tb, LANES), pidx), pl.BlockSpec((tb, LANES), sidx),
                  pl.BlockSpec((tb, LANES), lambda i, *_: (i, 0)),
                  pl.BlockSpec((tb, d), pidx), pl.BlockSpec((tb, d), sidx),
                  pl.BlockSpec((1, 1, d), lambda i, *_: (jnp.minimum(i, nbp - 1) // blocks_per_seq, 0,
                                                         g2_lane_block)),
                  pl.BlockSpec((mods.shape[0], d), lambda i, *_: (0, g2_lane_block)),
                  pl.BlockSpec((1, d), lambda i, *_: (0, 0)), pl.BlockSpec((1, d), lambda i, *_: (0, 0)),
                  pl.BlockSpec(memory_space=pl.ANY)],
        out_specs=[pl.BlockSpec((tb, d), pidx), pl.BlockSpec((tb, d), sidx)],
        scratch_shapes=[pltpu.VMEM((2, r_max, d), _BF16), pltpu.SemaphoreType.DMA((2,))])
    return pl.pallas_call(
        functools.partial(_combine_kernel, nbp=nbp, alpha=alpha),
        grid_spec=grid_spec,
        out_shape=[jax.ShapeDtypeStruct(x1p.shape, _F32), jax.ShapeDtypeStruct(x1s.shape, _F32)],
        compiler_params=_params(("arbitrary",)),
        name="combine",
    )(flat(plan["seg"]), flat(plan["loc"]), flat(plan["goff"]),
      rp, rs, d4, x1p, x1s, modp, mods, ln2_g, ln2_b, ys)


def _complex_powers(zr, zi, n):
    k = jnp.arange(n, dtype=jnp.int32)[None, :]
    pr = jnp.ones((zr.shape[0], n), _F32)
    pi = jnp.zeros((zr.shape[0], n), _F32)
    sr, si = zr[:, None], zi[:, None]
    bit = 1
    while bit < n:
        on = (k & bit) != 0
        mr, mi = jnp.where(on, sr, 1.0), jnp.where(on, si, 0.0)
        pr, pi = pr * mr - pi * mi, pr * mi + pi * mr
        sr, si = sr * sr - si * si, 2.0 * sr * si
        bit *= 2
    return pr, pi


def _block_diag(m):
    g, a, b = m.shape
    gb = GROUPS_PER_BLOCK
    m = m.reshape(g // gb, gb, a, 1, b)
    eye = jnp.eye(gb, dtype=m.dtype).reshape(1, gb, 1, gb, 1)
    return (m * eye).reshape(g // gb, gb * a, gb * b)


def _layer_weights(p, q):
    d = p["w_in"].shape[0]
    row = lambda v: v.reshape(1, -1).astype(_F32)
    lam_re, lam_im = p["lam_re"].astype(_F32), p["lam_im"].astype(_F32)
    dt = jnp.exp(p["log_dt"].astype(_F32))[:, None]
    mag = jnp.exp(lam_re * dt)
    lbr, lbi = mag * jnp.cos(lam_im * dt), mag * jnp.sin(lam_im * dt)
    den = lam_re * lam_re + lam_im * lam_im
    nr, ni = lbr - 1.0, lbi
    fr, fi = (nr * lam_re + ni * lam_im) / den, (ni * lam_re - nr * lam_im) / den
    b_re, b_im = p["b_re"].astype(_F32), p["b_im"].astype(_F32)
    bbr = fr[..., None] * b_re - fi[..., None] * b_im
    bbi = fr[..., None] * b_im + fi[..., None] * b_re
    nstate = lam_re.size
    mod2 = lbr * lbr + lbi * lbi
    flat = lambda v: v.reshape(nstate)
    epr, epi = _complex_powers(flat(lbr), flat(lbi), q)
    eir, eii = _complex_powers(flat(lbr / mod2), flat(-lbi / mod2), q)
    table = lambda t: t
    bcast = lambda v: jnp.broadcast_to(v.reshape(nstate, 1), (nstate, LANES))
    c_re, c_im = p["c_re"].astype(_F32), p["c_im"].astype(_F32)
    w_r = p["w_router"].astype(_F32)
    w_r = jnp.pad(w_r, ((0, 0), (0, LANES - w_r.shape[1])))
    wr_hi = _bf(w_r)
    return dict(
        w_in=_bf(p["w_in"]), b_in=row(p["b_in"]), w_dw=p["w_dw"].astype(_F32), b_dw=row(p["b_dw"]),
        cln_g=row(p["conv_ln_g"]), cln_b=row(p["conv_ln_b"]), w_pw=_bf(p["w_pw"]), b_pw=row(p["b_pw"]),
        bre=_bf(_block_diag(bbr)), bim=_bf(_block_diag(bbi)),
        cc=_bf(jnp.concatenate([_block_diag(c_re), _block_diag(-c_im)], axis=2)),
        einv_r=table(eir), einv_i=table(eii), epow_r=table(epr), epow_i=table(epi),
        lamb_r=bcast(lbr), lamb_i=bcast(lbi), lrow_r=lbr.reshape(1, nstate), lrow_i=lbi.reshape(1, nstate),
        dskip=row(p["d_skip"]),
        w_sv=_bf(p["w_sv"]), b_sv=row(p["b_sv"]), w_sg=_bf(p["w_sg"]), b_sg=row(p["b_sg"]),
        w_out=_bf(p["w_out"]), b_out=row(p["b_out"]), ln1_g=row(p["ln1_g"]), ln1_b=row(p["ln1_b"]),
        wr_hi=wr_hi, wr_lo=_bf(w_r - wr_hi.astype(_F32)),
        b_r=p["b_router"].astype(_F32).reshape(-1, 1),
        ln2_g=row(p["ln2_g"]), ln2_b=row(p["ln2_b"]))


def _layer(xp, xs_tm, c_all, hist_tm, h0r, h0i, p, alpha):
    b, l, d = xp.shape
    steps, nbs, _ = xs_tm.shape
    tb = TOKEN_BLOCK
    assert l % tb == 0 and l >= CONV_WIDTH - 1 and (steps * nbs) % tb == 0 and tb % nbs == 0
    wts = _layer_weights(p, tb)
    mod = _ada(c_all, p["w_ada"].astype(_F32), p["b_ada"].astype(_F32))
    modp, mods = mod[:b].reshape(b, 1, -1), mod[b:]

    x1p, h2p, rp, cntp, conv_p, sre_p, sim_p = _mixer_prompt(xp, modp, wts, alpha)
    x1s, h2s, rs, cnts, conv_s, sre_s, sim_s = _mixer_sample(xs_tm, mods, hist_tm, h0r, h0i, wts, alpha)

    n_tok = b * l + steps * nbs
    nb, r_max, nt_max = _plan_sizes(n_tok)
    cnt = jnp.concatenate([cntp.reshape(-1, LANES), cnts.reshape(-1, LANES)], axis=0)[:, :N_EXPERTS]
    plan = _routing_plan(cnt.astype(jnp.int32), nt_max)

    flat = lambda a: a.reshape(b * l, a.shape[-1])
    xs_sorted, d4 = _dispatch(plan, flat(h2p), h2s, flat(rp), rs, r_max, nt_max)
    ys_sorted = _experts(plan, xs_sorted, p["w1"], p["b1"], p["w2"], p["b2"], nt_max)
    yp, ysm = _combine(plan, ys_sorted, flat(rp), rs, d4, flat(x1p), x1s, modp, mods,
                       wts["ln2_g"], wts["ln2_b"], r_max, alpha, l // tb)
    return (yp.reshape(b, l, d), ysm.reshape(steps, nbs, d), conv_p, sre_p[:, 0], sim_p[:, 0],
            conv_s, sre_s, sim_s)


def kernel(x_prompt, x_sample, state_conv, state_ssm_re, state_ssm_im, c_prompt, c_sample, w_ada, b_ada, w_in, b_in, w_dw, b_dw, conv_ln_g, conv_ln_b, w_pw, b_pw, lam_re, lam_im, log_dt, b_re, b_im, c_re, c_im, d_skip, w_sv, b_sv, w_sg, b_sg, w_out, b_out, ln1_g, ln1_b, w_router, b_router, w1, b1, w2, b2, ln2_g, ln2_b):
    stacked = dict(w_ada=w_ada, b_ada=b_ada, w_in=w_in, b_in=b_in, w_dw=w_dw, b_dw=b_dw,
                   conv_ln_g=conv_ln_g, conv_ln_b=conv_ln_b, w_pw=w_pw, b_pw=b_pw, lam_re=lam_re,
                   lam_im=lam_im, log_dt=log_dt, b_re=b_re, b_im=b_im, c_re=c_re, c_im=c_im,
                   d_skip=d_skip, w_sv=w_sv, b_sv=b_sv, w_sg=w_sg, b_sg=b_sg, w_out=w_out, b_out=b_out,
                   ln1_g=ln1_g, ln1_b=ln1_b, w_router=w_router, b_router=b_router, w1=w1, b1=b1,
                   w2=w2, b2=b2, ln2_g=ln2_g, ln2_b=ln2_b)
    depth = w_ada.shape[0]
    alpha = (2 * depth) ** 0.25
    b = x_prompt.shape[0]
    nbs = x_sample.shape[0]
    g, s = state_ssm_re.shape[2], state_ssm_re.shape[3]
    xp = x_prompt
    xs_tm = jnp.transpose(x_sample, (1, 0, 2))
    c_all = jnp.concatenate([c_prompt, c_sample], axis=0)
    conv_ps, re_ps, im_ps, conv_ss, re_ss, im_ss = [], [], [], [], [], []
    for layer in range(depth):
        p = {k: v[layer] for k, v in stacked.items()}
        hist_tm = jnp.transpose(state_conv[layer], (1, 0, 2))
        h0r = state_ssm_re[layer].reshape(nbs, g * s)
        h0i = state_ssm_im[layer].reshape(nbs, g * s)
        xp, xs_tm, conv_p, sre_p, sim_p, conv_s, sre_s, sim_s = _layer(
            xp, xs_tm, c_all, hist_tm, h0r, h0i, p, alpha)
        conv_ps.append(conv_p.astype(state_conv.dtype))
        re_ps.append(sre_p.reshape(b, g, s))
        im_ps.append(sim_p.reshape(b, g, s))
        conv_ss.append(jnp.transpose(conv_s, (1, 0, 2)).astype(state_conv.dtype))
        re_ss.append(sre_s.reshape(nbs, g, s))
        im_ss.append(sim_s.reshape(nbs, g, s))
    return (xp, jnp.transpose(xs_tm, (1, 0, 2)), jnp.stack(conv_ps), jnp.stack(re_ps), jnp.stack(im_ps),
            jnp.stack(conv_ss), jnp.stack(re_ss), jnp.stack(im_ss))
```

```python
import functools

import jax
import jax.numpy as jnp
from jax import lax
from jax.experimental import pallas as pl
from jax.experimental.pallas import tpu as pltpu

CONV_WIDTH = 31
SSM_GROUP = 16
SSM_STATE = 64
N_EXPERTS = 32
TOP_K = 4
SWIGLU_LIMIT = 7.0
SWIGLU_ALPHA = 1.702
LN_EPS = 1e-5

LANES = 128
SUBLANES = 8
TOKEN_BLOCK = 256
EXPERT_TILE = 512
SEG_PAD = 16
GROUPS_PER_BLOCK = LANES // SSM_GROUP
CONV_HIST = 32
CONV_ROWS = 32
ROUTE_IDX = N_EXPERTS
ROUTE_GATE = N_EXPERTS + TOP_K
VMEM_LIMIT = 56 * 1024 * 1024

_F32 = jnp.float32
_BF16 = jnp.bfloat16


def _bf(x):
    return x.astype(_BF16)


def _dot(a, b):
    return jnp.dot(a, b, preferred_element_type=_F32)


def _dot_nt(a, b):
    return lax.dot_general(a, b, (((1,), (1,)), ((), ())), preferred_element_type=_F32)


def _split(x):
    hi = _bf(x)
    lo = _bf(x - hi.astype(_F32))
    return hi, lo


def _sigmoid(x):
    return 1.0 / (1.0 + jnp.exp(-x))


def _gelu_tanh(x):
    return 0.5 * x * (1.0 + jnp.tanh(0.7978845608028654 * (x + 0.044715 * (x * x * x))))


def _layer_norm(x, g, b):
    mu = jnp.mean(x, axis=-1, keepdims=True)
    xc = x - mu
    var = jnp.mean(xc * xc, axis=-1, keepdims=True)
    return xc * lax.rsqrt(var + LN_EPS) * g + b


def _const_spec(shape):
    nd = len(shape)
    return pl.BlockSpec(shape, lambda *_: (0,) * nd)


def _params(sem):
    return pltpu.CompilerParams(dimension_semantics=sem, vmem_limit_bytes=VMEM_LIMIT)


def _ada_kernel(c_ref, w_ref, b_ref, o_ref):
    c = c_ref[...]
    s_hi, s_lo = _split(c * _sigmoid(c))
    w_hi, w_lo = _split(w_ref[...])
    o_ref[...] = _dot(s_hi, w_hi) + _dot(s_lo, w_hi) + _dot(s_hi, w_lo) + b_ref[...]


def _ada(c, w, b):
    n, d = c.shape
    cols = w.shape[1]
    tn = d
    return pl.pallas_call(
        _ada_kernel,
        grid=(cols // tn,),
        in_specs=[pl.BlockSpec((n, d), lambda j: (0, 0)),
                  pl.BlockSpec((d, tn), lambda j: (0, j)),
                  pl.BlockSpec((1, tn), lambda j: (0, j))],
        out_specs=pl.BlockSpec((n, tn), lambda j: (0, j)),
        out_shape=jax.ShapeDtypeStruct((n, cols), _F32),
        compiler_params=_params(("arbitrary",)),
        name="ada",
    )(c, w, b.reshape(1, cols))


def _tiled(fn, tile, *args, vecs=()):
    n, m = args[0].shape
    tr, tc = min(tile[0], n), min(tile[1], m)
    rows_out = None
    for r0 in range(0, n, tr):
        cols_out = None
        for c0 in range(0, m, tc):
            res = fn(*[a[r0:r0 + tr, c0:c0 + tc] for a in args], *[v[:, c0:c0 + tc] for v in vecs])
            res = res if isinstance(res, tuple) else (res,)
            if cols_out is None:
                cols_out = [[] for _ in res]
            for acc, v in zip(cols_out, res):
                acc.append(v)
        row_vals = [c[0] if len(c) == 1 else jnp.concatenate(c, axis=1) for c in cols_out]
        if rows_out is None:
            rows_out = [[] for _ in row_vals]
        for acc, v in zip(rows_out, row_vals):
            acc.append(v)
    outs = [r[0] if len(r) == 1 else jnp.concatenate(r, axis=0) for r in rows_out]
    return outs[0] if len(outs) == 1 else tuple(outs)


def _in_proj(hb, w_in, b_in, lo, hi):
    return _dot(hb, w_in[:, lo:hi]) + b_in[:, lo:hi]


def _conv_tail(v, cln_g, cln_b, w_pw, b_pw):
    g, b = cln_g[...], cln_b[...]

    def norm_swish(vb):
        vb = _layer_norm(vb, g, b)
        return vb * _sigmoid(vb)
    v = _tiled(norm_swish, (SUBLANES, v.shape[1]), v)
    return _dot(_bf(v), w_pw[...]) + b_pw[...]


def _ssm_tail(y, zs, dskip, w_sv, b_sv, w_sg, b_sg):
    yg = _tiled(lambda yb, zb, db: _bf(_gelu_tanh(yb + db * zb)), (2 * SUBLANES, 4 * LANES), y, zs, vecs=(dskip,))
    sv, sg = _dot(yg, w_sv[...]), _dot(yg, w_sg[...])
    return _tiled(lambda a, b, bv, bg: (a + bv) * _sigmoid(b + bg), (SUBLANES, 4 * LANES),
                  sv, sg, vecs=(b_sv, b_sg))


def _route(h2, wr_hi, wr_lo, b_r):
    n = h2.shape[0]
    ne = b_r.shape[0]
    h_hi, h_lo = _split(h2)
    logits = (_dot(h_hi, wr_hi[...]) + _dot(h_lo, wr_hi[...]) + _dot(h_hi, wr_lo[...])).T[:ne, :] + b_r[...]
    row = lax.broadcasted_iota(jnp.int32, (ne, n), 0).astype(_F32)
    neg = jnp.float32(-jnp.inf)
    cur = logits
    vals, idxs = [], []
    for _ in range(TOP_K):
        m = jnp.max(cur, axis=0, keepdims=True)
        idx = jnp.min(jnp.where(cur == m, row, float(ne)), axis=0, keepdims=True)
        vals.append(m)
        idxs.append(idx)
        cur = jnp.where(row == idx, neg, cur)
    es = [jnp.exp(v - vals[0]) for v in vals]
    tot = es[0]
    for e in es[1:]:
        tot = tot + e
    inv = 1.0 / tot
    comb = jnp.zeros((ne, n), _F32)
    row8 = lax.broadcasted_iota(jnp.int32, (2 * TOP_K, n), 0)
    extra = jnp.zeros((2 * TOP_K, n), _F32)
    for k in range(TOP_K):
        gate = es[k] * inv
        comb = comb + jnp.where(row == idxs[k], gate, 0.0)
        extra = extra + jnp.where(row8 == k, idxs[k], 0.0) + jnp.where(row8 == TOP_K + k, gate, 0.0)
    rest = jnp.zeros((LANES - ne - 2 * TOP_K, n), _F32)
    return jnp.concatenate([comb, extra, rest], axis=0).T


def _block_counts(route):
    lane = lax.broadcasted_iota(jnp.int32, route.shape, 1)
    member = jnp.where((route > 0.0) & (lane < N_EXPERTS), 1.0, 0.0)
    return jnp.sum(member, axis=0, keepdims=True)


def _mixer_tail(x, zg, conv_out, ssm_out, g1, sh2, sc2, alpha, w_out, b_out,
                ln1_g, ln1_b, wr_hi, wr_lo, b_r):
    d = x.shape[1]
    merged = _tiled(lambda gc, gs, co, so: _bf(_sigmoid(gc) * co + _sigmoid(gs) * so),
                    (2 * SUBLANES, 4 * LANES), zg[:, :d], zg[:, d:], conv_out, ssm_out)
    m = _dot(merged, w_out[...])

    def norm_mod(xb, mb, gb, scb, shb, bo, lg, lb):
        x1b = _layer_norm(alpha * xb + gb * (mb + bo), lg, lb)
        return x1b, x1b * (1.0 + scb) + shb
    mods = (g1, sc2, sh2)
    per_seq = g1.shape[0] == 1
    x1, h2 = _tiled(norm_mod, (SUBLANES, d), x, m, *(() if per_seq else mods),
                    vecs=(mods if per_seq else ()) + (b_out, ln1_g, ln1_b))
    return x1, h2, _route(h2, wr_hi, wr_lo, b_r)


def _mixer_prompt_kernel(x_ref, mod_ref, w_in, b_in, w_dw, b_dw, cln_g, cln_b, w_pw, b_pw,
                         bre, bim, cc, einv_r, einv_i, epow_r, epow_i, lamb_r, lamb_i, dskip,
                         w_sv, b_sv, w_sg, b_sg, w_out, b_out, ln1_g, ln1_b, wr_hi, wr_lo, b_r,
                         x1_ref, h2_ref, route_ref, cnt_ref, cstate_ref, sre_ref, sim_ref,
                         ubuf, ush, car_r, car_i, *, alpha):
    c = pl.program_id(1)
    last = pl.num_programs(1) - 1
    tl, d = x_ref.shape[1], x_ref.shape[2]
    dc = w_pw.shape[0]
    dsm = dskip.shape[1]
    nj = bre.shape[0]
    rows = bre.shape[1]

    @pl.when(c == 0)
    def _():
        ubuf[0:CONV_HIST, :] = jnp.zeros((CONV_HIST, dc), _F32)
        car_r[...] = jnp.zeros(car_r.shape, _F32)
        car_i[...] = jnp.zeros(car_i.shape, _F32)

    x = x_ref[0]
    mod = mod_ref[0]
    sh1, sc1, g1 = mod[:, 0:d], mod[:, d:2 * d], mod[:, 2 * d:3 * d]
    sh2, sc2 = mod[:, 3 * d:4 * d], mod[:, 4 * d:5 * d]
    hb = _bf(x * (1.0 + sc1) + sh1)

    u = _in_proj(hb, w_in, b_in, 0, dc) * _sigmoid(_in_proj(hb, w_in, b_in, dc, 2 * dc))
    ubuf[CONV_HIST:CONV_HIST + tl, :] = u
    span = ush.shape[1]
    for r in range(1, SUBLANES):
        ush[r - 1] = ubuf[r:r + span, :]

    n_blocks = tl // CONV_ROWS
    gate_lo = 2 * dc + dsm
    gate_w = 2 * d // n_blocks
    vblocks, zg_cols = [], []
    for blk in range(n_blocks):
        r0 = blk * CONV_ROWS
        acc = jnp.broadcast_to(b_dw[...], (CONV_ROWS, dc))
        for k in range(CONV_WIDTH):
            q, r = divmod(CONV_HIST - (CONV_WIDTH - 1) + k, SUBLANES)
            rows_k = slice(r0 + q * SUBLANES, r0 + q * SUBLANES + CONV_ROWS)
            tap = ubuf[rows_k, :] if r == 0 else ush[r - 1, rows_k, :]
            acc = acc + w_dw[k:k + 1, :] * tap
        vblocks.append(acc)
        zg_cols.append(_in_proj(hb, w_in, b_in, gate_lo + blk * gate_w, gate_lo + (blk + 1) * gate_w))
    ubuf[0:CONV_HIST, :] = ubuf[tl:tl + CONV_HIST, :]
    conv_out = _conv_tail(jnp.concatenate(vblocks, axis=0), cln_g, cln_b, w_pw, b_pw)

    zs = _in_proj(hb, w_in, b_in, 2 * dc, 2 * dc + dsm)
    zst = _bf(zs.T)
    tri = jnp.where(lax.broadcasted_iota(jnp.int32, (tl, tl), 0)
                    <= lax.broadcasted_iota(jnp.int32, (tl, tl), 1), 1.0, 0.0).astype(_BF16)
    def project(j):
        zj = zst[j * LANES:(j + 1) * LANES, :]
        return _dot(bre[j], zj), _dot(bim[j], zj)

    def scale_in(j, bu):
        r = slice(j * rows, (j + 1) * rows)
        eir, eii = einv_r[r, :], einv_i[r, :]
        return _bf(bu[0] * eir - bu[1] * eii), _bf(bu[0] * eii + bu[1] * eir)

    def prefix(v):
        return _dot(v[0], tri), _dot(v[1], tri)

    def scale_out(j, cum):
        r = slice(j * rows, (j + 1) * rows)
        cr = jnp.broadcast_to(car_r[r, LANES - 1:LANES], (rows, LANES))
        ci = jnp.broadcast_to(car_i[r, LANES - 1:LANES], (rows, LANES))
        lr, li = lamb_r[r, :], lamb_i[r, :]
        cum_r = cum[0] + jnp.concatenate([lr * cr - li * ci] * (tl // LANES), axis=1)
        cum_i = cum[1] + jnp.concatenate([lr * ci + li * cr] * (tl // LANES), axis=1)
        epr, epi = epow_r[r, :], epow_i[r, :]
        h_r = cum_r * epr - cum_i * epi
        h_i = cum_r * epi + cum_i * epr
        car_r[r, :] = h_r[:, tl - LANES:tl]
        car_i[r, :] = h_i[:, tl - LANES:tl]
        return jnp.concatenate([_bf(h_r), _bf(h_i)], axis=0)

    bu, v, cum, hcat, yts = {}, {}, {}, {}, {}
    for t in range(nj + 4):
        if t < nj:
            bu[t] = project(t)
        if 0 <= t - 2 < nj:
            cum[t - 2] = prefix(v.pop(t - 2))
        if 0 <= t - 4 < nj:
            yts[t - 4] = _dot(cc[t - 4], hcat.pop(t - 4))
        if 0 <= t - 1 < nj:
            v[t - 1] = scale_in(t - 1, bu.pop(t - 1))
        if 0 <= t - 3 < nj:
            hcat[t - 3] = scale_out(t - 3, cum.pop(t - 3))
    ssm_out = _ssm_tail(jnp.concatenate([yts[j] for j in range(nj)], axis=0).T, zs, dskip,
                        w_sv, b_sv, w_sg, b_sg)

    x1, h2, route = _mixer_tail(x, jnp.concatenate(zg_cols, axis=1), conv_out, ssm_out, g1, sh2, sc2, alpha,
                                w_out, b_out, ln1_g, ln1_b, wr_hi, wr_lo, b_r)
    x1_ref[0] = x1
    h2_ref[0] = _bf(h2)
    route_ref[0] = route
    cnt_ref[0, 0] = _block_counts(route)

    @pl.when(c == last)
    def _():
        cstate_ref[0] = ubuf[CONV_HIST + tl - (CONV_WIDTH - 1):CONV_HIST + tl, :]
        sre_ref[0] = car_r[...].T[LANES - 1:LANES, :]
        sim_ref[0] = car_i[...].T[LANES - 1:LANES, :]


def _mixer_prompt(x, mod, wts, alpha):
    b, l, d = x.shape
    tl = TOKEN_BLOCK
    nc = l // tl
    dc = wts["w_pw"].shape[0]
    nstate = wts["lamb_r"].shape[0]
    names = ["w_in", "b_in", "w_dw", "b_dw", "cln_g", "cln_b", "w_pw", "b_pw", "bre", "bim", "cc",
             "einv_r", "einv_i", "epow_r", "epow_i", "lamb_r", "lamb_i", "dskip",
             "w_sv", "b_sv", "w_sg", "b_sg", "w_out", "b_out", "ln1_g", "ln1_b",
             "wr_hi", "wr_lo", "b_r"]
    consts = [wts[n] for n in names]
    tok = lambda bi, ci: (bi, ci, 0)
    seq = lambda bi, ci: (bi, 0, 0)
    return pl.pallas_call(
        functools.partial(_mixer_prompt_kernel, alpha=alpha),
        grid=(b, nc),
        in_specs=[pl.BlockSpec((1, tl, d), tok), pl.BlockSpec((1, 1, mod.shape[-1]), seq)]
                 + [_const_spec(a.shape) for a in consts],
        out_specs=[pl.BlockSpec((1, tl, d), tok), pl.BlockSpec((1, tl, d), tok),
                   pl.BlockSpec((1, tl, LANES), tok),
                   pl.BlockSpec((1, 1, 1, LANES), lambda bi, ci: (bi, ci, 0, 0)),
                   pl.BlockSpec((1, CONV_WIDTH - 1, dc), seq),
                   pl.BlockSpec((1, 1, nstate), seq), pl.BlockSpec((1, 1, nstate), seq)],
        out_shape=[jax.ShapeDtypeStruct((b, l, d), _F32), jax.ShapeDtypeStruct((b, l, d), _BF16),
                   jax.ShapeDtypeStruct((b, l, LANES), _F32),
                   jax.ShapeDtypeStruct((b, nc, 1, LANES), _F32),
                   jax.ShapeDtypeStruct((b, CONV_WIDTH - 1, dc), _F32),
                   jax.ShapeDtypeStruct((b, 1, nstate), _F32),
                   jax.ShapeDtypeStruct((b, 1, nstate), _F32)],
        scratch_shapes=[pltpu.VMEM((CONV_HIST + tl, dc), _F32),
                        pltpu.VMEM((SUBLANES - 1, tl + CONV_HIST - SUBLANES, dc), _F32),
                        pltpu.VMEM((nstate, LANES), _F32), pltpu.VMEM((nstate, LANES), _F32)],
        compiler_params=_params(("arbitrary", "arbitrary")),
        name="mixer_prompt",
    )(x, mod, *consts)


def _mixer_sample_kernel(x_ref, mod_ref, hist_ref, h0r_ref, h0i_ref,
                         w_in, b_in, w_dw, b_dw, cln_g, cln_b, w_pw, b_pw,
                         bre, bim, cc, lrow_r, lrow_i, dskip,
                         w_sv, b_sv, w_sg, b_sg, w_out, b_out, ln1_g, ln1_b, wr_hi, wr_lo, b_r,
                         x1_ref, h2_ref, route_ref, cnt_ref, cstate_ref, sre_ref, sim_ref, *, alpha):
    steps, nb, d = x_ref.shape
    dc = w_pw.shape[0]
    dsm = dskip.shape[1]
    nj = bre.shape[0]
    rows = bre.shape[1]
    nhist = CONV_WIDTH - 1
    x = x_ref[...].reshape(steps * nb, d)
    mod = jnp.concatenate([mod_ref[...]] * steps, axis=0)
    sh1, sc1, g1 = mod[:, 0:d], mod[:, d:2 * d], mod[:, 2 * d:3 * d]
    sh2, sc2 = mod[:, 3 * d:4 * d], mod[:, 4 * d:5 * d]
    hb = _bf(x * (1.0 + sc1) + sh1)

    u = _in_proj(hb, w_in, b_in, 0, dc) * _sigmoid(_in_proj(hb, w_in, b_in, dc, 2 * dc))
    us = [u[t * nb:(t + 1) * nb, :] for t in range(steps)]
    vs = []
    for t in range(steps):
        acc = jnp.broadcast_to(b_dw[...], (nb, dc))
        for i in range(t, nhist):
            acc = acc + w_dw[i - t:i - t + 1, :] * hist_ref[i]
        for s in range(t + 1):
            acc = acc + w_dw[nhist + s - t:nhist + s - t + 1, :] * us[s]
        vs.append(acc)
    for i in range(nhist):
        src = i + steps
        cstate_ref[i] = hist_ref[src] if src < nhist else us[src - nhist]
    conv_out = _conv_tail(jnp.concatenate(vs, axis=0), cln_g, cln_b, w_pw, b_pw)

    zs = _in_proj(hb, w_in, b_in, 2 * dc, 2 * dc + dsm)
    zsb = _bf(zs)
    h_r, h_i = h0r_ref[...], h0i_ref[...]
    lr, li = lrow_r[...], lrow_i[...]
    ys = []
    for t in range(steps):
        zt = zsb[t * nb:(t + 1) * nb, :]
        bu_r = jnp.concatenate([_dot_nt(zt[:, j * LANES:(j + 1) * LANES], bre[j]) for j in range(nj)], axis=1)
        bu_i = jnp.concatenate([_dot_nt(zt[:, j * LANES:(j + 1) * LANES], bim[j]) for j in range(nj)], axis=1)
        h_r, h_i = lr * h_r - li * h_i + bu_r, lr * h_i + li * h_r + bu_i
        hrb, hib = _bf(h_r), _bf(h_i)
        ys.append(jnp.concatenate(
            [_dot_nt(jnp.concatenate([hrb[:, j * rows:(j + 1) * rows], hib[:, j * rows:(j + 1) * rows]], axis=1),
                     cc[j]) for j in range(nj)], axis=1))
    sre_ref[...] = h_r
    sim_ref[...] = h_i
    ssm_out = _ssm_tail(jnp.concatenate(ys, axis=0), zs, dskip, w_sv, b_sv, w_sg, b_sg)

    zg = _in_proj(hb, w_in, b_in, 2 * dc + dsm, 2 * dc + dsm + 2 * d)
    x1, h2, route = _mixer_tail(x, zg, conv_out, ssm_out, g1, sh2, sc2, alpha,
                                w_out, b_out, ln1_g, ln1_b, wr_hi, wr_lo, b_r)
    x1_ref[...] = x1
    h2_ref[...] = _bf(h2)
    route_ref[...] = route
    for blk in range(cnt_ref.shape[0]):
        cnt_ref[blk] = _block_counts(route[blk * TOKEN_BLOCK:(blk + 1) * TOKEN_BLOCK, :])


def _mixer_sample(x_tm, mod, hist_tm, h0r, h0i, wts, alpha):
    steps, nb, d = x_tm.shape
    n = steps * nb
    dc = wts["w_pw"].shape[0]
    nstate = h0r.shape[1]
    names = ["w_in", "b_in", "w_dw", "b_dw", "cln_g", "cln_b", "w_pw", "b_pw", "bre", "bim", "cc",
             "lrow_r", "lrow_i", "dskip", "w_sv", "b_sv", "w_sg", "b_sg", "w_out", "b_out",
             "ln1_g", "ln1_b", "wr_hi", "wr_lo", "b_r"]
    args = [x_tm, mod, hist_tm, h0r, h0i] + [wts[k] for k in names]
    return pl.pallas_call(
        functools.partial(_mixer_sample_kernel, alpha=alpha),
        grid=(1,),
        in_specs=[_const_spec(a.shape) for a in args],
        out_specs=[_const_spec((n, d)), _const_spec((n, d)), _const_spec((n, LANES)),
                   _const_spec((n // TOKEN_BLOCK, 1, LANES)),
                   _const_spec((CONV_WIDTH - 1, nb, dc)),
                   _const_spec((nb, nstate)), _const_spec((nb, nstate))],
        out_shape=[jax.ShapeDtypeStruct((n, d), _F32), jax.ShapeDtypeStruct((n, d), _BF16),
                   jax.ShapeDtypeStruct((n, LANES), _F32),
                   jax.ShapeDtypeStruct((n // TOKEN_BLOCK, 1, LANES), _F32),
                   jax.ShapeDtypeStruct((CONV_WIDTH - 1, nb, dc), _F32),
                   jax.ShapeDtypeStruct((nb, nstate), _F32), jax.ShapeDtypeStruct((nb, nstate), _F32)],
        compiler_params=_params(("arbitrary",)),
        name="mixer_sample",
    )(*args)


def _round_up(x, m):
    return (x + m - 1) // m * m


def _plan_sizes(n_tokens):
    nb = n_tokens // TOKEN_BLOCK
    r_max = _round_up(TOP_K * TOKEN_BLOCK + 2 * N_EXPERTS * (SEG_PAD - 1), 2 * LANES)
    rows_max = n_tokens * TOP_K + N_EXPERTS * (EXPERT_TILE - 1)
    nt_max = -(-rows_max // EXPERT_TILE)
    return nb, r_max, nt_max


def _routing_plan(cnt, nt_max):
    nb, ne = cnt.shape
    before = jnp.cumsum(cnt, axis=0) - cnt
    tot = jnp.sum(cnt, axis=0)
    cin = before % SEG_PAD
    last = (jnp.arange(nb, dtype=jnp.int32) == nb - 1)[:, None]
    active = (cnt > 0) | (last & (cin > 0))
    seg = jnp.where(active, _round_up(cin + cnt, SEG_PAD), 0)
    loc = jnp.cumsum(seg, axis=1) - seg
    totpad = _round_up(tot, EXPERT_TILE)
    eend = jnp.cumsum(totpad)
    estart = eend - totpad
    goff = estart[None, :] + before - cin
    nfull = jnp.where(last, seg, jnp.where(active, (cin + cnt) // SEG_PAD * SEG_PAD, 0))
    cout = jnp.where(last | ~active, 0, (cin + cnt) % SEG_PAD)
    n_tiles = (eend[-1] // EXPERT_TILE).astype(jnp.int32)
    tiles = jnp.minimum(jnp.arange(nt_max, dtype=jnp.int32), n_tiles - 1)
    te = jnp.sum(((eend // EXPERT_TILE)[None, :] <= tiles[:, None]).astype(jnp.int32), axis=1)
    te = jnp.minimum(te, ne - 1)
    tot16 = _round_up(tot, SEG_PAD)
    return dict(seg=seg, loc=loc, goff=goff, nfull=nfull, cin=jnp.where(active, cin, 0), cout=cout,
                used=jnp.sum(seg, axis=1).astype(jnp.int32),
                n_tiles=n_tiles.reshape(1), te=te,
                npad=(totpad - tot16).astype(jnp.int32), pad_base=(estart + tot16).astype(jnp.int32))


def _dest_columns(route, loc_row):
    n = route.shape[0]
    lane_i = lax.broadcasted_iota(jnp.int32, (n, LANES), 1)
    lane = lane_i.astype(_F32)
    member = jnp.where((route > 0.0) & (lane_i < N_EXPERTS), 1.0, 0.0).astype(_BF16)
    strict = jnp.where(lax.broadcasted_iota(jnp.int32, (n, n), 1)
                       < lax.broadcasted_iota(jnp.int32, (n, n), 0), 1.0, 0.0).astype(_BF16)
    dest = _dot(strict, member) + loc_row
    d4 = jnp.zeros((n, LANES), _F32)
    for k in range(TOP_K):
        idx = route[:, ROUTE_IDX + k:ROUTE_IDX + k + 1]
        dk = jnp.sum(jnp.where(lane == idx, dest, 0.0), axis=-1, keepdims=True)
        d4 = d4 + jnp.where(lane_i == k, dk, 0.0)
    return d4


def _typical_rows(r_max):
    return min(r_max, _round_up(TOP_K * TOKEN_BLOCK + N_EXPERTS * SEG_PAD, 2 * LANES))


def _dispatch_kernel(nfull, sloc, goff, cin, cout, used, npad, pad_base, hp_ref, hs_ref, rp_ref, rs_ref, loc_ref,
                     xs_hbm, d4_ref, buf, stage, zbuf, sem, zsem, *, nbp):
    i = pl.program_id(0)
    nblk = pl.num_programs(0)
    slot = lax.rem(i, 2)
    tb = hp_ref.shape[0]
    r_max = buf.shape[1]
    is_p = i < nbp
    h = jnp.where(is_p, hp_ref[...], hs_ref[...])
    route = jnp.where(is_p, rp_ref[...], rs_ref[...])
    d4 = _dest_columns(route, loc_ref[0])
    d4_ref[...] = d4
    d4t = d4.T

    def compact(rows):
        row = lax.broadcasted_iota(jnp.int32, (rows, tb), 0).astype(_F32)
        p = jnp.zeros((rows, tb), _F32)
        for k in range(TOP_K):
            p = p + jnp.where(row == d4t[k:k + 1, :], 1.0, 0.0)
        buf[slot, 0:rows, :] = _bf(_dot(_bf(p), h))

    r_typ = _typical_rows(r_max)

    @pl.when(i == 0)
    def _():
        buf[...] = jnp.zeros(buf.shape, buf.dtype)
        stage[...] = jnp.zeros(stage.shape, stage.dtype)

    @pl.when(used[i] <= r_typ)
    def _():
        compact(r_typ)

    @pl.when(used[i] > r_typ)
    def _():
        compact(r_max)

    def seg_copy(blk, s, e):
        n = pl.multiple_of(nfull[blk * N_EXPERTS + e], SEG_PAD)
        src = pl.multiple_of(sloc[blk * N_EXPERTS + e], SEG_PAD)
        dst = pl.multiple_of(goff[blk * N_EXPERTS + e], SEG_PAD)
        return pltpu.make_async_copy(buf.at[s, pl.ds(src, n), :], xs_hbm.at[pl.ds(dst, n), :], sem.at[s])

    def pad_copy(e):
        n = pl.multiple_of(npad[e], SEG_PAD)
        dst = pl.multiple_of(pad_base[e], SEG_PAD)
        return pltpu.make_async_copy(zbuf.at[pl.ds(0, n), :], xs_hbm.at[pl.ds(dst, n), :], zsem.at[0])

    def for_segments(blk, fn):
        for e in range(N_EXPERTS):
            @pl.when(nfull[blk * N_EXPERTS + e] > 0)
            def _(e=e):
                fn(e)

    for e in range(N_EXPERTS):
        first = pl.ds(pl.multiple_of(sloc[i * N_EXPERTS + e], SEG_PAD), SEG_PAD)
        carried = jnp.where(cin[i * N_EXPERTS + e] > 0, stage[e], jnp.zeros_like(stage[e]))
        buf[slot, first, :] = buf[slot, first, :] + carried

    for_segments(i, lambda e: seg_copy(i, slot, e).start())

    for e in range(N_EXPERTS):
        rest = pl.multiple_of(sloc[i * N_EXPERTS + e] + nfull[i * N_EXPERTS + e], SEG_PAD)
        stage[e] = jnp.where(cout[i * N_EXPERTS + e] > 0, buf[slot, pl.ds(rest, SEG_PAD), :], stage[e])

    @pl.when(i > 0)
    def _():
        for_segments(i - 1, lambda e: seg_copy(i - 1, 1 - slot, e).wait())

    @pl.when(i == nblk - 1)
    def _():
        zbuf[...] = jnp.zeros(zbuf.shape, zbuf.dtype)
        for e in range(N_EXPERTS):
            @pl.when(npad[e] > 0)
            def _(e=e):
                pad_copy(e).start()
        for e in range(N_EXPERTS):
            @pl.when(npad[e] > 0)
            def _(e=e):
                pad_copy(e).wait()
        for_segments(i, lambda e: seg_copy(i, slot, e).wait())


def _dispatch(plan, h2p, h2s, rp, rs, r_max, nt_max):
    tb = TOKEN_BLOCK
    d = h2p.shape[1]
    nbp, nbs = h2p.shape[0] // tb, h2s.shape[0] // tb
    nb = nbp + nbs
    first_row = (plan["loc"] + plan["cin"]).astype(_F32)
    loc = jnp.pad(first_row, ((0, 0), (0, LANES - N_EXPERTS))).reshape(nb, 1, LANES)
    pidx = lambda i, *_: (jnp.minimum(i, nbp - 1), 0)
    sidx = lambda i, *_: (jnp.maximum(i - nbp, 0), 0)
    grid_spec = pltpu.PrefetchScalarGridSpec(
        num_scalar_prefetch=8,
        grid=(nb,),
        in_specs=[pl.BlockSpec((tb, d), pidx), pl.BlockSpec((tb, d), sidx),
                  pl.BlockSpec((tb, LANES), pidx), pl.BlockSpec((tb, LANES), sidx),
                  pl.BlockSpec((1, 1, LANES), lambda i, *_: (i, 0, 0))],
        out_specs=[pl.BlockSpec(memory_space=pl.ANY), pl.BlockSpec((tb, LANES), lambda i, *_: (i, 0))],
        scratch_shapes=[pltpu.VMEM((2, r_max, d), _BF16), pltpu.VMEM((N_EXPERTS, SEG_PAD, d), _BF16),
                        pltpu.VMEM((EXPERT_TILE, d), _BF16),
                        pltpu.SemaphoreType.DMA((2,)), pltpu.SemaphoreType.DMA((1,))])
    flat = lambda a: a.astype(jnp.int32).reshape(-1)
    return pl.pallas_call(
        functools.partial(_dispatch_kernel, nbp=nbp),
        grid_spec=grid_spec,
        out_shape=[jax.ShapeDtypeStruct((nt_max * EXPERT_TILE, d), _BF16),
                   jax.ShapeDtypeStruct((nb * tb, LANES), _F32)],
        compiler_params=_params(("arbitrary",)),
        name="dispatch",
    )(flat(plan["nfull"]), flat(plan["loc"]), flat(plan["goff"]), flat(plan["cin"]), flat(plan["cout"]),
      plan["used"], plan["npad"], plan["pad_base"], h2p, h2s, rp, rs, loc)


def _expert_kernel(te, n_tiles, x_ref, w1_ref, b1_ref, w2_ref, b2_ref, y_ref, w1b, w2b):
    i = pl.program_id(0)
    dff = w2_ref.shape[1]

    @pl.when(i < n_tiles[0])
    def _():
        prev = te[jnp.maximum(i - 1, 0)]

        @pl.when((i == 0) | (te[i] != prev))
        def _():
            w1b[...] = _bf(w1_ref[0])
            w2b[...] = _bf(w2_ref[0])

        gu = _dot(x_ref[...], w1b[...]) + b1_ref[0]
        g = jnp.minimum(gu[:, :dff], SWIGLU_LIMIT)
        up = jnp.clip(gu[:, dff:], -SWIGLU_LIMIT, SWIGLU_LIMIT)
        act = g * _sigmoid(SWIGLU_ALPHA * g) * (up + 1.0)
        y_ref[...] = _bf(_dot(_bf(act), w2b[...]) + b2_ref[0])


def _experts(plan, xs, w1, b1, w2, b2, nt_max):
    ne, d, dff2 = w1.shape
    dff = w2.shape[1]
    tm = EXPERT_TILE
    tile = lambda i, te, nt: (jnp.minimum(i, nt[0] - 1), 0)
    wsel = lambda i, te, nt: (te[i], 0, 0)
    grid_spec = pltpu.PrefetchScalarGridSpec(
        num_scalar_prefetch=2,
        grid=(nt_max,),
        in_specs=[pl.BlockSpec((tm, d), tile),
                  pl.BlockSpec((1, d, dff2), wsel), pl.BlockSpec((1, 1, dff2), wsel),
                  pl.BlockSpec((1, dff, d), wsel), pl.BlockSpec((1, 1, d), wsel)],
        out_specs=pl.BlockSpec((tm, d), tile),
        scratch_shapes=[pltpu.VMEM((d, dff2), _BF16), pltpu.VMEM((dff, d), _BF16)])
    return pl.pallas_call(
        _expert_kernel,
        grid_spec=grid_spec,
        out_shape=jax.ShapeDtypeStruct(xs.shape, _BF16),
        compiler_params=_params(("arbitrary",)),
        name="experts",
    )(plan["te"], plan["n_tiles"], xs, w1, b1.reshape(ne, 1, dff2), w2, b2.reshape(ne, 1, d))


def _combine_kernel(seg, sloc, goff, used, rp_ref, rs_ref, d4_ref, x1p_ref, x1s_ref, g2p_ref, g2s_ref,
                    ln2_g, ln2_b, ys_hbm, yp_ref, ysm_ref, buf, fbuf, sem, *, nbp, alpha):
    i = pl.program_id(0)
    nblk = pl.num_programs(0)
    slot = lax.rem(i, 2)
    tb = rp_ref.shape[0]
    r_max = buf.shape[1]

    def seg_copy(blk, s, e):
        n = pl.multiple_of(seg[blk * N_EXPERTS + e], SEG_PAD)
        src = pl.multiple_of(goff[blk * N_EXPERTS + e], SEG_PAD)
        dst = pl.multiple_of(sloc[blk * N_EXPERTS + e], SEG_PAD)
        return pltpu.make_async_copy(ys_hbm.at[pl.ds(src, n), :], buf.at[s, pl.ds(dst, n), :], sem.at[s])

    def for_segments(blk, fn):
        for e in range(N_EXPERTS):
            @pl.when(seg[blk * N_EXPERTS + e] > 0)
            def _(e=e):
                fn(e)

    @pl.when(i == 0)
    def _():
        buf[...] = jnp.zeros(buf.shape, buf.dtype)
        for_segments(0, lambda e: seg_copy(0, 0, e).start())

    @pl.when(i + 1 < nblk)
    def _():
        for_segments(i + 1, lambda e: seg_copy(i + 1, 1 - slot, e).start())

    for_segments(i, lambda e: seg_copy(i, slot, e).wait())

    is_p = i < nbp
    route = jnp.where(is_p, rp_ref[...], rs_ref[...])
    d4 = d4_ref[...]

    def gather(rows):
        col = lax.broadcasted_iota(jnp.int32, (tb, rows), 1).astype(_F32)
        pg = jnp.zeros((tb, rows), _F32)
        for k in range(TOP_K):
            gate = route[:, ROUTE_GATE + k:ROUTE_GATE + k + 1]
            pg = pg + jnp.where(col == d4[:, k:k + 1], gate, 0.0)
        fbuf[...] = _dot(_bf(pg), buf[slot, 0:rows, :])

    r_typ = _typical_rows(r_max)

    @pl.when(used[i] <= r_typ)
    def _():
        gather(r_typ)

    @pl.when(used[i] > r_typ)
    def _():
        gather(r_max)

    f = fbuf[...]
    x1 = jnp.where(is_p, x1p_ref[...], x1s_ref[...])
    g2s = jnp.concatenate([g2s_ref[...]] * (tb // g2s_ref.shape[0]), axis=0)
    g2 = jnp.where(is_p, jnp.broadcast_to(g2p_ref[0], g2s.shape), g2s)
    y = _layer_norm(alpha * x1 + g2 * f, ln2_g[...], ln2_b[...])

    @pl.when(is_p)
    def _():
        yp_ref[...] = y

    @pl.when(jnp.logical_not(is_p))
    def _():
        ysm_ref[...] = y


def _combine(plan, ys, rp, rs, d4, x1p, x1s, modp, mods, ln2_g, ln2_b, r_max, alpha, blocks_per_seq):
    tb = TOKEN_BLOCK
    d = x1p.shape[1]
    nbp, nbs = x1p.shape[0] // tb, x1s.shape[0] // tb
    nb = nbp + nbs
    pidx = lambda i, *_: (jnp.minimum(i, nbp - 1), 0)
    sidx = lambda i, *_: (jnp.maximum(i - nbp, 0), 0)
    g2_lane_block = 5
    flat = lambda a: a.astype(jnp.int32).reshape(-1)
    grid_spec = pltpu.PrefetchScalarGridSpec(
        num_scalar_prefetch=4,
        grid=(nb,),
        in_specs=[pl.BlockSpec((tb, LANES), pidx), pl.BlockSpec((tb, LANES), sidx),
                  pl.BlockSpec((tb, LANES), lambda i, *_: (i, 0)),
                  pl.BlockSpec((tb, d), pidx), pl.BlockSpec((tb, d), sidx),
                  pl.BlockSpec((1, 1, d), lambda i, *_: (jnp.minimum(i, nbp - 1) // blocks_per_seq, 0,
                                                         g2_lane_block)),
                  pl.BlockSpec((mods.shape[0], d), lambda i, *_: (0, g2_lane_block)),
                  pl.BlockSpec((1, d), lambda i, *_: (0, 0)), pl.BlockSpec((1, d), lambda i, *_: (0, 0)),
                  pl.BlockSpec(memory_space=pl.ANY)],
        out_specs=[pl.BlockSpec((tb, d), pidx), pl.BlockSpec((tb, d), sidx)],
        scratch_shapes=[pltpu.VMEM((2, r_max, d), _BF16), pltpu.VMEM((tb, d), _F32),
                        pltpu.SemaphoreType.DMA((2,))])
    return pl.pallas_call(
        functools.partial(_combine_kernel, nbp=nbp, alpha=alpha),
        grid_spec=grid_spec,
        out_shape=[jax.ShapeDtypeStruct(x1p.shape, _F32), jax.ShapeDtypeStruct(x1s.shape, _F32)],
        compiler_params=_params(("arbitrary",)),
        name="combine",
    )(flat(plan["seg"]), flat(plan["loc"]), flat(plan["goff"]), plan["used"],
      rp, rs, d4, x1p, x1s, modp, mods, ln2_g, ln2_b, ys)


def _complex_powers(zr, zi, n):
    low = 16
    if n <= low or n % low:
        return _bit_powers(zr, zi, n)
    lr, li = _bit_powers(zr, zi, low)
    sr, si = zr, zi
    for _ in range(4):
        sr, si = sr * sr - si * si, 2.0 * sr * si
    hr, hi = _bit_powers(sr, si, n // low)
    pr = hr[:, :, None] * lr[:, None, :] - hi[:, :, None] * li[:, None, :]
    pi = hr[:, :, None] * li[:, None, :] + hi[:, :, None] * lr[:, None, :]
    return pr.reshape(-1, n), pi.reshape(-1, n)


def _bit_powers(zr, zi, n):
    k = jnp.arange(n, dtype=jnp.int32)[None, :]
    pr = jnp.ones((zr.shape[0], n), _F32)
    pi = jnp.zeros((zr.shape[0], n), _F32)
    sr, si = zr[:, None], zi[:, None]
    bit = 1
    while bit < n:
        on = (k & bit) != 0
        mr, mi = jnp.where(on, sr, 1.0), jnp.where(on, si, 0.0)
        pr, pi = pr * mr - pi * mi, pr * mi + pi * mr
        sr, si = sr * sr - si * si, 2.0 * sr * si
        bit *= 2
    return pr, pi


def _block_diag(m):
    g, a, b = m.shape
    gb = GROUPS_PER_BLOCK
    m = m.reshape(g // gb, gb, a, 1, b)
    eye = jnp.eye(gb, dtype=m.dtype).reshape(1, gb, 1, gb, 1)
    return (m * eye).reshape(g // gb, gb * a, gb * b)


def _layer_weights(p, q):
    d = p["w_in"].shape[0]
    row = lambda v: v.reshape(1, -1).astype(_F32)
    lam_re, lam_im = p["lam_re"].astype(_F32), p["lam_im"].astype(_F32)
    dt = jnp.exp(p["log_dt"].astype(_F32))[:, None]
    mag = jnp.exp(lam_re * dt)
    lbr, lbi = mag * jnp.cos(lam_im * dt), mag * jnp.sin(lam_im * dt)
    den = lam_re * lam_re + lam_im * lam_im
    nr, ni = lbr - 1.0, lbi
    fr, fi = (nr * lam_re + ni * lam_im) / den, (ni * lam_re - nr * lam_im) / den
    b_re, b_im = p["b_re"].astype(_F32), p["b_im"].astype(_F32)
    bbr = fr[..., None] * b_re - fi[..., None] * b_im
    bbi = fr[..., None] * b_im + fi[..., None] * b_re
    nstate = lam_re.size
    mod2 = lbr * lbr + lbi * lbi
    flat = lambda v: v.reshape(nstate)
    epr, epi = _complex_powers(flat(lbr), flat(lbi), q)
    eir, eii = _complex_powers(flat(lbr / mod2), flat(-lbi / mod2), q)
    table = lambda t: t
    bcast = lambda v: jnp.broadcast_to(v.reshape(nstate, 1), (nstate, LANES))
    c_re, c_im = p["c_re"].astype(_F32), p["c_im"].astype(_F32)
    w_r = p["w_router"].astype(_F32)
    w_r = jnp.pad(w_r, ((0, 0), (0, LANES - w_r.shape[1])))
    wr_hi = _bf(w_r)
    return dict(
        w_in=_bf(p["w_in"]), b_in=row(p["b_in"]), w_dw=p["w_dw"].astype(_F32), b_dw=row(p["b_dw"]),
        cln_g=row(p["conv_ln_g"]), cln_b=row(p["conv_ln_b"]), w_pw=_bf(p["w_pw"]), b_pw=row(p["b_pw"]),
        bre=_bf(_block_diag(bbr)), bim=_bf(_block_diag(bbi)),
        cc=_bf(jnp.concatenate([_block_diag(c_re), _block_diag(-c_im)], axis=2)),
        einv_r=table(eir), einv_i=table(eii), epow_r=table(epr), epow_i=table(epi),
        lamb_r=bcast(lbr), lamb_i=bcast(lbi), lrow_r=lbr.reshape(1, nstate), lrow_i=lbi.reshape(1, nstate),
        dskip=row(p["d_skip"]),
        w_sv=_bf(p["w_sv"]), b_sv=row(p["b_sv"]), w_sg=_bf(p["w_sg"]), b_sg=row(p["b_sg"]),
        w_out=_bf(p["w_out"]), b_out=row(p["b_out"]), ln1_g=row(p["ln1_g"]), ln1_b=row(p["ln1_b"]),
        wr_hi=wr_hi, wr_lo=_bf(w_r - wr_hi.astype(_F32)),
        b_r=p["b_router"].astype(_F32).reshape(-1, 1),
        ln2_g=row(p["ln2_g"]), ln2_b=row(p["ln2_b"]))


def _layer(xp, xs_tm, c_all, hist_tm, h0r, h0i, p, alpha):
    b, l, d = xp.shape
    steps, nbs, _ = xs_tm.shape
    tb = TOKEN_BLOCK
    assert l % tb == 0 and l >= CONV_WIDTH - 1 and (steps * nbs) % tb == 0 and tb % nbs == 0
    wts = _layer_weights(p, tb)
    mod = _ada(c_all, p["w_ada"].astype(_F32), p["b_ada"].astype(_F32))
    modp, mods = mod[:b].reshape(b, 1, -1), mod[b:]

    x1p, h2p, rp, cntp, conv_p, sre_p, sim_p = _mixer_prompt(xp, modp, wts, alpha)
    x1s, h2s, rs, cnts, conv_s, sre_s, sim_s = _mixer_sample(xs_tm, mods, hist_tm, h0r, h0i, wts, alpha)

    n_tok = b * l + steps * nbs
    nb, r_max, nt_max = _plan_sizes(n_tok)
    cnt = jnp.concatenate([cntp.reshape(-1, LANES), cnts.reshape(-1, LANES)], axis=0)[:, :N_EXPERTS]
    plan = _routing_plan(cnt.astype(jnp.int32), nt_max)

    flat = lambda a: a.reshape(b * l, a.shape[-1])
    xs_sorted, d4 = _dispatch(plan, flat(h2p), h2s, flat(rp), rs, r_max, nt_max)
    ys_sorted = _experts(plan, xs_sorted, p["w1"], p["b1"], p["w2"], p["b2"], nt_max)
    yp, ysm = _combine(plan, ys_sorted, flat(rp), rs, d4, flat(x1p), x1s, modp, mods,
                       wts["ln2_g"], wts["ln2_b"], r_max, alpha, l // tb)
    return (yp.reshape(b, l, d), ysm.reshape(steps, nbs, d), conv_p, sre_p[:, 0], sim_p[:, 0],
            conv_s, sre_s, sim_s)


def kernel(x_prompt, x_sample, state_conv, state_ssm_re, state_ssm_im, c_prompt, c_sample, w_ada, b_ada, w_in, b_in, w_dw, b_dw, conv_ln_g, conv_ln_b, w_pw, b_pw, lam_re, lam_im, log_dt, b_re, b_im, c_re, c_im, d_skip, w_sv, b_sv, w_sg, b_sg, w_out, b_out, ln1_g, ln1_b, w_router, b_router, w1, b1, w2, b2, ln2_g, ln2_b):
    stacked = dict(w_ada=w_ada, b_ada=b_ada, w_in=w_in, b_in=b_in, w_dw=w_dw, b_dw=b_dw,
                   conv_ln_g=conv_ln_g, conv_ln_b=conv_ln_b, w_pw=w_pw, b_pw=b_pw, lam_re=lam_re,
                   lam_im=lam_im, log_dt=log_dt, b_re=b_re, b_im=b_im, c_re=c_re, c_im=c_im,
                   d_skip=d_skip, w_sv=w_sv, b_sv=b_sv, w_sg=w_sg, b_sg=b_sg, w_out=w_out, b_out=b_out,
                   ln1_g=ln1_g, ln1_b=ln1_b, w_router=w_router, b_router=b_router, w1=w1, b1=b1,
                   w2=w2, b2=b2, ln2_g=ln2_g, ln2_b=ln2_b)
    depth = w_ada.shape[0]
    alpha = (2 * depth) ** 0.25
    b = x_prompt.shape[0]
    nbs = x_sample.shape[0]
    g, s = state_ssm_re.shape[2], state_ssm_re.shape[3]
    xp = x_prompt
    xs_tm = jnp.transpose(x_sample, (1, 0, 2))
    c_all = jnp.concatenate([c_prompt, c_sample], axis=0)
    conv_ps, re_ps, im_ps, conv_ss, re_ss, im_ss = [], [], [], [], [], []
    for layer in range(depth):
        p = {k: v[layer] for k, v in stacked.items()}
        hist_tm = jnp.transpose(state_conv[layer], (1, 0, 2))
        h0r = state_ssm_re[layer].reshape(nbs, g * s)
        h0i = state_ssm_im[layer].reshape(nbs, g * s)
        xp, xs_tm, conv_p, sre_p, sim_p, conv_s, sre_s, sim_s = _layer(
            xp, xs_tm, c_all, hist_tm, h0r, h0i, p, alpha)
        conv_ps.append(conv_p.astype(state_conv.dtype))
        re_ps.append(sre_p.reshape(b, g, s))
        im_ps.append(sim_p.reshape(b, g, s))
        conv_ss.append(jnp.transpose(conv_s, (1, 0, 2)).astype(state_conv.dtype))
        re_ss.append(sre_s.reshape(nbs, g, s))
        im_ss.append(sim_s.reshape(nbs, g, s))
    return (xp, jnp.transpose(xs_tm, (1, 0, 2)), jnp.stack(conv_ps), jnp.stack(re_ps), jnp.stack(im_ps),
            jnp.stack(conv_ss), jnp.stack(re_ss), jnp.stack(im_ss))
```

```python
import functools

import jax
import jax.numpy as jnp
from jax import lax
from jax.experimental import pallas as pl
from jax.experimental.pallas import tpu as pltpu

CONV_WIDTH = 31
SSM_GROUP = 16
SSM_STATE = 64
N_EXPERTS = 32
TOP_K = 4
SWIGLU_LIMIT = 7.0
SWIGLU_ALPHA = 1.702
LN_EPS = 1e-5

LANES = 128
SUBLANES = 8
TOKEN_BLOCK = 256
EXPERT_TILE = 512
SEG_PAD = 16
GROUPS_PER_BLOCK = LANES // SSM_GROUP
CONV_HIST = 32
CONV_ROWS = 32
ROUTE_IDX = N_EXPERTS
ROUTE_GATE = N_EXPERTS + TOP_K
VMEM_LIMIT = 56 * 1024 * 1024

_F32 = jnp.float32
_BF16 = jnp.bfloat16


def _bf(x):
    return x.astype(_BF16)


def _dot(a, b):
    return jnp.dot(a, b, preferred_element_type=_F32)


def _dot_nt(a, b):
    return lax.dot_general(a, b, (((1,), (1,)), ((), ())), preferred_element_type=_F32)


def _split(x):
    hi = _bf(x)
    lo = _bf(x - hi.astype(_F32))
    return hi, lo


def _sigmoid(x):
    return 1.0 / (1.0 + jnp.exp(-x))


def _gelu_tanh(x):
    return 0.5 * x * (1.0 + jnp.tanh(0.7978845608028654 * (x + 0.044715 * (x * x * x))))


def _layer_norm(x, g, b):
    mu = jnp.mean(x, axis=-1, keepdims=True)
    xc = x - mu
    var = jnp.mean(xc * xc, axis=-1, keepdims=True)
    return xc * lax.rsqrt(var + LN_EPS) * g + b


def _const_spec(shape):
    nd = len(shape)
    return pl.BlockSpec(shape, lambda *_: (0,) * nd)


def _params(sem):
    return pltpu.CompilerParams(dimension_semantics=sem, vmem_limit_bytes=VMEM_LIMIT)


def _ada_kernel(c_ref, w_ref, b_ref, o_ref):
    c = c_ref[...]
    s_hi, s_lo = _split(c * _sigmoid(c))
    w_hi, w_lo = _split(w_ref[...])
    o_ref[...] = _dot(s_hi, w_hi) + _dot(s_lo, w_hi) + _dot(s_hi, w_lo) + b_ref[...]


def _ada(c, w, b):
    n, d = c.shape
    cols = w.shape[1]
    tn = d
    return pl.pallas_call(
        _ada_kernel,
        grid=(cols // tn,),
        in_specs=[pl.BlockSpec((n, d), lambda j: (0, 0)),
                  pl.BlockSpec((d, tn), lambda j: (0, j)),
                  pl.BlockSpec((1, tn), lambda j: (0, j))],
        out_specs=pl.BlockSpec((n, tn), lambda j: (0, j)),
        out_shape=jax.ShapeDtypeStruct((n, cols), _F32),
        compiler_params=_params(("arbitrary",)),
        name="ada",
    )(c, w, b.reshape(1, cols))


def _tiled(fn, tile, *args, vecs=()):
    n, m = args[0].shape
    tr, tc = min(tile[0], n), min(tile[1], m)
    rows_out = None
    for r0 in range(0, n, tr):
        cols_out = None
        for c0 in range(0, m, tc):
            res = fn(*[a[r0:r0 + tr, c0:c0 + tc] for a in args], *[v[:, c0:c0 + tc] for v in vecs])
            res = res if isinstance(res, tuple) else (res,)
            if cols_out is None:
                cols_out = [[] for _ in res]
            for acc, v in zip(cols_out, res):
                acc.append(v)
        row_vals = [c[0] if len(c) == 1 else jnp.concatenate(c, axis=1) for c in cols_out]
        if rows_out is None:
            rows_out = [[] for _ in row_vals]
        for acc, v in zip(rows_out, row_vals):
            acc.append(v)
    outs = [r[0] if len(r) == 1 else jnp.concatenate(r, axis=0) for r in rows_out]
    return outs[0] if len(outs) == 1 else tuple(outs)


def _in_proj(hb, w_in, b_in, lo, hi):
    return _dot(hb, w_in[:, lo:hi]) + b_in[:, lo:hi]


def _conv_tail(v, cln_g, cln_b, w_pw, b_pw):
    g, b = cln_g[...], cln_b[...]

    def norm_swish(vb):
        vb = _layer_norm(vb, g, b)
        return vb * _sigmoid(vb)
    v = _tiled(norm_swish, (SUBLANES, v.shape[1]), v)
    return _dot(_bf(v), w_pw[...]) + b_pw[...]


def _ssm_tail(y, zs, dskip, w_sv, b_sv, w_sg, b_sg):
    yg = _tiled(lambda yb, zb, db: _bf(_gelu_tanh(yb + db * zb)), (2 * SUBLANES, 4 * LANES), y, zs, vecs=(dskip,))
    sv, sg = _dot(yg, w_sv[...]), _dot(yg, w_sg[...])
    return _tiled(lambda a, b, bv, bg: (a + bv) * _sigmoid(b + bg), (SUBLANES, 4 * LANES),
                  sv, sg, vecs=(b_sv, b_sg))


def _route(h2, wr_hi, wr_lo, b_r):
    n = h2.shape[0]
    ne = b_r.shape[0]
    h_hi, h_lo = _split(h2)
    logits = (_dot(h_hi, wr_hi[...]) + _dot(h_lo, wr_hi[...]) + _dot(h_hi, wr_lo[...])).T[:ne, :] + b_r[...]
    row = lax.broadcasted_iota(jnp.int32, (ne, n), 0).astype(_F32)
    neg = jnp.float32(-jnp.inf)
    cur = logits
    vals, idxs = [], []
    for _ in range(TOP_K):
        m = jnp.max(cur, axis=0, keepdims=True)
        idx = jnp.min(jnp.where(cur == m, row, float(ne)), axis=0, keepdims=True)
        vals.append(m)
        idxs.append(idx)
        cur = jnp.where(row == idx, neg, cur)
    es = [jnp.exp(v - vals[0]) for v in vals]
    tot = es[0]
    for e in es[1:]:
        tot = tot + e
    inv = 1.0 / tot
    comb = jnp.zeros((ne, n), _F32)
    row8 = lax.broadcasted_iota(jnp.int32, (2 * TOP_K, n), 0)
    extra = jnp.zeros((2 * TOP_K, n), _F32)
    for k in range(TOP_K):
        gate = es[k] * inv
        comb = comb + jnp.where(row == idxs[k], gate, 0.0)
        extra = extra + jnp.where(row8 == k, idxs[k], 0.0) + jnp.where(row8 == TOP_K + k, gate, 0.0)
    rest = jnp.zeros((LANES - ne - 2 * TOP_K, n), _F32)
    return jnp.concatenate([comb, extra, rest], axis=0).T


def _members(route):
    lane = lax.broadcasted_iota(jnp.int32, route.shape, 1).astype(_F32)
    member = jnp.zeros(route.shape, _F32)
    for k in range(TOP_K):
        member = member + jnp.where(lane == route[:, ROUTE_IDX + k:ROUTE_IDX + k + 1], 1.0, 0.0)
    return member


def _block_counts(route):
    return jnp.sum(_members(route), axis=0, keepdims=True)


def _mixer_tail(x, zg, conv_out, ssm_out, g1, sh2, sc2, alpha, w_out, b_out,
                ln1_g, ln1_b, wr_hi, wr_lo, b_r):
    d = x.shape[1]
    merged = _tiled(lambda gc, gs, co, so: _bf(_sigmoid(gc) * co + _sigmoid(gs) * so),
                    (2 * SUBLANES, 4 * LANES), zg[:, :d], zg[:, d:], conv_out, ssm_out)
    m = _dot(merged, w_out[...])

    def norm_mod(xb, mb, gb, scb, shb, bo, lg, lb):
        x1b = _layer_norm(alpha * xb + gb * (mb + bo), lg, lb)
        return x1b, x1b * (1.0 + scb) + shb
    mods = (g1, sc2, sh2)
    per_seq = g1.shape[0] == 1
    x1, h2 = _tiled(norm_mod, (SUBLANES, d), x, m, *(() if per_seq else mods),
                    vecs=(mods if per_seq else ()) + (b_out, ln1_g, ln1_b))
    return x1, h2, _route(h2, wr_hi, wr_lo, b_r)


def _mixer_prompt_kernel(x_ref, mod_ref, w_in, b_in, w_dw, b_dw, cln_g, cln_b, w_pw, b_pw,
                         bre, bim, cc, einv_r, einv_i, epow_r, epow_i, lamb_r, lamb_i, dskip,
                         w_sv, b_sv, w_sg, b_sg, w_out, b_out, ln1_g, ln1_b, wr_hi, wr_lo, b_r,
                         x1_ref, h2_ref, route_ref, cnt_ref, cstate_ref, sre_ref, sim_ref,
                         ubuf, ush, car_r, car_i, *, alpha):
    c = pl.program_id(1)
    last = pl.num_programs(1) - 1
    tl, d = x_ref.shape[1], x_ref.shape[2]
    dc = w_pw.shape[0]
    dsm = dskip.shape[1]
    nj = bre.shape[0]
    rows = bre.shape[1]

    @pl.when(c == 0)
    def _():
        ubuf[0:CONV_HIST, :] = jnp.zeros((CONV_HIST, dc), _F32)
        car_r[...] = jnp.zeros(car_r.shape, _F32)
        car_i[...] = jnp.zeros(car_i.shape, _F32)

    x = x_ref[0]
    mod = mod_ref[0]
    sh1, sc1, g1 = mod[:, 0:d], mod[:, d:2 * d], mod[:, 2 * d:3 * d]
    sh2, sc2 = mod[:, 3 * d:4 * d], mod[:, 4 * d:5 * d]
    hb = _bf(x * (1.0 + sc1) + sh1)

    u = _in_proj(hb, w_in, b_in, 0, dc) * _sigmoid(_in_proj(hb, w_in, b_in, dc, 2 * dc))
    ubuf[CONV_HIST:CONV_HIST + tl, :] = u
    span = ush.shape[1]
    for r in range(1, SUBLANES):
        ush[r - 1] = ubuf[r:r + span, :]

    n_blocks = tl // CONV_ROWS
    gate_lo = 2 * dc + dsm
    gate_w = 2 * d // n_blocks
    vblocks, zg_cols = [], []
    for blk in range(n_blocks):
        r0 = blk * CONV_ROWS
        acc = jnp.broadcast_to(b_dw[...], (CONV_ROWS, dc))
        for k in range(CONV_WIDTH):
            q, r = divmod(CONV_HIST - (CONV_WIDTH - 1) + k, SUBLANES)
            rows_k = slice(r0 + q * SUBLANES, r0 + q * SUBLANES + CONV_ROWS)
            tap = ubuf[rows_k, :] if r == 0 else ush[r - 1, rows_k, :]
            acc = acc + w_dw[k:k + 1, :] * tap
        vblocks.append(acc)
        zg_cols.append(_in_proj(hb, w_in, b_in, gate_lo + blk * gate_w, gate_lo + (blk + 1) * gate_w))
    ubuf[0:CONV_HIST, :] = ubuf[tl:tl + CONV_HIST, :]
    conv_out = _conv_tail(jnp.concatenate(vblocks, axis=0), cln_g, cln_b, w_pw, b_pw)

    zs = _in_proj(hb, w_in, b_in, 2 * dc, 2 * dc + dsm)
    zst = _bf(zs.T)
    tri = jnp.where(lax.broadcasted_iota(jnp.int32, (tl, tl), 0)
                    <= lax.broadcasted_iota(jnp.int32, (tl, tl), 1), 1.0, 0.0).astype(_BF16)
    def project(j):
        zj = zst[j * LANES:(j + 1) * LANES, :]
        return _dot(bre[j], zj), _dot(bim[j], zj)

    def scale_in(j, bu):
        r = slice(j * rows, (j + 1) * rows)
        eir, eii = einv_r[r, :], einv_i[r, :]
        return _bf(bu[0] * eir - bu[1] * eii), _bf(bu[0] * eii + bu[1] * eir)

    def prefix(v):
        return _dot(v[0], tri), _dot(v[1], tri)

    def scale_out(j, cum):
        r = slice(j * rows, (j + 1) * rows)
        cr = jnp.broadcast_to(car_r[r, LANES - 1:LANES], (rows, LANES))
        ci = jnp.broadcast_to(car_i[r, LANES - 1:LANES], (rows, LANES))
        lr, li = lamb_r[r, :], lamb_i[r, :]
        cum_r = cum[0] + jnp.concatenate([lr * cr - li * ci] * (tl // LANES), axis=1)
        cum_i = cum[1] + jnp.concatenate([lr * ci + li * cr] * (tl // LANES), axis=1)
        epr, epi = epow_r[r, :], epow_i[r, :]
        h_r = cum_r * epr - cum_i * epi
        h_i = cum_r * epi + cum_i * epr
        car_r[r, :] = h_r[:, tl - LANES:tl]
        car_i[r, :] = h_i[:, tl - LANES:tl]
        return jnp.concatenate([_bf(h_r), _bf(h_i)], axis=0)

    bu, v, cum, hcat, yts = {}, {}, {}, {}, {}
    for t in range(nj + 4):
        if t < nj:
            bu[t] = project(t)
        if 0 <= t - 2 < nj:
            cum[t - 2] = prefix(v.pop(t - 2))
        if 0 <= t - 4 < nj:
            yts[t - 4] = _dot(cc[t - 4], hcat.pop(t - 4))
        if 0 <= t - 1 < nj:
            v[t - 1] = scale_in(t - 1, bu.pop(t - 1))
        if 0 <= t - 3 < nj:
            hcat[t - 3] = scale_out(t - 3, cum.pop(t - 3))
    ssm_out = _ssm_tail(jnp.concatenate([yts[j] for j in range(nj)], axis=0).T, zs, dskip,
                        w_sv, b_sv, w_sg, b_sg)

    x1, h2, route = _mixer_tail(x, jnp.concatenate(zg_cols, axis=1), conv_out, ssm_out, g1, sh2, sc2, alpha,
                                w_out, b_out, ln1_g, ln1_b, wr_hi, wr_lo, b_r)
    x1_ref[0] = x1
    h2_ref[0] = _bf(h2)
    route_ref[0] = route
    cnt_ref[0, 0] = _block_counts(route)

    @pl.when(c == last)
    def _():
        cstate_ref[0] = ubuf[CONV_HIST + tl - (CONV_WIDTH - 1):CONV_HIST + tl, :]
        sre_ref[0] = car_r[...].T[LANES - 1:LANES, :]
        sim_ref[0] = car_i[...].T[LANES - 1:LANES, :]


def _mixer_prompt(x, mod, wts, alpha):
    b, l, d = x.shape
    tl = TOKEN_BLOCK
    nc = l // tl
    dc = wts["w_pw"].shape[0]
    nstate = wts["lamb_r"].shape[0]
    names = ["w_in", "b_in", "w_dw", "b_dw", "cln_g", "cln_b", "w_pw", "b_pw", "bre", "bim", "cc",
             "einv_r", "einv_i", "epow_r", "epow_i", "lamb_r", "lamb_i", "dskip",
             "w_sv", "b_sv", "w_sg", "b_sg", "w_out", "b_out", "ln1_g", "ln1_b",
             "wr_hi", "wr_lo", "b_r"]
    consts = [wts[n] for n in names]
    tok = lambda bi, ci: (bi, ci, 0)
    seq = lambda bi, ci: (bi, 0, 0)
    return pl.pallas_call(
        functools.partial(_mixer_prompt_kernel, alpha=alpha),
        grid=(b, nc),
        in_specs=[pl.BlockSpec((1, tl, d), tok), pl.BlockSpec((1, 1, mod.shape[-1]), seq)]
                 + [_const_spec(a.shape) for a in consts],
        out_specs=[pl.BlockSpec((1, tl, d), tok), pl.BlockSpec((1, tl, d), tok),
                   pl.BlockSpec((1, tl, LANES), tok),
                   pl.BlockSpec((1, 1, 1, LANES), lambda bi, ci: (bi, ci, 0, 0)),
                   pl.BlockSpec((1, CONV_WIDTH - 1, dc), seq),
                   pl.BlockSpec((1, 1, nstate), seq), pl.BlockSpec((1, 1, nstate), seq)],
        out_shape=[jax.ShapeDtypeStruct((b, l, d), _F32), jax.ShapeDtypeStruct((b, l, d), _BF16),
                   jax.ShapeDtypeStruct((b, l, LANES), _F32),
                   jax.ShapeDtypeStruct((b, nc, 1, LANES), _F32),
                   jax.ShapeDtypeStruct((b, CONV_WIDTH - 1, dc), _F32),
                   jax.ShapeDtypeStruct((b, 1, nstate), _F32),
                   jax.ShapeDtypeStruct((b, 1, nstate), _F32)],
        scratch_shapes=[pltpu.VMEM((CONV_HIST + tl, dc), _F32),
                        pltpu.VMEM((SUBLANES - 1, tl + CONV_HIST - SUBLANES, dc), _F32),
                        pltpu.VMEM((nstate, LANES), _F32), pltpu.VMEM((nstate, LANES), _F32)],
        compiler_params=_params(("arbitrary", "arbitrary")),
        name="mixer_prompt",
    )(x, mod, *consts)


def _mixer_sample_kernel(x_ref, mod_ref, hist_ref, h0r_ref, h0i_ref,
                         w_in, b_in, w_dw, b_dw, cln_g, cln_b, w_pw, b_pw,
                         bre, bim, cc, lrow_r, lrow_i, dskip,
                         w_sv, b_sv, w_sg, b_sg, w_out, b_out, ln1_g, ln1_b, wr_hi, wr_lo, b_r,
                         x1_ref, h2_ref, route_ref, cnt_ref, cstate_ref, sre_ref, sim_ref, *, alpha):
    steps, nb, d = x_ref.shape
    dc = w_pw.shape[0]
    dsm = dskip.shape[1]
    nj = bre.shape[0]
    rows = bre.shape[1]
    nhist = CONV_WIDTH - 1
    x = x_ref[...].reshape(steps * nb, d)
    mod = jnp.concatenate([mod_ref[...]] * steps, axis=0)
    sh1, sc1, g1 = mod[:, 0:d], mod[:, d:2 * d], mod[:, 2 * d:3 * d]
    sh2, sc2 = mod[:, 3 * d:4 * d], mod[:, 4 * d:5 * d]
    hb = _bf(x * (1.0 + sc1) + sh1)

    u = _in_proj(hb, w_in, b_in, 0, dc) * _sigmoid(_in_proj(hb, w_in, b_in, dc, 2 * dc))
    us = [u[t * nb:(t + 1) * nb, :] for t in range(steps)]
    vs = []
    for t in range(steps):
        acc = jnp.broadcast_to(b_dw[...], (nb, dc))
        for i in range(t, nhist):
            acc = acc + w_dw[i - t:i - t + 1, :] * hist_ref[i]
        for s in range(t + 1):
            acc = acc + w_dw[nhist + s - t:nhist + s - t + 1, :] * us[s]
        vs.append(acc)
    for i in range(nhist):
        src = i + steps
        cstate_ref[i] = hist_ref[src] if src < nhist else us[src - nhist]
    conv_out = _conv_tail(jnp.concatenate(vs, axis=0), cln_g, cln_b, w_pw, b_pw)

    zs = _in_proj(hb, w_in, b_in, 2 * dc, 2 * dc + dsm)
    zsb = _bf(zs)
    h_r, h_i = h0r_ref[...], h0i_ref[...]
    lr, li = lrow_r[...], lrow_i[...]
    ys = []
    for t in range(steps):
        zt = zsb[t * nb:(t + 1) * nb, :]
        bu_r = jnp.concatenate([_dot_nt(zt[:, j * LANES:(j + 1) * LANES], bre[j]) for j in range(nj)], axis=1)
        bu_i = jnp.concatenate([_dot_nt(zt[:, j * LANES:(j + 1) * LANES], bim[j]) for j in range(nj)], axis=1)
        h_r, h_i = lr * h_r - li * h_i + bu_r, lr * h_i + li * h_r + bu_i
        hrb, hib = _bf(h_r), _bf(h_i)
        ys.append(jnp.concatenate(
            [_dot_nt(jnp.concatenate([hrb[:, j * rows:(j + 1) * rows], hib[:, j * rows:(j + 1) * rows]], axis=1),
                     cc[j]) for j in range(nj)], axis=1))
    sre_ref[...] = h_r
    sim_ref[...] = h_i
    ssm_out = _ssm_tail(jnp.concatenate(ys, axis=0), zs, dskip, w_sv, b_sv, w_sg, b_sg)

    zg = _in_proj(hb, w_in, b_in, 2 * dc + dsm, 2 * dc + dsm + 2 * d)
    x1, h2, route = _mixer_tail(x, zg, conv_out, ssm_out, g1, sh2, sc2, alpha,
                                w_out, b_out, ln1_g, ln1_b, wr_hi, wr_lo, b_r)
    x1_ref[...] = x1
    h2_ref[...] = _bf(h2)
    route_ref[...] = route
    for blk in range(cnt_ref.shape[0]):
        cnt_ref[blk] = _block_counts(route[blk * TOKEN_BLOCK:(blk + 1) * TOKEN_BLOCK, :])


def _mixer_sample(x_tm, mod, hist_tm, h0r, h0i, wts, alpha):
    steps, nb, d = x_tm.shape
    n = steps * nb
    dc = wts["w_pw"].shape[0]
    nstate = h0r.shape[1]
    names = ["w_in", "b_in", "w_dw", "b_dw", "cln_g", "cln_b", "w_pw", "b_pw", "bre", "bim", "cc",
             "lrow_r", "lrow_i", "dskip", "w_sv", "b_sv", "w_sg", "b_sg", "w_out", "b_out",
             "ln1_g", "ln1_b", "wr_hi", "wr_lo", "b_r"]
    args = [x_tm, mod, hist_tm, h0r, h0i] + [wts[k] for k in names]
    return pl.pallas_call(
        functools.partial(_mixer_sample_kernel, alpha=alpha),
        grid=(1,),
        in_specs=[_const_spec(a.shape) for a in args],
        out_specs=[_const_spec((n, d)), _const_spec((n, d)), _const_spec((n, LANES)),
                   _const_spec((n // TOKEN_BLOCK, 1, LANES)),
                   _const_spec((CONV_WIDTH - 1, nb, dc)),
                   _const_spec((nb, nstate)), _const_spec((nb, nstate))],
        out_shape=[jax.ShapeDtypeStruct((n, d), _F32), jax.ShapeDtypeStruct((n, d), _BF16),
                   jax.ShapeDtypeStruct((n, LANES), _F32),
                   jax.ShapeDtypeStruct((n // TOKEN_BLOCK, 1, LANES), _F32),
                   jax.ShapeDtypeStruct((CONV_WIDTH - 1, nb, dc), _F32),
                   jax.ShapeDtypeStruct((nb, nstate), _F32), jax.ShapeDtypeStruct((nb, nstate), _F32)],
        compiler_params=_params(("arbitrary",)),
        name="mixer_sample",
    )(*args)


def _round_up(x, m):
    return (x + m - 1) // m * m


def _plan_sizes(n_tokens):
    nb = n_tokens // TOKEN_BLOCK
    r_max = _round_up(TOP_K * TOKEN_BLOCK + 2 * N_EXPERTS * (SEG_PAD - 1), 2 * LANES)
    rows_max = n_tokens * TOP_K + N_EXPERTS * (EXPERT_TILE - 1)
    nt_max = -(-rows_max // EXPERT_TILE)
    return nb, r_max, nt_max


def _routing_plan(cnt, nt_max):
    nb, ne = cnt.shape
    before = jnp.cumsum(cnt, axis=0) - cnt
    tot = jnp.sum(cnt, axis=0)
    cin = before % SEG_PAD
    last = (jnp.arange(nb, dtype=jnp.int32) == nb - 1)[:, None]
    active = (cnt > 0) | (last & (cin > 0))
    seg = jnp.where(active, _round_up(cin + cnt, SEG_PAD), 0)
    loc = jnp.cumsum(seg, axis=1) - seg
    totpad = _round_up(tot, EXPERT_TILE)
    eend = jnp.cumsum(totpad)
    estart = eend - totpad
    goff = estart[None, :] + before - cin
    nfull = jnp.where(last, seg, jnp.where(active, (cin + cnt) // SEG_PAD * SEG_PAD, 0))
    cout = jnp.where(last | ~active, 0, (cin + cnt) % SEG_PAD)
    n_tiles = (eend[-1] // EXPERT_TILE).astype(jnp.int32)
    tiles = jnp.minimum(jnp.arange(nt_max, dtype=jnp.int32), n_tiles - 1)
    te = jnp.sum(((eend // EXPERT_TILE)[None, :] <= tiles[:, None]).astype(jnp.int32), axis=1)
    te = jnp.minimum(te, ne - 1)
    tot16 = _round_up(tot, SEG_PAD)
    return dict(seg=seg, loc=loc, goff=goff, nfull=nfull, cin=jnp.where(active, cin, 0), cout=cout,
                used=jnp.sum(seg, axis=1).astype(jnp.int32),
                n_tiles=n_tiles.reshape(1), te=te,
                npad=(totpad - tot16).astype(jnp.int32), pad_base=(estart + tot16).astype(jnp.int32))


def _dest_columns(route, loc_row):
    n = route.shape[0]
    lane_i = lax.broadcasted_iota(jnp.int32, (n, LANES), 1)
    lane = lane_i.astype(_F32)
    member = _bf(_members(route))
    strict = jnp.where(lax.broadcasted_iota(jnp.int32, (n, n), 1)
                       < lax.broadcasted_iota(jnp.int32, (n, n), 0), 1.0, 0.0).astype(_BF16)
    dest = _dot(strict, member) + loc_row
    d4 = jnp.zeros((n, LANES), _F32)
    for k in range(TOP_K):
        idx = route[:, ROUTE_IDX + k:ROUTE_IDX + k + 1]
        dk = jnp.sum(jnp.where(lane == idx, dest, 0.0), axis=-1, keepdims=True)
        d4 = d4 + jnp.where(lane_i == k, dk, 0.0)
    return d4


def _typical_rows(r_max):
    return min(r_max, _round_up(TOP_K * TOKEN_BLOCK + N_EXPERTS * SEG_PAD, 2 * LANES))


def _dispatch_kernel(nfull, sloc, goff, cin, cout, used, npad, pad_base, hp_ref, hs_ref, rp_ref, rs_ref, loc_ref,
                     xs_hbm, d4_ref, buf, stage, zbuf, sem, zsem, *, nbp):
    i = pl.program_id(0)
    nblk = pl.num_programs(0)
    slot = lax.rem(i, 2)
    tb = hp_ref.shape[0]
    r_max = buf.shape[1]
    is_p = i < nbp
    h = jnp.where(is_p, hp_ref[...], hs_ref[...])
    route = jnp.where(is_p, rp_ref[...], rs_ref[...])
    d4 = _dest_columns(route, loc_ref[0])
    d4_ref[...] = d4
    d4t = d4.T

    def compact(rows):
        row = lax.broadcasted_iota(jnp.int32, (rows, tb), 0).astype(_F32)
        p = jnp.zeros((rows, tb), _F32)
        for k in range(TOP_K):
            p = p + jnp.where(row == d4t[k:k + 1, :], 1.0, 0.0)
        buf[slot, 0:rows, :] = _bf(_dot(_bf(p), h))

    r_typ = _typical_rows(r_max)

    @pl.when(i == 0)
    def _():
        buf[...] = jnp.zeros(buf.shape, buf.dtype)
        stage[...] = jnp.zeros(stage.shape, stage.dtype)

    @pl.when(used[i] <= r_typ)
    def _():
        compact(r_typ)

    @pl.when(used[i] > r_typ)
    def _():
        compact(r_max)

    def seg_copy(blk, s, e):
        n = pl.multiple_of(nfull[blk * N_EXPERTS + e], SEG_PAD)
        src = pl.multiple_of(sloc[blk * N_EXPERTS + e], SEG_PAD)
        dst = pl.multiple_of(goff[blk * N_EXPERTS + e], SEG_PAD)
        return pltpu.make_async_copy(buf.at[s, pl.ds(src, n), :], xs_hbm.at[pl.ds(dst, n), :], sem.at[s])

    def pad_copy(e):
        n = pl.multiple_of(npad[e], SEG_PAD)
        dst = pl.multiple_of(pad_base[e], SEG_PAD)
        return pltpu.make_async_copy(zbuf.at[pl.ds(0, n), :], xs_hbm.at[pl.ds(dst, n), :], zsem.at[0])

    def for_segments(blk, fn):
        for e in range(N_EXPERTS):
            @pl.when(nfull[blk * N_EXPERTS + e] > 0)
            def _(e=e):
                fn(e)

    for e in range(N_EXPERTS):
        first = pl.ds(pl.multiple_of(sloc[i * N_EXPERTS + e], SEG_PAD), SEG_PAD)
        carried = jnp.where(cin[i * N_EXPERTS + e] > 0, stage[e], jnp.zeros_like(stage[e]))
        buf[slot, first, :] = buf[slot, first, :] + carried

    for_segments(i, lambda e: seg_copy(i, slot, e).start())

    for e in range(N_EXPERTS):
        rest = pl.multiple_of(sloc[i * N_EXPERTS + e] + nfull[i * N_EXPERTS + e], SEG_PAD)
        stage[e] = jnp.where(cout[i * N_EXPERTS + e] > 0, buf[slot, pl.ds(rest, SEG_PAD), :], stage[e])

    @pl.when(i > 0)
    def _():
        for_segments(i - 1, lambda e: seg_copy(i - 1, 1 - slot, e).wait())

    @pl.when(i == nblk - 1)
    def _():
        zbuf[...] = jnp.zeros(zbuf.shape, zbuf.dtype)
        for e in range(N_EXPERTS):
            @pl.when(npad[e] > 0)
            def _(e=e):
                pad_copy(e).start()
        for e in range(N_EXPERTS):
            @pl.when(npad[e] > 0)
            def _(e=e):
                pad_copy(e).wait()
        for_segments(i, lambda e: seg_copy(i, slot, e).wait())


def _dispatch(plan, h2p, h2s, rp, rs, r_max, nt_max):
    tb = TOKEN_BLOCK
    d = h2p.shape[1]
    nbp, nbs = h2p.shape[0] // tb, h2s.shape[0] // tb
    nb = nbp + nbs
    first_row = (plan["loc"] + plan["cin"]).astype(_F32)
    loc = jnp.pad(first_row, ((0, 0), (0, LANES - N_EXPERTS))).reshape(nb, 1, LANES)
    pidx = lambda i, *_: (jnp.minimum(i, nbp - 1), 0)
    sidx = lambda i, *_: (jnp.maximum(i - nbp, 0), 0)
    grid_spec = pltpu.PrefetchScalarGridSpec(
        num_scalar_prefetch=8,
        grid=(nb,),
        in_specs=[pl.BlockSpec((tb, d), pidx), pl.BlockSpec((tb, d), sidx),
                  pl.BlockSpec((tb, LANES), pidx), pl.BlockSpec((tb, LANES), sidx),
                  pl.BlockSpec((1, 1, LANES), lambda i, *_: (i, 0, 0))],
        out_specs=[pl.BlockSpec(memory_space=pl.ANY), pl.BlockSpec((tb, LANES), lambda i, *_: (i, 0))],
        scratch_shapes=[pltpu.VMEM((2, r_max, d), _BF16), pltpu.VMEM((N_EXPERTS, SEG_PAD, d), _BF16),
                        pltpu.VMEM((EXPERT_TILE, d), _BF16),
                        pltpu.SemaphoreType.DMA((2,)), pltpu.SemaphoreType.DMA((1,))])
    flat = lambda a: a.astype(jnp.int32).reshape(-1)
    return pl.pallas_call(
        functools.partial(_dispatch_kernel, nbp=nbp),
        grid_spec=grid_spec,
        out_shape=[jax.ShapeDtypeStruct((nt_max * EXPERT_TILE, d), _BF16),
                   jax.ShapeDtypeStruct((nb * tb, LANES), _F32)],
        compiler_params=_params(("arbitrary",)),
        name="dispatch",
    )(flat(plan["nfull"]), flat(plan["loc"]), flat(plan["goff"]), flat(plan["cin"]), flat(plan["cout"]),
      plan["used"], plan["npad"], plan["pad_base"], h2p, h2s, rp, rs, loc)


def _expert_kernel(te, n_tiles, x_ref, w1_ref, b1_ref, w2_ref, b2_ref, y_ref, w1b, w2b):
    i = pl.program_id(0)
    dff = w2_ref.shape[1]

    @pl.when(i < n_tiles[0])
    def _():
        prev = te[jnp.maximum(i - 1, 0)]

        @pl.when((i == 0) | (te[i] != prev))
        def _():
            w1b[...] = _bf(w1_ref[0])
            w2b[...] = _bf(w2_ref[0])

        gu = _dot(x_ref[...], w1b[...]) + b1_ref[0]
        g = jnp.minimum(gu[:, :dff], SWIGLU_LIMIT)
        up = jnp.clip(gu[:, dff:], -SWIGLU_LIMIT, SWIGLU_LIMIT)
        act = g * _sigmoid(SWIGLU_ALPHA * g) * (up + 1.0)
        y_ref[...] = _bf(_dot(_bf(act), w2b[...]) + b2_ref[0])


def _experts(plan, xs, w1, b1, w2, b2, nt_max):
    ne, d, dff2 = w1.shape
    dff = w2.shape[1]
    tm = EXPERT_TILE
    tile = lambda i, te, nt: (jnp.minimum(i, nt[0] - 1), 0)
    wsel = lambda i, te, nt: (te[i], 0, 0)
    grid_spec = pltpu.PrefetchScalarGridSpec(
        num_scalar_prefetch=2,
        grid=(nt_max,),
        in_specs=[pl.BlockSpec((tm, d), tile),
                  pl.BlockSpec((1, d, dff2), wsel), pl.BlockSpec((1, 1, dff2), wsel),
                  pl.BlockSpec((1, dff, d), wsel), pl.BlockSpec((1, 1, d), wsel)],
        out_specs=pl.BlockSpec((tm, d), tile),
        scratch_shapes=[pltpu.VMEM((d, dff2), _BF16), pltpu.VMEM((dff, d), _BF16)])
    return pl.pallas_call(
        _expert_kernel,
        grid_spec=grid_spec,
        out_shape=jax.ShapeDtypeStruct(xs.shape, _BF16),
        compiler_params=_params(("arbitrary",)),
        name="experts",
    )(plan["te"], plan["n_tiles"], xs, w1, b1.reshape(ne, 1, dff2), w2, b2.reshape(ne, 1, d))


def _combine_kernel(seg, sloc, goff, used, rp_ref, rs_ref, d4_ref, x1p_ref, x1s_ref, g2p_ref, g2s_ref,
                    ln2_g, ln2_b, ys_hbm, yp_ref, ysm_ref, buf, fbuf, sem, *, nbp, alpha):
    i = pl.program_id(0)
    nblk = pl.num_programs(0)
    slot = lax.rem(i, 2)
    tb = rp_ref.shape[0]
    r_max = buf.shape[1]

    def seg_copy(blk, s, e):
        n = pl.multiple_of(seg[blk * N_EXPERTS + e], SEG_PAD)
        src = pl.multiple_of(goff[blk * N_EXPERTS + e], SEG_PAD)
        dst = pl.multiple_of(sloc[blk * N_EXPERTS + e], SEG_PAD)
        return pltpu.make_async_copy(ys_hbm.at[pl.ds(src, n), :], buf.at[s, pl.ds(dst, n), :], sem.at[s])

    def for_segments(blk, fn):
        for e in range(N_EXPERTS):
            @pl.when(seg[blk * N_EXPERTS + e] > 0)
            def _(e=e):
                fn(e)

    @pl.when(i == 0)
    def _():
        buf[...] = jnp.zeros(buf.shape, buf.dtype)
        for_segments(0, lambda e: seg_copy(0, 0, e).start())

    @pl.when(i + 1 < nblk)
    def _():
        for_segments(i + 1, lambda e: seg_copy(i + 1, 1 - slot, e).start())

    for_segments(i, lambda e: seg_copy(i, slot, e).wait())

    is_p = i < nbp
    route = jnp.where(is_p, rp_ref[...], rs_ref[...])
    d4 = d4_ref[...]

    def gather(rows):
        col = lax.broadcasted_iota(jnp.int32, (tb, rows), 1).astype(_F32)
        pg = jnp.zeros((tb, rows), _F32)
        for k in range(TOP_K):
            gate = route[:, ROUTE_GATE + k:ROUTE_GATE + k + 1]
            pg = pg + jnp.where(col == d4[:, k:k + 1], gate, 0.0)
        fbuf[...] = _dot(_bf(pg), buf[slot, 0:rows, :])

    r_typ = _typical_rows(r_max)

    @pl.when(used[i] <= r_typ)
    def _():
        gather(r_typ)

    @pl.when(used[i] > r_typ)
    def _():
        gather(r_max)

    f = fbuf[...]
    x1 = jnp.where(is_p, x1p_ref[...], x1s_ref[...])
    g2s = jnp.concatenate([g2s_ref[...]] * (tb // g2s_ref.shape[0]), axis=0)
    g2 = jnp.where(is_p, jnp.broadcast_to(g2p_ref[0], g2s.shape), g2s)
    y = _layer_norm(alpha * x1 + g2 * f, ln2_g[...], ln2_b[...])

    @pl.when(is_p)
    def _():
        yp_ref[...] = y

    @pl.when(jnp.logical_not(is_p))
    def _():
        ysm_ref[...] = y


def _combine(plan, ys, rp, rs, d4, x1p, x1s, modp, mods, ln2_g, ln2_b, r_max, alpha, blocks_per_seq):
    tb = TOKEN_BLOCK
    d = x1p.shape[1]
    nbp, nbs = x1p.shape[0] // tb, x1s.shape[0] // tb
    nb = nbp + nbs
    pidx = lambda i, *_: (jnp.minimum(i, nbp - 1), 0)
    sidx = lambda i, *_: (jnp.maximum(i - nbp, 0), 0)
    g2_lane_block = 5
    flat = lambda a: a.astype(jnp.int32).reshape(-1)
    grid_spec = pltpu.PrefetchScalarGridSpec(
        num_scalar_prefetch=4,
        grid=(nb,),
        in_specs=[pl.BlockSpec((tb, LANES), pidx), pl.BlockSpec((tb, LANES), sidx),
                  pl.BlockSpec((tb, LANES), lambda i, *_: (i, 0)),
                  pl.BlockSpec((tb, d), pidx), pl.BlockSpec((tb, d), sidx),
                  pl.BlockSpec((1, 1, d), lambda i, *_: (jnp.minimum(i, nbp - 1) // blocks_per_seq, 0,
                                                         g2_lane_block)),
                  pl.BlockSpec((mods.shape[0], d), lambda i, *_: (0, g2_lane_block)),
                  pl.BlockSpec((1, d), lambda i, *_: (0, 0)), pl.BlockSpec((1, d), lambda i, *_: (0, 0)),
                  pl.BlockSpec(memory_space=pl.ANY)],
        out_specs=[pl.BlockSpec((tb, d), pidx), pl.BlockSpec((tb, d), sidx)],
        scratch_shapes=[pltpu.VMEM((2, r_max, d), _BF16), pltpu.VMEM((tb, d), _F32),
                        pltpu.SemaphoreType.DMA((2,))])
    return pl.pallas_call(
        functools.partial(_combine_kernel, nbp=nbp, alpha=alpha),
        grid_spec=grid_spec,
        out_shape=[jax.ShapeDtypeStruct(x1p.shape, _F32), jax.ShapeDtypeStruct(x1s.shape, _F32)],
        compiler_params=_params(("arbitrary",)),
        name="combine",
    )(flat(plan["seg"]), flat(plan["loc"]), flat(plan["goff"]), plan["used"],
      rp, rs, d4, x1p, x1s, modp, mods, ln2_g, ln2_b, ys)


def _complex_powers(zr, zi, n):
    low = 16
    if n <= low or n % low:
        return _bit_powers(zr, zi, n)
    lr, li = _bit_powers(zr, zi, low)
    sr, si = zr, zi
    for _ in range(4):
        sr, si = sr * sr - si * si, 2.0 * sr * si
    hr, hi = _bit_powers(sr, si, n // low)
    pr = hr[:, :, None] * lr[:, None, :] - hi[:, :, None] * li[:, None, :]
    pi = hr[:, :, None] * li[:, None, :] + hi[:, :, None] * lr[:, None, :]
    return pr.reshape(-1, n), pi.reshape(-1, n)


def _bit_powers(zr, zi, n):
    k = jnp.arange(n, dtype=jnp.int32)[None, :]
    pr = jnp.ones((zr.shape[0], n), _F32)
    pi = jnp.zeros((zr.shape[0], n), _F32)
    sr, si = zr[:, None], zi[:, None]
    bit = 1
    while bit < n:
        on = (k & bit) != 0
        mr, mi = jnp.where(on, sr, 1.0), jnp.where(on, si, 0.0)
        pr, pi = pr * mr - pi * mi, pr * mi + pi * mr
        sr, si = sr * sr - si * si, 2.0 * sr * si
        bit *= 2
    return pr, pi


def _block_diag(m):
    g, a, b = m.shape
    gb = GROUPS_PER_BLOCK
    m = m.reshape(g // gb, gb, a, 1, b)
    eye = jnp.eye(gb, dtype=m.dtype).reshape(1, gb, 1, gb, 1)
    return (m * eye).reshape(g // gb, gb * a, gb * b)


def _layer_weights(p, q):
    d = p["w_in"].shape[0]
    row = lambda v: v.reshape(1, -1).astype(_F32)
    lam_re, lam_im = p["lam_re"].astype(_F32), p["lam_im"].astype(_F32)
    dt = jnp.exp(p["log_dt"].astype(_F32))[:, None]
    mag = jnp.exp(lam_re * dt)
    lbr, lbi = mag * jnp.cos(lam_im * dt), mag * jnp.sin(lam_im * dt)
    den = lam_re * lam_re + lam_im * lam_im
    nr, ni = lbr - 1.0, lbi
    fr, fi = (nr * lam_re + ni * lam_im) / den, (ni * lam_re - nr * lam_im) / den
    b_re, b_im = p["b_re"].astype(_F32), p["b_im"].astype(_F32)
    bbr = fr[..., None] * b_re - fi[..., None] * b_im
    bbi = fr[..., None] * b_im + fi[..., None] * b_re
    nstate = lam_re.size
    mod2 = lbr * lbr + lbi * lbi
    flat = lambda v: v.reshape(nstate)
    epr, epi = _complex_powers(flat(lbr), flat(lbi), q)
    eir, eii = _complex_powers(flat(lbr / mod2), flat(-lbi / mod2), q)
    table = lambda t: t
    bcast = lambda v: jnp.broadcast_to(v.reshape(nstate, 1), (nstate, LANES))
    c_re, c_im = p["c_re"].astype(_F32), p["c_im"].astype(_F32)
    w_r = p["w_router"].astype(_F32)
    w_r = jnp.pad(w_r, ((0, 0), (0, LANES - w_r.shape[1])))
    wr_hi = _bf(w_r)
    return dict(
        w_in=_bf(p["w_in"]), b_in=row(p["b_in"]), w_dw=p["w_dw"].astype(_F32), b_dw=row(p["b_dw"]),
        cln_g=row(p["conv_ln_g"]), cln_b=row(p["conv_ln_b"]), w_pw=_bf(p["w_pw"]), b_pw=row(p["b_pw"]),
        bre=_bf(_block_diag(bbr)), bim=_bf(_block_diag(bbi)),
        cc=_bf(jnp.concatenate([_block_diag(c_re), _block_diag(-c_im)], axis=2)),
        einv_r=table(eir), einv_i=table(eii), epow_r=table(epr), epow_i=table(epi),
        lamb_r=bcast(lbr), lamb_i=bcast(lbi), lrow_r=lbr.reshape(1, nstate), lrow_i=lbi.reshape(1, nstate),
        dskip=row(p["d_skip"]),
        w_sv=_bf(p["w_sv"]), b_sv=row(p["b_sv"]), w_sg=_bf(p["w_sg"]), b_sg=row(p["b_sg"]),
        w_out=_bf(p["w_out"]), b_out=row(p["b_out"]), ln1_g=row(p["ln1_g"]), ln1_b=row(p["ln1_b"]),
        wr_hi=wr_hi, wr_lo=_bf(w_r - wr_hi.astype(_F32)),
        b_r=p["b_router"].astype(_F32).reshape(-1, 1),
        ln2_g=row(p["ln2_g"]), ln2_b=row(p["ln2_b"]))


def _layer(xp, xs_tm, c_all, hist_tm, h0r, h0i, p, alpha):
    b, l, d = xp.shape
    steps, nbs, _ = xs_tm.shape
    tb = TOKEN_BLOCK
    assert l % tb == 0 and l >= CONV_WIDTH - 1 and (steps * nbs) % tb == 0 and tb % nbs == 0
    wts = _layer_weights(p, tb)
    mod = _ada(c_all, p["w_ada"].astype(_F32), p["b_ada"].astype(_F32))
    modp, mods = mod[:b].reshape(b, 1, -1), mod[b:]

    x1p, h2p, rp, cntp, conv_p, sre_p, sim_p = _mixer_prompt(xp, modp, wts, alpha)
    x1s, h2s, rs, cnts, conv_s, sre_s, sim_s = _mixer_sample(xs_tm, mods, hist_tm, h0r, h0i, wts, alpha)

    n_tok = b * l + steps * nbs
    nb, r_max, nt_max = _plan_sizes(n_tok)
    cnt = jnp.concatenate([cntp.reshape(-1, LANES), cnts.reshape(-1, LANES)], axis=0)[:, :N_EXPERTS]
    plan = _routing_plan(cnt.astype(jnp.int32), nt_max)

    flat = lambda a: a.reshape(b * l, a.shape[-1])
    xs_sorted, d4 = _dispatch(plan, flat(h2p), h2s, flat(rp), rs, r_max, nt_max)
    ys_sorted = _experts(plan, xs_sorted, p["w1"], p["b1"], p["w2"], p["b2"], nt_max)
    yp, ysm = _combine(plan, ys_sorted, flat(rp), rs, d4, flat(x1p), x1s, modp, mods,
                       wts["ln2_g"], wts["ln2_b"], r_max, alpha, l // tb)
    return (yp.reshape(b, l, d), ysm.reshape(steps, nbs, d), conv_p, sre_p[:, 0], sim_p[:, 0],
            conv_s, sre_s, sim_s)


def kernel(x_prompt, x_sample, state_conv, state_ssm_re, state_ssm_im, c_prompt, c_sample, w_ada, b_ada, w_in, b_in, w_dw, b_dw, conv_ln_g, conv_ln_b, w_pw, b_pw, lam_re, lam_im, log_dt, b_re, b_im, c_re, c_im, d_skip, w_sv, b_sv, w_sg, b_sg, w_out, b_out, ln1_g, ln1_b, w_router, b_router, w1, b1, w2, b2, ln2_g, ln2_b):
    stacked = dict(w_ada=w_ada, b_ada=b_ada, w_in=w_in, b_in=b_in, w_dw=w_dw, b_dw=b_dw,
                   conv_ln_g=conv_ln_g, conv_ln_b=conv_ln_b, w_pw=w_pw, b_pw=b_pw, lam_re=lam_re,
                   lam_im=lam_im, log_dt=log_dt, b_re=b_re, b_im=b_im, c_re=c_re, c_im=c_im,
                   d_skip=d_skip, w_sv=w_sv, b_sv=b_sv, w_sg=w_sg, b_sg=b_sg, w_out=w_out, b_out=b_out,
                   ln1_g=ln1_g, ln1_b=ln1_b, w_router=w_router, b_router=b_router, w1=w1, b1=b1,
                   w2=w2, b2=b2, ln2_g=ln2_g, ln2_b=ln2_b)
    depth = w_ada.shape[0]
    alpha = (2 * depth) ** 0.25
    b = x_prompt.shape[0]
    nbs = x_sample.shape[0]
    g, s = state_ssm_re.shape[2], state_ssm_re.shape[3]
    xp = x_prompt
    xs_tm = jnp.transpose(x_sample, (1, 0, 2))
    c_all = jnp.concatenate([c_prompt, c_sample], axis=0)
    conv_ps, re_ps, im_ps, conv_ss, re_ss, im_ss = [], [], [], [], [], []
    for layer in range(depth):
        p = {k: v[layer] for k, v in stacked.items()}
        hist_tm = jnp.transpose(state_conv[layer], (1, 0, 2))
        h0r = state_ssm_re[layer].reshape(nbs, g * s)
        h0i = state_ssm_im[layer].reshape(nbs, g * s)
        xp, xs_tm, conv_p, sre_p, sim_p, conv_s, sre_s, sim_s = _layer(
            xp, xs_tm, c_all, hist_tm, h0r, h0i, p, alpha)
        conv_ps.append(conv_p.astype(state_conv.dtype))
        re_ps.append(sre_p.reshape(b, g, s))
        im_ps.append(sim_p.reshape(b, g, s))
        conv_ss.append(jnp.transpose(conv_s, (1, 0, 2)).astype(state_conv.dtype))
        re_ss.append(sre_s.reshape(nbs, g, s))
        im_ss.append(sim_s.reshape(nbs, g, s))
    return (xp, jnp.transpose(xs_tm, (1, 0, 2)), jnp.stack(conv_ps), jnp.stack(re_ps), jnp.stack(im_ps),
            jnp.stack(conv_ss), jnp.stack(re_ss), jnp.stack(im_ss))
```

```python
import functools

import jax
import jax.numpy as jnp
from jax import lax
from jax.experimental import pallas as pl
from jax.experimental.pallas import tpu as pltpu

CONV_WIDTH = 31
SSM_GROUP = 16
SSM_STATE = 64
N_EXPERTS = 32
TOP_K = 4
SWIGLU_LIMIT = 7.0
SWIGLU_ALPHA = 1.702
LN_EPS = 1e-5

LANES = 128
SUBLANES = 8
TOKEN_BLOCK = 256
EXPERT_TILE = 512
SEG_PAD = 16
GROUPS_PER_BLOCK = LANES // SSM_GROUP
CONV_HIST = 32
CONV_ROWS = 32
ROUTE_IDX = N_EXPERTS
ROUTE_GATE = N_EXPERTS + TOP_K
VMEM_LIMIT = 56 * 1024 * 1024

_F32 = jnp.float32
_BF16 = jnp.bfloat16


def _bf(x):
    return x.astype(_BF16)


def _dot(a, b):
    return jnp.dot(a, b, preferred_element_type=_F32)


def _dot_nt(a, b):
    return lax.dot_general(a, b, (((1,), (1,)), ((), ())), preferred_element_type=_F32)


def _split(x):
    hi = _bf(x)
    lo = _bf(x - hi.astype(_F32))
    return hi, lo


def _sigmoid(x):
    return 1.0 / (1.0 + jnp.exp(-x))


def _gelu_tanh(x):
    return 0.5 * x * (1.0 + jnp.tanh(0.7978845608028654 * (x + 0.044715 * (x * x * x))))


def _layer_norm(x, g, b):
    mu = jnp.mean(x, axis=-1, keepdims=True)
    xc = x - mu
    var = jnp.mean(xc * xc, axis=-1, keepdims=True)
    return xc * lax.rsqrt(var + LN_EPS) * g + b


def _const_spec(shape):
    nd = len(shape)
    return pl.BlockSpec(shape, lambda *_: (0,) * nd)


def _params(sem):
    return pltpu.CompilerParams(dimension_semantics=sem, vmem_limit_bytes=VMEM_LIMIT)


def _ada_kernel(c_ref, w_ref, b_ref, o_ref):
    c = c_ref[...]
    s_hi, s_lo = _split(c * _sigmoid(c))
    w_hi, w_lo = _split(w_ref[...])
    o_ref[...] = _dot(s_hi, w_hi) + _dot(s_lo, w_hi) + _dot(s_hi, w_lo) + b_ref[...]


def _ada(c, w, b):
    n, d = c.shape
    cols = w.shape[1]
    tn = d
    return pl.pallas_call(
        _ada_kernel,
        grid=(cols // tn,),
        in_specs=[pl.BlockSpec((n, d), lambda j: (0, 0)),
                  pl.BlockSpec((d, tn), lambda j: (0, j)),
                  pl.BlockSpec((1, tn), lambda j: (0, j))],
        out_specs=pl.BlockSpec((n, tn), lambda j: (0, j)),
        out_shape=jax.ShapeDtypeStruct((n, cols), _F32),
        compiler_params=_params(("arbitrary",)),
        name="ada",
    )(c, w, b.reshape(1, cols))


def _tiled(fn, tile, *args, vecs=()):
    n, m = args[0].shape
    tr, tc = min(tile[0], n), min(tile[1], m)
    rows_out = None
    for r0 in range(0, n, tr):
        cols_out = None
        for c0 in range(0, m, tc):
            res = fn(*[a[r0:r0 + tr, c0:c0 + tc] for a in args], *[v[:, c0:c0 + tc] for v in vecs])
            res = res if isinstance(res, tuple) else (res,)
            if cols_out is None:
                cols_out = [[] for _ in res]
            for acc, v in zip(cols_out, res):
                acc.append(v)
        row_vals = [c[0] if len(c) == 1 else jnp.concatenate(c, axis=1) for c in cols_out]
        if rows_out is None:
            rows_out = [[] for _ in row_vals]
        for acc, v in zip(rows_out, row_vals):
            acc.append(v)
    outs = [r[0] if len(r) == 1 else jnp.concatenate(r, axis=0) for r in rows_out]
    return outs[0] if len(outs) == 1 else tuple(outs)


def _in_proj(hb, w_in, b_in, lo, hi):
    return _dot(hb, w_in[:, lo:hi]) + b_in[:, lo:hi]


def _conv_tail(v, cln_g, cln_b, w_pw, b_pw):
    g, b = cln_g[...], cln_b[...]

    def norm_swish(vb):
        vb = _layer_norm(vb, g, b)
        return vb * _sigmoid(vb)
    v = _tiled(norm_swish, (SUBLANES, v.shape[1]), v)
    return _dot(_bf(v), w_pw[...]) + b_pw[...]


def _ssm_tail(y, zs, dskip, w_sv, b_sv, w_sg, b_sg):
    yg = _tiled(lambda yb, zb, db: _bf(_gelu_tanh(yb + db * zb)), (2 * SUBLANES, 4 * LANES), y, zs, vecs=(dskip,))
    sv, sg = _dot(yg, w_sv[...]), _dot(yg, w_sg[...])
    return _tiled(lambda a, b, bv, bg: (a + bv) * _sigmoid(b + bg), (SUBLANES, 4 * LANES),
                  sv, sg, vecs=(b_sv, b_sg))


def _route(h2, wr_hi, wr_lo, b_r):
    n = h2.shape[0]
    ne = b_r.shape[0]
    h_hi, h_lo = _split(h2)
    logits = (_dot(h_hi, wr_hi[...]) + _dot(h_lo, wr_hi[...]) + _dot(h_hi, wr_lo[...])).T[:ne, :] + b_r[...]
    row = lax.broadcasted_iota(jnp.int32, (ne, n), 0).astype(_F32)
    neg = jnp.float32(-jnp.inf)
    cur = logits
    vals, idxs = [], []
    for _ in range(TOP_K):
        m = jnp.max(cur, axis=0, keepdims=True)
        idx = jnp.min(jnp.where(cur == m, row, float(ne)), axis=0, keepdims=True)
        vals.append(m)
        idxs.append(idx)
        cur = jnp.where(row == idx, neg, cur)
    es = [jnp.exp(v - vals[0]) for v in vals]
    tot = es[0]
    for e in es[1:]:
        tot = tot + e
    inv = 1.0 / tot
    selected = jnp.zeros((ne, n), _F32)
    row8 = lax.broadcasted_iota(jnp.int32, (2 * TOP_K, n), 0)
    extra = jnp.zeros((2 * TOP_K, n), _F32)
    for k in range(TOP_K):
        gate = es[k] * inv
        selected = selected + jnp.where(row == idxs[k], 1.0, 0.0)
        extra = extra + jnp.where(row8 == k, idxs[k], 0.0) + jnp.where(row8 == TOP_K + k, gate, 0.0)
    rest = jnp.zeros((LANES - ne - 2 * TOP_K, n), _F32)
    return jnp.concatenate([selected, extra, rest], axis=0).T


def _members(route):
    lane = lax.broadcasted_iota(jnp.int32, route.shape, 1)
    return jnp.where(lane < N_EXPERTS, route, 0.0)


def _block_counts(route):
    return jnp.sum(_members(route), axis=0, keepdims=True)


def _mixer_tail(x, zg, conv_out, ssm_out, g1, sh2, sc2, alpha, w_out, b_out,
                ln1_g, ln1_b, wr_hi, wr_lo, b_r):
    d = x.shape[1]
    merged = _tiled(lambda gc, gs, co, so: _bf(_sigmoid(gc) * co + _sigmoid(gs) * so),
                    (2 * SUBLANES, 4 * LANES), zg[:, :d], zg[:, d:], conv_out, ssm_out)
    m = _dot(merged, w_out[...])

    def norm_mod(xb, mb, gb, scb, shb, bo, lg, lb):
        x1b = _layer_norm(alpha * xb + gb * (mb + bo), lg, lb)
        return x1b, x1b * (1.0 + scb) + shb
    mods = (g1, sc2, sh2)
    per_seq = g1.shape[0] == 1
    x1, h2 = _tiled(norm_mod, (SUBLANES, d), x, m, *(() if per_seq else mods),
                    vecs=(mods if per_seq else ()) + (b_out, ln1_g, ln1_b))
    return x1, h2, _route(h2, wr_hi, wr_lo, b_r)


def _mixer_prompt_kernel(x_ref, mod_ref, w_in, b_in, w_dw, b_dw, cln_g, cln_b, w_pw, b_pw,
                         bre, bim, cc, einv_r, einv_i, epow_r, epow_i, lamb_r, lamb_i, dskip,
                         w_sv, b_sv, w_sg, b_sg, w_out, b_out, ln1_g, ln1_b, wr_hi, wr_lo, b_r,
                         x1_ref, h2_ref, route_ref, cnt_ref, cstate_ref, sre_ref, sim_ref,
                         ubuf, ush, car_r, car_i, *, alpha):
    c = pl.program_id(1)
    last = pl.num_programs(1) - 1
    tl, d = x_ref.shape[1], x_ref.shape[2]
    dc = w_pw.shape[0]
    dsm = dskip.shape[1]
    nj = bre.shape[0]
    rows = bre.shape[1]

    @pl.when(c == 0)
    def _():
        ubuf[0:CONV_HIST, :] = jnp.zeros((CONV_HIST, dc), _F32)
        car_r[...] = jnp.zeros(car_r.shape, _F32)
        car_i[...] = jnp.zeros(car_i.shape, _F32)

    x = x_ref[0]
    mod = mod_ref[0]
    sh1, sc1, g1 = mod[:, 0:d], mod[:, d:2 * d], mod[:, 2 * d:3 * d]
    sh2, sc2 = mod[:, 3 * d:4 * d], mod[:, 4 * d:5 * d]
    hb = _bf(x * (1.0 + sc1) + sh1)

    u = _in_proj(hb, w_in, b_in, 0, dc) * _sigmoid(_in_proj(hb, w_in, b_in, dc, 2 * dc))
    ubuf[CONV_HIST:CONV_HIST + tl, :] = u
    span = ush.shape[1]
    for r in range(1, SUBLANES):
        ush[r - 1] = ubuf[r:r + span, :]

    n_blocks = tl // CONV_ROWS
    gate_lo = 2 * dc + dsm
    gate_w = 2 * d // n_blocks
    vblocks, zg_cols = [], []
    for blk in range(n_blocks):
        r0 = blk * CONV_ROWS
        acc = jnp.broadcast_to(b_dw[...], (CONV_ROWS, dc))
        for k in range(CONV_WIDTH):
            q, r = divmod(CONV_HIST - (CONV_WIDTH - 1) + k, SUBLANES)
            rows_k = slice(r0 + q * SUBLANES, r0 + q * SUBLANES + CONV_ROWS)
            tap = ubuf[rows_k, :] if r == 0 else ush[r - 1, rows_k, :]
            acc = acc + w_dw[k:k + 1, :] * tap
        vblocks.append(acc)
        zg_cols.append(_in_proj(hb, w_in, b_in, gate_lo + blk * gate_w, gate_lo + (blk + 1) * gate_w))
    ubuf[0:CONV_HIST, :] = ubuf[tl:tl + CONV_HIST, :]
    conv_out = _conv_tail(jnp.concatenate(vblocks, axis=0), cln_g, cln_b, w_pw, b_pw)

    zs = _in_proj(hb, w_in, b_in, 2 * dc, 2 * dc + dsm)
    zst = _bf(zs.T)
    tri = jnp.where(lax.broadcasted_iota(jnp.int32, (tl, tl), 0)
                    <= lax.broadcasted_iota(jnp.int32, (tl, tl), 1), 1.0, 0.0).astype(_BF16)
    def project(j):
        zj = zst[j * LANES:(j + 1) * LANES, :]
        return _dot(bre[j], zj), _dot(bim[j], zj)

    def scale_in(j, bu):
        r = slice(j * rows, (j + 1) * rows)
        eir, eii = einv_r[r, :], einv_i[r, :]
        return _bf(bu[0] * eir - bu[1] * eii), _bf(bu[0] * eii + bu[1] * eir)

    def prefix(v):
        return _dot(v[0], tri), _dot(v[1], tri)

    def scale_out(j, cum):
        r = slice(j * rows, (j + 1) * rows)
        cr = jnp.broadcast_to(car_r[r, LANES - 1:LANES], (rows, LANES))
        ci = jnp.broadcast_to(car_i[r, LANES - 1:LANES], (rows, LANES))
        lr, li = lamb_r[r, :], lamb_i[r, :]
        cum_r = cum[0] + jnp.concatenate([lr * cr - li * ci] * (tl // LANES), axis=1)
        cum_i = cum[1] + jnp.concatenate([lr * ci + li * cr] * (tl // LANES), axis=1)
        epr, epi = epow_r[r, :], epow_i[r, :]
        h_r = cum_r * epr - cum_i * epi
        h_i = cum_r * epi + cum_i * epr
        car_r[r, :] = h_r[:, tl - LANES:tl]
        car_i[r, :] = h_i[:, tl - LANES:tl]
        return jnp.concatenate([_bf(h_r), _bf(h_i)], axis=0)

    bu, v, cum, hcat, yts = {}, {}, {}, {}, {}
    for t in range(nj + 4):
        if t < nj:
            bu[t] = project(t)
        if 0 <= t - 2 < nj:
            cum[t - 2] = prefix(v.pop(t - 2))
        if 0 <= t - 4 < nj:
            yts[t - 4] = _dot(cc[t - 4], hcat.pop(t - 4))
        if 0 <= t - 1 < nj:
            v[t - 1] = scale_in(t - 1, bu.pop(t - 1))
        if 0 <= t - 3 < nj:
            hcat[t - 3] = scale_out(t - 3, cum.pop(t - 3))
    ssm_out = _ssm_tail(jnp.concatenate([yts[j] for j in range(nj)], axis=0).T, zs, dskip,
                        w_sv, b_sv, w_sg, b_sg)

    x1, h2, route = _mixer_tail(x, jnp.concatenate(zg_cols, axis=1), conv_out, ssm_out, g1, sh2, sc2, alpha,
                                w_out, b_out, ln1_g, ln1_b, wr_hi, wr_lo, b_r)
    x1_ref[0] = x1
    h2_ref[0] = _bf(h2)
    route_ref[0] = route
    cnt_ref[0, 0] = _block_counts(route)

    @pl.when(c == last)
    def _():
        cstate_ref[0] = ubuf[CONV_HIST + tl - (CONV_WIDTH - 1):CONV_HIST + tl, :]
        sre_ref[0] = car_r[...].T[LANES - 1:LANES, :]
        sim_ref[0] = car_i[...].T[LANES - 1:LANES, :]


def _mixer_prompt(x, mod, wts, alpha):
    b, l, d = x.shape
    tl = TOKEN_BLOCK
    nc = l // tl
    dc = wts["w_pw"].shape[0]
    nstate = wts["lamb_r"].shape[0]
    names = ["w_in", "b_in", "w_dw", "b_dw", "cln_g", "cln_b", "w_pw", "b_pw", "bre", "bim", "cc",
             "einv_r", "einv_i", "epow_r", "epow_i", "lamb_r", "lamb_i", "dskip",
             "w_sv", "b_sv", "w_sg", "b_sg", "w_out", "b_out", "ln1_g", "ln1_b",
             "wr_hi", "wr_lo", "b_r"]
    consts = [wts[n] for n in names]
    tok = lambda bi, ci: (bi, ci, 0)
    seq = lambda bi, ci: (bi, 0, 0)
    return pl.pallas_call(
        functools.partial(_mixer_prompt_kernel, alpha=alpha),
        grid=(b, nc),
        in_specs=[pl.BlockSpec((1, tl, d), tok), pl.BlockSpec((1, 1, mod.shape[-1]), seq)]
                 + [_const_spec(a.shape) for a in consts],
        out_specs=[pl.BlockSpec((1, tl, d), tok), pl.BlockSpec((1, tl, d), tok),
                   pl.BlockSpec((1, tl, LANES), tok),
                   pl.BlockSpec((1, 1, 1, LANES), lambda bi, ci: (bi, ci, 0, 0)),
                   pl.BlockSpec((1, CONV_WIDTH - 1, dc), seq),
                   pl.BlockSpec((1, 1, nstate), seq), pl.BlockSpec((1, 1, nstate), seq)],
        out_shape=[jax.ShapeDtypeStruct((b, l, d), _F32), jax.ShapeDtypeStruct((b, l, d), _BF16),
                   jax.ShapeDtypeStruct((b, l, LANES), _F32),
                   jax.ShapeDtypeStruct((b, nc, 1, LANES), _F32),
                   jax.ShapeDtypeStruct((b, CONV_WIDTH - 1, dc), _F32),
                   jax.ShapeDtypeStruct((b, 1, nstate), _F32),
                   jax.ShapeDtypeStruct((b, 1, nstate), _F32)],
        scratch_shapes=[pltpu.VMEM((CONV_HIST + tl, dc), _F32),
                        pltpu.VMEM((SUBLANES - 1, tl + CONV_HIST - SUBLANES, dc), _F32),
                        pltpu.VMEM((nstate, LANES), _F32), pltpu.VMEM((nstate, LANES), _F32)],
        compiler_params=_params(("arbitrary", "arbitrary")),
        name="mixer_prompt",
    )(x, mod, *consts)


def _mixer_sample_kernel(x_ref, mod_ref, hist_ref, h0r_ref, h0i_ref,
                         w_in, b_in, w_dw, b_dw, cln_g, cln_b, w_pw, b_pw,
                         bre, bim, cc, lrow_r, lrow_i, dskip,
                         w_sv, b_sv, w_sg, b_sg, w_out, b_out, ln1_g, ln1_b, wr_hi, wr_lo, b_r,
                         x1_ref, h2_ref, route_ref, cnt_ref, cstate_ref, sre_ref, sim_ref, *, alpha):
    steps, nb, d = x_ref.shape
    dc = w_pw.shape[0]
    dsm = dskip.shape[1]
    nj = bre.shape[0]
    rows = bre.shape[1]
    nhist = CONV_WIDTH - 1
    x = x_ref[...].reshape(steps * nb, d)
    mod = jnp.concatenate([mod_ref[...]] * steps, axis=0)
    sh1, sc1, g1 = mod[:, 0:d], mod[:, d:2 * d], mod[:, 2 * d:3 * d]
    sh2, sc2 = mod[:, 3 * d:4 * d], mod[:, 4 * d:5 * d]
    hb = _bf(x * (1.0 + sc1) + sh1)

    u = _in_proj(hb, w_in, b_in, 0, dc) * _sigmoid(_in_proj(hb, w_in, b_in, dc, 2 * dc))
    us = [u[t * nb:(t + 1) * nb, :] for t in range(steps)]
    vs = []
    for t in range(steps):
        acc = jnp.broadcast_to(b_dw[...], (nb, dc))
        for i in range(t, nhist):
            acc = acc + w_dw[i - t:i - t + 1, :] * hist_ref[i]
        for s in range(t + 1):
            acc = acc + w_dw[nhist + s - t:nhist + s - t + 1, :] * us[s]
        vs.append(acc)
    for i in range(nhist):
        src = i + steps
        cstate_ref[i] = hist_ref[src] if src < nhist else us[src - nhist]
    conv_out = _conv_tail(jnp.concatenate(vs, axis=0), cln_g, cln_b, w_pw, b_pw)

    zs = _in_proj(hb, w_in, b_in, 2 * dc, 2 * dc + dsm)
    zsb = _bf(zs)
    h_r, h_i = h0r_ref[...], h0i_ref[...]
    lr, li = lrow_r[...], lrow_i[...]
    ys = []
    for t in range(steps):
        zt = zsb[t * nb:(t + 1) * nb, :]
        bu_r = jnp.concatenate([_dot_nt(zt[:, j * LANES:(j + 1) * LANES], bre[j]) for j in range(nj)], axis=1)
        bu_i = jnp.concatenate([_dot_nt(zt[:, j * LANES:(j + 1) * LANES], bim[j]) for j in range(nj)], axis=1)
        h_r, h_i = lr * h_r - li * h_i + bu_r, lr * h_i + li * h_r + bu_i
        hrb, hib = _bf(h_r), _bf(h_i)
        ys.append(jnp.concatenate(
            [_dot_nt(jnp.concatenate([hrb[:, j * rows:(j + 1) * rows], hib[:, j * rows:(j + 1) * rows]], axis=1),
                     cc[j]) for j in range(nj)], axis=1))
    sre_ref[...] = h_r
    sim_ref[...] = h_i
    ssm_out = _ssm_tail(jnp.concatenate(ys, axis=0), zs, dskip, w_sv, b_sv, w_sg, b_sg)

    zg = _in_proj(hb, w_in, b_in, 2 * dc + dsm, 2 * dc + dsm + 2 * d)
    x1, h2, route = _mixer_tail(x, zg, conv_out, ssm_out, g1, sh2, sc2, alpha,
                                w_out, b_out, ln1_g, ln1_b, wr_hi, wr_lo, b_r)
    x1_ref[...] = x1
    h2_ref[...] = _bf(h2)
    route_ref[...] = route
    for blk in range(cnt_ref.shape[0]):
        cnt_ref[blk] = _block_counts(route[blk * TOKEN_BLOCK:(blk + 1) * TOKEN_BLOCK, :])


def _mixer_sample(x_tm, mod, hist_tm, h0r, h0i, wts, alpha):
    steps, nb, d = x_tm.shape
    n = steps * nb
    dc = wts["w_pw"].shape[0]
    nstate = h0r.shape[1]
    names = ["w_in", "b_in", "w_dw", "b_dw", "cln_g", "cln_b", "w_pw", "b_pw", "bre", "bim", "cc",
             "lrow_r", "lrow_i", "dskip", "w_sv", "b_sv", "w_sg", "b_sg", "w_out", "b_out",
             "ln1_g", "ln1_b", "wr_hi", "wr_lo", "b_r"]
    args = [x_tm, mod, hist_tm, h0r, h0i] + [wts[k] for k in names]
    return pl.pallas_call(
        functools.partial(_mixer_sample_kernel, alpha=alpha),
        grid=(1,),
        in_specs=[_const_spec(a.shape) for a in args],
        out_specs=[_const_spec((n, d)), _const_spec((n, d)), _const_spec((n, LANES)),
                   _const_spec((n // TOKEN_BLOCK, 1, LANES)),
                   _const_spec((CONV_WIDTH - 1, nb, dc)),
                   _const_spec((nb, nstate)), _const_spec((nb, nstate))],
        out_shape=[jax.ShapeDtypeStruct((n, d), _F32), jax.ShapeDtypeStruct((n, d), _BF16),
                   jax.ShapeDtypeStruct((n, LANES), _F32),
                   jax.ShapeDtypeStruct((n // TOKEN_BLOCK, 1, LANES), _F32),
                   jax.ShapeDtypeStruct((CONV_WIDTH - 1, nb, dc), _F32),
                   jax.ShapeDtypeStruct((nb, nstate), _F32), jax.ShapeDtypeStruct((nb, nstate), _F32)],
        compiler_params=_params(("arbitrary",)),
        name="mixer_sample",
    )(*args)


def _round_up(x, m):
    return (x + m - 1) // m * m


def _plan_sizes(n_tokens):
    nb = n_tokens // TOKEN_BLOCK
    r_max = _round_up(TOP_K * TOKEN_BLOCK + 2 * N_EXPERTS * (SEG_PAD - 1), 2 * LANES)
    rows_max = n_tokens * TOP_K + N_EXPERTS * (EXPERT_TILE - 1)
    nt_max = -(-rows_max // EXPERT_TILE)
    return nb, r_max, nt_max


def _routing_plan(cnt, nt_max):
    nb, ne = cnt.shape
    before = jnp.cumsum(cnt, axis=0) - cnt
    tot = jnp.sum(cnt, axis=0)
    cin = before % SEG_PAD
    last = (jnp.arange(nb, dtype=jnp.int32) == nb - 1)[:, None]
    active = (cnt > 0) | (last & (cin > 0))
    seg = jnp.where(active, _round_up(cin + cnt, SEG_PAD), 0)
    loc = jnp.cumsum(seg, axis=1) - seg
    totpad = _round_up(tot, EXPERT_TILE)
    eend = jnp.cumsum(totpad)
    estart = eend - totpad
    goff = estart[None, :] + before - cin
    nfull = jnp.where(last, seg, jnp.where(active, (cin + cnt) // SEG_PAD * SEG_PAD, 0))
    cout = jnp.where(last | ~active, 0, (cin + cnt) % SEG_PAD)
    n_tiles = (eend[-1] // EXPERT_TILE).astype(jnp.int32)
    tiles = jnp.minimum(jnp.arange(nt_max, dtype=jnp.int32), n_tiles - 1)
    te = jnp.sum(((eend // EXPERT_TILE)[None, :] <= tiles[:, None]).astype(jnp.int32), axis=1)
    te = jnp.minimum(te, ne - 1)
    tot16 = _round_up(tot, SEG_PAD)
    return dict(seg=seg, loc=loc, goff=goff, nfull=nfull, cin=jnp.where(active, cin, 0), cout=cout,
                used=jnp.sum(seg, axis=1).astype(jnp.int32),
                n_tiles=n_tiles.reshape(1), te=te,
                npad=(totpad - tot16).astype(jnp.int32), pad_base=(estart + tot16).astype(jnp.int32))


def _dest_columns(route, loc_row):
    n = route.shape[0]
    lane_i = lax.broadcasted_iota(jnp.int32, (n, LANES), 1)
    lane = lane_i.astype(_F32)
    member = _bf(_members(route))
    strict = jnp.where(lax.broadcasted_iota(jnp.int32, (n, n), 1)
                       < lax.broadcasted_iota(jnp.int32, (n, n), 0), 1.0, 0.0).astype(_BF16)
    dest = _dot(strict, member) + loc_row
    d4 = jnp.zeros((n, LANES), _F32)
    for k in range(TOP_K):
        idx = route[:, ROUTE_IDX + k:ROUTE_IDX + k + 1]
        dk = jnp.sum(jnp.where(lane == idx, dest, 0.0), axis=-1, keepdims=True)
        d4 = d4 + jnp.where(lane_i == k, dk, 0.0)
    return d4


def _typical_rows(r_max):
    return min(r_max, _round_up(TOP_K * TOKEN_BLOCK + N_EXPERTS * SEG_PAD, 2 * LANES))


def _dispatch_kernel(nfull, sloc, goff, cin, cout, used, npad, pad_base, hp_ref, hs_ref, rp_ref, rs_ref, loc_ref,
                     xs_hbm, d4_ref, buf, stage, zbuf, sem, zsem, *, nbp):
    i = pl.program_id(0)
    nblk = pl.num_programs(0)
    slot = lax.rem(i, 2)
    tb = hp_ref.shape[0]
    r_max = buf.shape[1]
    is_p = i < nbp
    h = jnp.where(is_p, hp_ref[...], hs_ref[...])
    route = jnp.where(is_p, rp_ref[...], rs_ref[...])
    d4 = _dest_columns(route, loc_ref[0])
    d4_ref[...] = d4
    d4t = d4.T

    def compact(rows):
        row = lax.broadcasted_iota(jnp.int32, (rows, tb), 0).astype(_F32)
        p = jnp.zeros((rows, tb), _F32)
        for k in range(TOP_K):
            p = p + jnp.where(row == d4t[k:k + 1, :], 1.0, 0.0)
        buf[slot, 0:rows, :] = _bf(_dot(_bf(p), h))

    r_typ = _typical_rows(r_max)

    @pl.when(i == 0)
    def _():
        buf[...] = jnp.zeros(buf.shape, buf.dtype)
        stage[...] = jnp.zeros(stage.shape, stage.dtype)

    @pl.when(used[i] <= r_typ)
    def _():
        compact(r_typ)

    @pl.when(used[i] > r_typ)
    def _():
        compact(r_max)

    def seg_copy(blk, s, e):
        n = pl.multiple_of(nfull[blk * N_EXPERTS + e], SEG_PAD)
        src = pl.multiple_of(sloc[blk * N_EXPERTS + e], SEG_PAD)
        dst = pl.multiple_of(goff[blk * N_EXPERTS + e], SEG_PAD)
        return pltpu.make_async_copy(buf.at[s, pl.ds(src, n), :], xs_hbm.at[pl.ds(dst, n), :], sem.at[s])

    def pad_copy(e):
        n = pl.multiple_of(npad[e], SEG_PAD)
        dst = pl.multiple_of(pad_base[e], SEG_PAD)
        return pltpu.make_async_copy(zbuf.at[pl.ds(0, n), :], xs_hbm.at[pl.ds(dst, n), :], zsem.at[0])

    def for_segments(blk, fn):
        for e in range(N_EXPERTS):
            @pl.when(nfull[blk * N_EXPERTS + e] > 0)
            def _(e=e):
                fn(e)

    for e in range(N_EXPERTS):
        first = pl.ds(pl.multiple_of(sloc[i * N_EXPERTS + e], SEG_PAD), SEG_PAD)
        carried = jnp.where(cin[i * N_EXPERTS + e] > 0, stage[e], jnp.zeros_like(stage[e]))
        buf[slot, first, :] = buf[slot, first, :] + carried

    for_segments(i, lambda e: seg_copy(i, slot, e).start())

    for e in range(N_EXPERTS):
        rest = pl.multiple_of(sloc[i * N_EXPERTS + e] + nfull[i * N_EXPERTS + e], SEG_PAD)
        stage[e] = jnp.where(cout[i * N_EXPERTS + e] > 0, buf[slot, pl.ds(rest, SEG_PAD), :], stage[e])

    @pl.when(i > 0)
    def _():
        for_segments(i - 1, lambda e: seg_copy(i - 1, 1 - slot, e).wait())

    @pl.when(i == nblk - 1)
    def _():
        zbuf[...] = jnp.zeros(zbuf.shape, zbuf.dtype)
        for e in range(N_EXPERTS):
            @pl.when(npad[e] > 0)
            def _(e=e):
                pad_copy(e).start()
        for e in range(N_EXPERTS):
            @pl.when(npad[e] > 0)
            def _(e=e):
                pad_copy(e).wait()
        for_segments(i, lambda e: seg_copy(i, slot, e).wait())


def _dispatch(plan, h2p, h2s, rp, rs, r_max, nt_max):
    tb = TOKEN_BLOCK
    d = h2p.shape[1]
    nbp, nbs = h2p.shape[0] // tb, h2s.shape[0] // tb
    nb = nbp + nbs
    first_row = (plan["loc"] + plan["cin"]).astype(_F32)
    loc = jnp.pad(first_row, ((0, 0), (0, LANES - N_EXPERTS))).reshape(nb, 1, LANES)
    pidx = lambda i, *_: (jnp.minimum(i, nbp - 1), 0)
    sidx = lambda i, *_: (jnp.maximum(i - nbp, 0), 0)
    grid_spec = pltpu.PrefetchScalarGridSpec(
        num_scalar_prefetch=8,
        grid=(nb,),
        in_specs=[pl.BlockSpec((tb, d), pidx), pl.BlockSpec((tb, d), sidx),
                  pl.BlockSpec((tb, LANES), pidx), pl.BlockSpec((tb, LANES), sidx),
                  pl.BlockSpec((1, 1, LANES), lambda i, *_: (i, 0, 0))],
        out_specs=[pl.BlockSpec(memory_space=pl.ANY), pl.BlockSpec((tb, LANES), lambda i, *_: (i, 0))],
        scratch_shapes=[pltpu.VMEM((2, r_max, d), _BF16), pltpu.VMEM((N_EXPERTS, SEG_PAD, d), _BF16),
                        pltpu.VMEM((EXPERT_TILE, d), _BF16),
                        pltpu.SemaphoreType.DMA((2,)), pltpu.SemaphoreType.DMA((1,))])
    flat = lambda a: a.astype(jnp.int32).reshape(-1)
    return pl.pallas_call(
        functools.partial(_dispatch_kernel, nbp=nbp),
        grid_spec=grid_spec,
        out_shape=[jax.ShapeDtypeStruct((nt_max * EXPERT_TILE, d), _BF16),
                   jax.ShapeDtypeStruct((nb * tb, LANES), _F32)],
        compiler_params=_params(("arbitrary",)),
        name="dispatch",
    )(flat(plan["nfull"]), flat(plan["loc"]), flat(plan["goff"]), flat(plan["cin"]), flat(plan["cout"]),
      plan["used"], plan["npad"], plan["pad_base"], h2p, h2s, rp, rs, loc)


def _expert_kernel(te, n_tiles, x_ref, w1_ref, b1_ref, w2_ref, b2_ref, y_ref, w1b, w2b):
    i = pl.program_id(0)
    dff = w2_ref.shape[1]

    @pl.when(i < n_tiles[0])
    def _():
        prev = te[jnp.maximum(i - 1, 0)]

        @pl.when((i == 0) | (te[i] != prev))
        def _():
            w1b[...] = _bf(w1_ref[0])
            w2b[...] = _bf(w2_ref[0])

        gu = _dot(x_ref[...], w1b[...]) + b1_ref[0]
        g = jnp.minimum(gu[:, :dff], SWIGLU_LIMIT)
        up = jnp.clip(gu[:, dff:], -SWIGLU_LIMIT, SWIGLU_LIMIT)
        act = g * _sigmoid(SWIGLU_ALPHA * g) * (up + 1.0)
        y_ref[...] = _bf(_dot(_bf(act), w2b[...]) + b2_ref[0])


def _experts(plan, xs, w1, b1, w2, b2, nt_max):
    ne, d, dff2 = w1.shape
    dff = w2.shape[1]
    tm = EXPERT_TILE
    tile = lambda i, te, nt: (jnp.minimum(i, nt[0] - 1), 0)
    wsel = lambda i, te, nt: (te[i], 0, 0)
    grid_spec = pltpu.PrefetchScalarGridSpec(
        num_scalar_prefetch=2,
        grid=(nt_max,),
        in_specs=[pl.BlockSpec((tm, d), tile),
                  pl.BlockSpec((1, d, dff2), wsel), pl.BlockSpec((1, 1, dff2), wsel),
                  pl.BlockSpec((1, dff, d), wsel), pl.BlockSpec((1, 1, d), wsel)],
        out_specs=pl.BlockSpec((tm, d), tile),
        scratch_shapes=[pltpu.VMEM((d, dff2), _BF16), pltpu.VMEM((dff, d), _BF16)])
    return pl.pallas_call(
        _expert_kernel,
        grid_spec=grid_spec,
        out_shape=jax.ShapeDtypeStruct(xs.shape, _BF16),
        compiler_params=_params(("arbitrary",)),
        name="experts",
    )(plan["te"], plan["n_tiles"], xs, w1, b1.reshape(ne, 1, dff2), w2, b2.reshape(ne, 1, d))


def _combine_kernel(seg, sloc, goff, used, rp_ref, rs_ref, d4_ref, x1p_ref, x1s_ref, g2p_ref, g2s_ref,
                    ln2_g, ln2_b, ys_hbm, yp_ref, ysm_ref, buf, fbuf, sem, *, nbp, alpha):
    i = pl.program_id(0)
    nblk = pl.num_programs(0)
    slot = lax.rem(i, 2)
    tb = rp_ref.shape[0]
    r_max = buf.shape[1]

    def seg_copy(blk, s, e):
        n = pl.multiple_of(seg[blk * N_EXPERTS + e], SEG_PAD)
        src = pl.multiple_of(goff[blk * N_EXPERTS + e], SEG_PAD)
        dst = pl.multiple_of(sloc[blk * N_EXPERTS + e], SEG_PAD)
        return pltpu.make_async_copy(ys_hbm.at[pl.ds(src, n), :], buf.at[s, pl.ds(dst, n), :], sem.at[s])

    def for_segments(blk, fn):
        for e in range(N_EXPERTS):
            @pl.when(seg[blk * N_EXPERTS + e] > 0)
            def _(e=e):
                fn(e)

    @pl.when(i == 0)
    def _():
        buf[...] = jnp.zeros(buf.shape, buf.dtype)
        for_segments(0, lambda e: seg_copy(0, 0, e).start())

    @pl.when(i + 1 < nblk)
    def _():
        for_segments(i + 1, lambda e: seg_copy(i + 1, 1 - slot, e).start())

    for_segments(i, lambda e: seg_copy(i, slot, e).wait())

    is_p = i < nbp
    route = jnp.where(is_p, rp_ref[...], rs_ref[...])
    d4 = d4_ref[...]

    def gather(rows):
        col = lax.broadcasted_iota(jnp.int32, (tb, rows), 1).astype(_F32)
        pg = jnp.zeros((tb, rows), _F32)
        for k in range(TOP_K):
            gate = route[:, ROUTE_GATE + k:ROUTE_GATE + k + 1]
            pg = pg + jnp.where(col == d4[:, k:k + 1], gate, 0.0)
        fbuf[...] = _dot(_bf(pg), buf[slot, 0:rows, :])

    r_typ = _typical_rows(r_max)

    @pl.when(used[i] <= r_typ)
    def _():
        gather(r_typ)

    @pl.when(used[i] > r_typ)
    def _():
        gather(r_max)

    f = fbuf[...]
    x1 = jnp.where(is_p, x1p_ref[...], x1s_ref[...])
    g2s = jnp.concatenate([g2s_ref[...]] * (tb // g2s_ref.shape[0]), axis=0)
    g2 = jnp.where(is_p, jnp.broadcast_to(g2p_ref[0], g2s.shape), g2s)
    y = _layer_norm(alpha * x1 + g2 * f, ln2_g[...], ln2_b[...])

    @pl.when(is_p)
    def _():
        yp_ref[...] = y

    @pl.when(jnp.logical_not(is_p))
    def _():
        ysm_ref[...] = y


def _combine(plan, ys, rp, rs, d4, x1p, x1s, modp, mods, ln2_g, ln2_b, r_max, alpha, blocks_per_seq):
    tb = TOKEN_BLOCK
    d = x1p.shape[1]
    nbp, nbs = x1p.shape[0] // tb, x1s.shape[0] // tb
    nb = nbp + nbs
    pidx = lambda i, *_: (jnp.minimum(i, nbp - 1), 0)
    sidx = lambda i, *_: (jnp.maximum(i - nbp, 0), 0)
    g2_lane_block = 5
    flat = lambda a: a.astype(jnp.int32).reshape(-1)
    grid_spec = pltpu.PrefetchScalarGridSpec(
        num_scalar_prefetch=4,
        grid=(nb,),
        in_specs=[pl.BlockSpec((tb, LANES), pidx), pl.BlockSpec((tb, LANES), sidx),
                  pl.BlockSpec((tb, LANES), lambda i, *_: (i, 0)),
                  pl.BlockSpec((tb, d), pidx), pl.BlockSpec((tb, d), sidx),
                  pl.BlockSpec((1, 1, d), lambda i, *_: (jnp.minimum(i, nbp - 1) // blocks_per_seq, 0,
                                                         g2_lane_block)),
                  pl.BlockSpec((mods.shape[0], d), lambda i, *_: (0, g2_lane_block)),
                  pl.BlockSpec((1, d), lambda i, *_: (0, 0)), pl.BlockSpec((1, d), lambda i, *_: (0, 0)),
                  pl.BlockSpec(memory_space=pl.ANY)],
        out_specs=[pl.BlockSpec((tb, d), pidx), pl.BlockSpec((tb, d), sidx)],
        scratch_shapes=[pltpu.VMEM((2, r_max, d), _BF16), pltpu.VMEM((tb, d), _F32),
                        pltpu.SemaphoreType.DMA((2,))])
    return pl.pallas_call(
        functools.partial(_combine_kernel, nbp=nbp, alpha=alpha),
        grid_spec=grid_spec,
        out_shape=[jax.ShapeDtypeStruct(x1p.shape, _F32), jax.ShapeDtypeStruct(x1s.shape, _F32)],
        compiler_params=_params(("arbitrary",)),
        name="combine",
    )(flat(plan["seg"]), flat(plan["loc"]), flat(plan["goff"]), plan["used"],
      rp, rs, d4, x1p, x1s, modp, mods, ln2_g, ln2_b, ys)


def _complex_powers(zr, zi, n):
    low = 16
    if n <= low or n % low:
        return _bit_powers(zr, zi, n)
    lr, li = _bit_powers(zr, zi, low)
    sr, si = zr, zi
    for _ in range(4):
        sr, si = sr * sr - si * si, 2.0 * sr * si
    hr, hi = _bit_powers(sr, si, n // low)
    pr = hr[:, :, None] * lr[:, None, :] - hi[:, :, None] * li[:, None, :]
    pi = hr[:, :, None] * li[:, None, :] + hi[:, :, None] * lr[:, None, :]
    return pr.reshape(-1, n), pi.reshape(-1, n)


def _bit_powers(zr, zi, n):
    k = jnp.arange(n, dtype=jnp.int32)[None, :]
    pr = jnp.ones((zr.shape[0], n), _F32)
    pi = jnp.zeros((zr.shape[0], n), _F32)
    sr, si = zr[:, None], zi[:, None]
    bit = 1
    while bit < n:
        on = (k & bit) != 0
        mr, mi = jnp.where(on, sr, 1.0), jnp.where(on, si, 0.0)
        pr, pi = pr * mr - pi * mi, pr * mi + pi * mr
        sr, si = sr * sr - si * si, 2.0 * sr * si
        bit *= 2
    return pr, pi


def _block_diag(m):
    g, a, b = m.shape
    gb = GROUPS_PER_BLOCK
    m = m.reshape(g // gb, gb, a, 1, b)
    eye = jnp.eye(gb, dtype=m.dtype).reshape(1, gb, 1, gb, 1)
    return (m * eye).reshape(g // gb, gb * a, gb * b)


def _layer_weights(p, q):
    d = p["w_in"].shape[0]
    row = lambda v: v.reshape(1, -1).astype(_F32)
    lam_re, lam_im = p["lam_re"].astype(_F32), p["lam_im"].astype(_F32)
    dt = jnp.exp(p["log_dt"].astype(_F32))[:, None]
    mag = jnp.exp(lam_re * dt)
    lbr, lbi = mag * jnp.cos(lam_im * dt), mag * jnp.sin(lam_im * dt)
    den = lam_re * lam_re + lam_im * lam_im
    nr, ni = lbr - 1.0, lbi
    fr, fi = (nr * lam_re + ni * lam_im) / den, (ni * lam_re - nr * lam_im) / den
    b_re, b_im = p["b_re"].astype(_F32), p["b_im"].astype(_F32)
    bbr = fr[..., None] * b_re - fi[..., None] * b_im
    bbi = fr[..., None] * b_im + fi[..., None] * b_re
    nstate = lam_re.size
    mod2 = lbr * lbr + lbi * lbi
    flat = lambda v: v.reshape(nstate)
    epr, epi = _complex_powers(flat(lbr), flat(lbi), q)
    eir, eii = _complex_powers(flat(lbr / mod2), flat(-lbi / mod2), q)
    table = lambda t: t
    bcast = lambda v: jnp.broadcast_to(v.reshape(nstate, 1), (nstate, LANES))
    c_re, c_im = p["c_re"].astype(_F32), p["c_im"].astype(_F32)
    w_r = p["w_router"].astype(_F32)
    w_r = jnp.pad(w_r, ((0, 0), (0, LANES - w_r.shape[1])))
    wr_hi = _bf(w_r)
    return dict(
        w_in=_bf(p["w_in"]), b_in=row(p["b_in"]), w_dw=p["w_dw"].astype(_F32), b_dw=row(p["b_dw"]),
        cln_g=row(p["conv_ln_g"]), cln_b=row(p["conv_ln_b"]), w_pw=_bf(p["w_pw"]), b_pw=row(p["b_pw"]),
        bre=_bf(_block_diag(bbr)), bim=_bf(_block_diag(bbi)),
        cc=_bf(jnp.concatenate([_block_diag(c_re), _block_diag(-c_im)], axis=2)),
        einv_r=table(eir), einv_i=table(eii), epow_r=table(epr), epow_i=table(epi),
        lamb_r=bcast(lbr), lamb_i=bcast(lbi), lrow_r=lbr.reshape(1, nstate), lrow_i=lbi.reshape(1, nstate),
        dskip=row(p["d_skip"]),
        w_sv=_bf(p["w_sv"]), b_sv=row(p["b_sv"]), w_sg=_bf(p["w_sg"]), b_sg=row(p["b_sg"]),
        w_out=_bf(p["w_out"]), b_out=row(p["b_out"]), ln1_g=row(p["ln1_g"]), ln1_b=row(p["ln1_b"]),
        wr_hi=wr_hi, wr_lo=_bf(w_r - wr_hi.astype(_F32)),
        b_r=p["b_router"].astype(_F32).reshape(-1, 1),
        ln2_g=row(p["ln2_g"]), ln2_b=row(p["ln2_b"]))


def _layer(xp, xs_tm, c_all, hist_tm, h0r, h0i, p, alpha):
    b, l, d = xp.shape
    steps, nbs, _ = xs_tm.shape
    tb = TOKEN_BLOCK
    assert l % tb == 0 and l >= CONV_WIDTH - 1 and (steps * nbs) % tb == 0 and tb % nbs == 0
    wts = _layer_weights(p, tb)
    mod = _ada(c_all, p["w_ada"].astype(_F32), p["b_ada"].astype(_F32))
    modp, mods = mod[:b].reshape(b, 1, -1), mod[b:]

    x1p, h2p, rp, cntp, conv_p, sre_p, sim_p = _mixer_prompt(xp, modp, wts, alpha)
    x1s, h2s, rs, cnts, conv_s, sre_s, sim_s = _mixer_sample(xs_tm, mods, hist_tm, h0r, h0i, wts, alpha)

    n_tok = b * l + steps * nbs
    nb, r_max, nt_max = _plan_sizes(n_tok)
    cnt = jnp.concatenate([cntp.reshape(-1, LANES), cnts.reshape(-1, LANES)], axis=0)[:, :N_EXPERTS]
    plan = _routing_plan(cnt.astype(jnp.int32), nt_max)

    flat = lambda a: a.reshape(b * l, a.shape[-1])
    xs_sorted, d4 = _dispatch(plan, flat(h2p), h2s, flat(rp), rs, r_max, nt_max)
    ys_sorted = _experts(plan, xs_sorted, p["w1"], p["b1"], p["w2"], p["b2"], nt_max)
    yp, ysm = _combine(plan, ys_sorted, flat(rp), rs, d4, flat(x1p), x1s, modp, mods,
                       wts["ln2_g"], wts["ln2_b"], r_max, alpha, l // tb)
    return (yp.reshape(b, l, d), ysm.reshape(steps, nbs, d), conv_p, sre_p[:, 0], sim_p[:, 0],
            conv_s, sre_s, sim_s)


def kernel(x_prompt, x_sample, state_conv, state_ssm_re, state_ssm_im, c_prompt, c_sample, w_ada, b_ada, w_in, b_in, w_dw, b_dw, conv_ln_g, conv_ln_b, w_pw, b_pw, lam_re, lam_im, log_dt, b_re, b_im, c_re, c_im, d_skip, w_sv, b_sv, w_sg, b_sg, w_out, b_out, ln1_g, ln1_b, w_router, b_router, w1, b1, w2, b2, ln2_g, ln2_b):
    stacked = dict(w_ada=w_ada, b_ada=b_ada, w_in=w_in, b_in=b_in, w_dw=w_dw, b_dw=b_dw,
                   conv_ln_g=conv_ln_g, conv_ln_b=conv_ln_b, w_pw=w_pw, b_pw=b_pw, lam_re=lam_re,
                   lam_im=lam_im, log_dt=log_dt, b_re=b_re, b_im=b_im, c_re=c_re, c_im=c_im,
                   d_skip=d_skip, w_sv=w_sv, b_sv=b_sv, w_sg=w_sg, b_sg=b_sg, w_out=w_out, b_out=b_out,
                   ln1_g=ln1_g, ln1_b=ln1_b, w_router=w_router, b_router=b_router, w1=w1, b1=b1,
                   w2=w2, b2=b2, ln2_g=ln2_g, ln2_b=ln2_b)
    depth = w_ada.shape[0]
    alpha = (2 * depth) ** 0.25
    b = x_prompt.shape[0]
    nbs = x_sample.shape[0]
    g, s = state_ssm_re.shape[2], state_ssm_re.shape[3]
    xp = x_prompt
    xs_tm = jnp.transpose(x_sample, (1, 0, 2))
    c_all = jnp.concatenate([c_prompt, c_sample], axis=0)
    conv_ps, re_ps, im_ps, conv_ss, re_ss, im_ss = [], [], [], [], [], []
    for layer in range(depth):
        p = {k: v[layer] for k, v in stacked.items()}
        hist_tm = jnp.transpose(state_conv[layer], (1, 0, 2))
        h0r = state_ssm_re[layer].reshape(nbs, g * s)
        h0i = state_ssm_im[layer].reshape(nbs, g * s)
        xp, xs_tm, conv_p, sre_p, sim_p, conv_s, sre_s, sim_s = _layer(
            xp, xs_tm, c_all, hist_tm, h0r, h0i, p, alpha)
        conv_ps.append(conv_p.astype(state_conv.dtype))
        re_ps.append(sre_p.reshape(b, g, s))
        im_ps.append(sim_p.reshape(b, g, s))
        conv_ss.append(jnp.transpose(conv_s, (1, 0, 2)).astype(state_conv.dtype))
        re_ss.append(sre_s.reshape(nbs, g, s))
        im_ss.append(sim_s.reshape(nbs, g, s))
    return (xp, jnp.transpose(xs_tm, (1, 0, 2)), jnp.stack(conv_ps), jnp.stack(re_ps), jnp.stack(im_ps),
            jnp.stack(conv_ss), jnp.stack(re_ss), jnp.stack(im_ss))
```

```python
import functools

import jax
import jax.numpy as jnp
from jax import lax
from jax.experimental import pallas as pl
from jax.experimental.pallas import tpu as pltpu

CONV_WIDTH = 31
SSM_GROUP = 16
SSM_STATE = 64
N_EXPERTS = 32
TOP_K = 4
SWIGLU_LIMIT = 7.0
SWIGLU_ALPHA = 1.702
LN_EPS = 1e-5

LANES = 128
SUBLANES = 8
TOKEN_BLOCK = 256
EXPERT_TILE = 512
SEG_PAD = 16
GROUPS_PER_BLOCK = LANES // SSM_GROUP
CONV_HIST = 32
CONV_ROWS = 32
ROUTE_IDX = N_EXPERTS
ROUTE_GATE = N_EXPERTS + TOP_K
VMEM_LIMIT = 56 * 1024 * 1024

_F32 = jnp.float32
_BF16 = jnp.bfloat16


def _bf(x):
    return x.astype(_BF16)


def _dot(a, b):
    return jnp.dot(a, b, preferred_element_type=_F32)


def _dot_nt(a, b):
    return lax.dot_general(a, b, (((1,), (1,)), ((), ())), preferred_element_type=_F32)


def _split(x):
    hi = _bf(x)
    lo = _bf(x - hi.astype(_F32))
    return hi, lo


def _sigmoid(x):
    return 1.0 / (1.0 + jnp.exp(-x))


def _gelu_tanh(x):
    return 0.5 * x * (1.0 + jnp.tanh(0.7978845608028654 * (x + 0.044715 * (x * x * x))))


def _layer_norm(x, g, b):
    mu = jnp.mean(x, axis=-1, keepdims=True)
    xc = x - mu
    var = jnp.mean(xc * xc, axis=-1, keepdims=True)
    return xc * lax.rsqrt(var + LN_EPS) * g + b


def _const_spec(shape):
    nd = len(shape)
    return pl.BlockSpec(shape, lambda *_: (0,) * nd)


def _params(sem):
    return pltpu.CompilerParams(dimension_semantics=sem, vmem_limit_bytes=VMEM_LIMIT)


def _ada_kernel(c_ref, w_ref, b_ref, o_ref):
    c = c_ref[...]
    s_hi, s_lo = _split(c * _sigmoid(c))
    w_hi, w_lo = _split(w_ref[...])
    o_ref[...] = _dot(s_hi, w_hi) + _dot(s_lo, w_hi) + _dot(s_hi, w_lo) + b_ref[...]


def _ada(c, w, b):
    n, d = c.shape
    cols = w.shape[1]
    tn = d
    return pl.pallas_call(
        _ada_kernel,
        grid=(cols // tn,),
        in_specs=[pl.BlockSpec((n, d), lambda j: (0, 0)),
                  pl.BlockSpec((d, tn), lambda j: (0, j)),
                  pl.BlockSpec((1, tn), lambda j: (0, j))],
        out_specs=pl.BlockSpec((n, tn), lambda j: (0, j)),
        out_shape=jax.ShapeDtypeStruct((n, cols), _F32),
        compiler_params=_params(("arbitrary",)),
        name="ada",
    )(c, w, b.reshape(1, cols))


def _tiled(fn, tile, *args, vecs=()):
    n, m = args[0].shape
    tr, tc = min(tile[0], n), min(tile[1], m)
    rows_out = None
    for r0 in range(0, n, tr):
        cols_out = None
        for c0 in range(0, m, tc):
            res = fn(*[a[r0:r0 + tr, c0:c0 + tc] for a in args], *[v[:, c0:c0 + tc] for v in vecs])
            res = res if isinstance(res, tuple) else (res,)
            if cols_out is None:
                cols_out = [[] for _ in res]
            for acc, v in zip(cols_out, res):
                acc.append(v)
        row_vals = [c[0] if len(c) == 1 else jnp.concatenate(c, axis=1) for c in cols_out]
        if rows_out is None:
            rows_out = [[] for _ in row_vals]
        for acc, v in zip(rows_out, row_vals):
            acc.append(v)
    outs = [r[0] if len(r) == 1 else jnp.concatenate(r, axis=0) for r in rows_out]
    return outs[0] if len(outs) == 1 else tuple(outs)


def _in_proj(hb, w_in, b_in, lo, hi):
    return _dot(hb, w_in[:, lo:hi]) + b_in[:, lo:hi]


def _conv_tail(v, cln_g, cln_b, w_pw, b_pw):
    g, b = cln_g[...], cln_b[...]

    def norm_swish(vb):
        vb = _layer_norm(vb, g, b)
        return vb * _sigmoid(vb)
    v = _tiled(norm_swish, (SUBLANES, v.shape[1]), v)
    return _dot(_bf(v), w_pw[...]) + b_pw[...]


def _ssm_tail(y, zs, dskip, w_sv, b_sv, w_sg, b_sg):
    yg = _tiled(lambda yb, zb, db: _bf(_gelu_tanh(yb + db * zb)), (2 * SUBLANES, 4 * LANES), y, zs, vecs=(dskip,))
    sv, sg = _dot(yg, w_sv[...]), _dot(yg, w_sg[...])
    return _tiled(lambda a, b, bv, bg: (a + bv) * _sigmoid(b + bg), (SUBLANES, 4 * LANES),
                  sv, sg, vecs=(b_sv, b_sg))


def _route(h2, wr_hi, wr_lo, b_r):
    n = h2.shape[0]
    ne = b_r.shape[0]
    h_hi, h_lo = _split(h2)
    logits = (_dot(h_hi, wr_hi[...]) + _dot(h_lo, wr_hi[...]) + _dot(h_hi, wr_lo[...])).T[:ne, :] + b_r[...]
    row = lax.broadcasted_iota(jnp.int32, (ne, n), 0).astype(_F32)
    neg = jnp.float32(-jnp.inf)
    cur = logits
    vals, idxs = [], []
    for _ in range(TOP_K):
        m = jnp.max(cur, axis=0, keepdims=True)
        idx = jnp.min(jnp.where(cur == m, row, float(ne)), axis=0, keepdims=True)
        vals.append(m)
        idxs.append(idx)
        cur = jnp.where(row == idx, neg, cur)
    es = [jnp.exp(v - vals[0]) for v in vals]
    tot = es[0]
    for e in es[1:]:
        tot = tot + e
    inv = 1.0 / tot
    selected = jnp.zeros((ne, n), _F32)
    row8 = lax.broadcasted_iota(jnp.int32, (2 * TOP_K, n), 0)
    extra = jnp.zeros((2 * TOP_K, n), _F32)
    for k in range(TOP_K):
        gate = es[k] * inv
        selected = selected + jnp.where(row == idxs[k], 1.0, 0.0)
        extra = extra + jnp.where(row8 == k, idxs[k], 0.0) + jnp.where(row8 == TOP_K + k, gate, 0.0)
    rest = jnp.zeros((LANES - ne - 2 * TOP_K, n), _F32)
    return jnp.concatenate([selected, extra, rest], axis=0).T


def _members(route):
    lane = lax.broadcasted_iota(jnp.int32, route.shape, 1)
    return jnp.where(lane < N_EXPERTS, route, 0.0)


def _block_counts(route):
    return jnp.sum(_members(route), axis=0, keepdims=True)


def _mixer_tail(x, zg, conv_out, ssm_out, g1, sh2, sc2, alpha, w_out, b_out,
                ln1_g, ln1_b, wr_hi, wr_lo, b_r):
    d = x.shape[1]
    merged = _tiled(lambda gc, gs, co, so: _bf(_sigmoid(gc) * co + _sigmoid(gs) * so),
                    (2 * SUBLANES, 4 * LANES), zg[:, :d], zg[:, d:], conv_out, ssm_out)
    m = _dot(merged, w_out[...])

    def norm_mod(xb, mb, gb, scb, shb, bo, lg, lb):
        x1b = _layer_norm(alpha * xb + gb * (mb + bo), lg, lb)
        return x1b, x1b * (1.0 + scb) + shb
    mods = (g1, sc2, sh2)
    per_seq = g1.shape[0] == 1
    x1, h2 = _tiled(norm_mod, (SUBLANES, d), x, m, *(() if per_seq else mods),
                    vecs=(mods if per_seq else ()) + (b_out, ln1_g, ln1_b))
    return x1, h2, _route(h2, wr_hi, wr_lo, b_r)


def _mixer_prompt_kernel(x_ref, mod_ref, w_in, b_in, w_dw, b_dw, cln_g, cln_b, w_pw, b_pw,
                         bre, bim, cc, einv_r, einv_i, epow_r, epow_i, lamb_r, lamb_i, dskip,
                         w_sv, b_sv, w_sg, b_sg, w_out, b_out, ln1_g, ln1_b, wr_hi, wr_lo, b_r,
                         x1_ref, h2_ref, route_ref, cnt_ref, cstate_ref, sre_ref, sim_ref,
                         ubuf, ush, car_r, car_i, *, alpha):
    c = pl.program_id(1)
    last = pl.num_programs(1) - 1
    tl, d = x_ref.shape[1], x_ref.shape[2]
    dc = w_pw.shape[0]
    dsm = dskip.shape[1]
    nj = bre.shape[0]
    rows = bre.shape[1]

    @pl.when(c == 0)
    def _():
        ubuf[0:CONV_HIST, :] = jnp.zeros((CONV_HIST, dc), _F32)
        car_r[...] = jnp.zeros(car_r.shape, _F32)
        car_i[...] = jnp.zeros(car_i.shape, _F32)

    x = x_ref[0]
    mod = mod_ref[0]
    sh1, sc1, g1 = mod[:, 0:d], mod[:, d:2 * d], mod[:, 2 * d:3 * d]
    sh2, sc2 = mod[:, 3 * d:4 * d], mod[:, 4 * d:5 * d]
    hb = _bf(x * (1.0 + sc1) + sh1)

    u = _in_proj(hb, w_in, b_in, 0, dc) * _sigmoid(_in_proj(hb, w_in, b_in, dc, 2 * dc))
    ubuf[CONV_HIST:CONV_HIST + tl, :] = u
    span = ush.shape[1]
    for r in range(1, SUBLANES):
        ush[r - 1] = ubuf[r:r + span, :]

    n_blocks = tl // CONV_ROWS
    gate_lo = 2 * dc + dsm
    gate_w = 2 * d // n_blocks
    vblocks, zg_cols = [], []
    for blk in range(n_blocks):
        r0 = blk * CONV_ROWS
        acc = jnp.broadcast_to(b_dw[...], (CONV_ROWS, dc))
        for k in range(CONV_WIDTH):
            q, r = divmod(CONV_HIST - (CONV_WIDTH - 1) + k, SUBLANES)
            rows_k = slice(r0 + q * SUBLANES, r0 + q * SUBLANES + CONV_ROWS)
            tap = ubuf[rows_k, :] if r == 0 else ush[r - 1, rows_k, :]
            acc = acc + w_dw[k:k + 1, :] * tap
        vblocks.append(acc)
        zg_cols.append(_in_proj(hb, w_in, b_in, gate_lo + blk * gate_w, gate_lo + (blk + 1) * gate_w))
    ubuf[0:CONV_HIST, :] = ubuf[tl:tl + CONV_HIST, :]
    conv_out = _conv_tail(jnp.concatenate(vblocks, axis=0), cln_g, cln_b, w_pw, b_pw)

    zs = _in_proj(hb, w_in, b_in, 2 * dc, 2 * dc + dsm)
    zst = _bf(zs.T)
    tri = jnp.where(lax.broadcasted_iota(jnp.int32, (tl, tl), 0)
                    <= lax.broadcasted_iota(jnp.int32, (tl, tl), 1), 1.0, 0.0).astype(_BF16)
    def project(j):
        zj = zst[j * LANES:(j + 1) * LANES, :]
        return _dot(bre[j], zj), _dot(bim[j], zj)

    def scale_in(j, bu):
        r = slice(j * rows, (j + 1) * rows)
        eir, eii = einv_r[r, :], einv_i[r, :]
        return _bf(bu[0] * eir - bu[1] * eii), _bf(bu[0] * eii + bu[1] * eir)

    def prefix(v):
        return _dot(v[0], tri), _dot(v[1], tri)

    def scale_out(j, cum):
        r = slice(j * rows, (j + 1) * rows)
        cr = jnp.broadcast_to(car_r[r, LANES - 1:LANES], (rows, LANES))
        ci = jnp.broadcast_to(car_i[r, LANES - 1:LANES], (rows, LANES))
        lr, li = lamb_r[r, :], lamb_i[r, :]
        cum_r = cum[0] + jnp.concatenate([lr * cr - li * ci] * (tl // LANES), axis=1)
        cum_i = cum[1] + jnp.concatenate([lr * ci + li * cr] * (tl // LANES), axis=1)
        epr, epi = epow_r[r, :], epow_i[r, :]
        h_r = cum_r * epr - cum_i * epi
        h_i = cum_r * epi + cum_i * epr
        car_r[r, :] = h_r[:, tl - LANES:tl]
        car_i[r, :] = h_i[:, tl - LANES:tl]
        return jnp.concatenate([_bf(h_r), _bf(h_i)], axis=0)

    bu, v, cum, hcat, yts = {}, {}, {}, {}, {}
    for t in range(nj + 4):
        if t < nj:
            bu[t] = project(t)
        if 0 <= t - 2 < nj:
            cum[t - 2] = prefix(v.pop(t - 2))
        if 0 <= t - 4 < nj:
            yts[t - 4] = _dot(cc[t - 4], hcat.pop(t - 4))
        if 0 <= t - 1 < nj:
            v[t - 1] = scale_in(t - 1, bu.pop(t - 1))
        if 0 <= t - 3 < nj:
            hcat[t - 3] = scale_out(t - 3, cum.pop(t - 3))
    ssm_out = _ssm_tail(jnp.concatenate([yts[j] for j in range(nj)], axis=0).T, zs, dskip,
                        w_sv, b_sv, w_sg, b_sg)

    x1, h2, route = _mixer_tail(x, jnp.concatenate(zg_cols, axis=1), conv_out, ssm_out, g1, sh2, sc2, alpha,
                                w_out, b_out, ln1_g, ln1_b, wr_hi, wr_lo, b_r)
    x1_ref[0] = x1
    h2_ref[0] = _bf(h2)
    route_ref[0] = route
    cnt_ref[0, 0] = _block_counts(route)

    @pl.when(c == last)
    def _():
        cstate_ref[0] = ubuf[CONV_HIST + tl - (CONV_WIDTH - 1):CONV_HIST + tl, :]
        sre_ref[0] = car_r[...].T[LANES - 1:LANES, :]
        sim_ref[0] = car_i[...].T[LANES - 1:LANES, :]


def _mixer_prompt(x, mod, wts, alpha):
    b, l, d = x.shape
    tl = TOKEN_BLOCK
    nc = l // tl
    dc = wts["w_pw"].shape[0]
    nstate = wts["lamb_r"].shape[0]
    names = ["w_in", "b_in", "w_dw", "b_dw", "cln_g", "cln_b", "w_pw", "b_pw", "bre", "bim", "cc",
             "einv_r", "einv_i", "epow_r", "epow_i", "lamb_r", "lamb_i", "dskip",
             "w_sv", "b_sv", "w_sg", "b_sg", "w_out", "b_out", "ln1_g", "ln1_b",
             "wr_hi", "wr_lo", "b_r"]
    consts = [wts[n] for n in names]
    tok = lambda bi, ci: (bi, ci, 0)
    seq = lambda bi, ci: (bi, 0, 0)
    return pl.pallas_call(
        functools.partial(_mixer_prompt_kernel, alpha=alpha),
        grid=(b, nc),
        in_specs=[pl.BlockSpec((1, tl, d), tok), pl.BlockSpec((1, 1, mod.shape[-1]), seq)]
                 + [_const_spec(a.shape) for a in consts],
        out_specs=[pl.BlockSpec((1, tl, d), tok), pl.BlockSpec((1, tl, d), tok),
                   pl.BlockSpec((1, tl, LANES), tok),
                   pl.BlockSpec((1, 1, 1, LANES), lambda bi, ci: (bi, ci, 0, 0)),
                   pl.BlockSpec((1, CONV_WIDTH - 1, dc), seq),
                   pl.BlockSpec((1, 1, nstate), seq), pl.BlockSpec((1, 1, nstate), seq)],
        out_shape=[jax.ShapeDtypeStruct((b, l, d), _F32), jax.ShapeDtypeStruct((b, l, d), _BF16),
                   jax.ShapeDtypeStruct((b, l, LANES), _F32),
                   jax.ShapeDtypeStruct((b, nc, 1, LANES), _F32),
                   jax.ShapeDtypeStruct((b, CONV_WIDTH - 1, dc), _F32),
                   jax.ShapeDtypeStruct((b, 1, nstate), _F32),
                   jax.ShapeDtypeStruct((b, 1, nstate), _F32)],
        scratch_shapes=[pltpu.VMEM((CONV_HIST + tl, dc), _F32),
                        pltpu.VMEM((SUBLANES - 1, tl + CONV_HIST - SUBLANES, dc), _F32),
                        pltpu.VMEM((nstate, LANES), _F32), pltpu.VMEM((nstate, LANES), _F32)],
        compiler_params=_params(("arbitrary", "arbitrary")),
        name="mixer_prompt",
    )(x, mod, *consts)


def _mixer_sample_kernel(x_ref, mod_ref, hist_ref, h0r_ref, h0i_ref,
                         w_in, b_in, w_dw, b_dw, cln_g, cln_b, w_pw, b_pw,
                         bre, bim, cc, lrow_r, lrow_i, dskip,
                         w_sv, b_sv, w_sg, b_sg, w_out, b_out, ln1_g, ln1_b, wr_hi, wr_lo, b_r,
                         x1_ref, h2_ref, route_ref, cnt_ref, cstate_ref, sre_ref, sim_ref, *, alpha):
    steps, nb, d = x_ref.shape
    dc = w_pw.shape[0]
    dsm = dskip.shape[1]
    nj = bre.shape[0]
    rows = bre.shape[1]
    nhist = CONV_WIDTH - 1
    x = x_ref[...].reshape(steps * nb, d)
    mod = jnp.concatenate([mod_ref[...]] * steps, axis=0)
    sh1, sc1, g1 = mod[:, 0:d], mod[:, d:2 * d], mod[:, 2 * d:3 * d]
    sh2, sc2 = mod[:, 3 * d:4 * d], mod[:, 4 * d:5 * d]
    hb = _bf(x * (1.0 + sc1) + sh1)

    u = _in_proj(hb, w_in, b_in, 0, dc) * _sigmoid(_in_proj(hb, w_in, b_in, dc, 2 * dc))
    us = [u[t * nb:(t + 1) * nb, :] for t in range(steps)]
    vs = []
    for t in range(steps):
        acc = jnp.broadcast_to(b_dw[...], (nb, dc))
        for i in range(t, nhist):
            acc = acc + w_dw[i - t:i - t + 1, :] * hist_ref[i]
        for s in range(t + 1):
            acc = acc + w_dw[nhist + s - t:nhist + s - t + 1, :] * us[s]
        vs.append(acc)
    for i in range(nhist):
        src = i + steps
        cstate_ref[i] = hist_ref[src] if src < nhist else us[src - nhist]
    conv_out = _conv_tail(jnp.concatenate(vs, axis=0), cln_g, cln_b, w_pw, b_pw)

    zs = _in_proj(hb, w_in, b_in, 2 * dc, 2 * dc + dsm)
    zsb = _bf(zs)
    h_r, h_i = h0r_ref[...], h0i_ref[...]
    lr, li = lrow_r[...], lrow_i[...]
    ys = []
    for t in range(steps):
        zt = zsb[t * nb:(t + 1) * nb, :]
        bu_r = jnp.concatenate([_dot_nt(zt[:, j * LANES:(j + 1) * LANES], bre[j]) for j in range(nj)], axis=1)
        bu_i = jnp.concatenate([_dot_nt(zt[:, j * LANES:(j + 1) * LANES], bim[j]) for j in range(nj)], axis=1)
        h_r, h_i = lr * h_r - li * h_i + bu_r, lr * h_i + li * h_r + bu_i
        hrb, hib = _bf(h_r), _bf(h_i)
        ys.append(jnp.concatenate(
            [_dot_nt(jnp.concatenate([hrb[:, j * rows:(j + 1) * rows], hib[:, j * rows:(j + 1) * rows]], axis=1),
                     cc[j]) for j in range(nj)], axis=1))
    sre_ref[...] = h_r
    sim_ref[...] = h_i
    ssm_out = _ssm_tail(jnp.concatenate(ys, axis=0), zs, dskip, w_sv, b_sv, w_sg, b_sg)

    zg = _in_proj(hb, w_in, b_in, 2 * dc + dsm, 2 * dc + dsm + 2 * d)
    x1, h2, route = _mixer_tail(x, zg, conv_out, ssm_out, g1, sh2, sc2, alpha,
                                w_out, b_out, ln1_g, ln1_b, wr_hi, wr_lo, b_r)
    x1_ref[...] = x1
    h2_ref[...] = _bf(h2)
    route_ref[...] = route
    for blk in range(cnt_ref.shape[0]):
        cnt_ref[blk] = _block_counts(route[blk * TOKEN_BLOCK:(blk + 1) * TOKEN_BLOCK, :])


def _mixer_sample(x_tm, mod, hist_tm, h0r, h0i, wts, alpha):
    steps, nb, d = x_tm.shape
    n = steps * nb
    dc = wts["w_pw"].shape[0]
    nstate = h0r.shape[1]
    names = ["w_in", "b_in", "w_dw", "b_dw", "cln_g", "cln_b", "w_pw", "b_pw", "bre", "bim", "cc",
             "lrow_r", "lrow_i", "dskip", "w_sv", "b_sv", "w_sg", "b_sg", "w_out", "b_out",
             "ln1_g", "ln1_b", "wr_hi", "wr_lo", "b_r"]
    args = [x_tm, mod, hist_tm, h0r, h0i] + [wts[k] for k in names]
    return pl.pallas_call(
        functools.partial(_mixer_sample_kernel, alpha=alpha),
        grid=(1,),
        in_specs=[_const_spec(a.shape) for a in args],
        out_specs=[_const_spec((n, d)), _const_spec((n, d)), _const_spec((n, LANES)),
                   _const_spec((n // TOKEN_BLOCK, 1, LANES)),
                   _const_spec((CONV_WIDTH - 1, nb, dc)),
                   _const_spec((nb, nstate)), _const_spec((nb, nstate))],
        out_shape=[jax.ShapeDtypeStruct((n, d), _F32), jax.ShapeDtypeStruct((n, d), _BF16),
                   jax.ShapeDtypeStruct((n, LANES), _F32),
                   jax.ShapeDtypeStruct((n // TOKEN_BLOCK, 1, LANES), _F32),
                   jax.ShapeDtypeStruct((CONV_WIDTH - 1, nb, dc), _F32),
                   jax.ShapeDtypeStruct((nb, nstate), _F32), jax.ShapeDtypeStruct((nb, nstate), _F32)],
        compiler_params=_params(("arbitrary",)),
        name="mixer_sample",
    )(*args)


def _round_up(x, m):
    return (x + m - 1) // m * m


def _plan_sizes(n_tokens):
    nb = n_tokens // TOKEN_BLOCK
    r_max = _round_up(TOP_K * TOKEN_BLOCK + 2 * N_EXPERTS * (SEG_PAD - 1), 2 * LANES)
    rows_max = n_tokens * TOP_K + N_EXPERTS * (EXPERT_TILE - 1)
    nt_max = -(-rows_max // EXPERT_TILE)
    return nb, r_max, nt_max


def _routing_plan(cnt, nt_max):
    nb, ne = cnt.shape
    before = jnp.cumsum(cnt, axis=0) - cnt
    tot = jnp.sum(cnt, axis=0)
    cin = before % SEG_PAD
    last = (jnp.arange(nb, dtype=jnp.int32) == nb - 1)[:, None]
    active = (cnt > 0) | (last & (cin > 0))
    seg = jnp.where(active, _round_up(cin + cnt, SEG_PAD), 0)
    loc = jnp.cumsum(seg, axis=1) - seg
    totpad = _round_up(tot, EXPERT_TILE)
    eend = jnp.cumsum(totpad)
    estart = eend - totpad
    goff = estart[None, :] + before - cin
    nfull = jnp.where(last, seg, jnp.where(active, (cin + cnt) // SEG_PAD * SEG_PAD, 0))
    cout = jnp.where(last | ~active, 0, (cin + cnt) % SEG_PAD)
    n_tiles = (eend[-1] // EXPERT_TILE).astype(jnp.int32)
    tiles = jnp.minimum(jnp.arange(nt_max, dtype=jnp.int32), n_tiles - 1)
    te = jnp.sum(((eend // EXPERT_TILE)[None, :] <= tiles[:, None]).astype(jnp.int32), axis=1)
    te = jnp.minimum(te, ne - 1)
    tot16 = _round_up(tot, SEG_PAD)
    return dict(seg=seg, loc=loc, goff=goff, nfull=nfull, cin=jnp.where(active, cin, 0), cout=cout,
                used=jnp.sum(seg, axis=1).astype(jnp.int32), sent=jnp.sum(nfull, axis=1).astype(jnp.int32),
                n_tiles=n_tiles.reshape(1), te=te,
                npad=(totpad - tot16).astype(jnp.int32), pad_base=(estart + tot16).astype(jnp.int32))


def _dest_columns(route, loc_row):
    n = route.shape[0]
    lane_i = lax.broadcasted_iota(jnp.int32, (n, LANES), 1)
    lane = lane_i.astype(_F32)
    member = _bf(_members(route))
    strict = jnp.where(lax.broadcasted_iota(jnp.int32, (n, n), 1)
                       < lax.broadcasted_iota(jnp.int32, (n, n), 0), 1.0, 0.0).astype(_BF16)
    dest = _dot(strict, member) + loc_row
    d4 = jnp.zeros((n, LANES), _F32)
    for k in range(TOP_K):
        idx = route[:, ROUTE_IDX + k:ROUTE_IDX + k + 1]
        dk = jnp.sum(jnp.where(lane == idx, dest, 0.0), axis=-1, keepdims=True)
        d4 = d4 + jnp.where(lane_i == k, dk, 0.0)
    return d4


def _typical_rows(r_max):
    return min(r_max, _round_up(TOP_K * TOKEN_BLOCK + N_EXPERTS * SEG_PAD, 2 * LANES))


def _dispatch_kernel(nfull, sloc, goff, cin, cout, used, sent, npad, pad_base, hp_ref, hs_ref, rp_ref, rs_ref, loc_ref,
                     xs_hbm, d4_ref, buf, stage, zbuf, sem, zsem, *, nbp):
    i = pl.program_id(0)
    nblk = pl.num_programs(0)
    slot = lax.rem(i, 2)
    tb = hp_ref.shape[0]
    r_max = buf.shape[1]
    is_p = i < nbp
    h = jnp.where(is_p, hp_ref[...], hs_ref[...])
    route = jnp.where(is_p, rp_ref[...], rs_ref[...])
    d4 = _dest_columns(route, loc_ref[0])
    d4_ref[...] = d4
    d4t = d4.T

    def compact(rows):
        row = lax.broadcasted_iota(jnp.int32, (rows, tb), 0).astype(_F32)
        p = jnp.zeros((rows, tb), _F32)
        for k in range(TOP_K):
            p = p + jnp.where(row == d4t[k:k + 1, :], 1.0, 0.0)
        buf[slot, 0:rows, :] = _bf(_dot(_bf(p), h))

    r_typ = _typical_rows(r_max)

    @pl.when(i == 0)
    def _():
        buf[...] = jnp.zeros(buf.shape, buf.dtype)
        stage[...] = jnp.zeros(stage.shape, stage.dtype)

    @pl.when(used[i] <= r_typ)
    def _():
        compact(r_typ)

    @pl.when(used[i] > r_typ)
    def _():
        compact(r_max)

    def seg_copy(blk, s, e):
        n = pl.multiple_of(nfull[blk * N_EXPERTS + e], SEG_PAD)
        src = pl.multiple_of(sloc[blk * N_EXPERTS + e], SEG_PAD)
        dst = pl.multiple_of(goff[blk * N_EXPERTS + e], SEG_PAD)
        return pltpu.make_async_copy(buf.at[s, pl.ds(src, n), :], xs_hbm.at[pl.ds(dst, n), :], sem.at[s])

    def pad_copy(e):
        n = pl.multiple_of(npad[e], SEG_PAD)
        dst = pl.multiple_of(pad_base[e], SEG_PAD)
        return pltpu.make_async_copy(zbuf.at[pl.ds(0, n), :], xs_hbm.at[pl.ds(dst, n), :], zsem.at[0])

    def for_segments(blk, fn):
        for e in range(N_EXPERTS):
            @pl.when(nfull[blk * N_EXPERTS + e] > 0)
            def _(e=e):
                fn(e)

    for e in range(N_EXPERTS):
        first = pl.ds(pl.multiple_of(sloc[i * N_EXPERTS + e], SEG_PAD), SEG_PAD)
        carried = jnp.where(cin[i * N_EXPERTS + e] > 0, stage[e], jnp.zeros_like(stage[e]))
        buf[slot, first, :] = buf[slot, first, :] + carried

    for_segments(i, lambda e: seg_copy(i, slot, e).start())

    for e in range(N_EXPERTS):
        rest = pl.multiple_of(sloc[i * N_EXPERTS + e] + nfull[i * N_EXPERTS + e], SEG_PAD)
        stage[e] = jnp.where(cout[i * N_EXPERTS + e] > 0, buf[slot, pl.ds(rest, SEG_PAD), :], stage[e])

    def wait_block(blk, s):
        @pl.when(sent[blk] > 0)
        def _():
            n = pl.multiple_of(sent[blk], SEG_PAD)
            pltpu.make_async_copy(buf.at[s, pl.ds(0, n), :], xs_hbm.at[pl.ds(0, n), :], sem.at[s]).wait()

    @pl.when(i > 0)
    def _():
        wait_block(i - 1, 1 - slot)

    @pl.when(i == nblk - 1)
    def _():
        zbuf[...] = jnp.zeros(zbuf.shape, zbuf.dtype)
        for e in range(N_EXPERTS):
            @pl.when(npad[e] > 0)
            def _(e=e):
                pad_copy(e).start()
        for e in range(N_EXPERTS):
            @pl.when(npad[e] > 0)
            def _(e=e):
                pad_copy(e).wait()
        wait_block(i, slot)


def _dispatch(plan, h2p, h2s, rp, rs, r_max, nt_max):
    tb = TOKEN_BLOCK
    d = h2p.shape[1]
    nbp, nbs = h2p.shape[0] // tb, h2s.shape[0] // tb
    nb = nbp + nbs
    first_row = (plan["loc"] + plan["cin"]).astype(_F32)
    loc = jnp.pad(first_row, ((0, 0), (0, LANES - N_EXPERTS))).reshape(nb, 1, LANES)
    pidx = lambda i, *_: (jnp.minimum(i, nbp - 1), 0)
    sidx = lambda i, *_: (jnp.maximum(i - nbp, 0), 0)
    grid_spec = pltpu.PrefetchScalarGridSpec(
        num_scalar_prefetch=9,
        grid=(nb,),
        in_specs=[pl.BlockSpec((tb, d), pidx), pl.BlockSpec((tb, d), sidx),
                  pl.BlockSpec((tb, LANES), pidx), pl.BlockSpec((tb, LANES), sidx),
                  pl.BlockSpec((1, 1, LANES), lambda i, *_: (i, 0, 0))],
        out_specs=[pl.BlockSpec(memory_space=pl.ANY), pl.BlockSpec((tb, LANES), lambda i, *_: (i, 0))],
        scratch_shapes=[pltpu.VMEM((2, r_max, d), _BF16), pltpu.VMEM((N_EXPERTS, SEG_PAD, d), _BF16),
                        pltpu.VMEM((EXPERT_TILE, d), _BF16),
                        pltpu.SemaphoreType.DMA((2,)), pltpu.SemaphoreType.DMA((1,))])
    flat = lambda a: a.astype(jnp.int32).reshape(-1)
    return pl.pallas_call(
        functools.partial(_dispatch_kernel, nbp=nbp),
        grid_spec=grid_spec,
        out_shape=[jax.ShapeDtypeStruct((nt_max * EXPERT_TILE, d), _BF16),
                   jax.ShapeDtypeStruct((nb * tb, LANES), _F32)],
        compiler_params=_params(("arbitrary",)),
        name="dispatch",
    )(flat(plan["nfull"]), flat(plan["loc"]), flat(plan["goff"]), flat(plan["cin"]), flat(plan["cout"]),
      plan["used"], plan["sent"], plan["npad"], plan["pad_base"], h2p, h2s, rp, rs, loc)


def _expert_kernel(te, n_tiles, x_ref, w1_ref, b1_ref, w2_ref, b2_ref, y_ref, w1b, w2b):
    i = pl.program_id(0)
    dff = w2_ref.shape[1]

    @pl.when(i < n_tiles[0])
    def _():
        prev = te[jnp.maximum(i - 1, 0)]

        @pl.when((i == 0) | (te[i] != prev))
        def _():
            w1b[...] = _bf(w1_ref[0])
            w2b[...] = _bf(w2_ref[0])

        gu = _dot(x_ref[...], w1b[...]) + b1_ref[0]
        g = jnp.minimum(gu[:, :dff], SWIGLU_LIMIT)
        up = jnp.clip(gu[:, dff:], -SWIGLU_LIMIT, SWIGLU_LIMIT)
        act = g * _sigmoid(SWIGLU_ALPHA * g) * (up + 1.0)
        y_ref[...] = _bf(_dot(_bf(act), w2b[...]) + b2_ref[0])


def _experts(plan, xs, w1, b1, w2, b2, nt_max):
    ne, d, dff2 = w1.shape
    dff = w2.shape[1]
    tm = EXPERT_TILE
    tile = lambda i, te, nt: (jnp.minimum(i, nt[0] - 1), 0)
    wsel = lambda i, te, nt: (te[i], 0, 0)
    grid_spec = pltpu.PrefetchScalarGridSpec(
        num_scalar_prefetch=2,
        grid=(nt_max,),
        in_specs=[pl.BlockSpec((tm, d), tile),
                  pl.BlockSpec((1, d, dff2), wsel), pl.BlockSpec((1, 1, dff2), wsel),
                  pl.BlockSpec((1, dff, d), wsel), pl.BlockSpec((1, 1, d), wsel)],
        out_specs=pl.BlockSpec((tm, d), tile),
        scratch_shapes=[pltpu.VMEM((d, dff2), _BF16), pltpu.VMEM((dff, d), _BF16)])
    return pl.pallas_call(
        _expert_kernel,
        grid_spec=grid_spec,
        out_shape=jax.ShapeDtypeStruct(xs.shape, _BF16),
        compiler_params=_params(("arbitrary",)),
        name="experts",
    )(plan["te"], plan["n_tiles"], xs, w1, b1.reshape(ne, 1, dff2), w2, b2.reshape(ne, 1, d))


def _combine_kernel(seg, sloc, goff, used, rp_ref, rs_ref, d4_ref, x1p_ref, x1s_ref, g2p_ref, g2s_ref,
                    ln2_g, ln2_b, ys_hbm, yp_ref, ysm_ref, buf, fbuf, sem, *, nbp, alpha):
    i = pl.program_id(0)
    nblk = pl.num_programs(0)
    slot = lax.rem(i, 2)
    tb = rp_ref.shape[0]
    r_max = buf.shape[1]

    def seg_copy(blk, s, e):
        n = pl.multiple_of(seg[blk * N_EXPERTS + e], SEG_PAD)
        src = pl.multiple_of(goff[blk * N_EXPERTS + e], SEG_PAD)
        dst = pl.multiple_of(sloc[blk * N_EXPERTS + e], SEG_PAD)
        return pltpu.make_async_copy(ys_hbm.at[pl.ds(src, n), :], buf.at[s, pl.ds(dst, n), :], sem.at[s])

    def for_segments(blk, fn):
        for e in range(N_EXPERTS):
            @pl.when(seg[blk * N_EXPERTS + e] > 0)
            def _(e=e):
                fn(e)

    @pl.when(i == 0)
    def _():
        buf[...] = jnp.zeros(buf.shape, buf.dtype)
        for_segments(0, lambda e: seg_copy(0, 0, e).start())

    @pl.when(i + 1 < nblk)
    def _():
        for_segments(i + 1, lambda e: seg_copy(i + 1, 1 - slot, e).start())

    @pl.when(used[i] > 0)
    def _():
        n = pl.multiple_of(used[i], SEG_PAD)
        pltpu.make_async_copy(ys_hbm.at[pl.ds(0, n), :], buf.at[slot, pl.ds(0, n), :], sem.at[slot]).wait()

    is_p = i < nbp
    route = jnp.where(is_p, rp_ref[...], rs_ref[...])
    d4 = d4_ref[...]

    def gather(rows):
        col = lax.broadcasted_iota(jnp.int32, (tb, rows), 1).astype(_F32)
        pg = jnp.zeros((tb, rows), _F32)
        for k in range(TOP_K):
            gate = route[:, ROUTE_GATE + k:ROUTE_GATE + k + 1]
            pg = pg + jnp.where(col == d4[:, k:k + 1], gate, 0.0)
        fbuf[...] = _dot(_bf(pg), buf[slot, 0:rows, :])

    r_typ = _typical_rows(r_max)

    @pl.when(used[i] <= r_typ)
    def _():
        gather(r_typ)

    @pl.when(used[i] > r_typ)
    def _():
        gather(r_max)

    f = fbuf[...]
    x1 = jnp.where(is_p, x1p_ref[...], x1s_ref[...])
    g2s = jnp.concatenate([g2s_ref[...]] * (tb // g2s_ref.shape[0]), axis=0)
    g2 = jnp.where(is_p, jnp.broadcast_to(g2p_ref[0], g2s.shape), g2s)
    y = _layer_norm(alpha * x1 + g2 * f, ln2_g[...], ln2_b[...])

    @pl.when(is_p)
    def _():
        yp_ref[...] = y

    @pl.when(jnp.logical_not(is_p))
    def _():
        ysm_ref[...] = y


def _combine(plan, ys, rp, rs, d4, x1p, x1s, modp, mods, ln2_g, ln2_b, r_max, alpha, blocks_per_seq):
    tb = TOKEN_BLOCK
    d = x1p.shape[1]
    nbp, nbs = x1p.shape[0] // tb, x1s.shape[0] // tb
    nb = nbp + nbs
    pidx = lambda i, *_: (jnp.minimum(i, nbp - 1), 0)
    sidx = lambda i, *_: (jnp.maximum(i - nbp, 0), 0)
    g2_lane_block = 5
    flat = lambda a: a.astype(jnp.int32).reshape(-1)
    grid_spec = pltpu.PrefetchScalarGridSpec(
        num_scalar_prefetch=4,
        grid=(nb,),
        in_specs=[pl.BlockSpec((tb, LANES), pidx), pl.BlockSpec((tb, LANES), sidx),
                  pl.BlockSpec((tb, LANES), lambda i, *_: (i, 0)),
                  pl.BlockSpec((tb, d), pidx), pl.BlockSpec((tb, d), sidx),
                  pl.BlockSpec((1, 1, d), lambda i, *_: (jnp.minimum(i, nbp - 1) // blocks_per_seq, 0,
                                                         g2_lane_block)),
                  pl.BlockSpec((mods.shape[0], d), lambda i, *_: (0, g2_lane_block)),
                  pl.BlockSpec((1, d), lambda i, *_: (0, 0)), pl.BlockSpec((1, d), lambda i, *_: (0, 0)),
                  pl.BlockSpec(memory_space=pl.ANY)],
        out_specs=[pl.BlockSpec((tb, d), pidx), pl.BlockSpec((tb, d), sidx)],
        scratch_shapes=[pltpu.VMEM((2, r_max, d), _BF16), pltpu.VMEM((tb, d), _F32),
                        pltpu.SemaphoreType.DMA((2,))])
    return pl.pallas_call(
        functools.partial(_combine_kernel, nbp=nbp, alpha=alpha),
        grid_spec=grid_spec,
        out_shape=[jax.ShapeDtypeStruct(x1p.shape, _F32), jax.ShapeDtypeStruct(x1s.shape, _F32)],
        compiler_params=_params(("arbitrary",)),
        name="combine",
    )(flat(plan["seg"]), flat(plan["loc"]), flat(plan["goff"]), plan["used"],
      rp, rs, d4, x1p, x1s, modp, mods, ln2_g, ln2_b, ys)


def _complex_powers(zr, zi, n):
    low = 16
    if n <= low or n % low:
        return _bit_powers(zr, zi, n)
    lr, li = _bit_powers(zr, zi, low)
    sr, si = zr, zi
    for _ in range(4):
        sr, si = sr * sr - si * si, 2.0 * sr * si
    hr, hi = _bit_powers(sr, si, n // low)
    pr = hr[:, :, None] * lr[:, None, :] - hi[:, :, None] * li[:, None, :]
    pi = hr[:, :, None] * li[:, None, :] + hi[:, :, None] * lr[:, None, :]
    return pr.reshape(-1, n), pi.reshape(-1, n)


def _bit_powers(zr, zi, n):
    k = jnp.arange(n, dtype=jnp.int32)[None, :]
    pr = jnp.ones((zr.shape[0], n), _F32)
    pi = jnp.zeros((zr.shape[0], n), _F32)
    sr, si = zr[:, None], zi[:, None]
    bit = 1
    while bit < n:
        on = (k & bit) != 0
        mr, mi = jnp.where(on, sr, 1.0), jnp.where(on, si, 0.0)
        pr, pi = pr * mr - pi * mi, pr * mi + pi * mr
        sr, si = sr * sr - si * si, 2.0 * sr * si
        bit *= 2
    return pr, pi


def _block_diag(m):
    g, a, b = m.shape
    gb = GROUPS_PER_BLOCK
    m = m.reshape(g // gb, gb, a, 1, b)
    eye = jnp.eye(gb, dtype=m.dtype).reshape(1, gb, 1, gb, 1)
    return (m * eye).reshape(g // gb, gb * a, gb * b)


def _layer_weights(p, q):
    d = p["w_in"].shape[0]
    row = lambda v: v.reshape(1, -1).astype(_F32)
    lam_re, lam_im = p["lam_re"].astype(_F32), p["lam_im"].astype(_F32)
    dt = jnp.exp(p["log_dt"].astype(_F32))[:, None]
    mag = jnp.exp(lam_re * dt)
    lbr, lbi = mag * jnp.cos(lam_im * dt), mag * jnp.sin(lam_im * dt)
    den = lam_re * lam_re + lam_im * lam_im
    nr, ni = lbr - 1.0, lbi
    fr, fi = (nr * lam_re + ni * lam_im) / den, (ni * lam_re - nr * lam_im) / den
    b_re, b_im = p["b_re"].astype(_F32), p["b_im"].astype(_F32)
    bbr = fr[..., None] * b_re - fi[..., None] * b_im
    bbi = fr[..., None] * b_im + fi[..., None] * b_re
    nstate = lam_re.size
    mod2 = lbr * lbr + lbi * lbi
    flat = lambda v: v.reshape(nstate)
    epr, epi = _complex_powers(flat(lbr), flat(lbi), q)
    eir, eii = _complex_powers(flat(lbr / mod2), flat(-lbi / mod2), q)
    table = lambda t: t
    bcast = lambda v: jnp.broadcast_to(v.reshape(nstate, 1), (nstate, LANES))
    c_re, c_im = p["c_re"].astype(_F32), p["c_im"].astype(_F32)
    w_r = p["w_router"].astype(_F32)
    w_r = jnp.pad(w_r, ((0, 0), (0, LANES - w_r.shape[1])))
    wr_hi = _bf(w_r)
    return dict(
        w_in=_bf(p["w_in"]), b_in=row(p["b_in"]), w_dw=p["w_dw"].astype(_F32), b_dw=row(p["b_dw"]),
        cln_g=row(p["conv_ln_g"]), cln_b=row(p["conv_ln_b"]), w_pw=_bf(p["w_pw"]), b_pw=row(p["b_pw"]),
        bre=_bf(_block_diag(bbr)), bim=_bf(_block_diag(bbi)),
        cc=_bf(jnp.concatenate([_block_diag(c_re), _block_diag(-c_im)], axis=2)),
        einv_r=table(eir), einv_i=table(eii), epow_r=table(epr), epow_i=table(epi),
        lamb_r=bcast(lbr), lamb_i=bcast(lbi), lrow_r=lbr.reshape(1, nstate), lrow_i=lbi.reshape(1, nstate),
        dskip=row(p["d_skip"]),
        w_sv=_bf(p["w_sv"]), b_sv=row(p["b_sv"]), w_sg=_bf(p["w_sg"]), b_sg=row(p["b_sg"]),
        w_out=_bf(p["w_out"]), b_out=row(p["b_out"]), ln1_g=row(p["ln1_g"]), ln1_b=row(p["ln1_b"]),
        wr_hi=wr_hi, wr_lo=_bf(w_r - wr_hi.astype(_F32)),
        b_r=p["b_router"].astype(_F32).reshape(-1, 1),
        ln2_g=row(p["ln2_g"]), ln2_b=row(p["ln2_b"]))


def _layer(xp, xs_tm, c_all, hist_tm, h0r, h0i, p, alpha):
    b, l, d = xp.shape
    steps, nbs, _ = xs_tm.shape
    tb = TOKEN_BLOCK
    assert l % tb == 0 and l >= CONV_WIDTH - 1 and (steps * nbs) % tb == 0 and tb % nbs == 0
    wts = _layer_weights(p, tb)
    mod = _ada(c_all, p["w_ada"].astype(_F32), p["b_ada"].astype(_F32))
    modp, mods = mod[:b].reshape(b, 1, -1), mod[b:]

    x1p, h2p, rp, cntp, conv_p, sre_p, sim_p = _mixer_prompt(xp, modp, wts, alpha)
    x1s, h2s, rs, cnts, conv_s, sre_s, sim_s = _mixer_sample(xs_tm, mods, hist_tm, h0r, h0i, wts, alpha)

    n_tok = b * l + steps * nbs
    nb, r_max, nt_max = _plan_sizes(n_tok)
    cnt = jnp.concatenate([cntp.reshape(-1, LANES), cnts.reshape(-1, LANES)], axis=0)[:, :N_EXPERTS]
    plan = _routing_plan(cnt.astype(jnp.int32), nt_max)

    flat = lambda a: a.reshape(b * l, a.shape[-1])
    xs_sorted, d4 = _dispatch(plan, flat(h2p), h2s, flat(rp), rs, r_max, nt_max)
    ys_sorted = _experts(plan, xs_sorted, p["w1"], p["b1"], p["w2"], p["b2"], nt_max)
    yp, ysm = _combine(plan, ys_sorted, flat(rp), rs, d4, flat(x1p), x1s, modp, mods,
                       wts["ln2_g"], wts["ln2_b"], r_max, alpha, l // tb)
    return (yp.reshape(b, l, d), ysm.reshape(steps, nbs, d), conv_p, sre_p[:, 0], sim_p[:, 0],
            conv_s, sre_s, sim_s)


def kernel(x_prompt, x_sample, state_conv, state_ssm_re, state_ssm_im, c_prompt, c_sample, w_ada, b_ada, w_in, b_in, w_dw, b_dw, conv_ln_g, conv_ln_b, w_pw, b_pw, lam_re, lam_im, log_dt, b_re, b_im, c_re, c_im, d_skip, w_sv, b_sv, w_sg, b_sg, w_out, b_out, ln1_g, ln1_b, w_router, b_router, w1, b1, w2, b2, ln2_g, ln2_b):
    stacked = dict(w_ada=w_ada, b_ada=b_ada, w_in=w_in, b_in=b_in, w_dw=w_dw, b_dw=b_dw,
                   conv_ln_g=conv_ln_g, conv_ln_b=conv_ln_b, w_pw=w_pw, b_pw=b_pw, lam_re=lam_re,
                   lam_im=lam_im, log_dt=log_dt, b_re=b_re, b_im=b_im, c_re=c_re, c_im=c_im,
                   d_skip=d_skip, w_sv=w_sv, b_sv=b_sv, w_sg=w_sg, b_sg=b_sg, w_out=w_out, b_out=b_out,
                   ln1_g=ln1_g, ln1_b=ln1_b, w_router=w_router, b_router=b_router, w1=w1, b1=b1,
                   w2=w2, b2=b2, ln2_g=ln2_g, ln2_b=ln2_b)
    depth = w_ada.shape[0]
    alpha = (2 * depth) ** 0.25
    b = x_prompt.shape[0]
    nbs = x_sample.shape[0]
    g, s = state_ssm_re.shape[2], state_ssm_re.shape[3]
    xp = x_prompt
    xs_tm = jnp.transpose(x_sample, (1, 0, 2))
    c_all = jnp.concatenate([c_prompt, c_sample], axis=0)
    conv_ps, re_ps, im_ps, conv_ss, re_ss, im_ss = [], [], [], [], [], []
    for layer in range(depth):
        p = {k: v[layer] for k, v in stacked.items()}
        hist_tm = jnp.transpose(state_conv[layer], (1, 0, 2))
        h0r = state_ssm_re[layer].reshape(nbs, g * s)
        h0i = state_ssm_im[layer].reshape(nbs, g * s)
        xp, xs_tm, conv_p, sre_p, sim_p, conv_s, sre_s, sim_s = _layer(
            xp, xs_tm, c_all, hist_tm, h0r, h0i, p, alpha)
        conv_ps.append(conv_p.astype(state_conv.dtype))
        re_ps.append(sre_p.reshape(b, g, s))
        im_ps.append(sim_p.reshape(b, g, s))
        conv_ss.append(jnp.transpose(conv_s, (1, 0, 2)).astype(state_conv.dtype))
        re_ss.append(sre_s.reshape(nbs, g, s))
        im_ss.append(sim_s.reshape(nbs, g, s))
    return (xp, jnp.transpose(xs_tm, (1, 0, 2)), jnp.stack(conv_ps), jnp.stack(re_ps), jnp.stack(im_ps),
            jnp.stack(conv_ss), jnp.stack(re_ss), jnp.stack(im_ss))
```

```python
import functools

import jax
import jax.numpy as jnp
from jax import lax
from jax.experimental import pallas as pl
from jax.experimental.pallas import tpu as pltpu

CONV_WIDTH = 31
SSM_GROUP = 16
SSM_STATE = 64
N_EXPERTS = 32
TOP_K = 4
SWIGLU_LIMIT = 7.0
SWIGLU_ALPHA = 1.702
LN_EPS = 1e-5

LANES = 128
SUBLANES = 8
TOKEN_BLOCK = 256
EXPERT_TILE = 512
SEG_PAD = 16
GROUPS_PER_BLOCK = LANES // SSM_GROUP
CONV_HIST = 32
CONV_ROWS = 32
ROUTE_IDX = N_EXPERTS
ROUTE_GATE = N_EXPERTS + TOP_K
VMEM_LIMIT = 56 * 1024 * 1024

_F32 = jnp.float32
_BF16 = jnp.bfloat16


def _bf(x):
    return x.astype(_BF16)


def _dot(a, b):
    return jnp.dot(a, b, preferred_element_type=_F32)


def _dot_nt(a, b):
    return lax.dot_general(a, b, (((1,), (1,)), ((), ())), preferred_element_type=_F32)


def _split(x):
    hi = _bf(x)
    lo = _bf(x - hi.astype(_F32))
    return hi, lo


def _sigmoid(x):
    return 1.0 / (1.0 + jnp.exp(-x))


def _gelu_tanh(x):
    return 0.5 * x * (1.0 + jnp.tanh(0.7978845608028654 * (x + 0.044715 * (x * x * x))))


def _layer_norm(x, g, b):
    mu = jnp.mean(x, axis=-1, keepdims=True)
    xc = x - mu
    var = jnp.mean(xc * xc, axis=-1, keepdims=True)
    return xc * lax.rsqrt(var + LN_EPS) * g + b


def _const_spec(shape):
    nd = len(shape)
    return pl.BlockSpec(shape, lambda *_: (0,) * nd)


def _params(sem):
    return pltpu.CompilerParams(dimension_semantics=sem, vmem_limit_bytes=VMEM_LIMIT)


def _ada_kernel(c_ref, w_ref, b_ref, o_ref):
    c = c_ref[...]
    s_hi, s_lo = _split(c * _sigmoid(c))
    w_hi, w_lo = _split(w_ref[...])
    o_ref[...] = _dot(s_hi, w_hi) + _dot(s_lo, w_hi) + _dot(s_hi, w_lo) + b_ref[...]


def _ada(c, w, b):
    n, d = c.shape
    cols = w.shape[1]
    tn = d
    return pl.pallas_call(
        _ada_kernel,
        grid=(cols // tn,),
        in_specs=[pl.BlockSpec((n, d), lambda j: (0, 0)),
                  pl.BlockSpec((d, tn), lambda j: (0, j)),
                  pl.BlockSpec((1, tn), lambda j: (0, j))],
        out_specs=pl.BlockSpec((n, tn), lambda j: (0, j)),
        out_shape=jax.ShapeDtypeStruct((n, cols), _F32),
        compiler_params=_params(("arbitrary",)),
        name="ada",
    )(c, w, b.reshape(1, cols))


def _tiled(fn, tile, *args, vecs=()):
    n, m = args[0].shape
    tr, tc = min(tile[0], n), min(tile[1], m)
    rows_out = None
    for r0 in range(0, n, tr):
        cols_out = None
        for c0 in range(0, m, tc):
            res = fn(*[a[r0:r0 + tr, c0:c0 + tc] for a in args], *[v[:, c0:c0 + tc] for v in vecs])
            res = res if isinstance(res, tuple) else (res,)
            if cols_out is None:
                cols_out = [[] for _ in res]
            for acc, v in zip(cols_out, res):
                acc.append(v)
        row_vals = [c[0] if len(c) == 1 else jnp.concatenate(c, axis=1) for c in cols_out]
        if rows_out is None:
            rows_out = [[] for _ in row_vals]
        for acc, v in zip(rows_out, row_vals):
            acc.append(v)
    outs = [r[0] if len(r) == 1 else jnp.concatenate(r, axis=0) for r in rows_out]
    return outs[0] if len(outs) == 1 else tuple(outs)


def _in_proj(hb, w_in, b_in, lo, hi):
    return _dot(hb, w_in[:, lo:hi]) + b_in[:, lo:hi]


def _conv_tail(v, cln_g, cln_b, w_pw, b_pw):
    g, b = cln_g[...], cln_b[...]

    def norm_swish(vb):
        vb = _layer_norm(vb, g, b)
        return vb * _sigmoid(vb)
    v = _tiled(norm_swish, (SUBLANES, v.shape[1]), v)
    return _dot(_bf(v), w_pw[...]) + b_pw[...]


def _ssm_tail(y, zs, dskip, w_sv, b_sv, w_sg, b_sg):
    yg = _tiled(lambda yb, zb, db: _bf(_gelu_tanh(yb + db * zb)), (2 * SUBLANES, 4 * LANES), y, zs, vecs=(dskip,))
    sv, sg = _dot(yg, w_sv[...]), _dot(yg, w_sg[...])
    return _tiled(lambda a, b, bv, bg: (a + bv) * _sigmoid(b + bg), (SUBLANES, 4 * LANES),
                  sv, sg, vecs=(b_sv, b_sg))


def _route(h2, wr_hi, wr_lo, b_r):
    n = h2.shape[0]
    ne = b_r.shape[0]
    h_hi, h_lo = _split(h2)
    logits = (_dot(h_hi, wr_hi[...]) + _dot(h_lo, wr_hi[...]) + _dot(h_hi, wr_lo[...])).T[:ne, :] + b_r[...]
    row = lax.broadcasted_iota(jnp.int32, (ne, n), 0).astype(_F32)
    neg = jnp.float32(-jnp.inf)
    cur = logits
    vals, idxs = [], []
    for _ in range(TOP_K):
        m = jnp.max(cur, axis=0, keepdims=True)
        idx = jnp.min(jnp.where(cur == m, row, float(ne)), axis=0, keepdims=True)
        vals.append(m)
        idxs.append(idx)
        cur = jnp.where(row == idx, neg, cur)
    es = [jnp.exp(v - vals[0]) for v in vals]
    tot = es[0]
    for e in es[1:]:
        tot = tot + e
    inv = 1.0 / tot
    selected = jnp.zeros((ne, n), _F32)
    row8 = lax.broadcasted_iota(jnp.int32, (2 * TOP_K, n), 0)
    extra = jnp.zeros((2 * TOP_K, n), _F32)
    for k in range(TOP_K):
        gate = es[k] * inv
        selected = selected + jnp.where(row == idxs[k], 1.0, 0.0)
        extra = extra + jnp.where(row8 == k, idxs[k], 0.0) + jnp.where(row8 == TOP_K + k, gate, 0.0)
    rest = jnp.zeros((LANES - ne - 2 * TOP_K, n), _F32)
    return jnp.concatenate([selected, extra, rest], axis=0).T


def _members(route):
    lane = lax.broadcasted_iota(jnp.int32, route.shape, 1)
    return jnp.where(lane < N_EXPERTS, route, 0.0)


def _block_counts(route):
    return jnp.sum(_members(route), axis=0, keepdims=True)


def _mixer_tail(x, zg, conv_out, ssm_out, g1, sh2, sc2, alpha, w_out, b_out,
                ln1_g, ln1_b, wr_hi, wr_lo, b_r):
    d = x.shape[1]
    merged = _tiled(lambda gc, gs, co, so: _bf(_sigmoid(gc) * co + _sigmoid(gs) * so),
                    (2 * SUBLANES, 4 * LANES), zg[:, :d], zg[:, d:], conv_out, ssm_out)
    m = _dot(merged, w_out[...])

    def norm_mod(xb, mb, gb, scb, shb, bo, lg, lb):
        x1b = _layer_norm(alpha * xb + gb * (mb + bo), lg, lb)
        return x1b, x1b * (1.0 + scb) + shb
    mods = (g1, sc2, sh2)
    per_seq = g1.shape[0] == 1
    x1, h2 = _tiled(norm_mod, (SUBLANES, d), x, m, *(() if per_seq else mods),
                    vecs=(mods if per_seq else ()) + (b_out, ln1_g, ln1_b))
    return x1, h2, _route(h2, wr_hi, wr_lo, b_r)


def _mixer_prompt_kernel(x_ref, mod_ref, w_in, b_in, w_dw, b_dw, cln_g, cln_b, w_pw, b_pw,
                         bre, bim, cc, einv_r, einv_i, epow_r, epow_i, lamb_r, lamb_i, dskip,
                         w_sv, b_sv, w_sg, b_sg, w_out, b_out, ln1_g, ln1_b, wr_hi, wr_lo, b_r,
                         x1_ref, h2_ref, route_ref, cnt_ref, cstate_ref, sre_ref, sim_ref,
                         ubuf, ush, car_r, car_i, *, alpha):
    c = pl.program_id(1)
    last = pl.num_programs(1) - 1
    tl, d = x_ref.shape[1], x_ref.shape[2]
    dc = w_pw.shape[0]
    dsm = dskip.shape[1]
    nj = bre.shape[0]
    rows = bre.shape[1]

    @pl.when(c == 0)
    def _():
        ubuf[0:CONV_HIST, :] = jnp.zeros((CONV_HIST, dc), _F32)
        car_r[...] = jnp.zeros(car_r.shape, _F32)
        car_i[...] = jnp.zeros(car_i.shape, _F32)

    x = x_ref[0]
    mod = mod_ref[0]
    sh1, sc1, g1 = mod[:, 0:d], mod[:, d:2 * d], mod[:, 2 * d:3 * d]
    sh2, sc2 = mod[:, 3 * d:4 * d], mod[:, 4 * d:5 * d]
    hb = _bf(x * (1.0 + sc1) + sh1)

    u = _in_proj(hb, w_in, b_in, 0, dc) * _sigmoid(_in_proj(hb, w_in, b_in, dc, 2 * dc))
    ubuf[CONV_HIST:CONV_HIST + tl, :] = u
    span = ush.shape[1]
    for r in range(1, SUBLANES):
        ush[r - 1] = ubuf[r:r + span, :]

    n_blocks = tl // CONV_ROWS
    gate_lo = 2 * dc + dsm
    gate_w = 2 * d // n_blocks
    vblocks, zg_cols = [], []
    for blk in range(n_blocks):
        r0 = blk * CONV_ROWS
        acc = jnp.broadcast_to(b_dw[...], (CONV_ROWS, dc))
        for k in range(CONV_WIDTH):
            q, r = divmod(CONV_HIST - (CONV_WIDTH - 1) + k, SUBLANES)
            rows_k = slice(r0 + q * SUBLANES, r0 + q * SUBLANES + CONV_ROWS)
            tap = ubuf[rows_k, :] if r == 0 else ush[r - 1, rows_k, :]
            acc = acc + w_dw[k:k + 1, :] * tap
        vblocks.append(acc)
        zg_cols.append(_in_proj(hb, w_in, b_in, gate_lo + blk * gate_w, gate_lo + (blk + 1) * gate_w))
    ubuf[0:CONV_HIST, :] = ubuf[tl:tl + CONV_HIST, :]
    conv_out = _conv_tail(jnp.concatenate(vblocks, axis=0), cln_g, cln_b, w_pw, b_pw)

    zs = _in_proj(hb, w_in, b_in, 2 * dc, 2 * dc + dsm)
    zst = _bf(zs.T)
    tri = jnp.where(lax.broadcasted_iota(jnp.int32, (tl, tl), 0)
                    <= lax.broadcasted_iota(jnp.int32, (tl, tl), 1), 1.0, 0.0).astype(_BF16)
    def project(j):
        zj = zst[j * LANES:(j + 1) * LANES, :]
        return _dot(bre[j], zj), _dot(bim[j], zj)

    def scale_in(j, bu):
        r = slice(j * rows, (j + 1) * rows)
        eir, eii = einv_r[r, :], einv_i[r, :]
        return _bf(bu[0] * eir - bu[1] * eii), _bf(bu[0] * eii + bu[1] * eir)

    def prefix(v):
        return _dot(v[0], tri), _dot(v[1], tri)

    def scale_out(j, cum):
        r = slice(j * rows, (j + 1) * rows)
        cr = jnp.broadcast_to(car_r[r, LANES - 1:LANES], (rows, LANES))
        ci = jnp.broadcast_to(car_i[r, LANES - 1:LANES], (rows, LANES))
        lr, li = lamb_r[r, :], lamb_i[r, :]
        cum_r = cum[0] + jnp.concatenate([lr * cr - li * ci] * (tl // LANES), axis=1)
        cum_i = cum[1] + jnp.concatenate([lr * ci + li * cr] * (tl // LANES), axis=1)
        epr, epi = epow_r[r, :], epow_i[r, :]
        h_r = cum_r * epr - cum_i * epi
        h_i = cum_r * epi + cum_i * epr
        car_r[r, :] = h_r[:, tl - LANES:tl]
        car_i[r, :] = h_i[:, tl - LANES:tl]
        return jnp.concatenate([_bf(h_r), _bf(h_i)], axis=0)

    bu, v, cum, hcat, yts = {}, {}, {}, {}, {}
    for t in range(nj + 4):
        if t < nj:
            bu[t] = project(t)
        if 0 <= t - 2 < nj:
            cum[t - 2] = prefix(v.pop(t - 2))
        if 0 <= t - 4 < nj:
            yts[t - 4] = _dot(cc[t - 4], hcat.pop(t - 4))
        if 0 <= t - 1 < nj:
            v[t - 1] = scale_in(t - 1, bu.pop(t - 1))
        if 0 <= t - 3 < nj:
            hcat[t - 3] = scale_out(t - 3, cum.pop(t - 3))
    ssm_out = _ssm_tail(jnp.concatenate([yts[j] for j in range(nj)], axis=0).T, zs, dskip,
                        w_sv, b_sv, w_sg, b_sg)

    x1, h2, route = _mixer_tail(x, jnp.concatenate(zg_cols, axis=1), conv_out, ssm_out, g1, sh2, sc2, alpha,
                                w_out, b_out, ln1_g, ln1_b, wr_hi, wr_lo, b_r)
    x1_ref[0] = x1
    h2_ref[0] = _bf(h2)
    route_ref[0] = route
    cnt_ref[0, 0] = _block_counts(route)

    @pl.when(c == last)
    def _():
        cstate_ref[0] = ubuf[CONV_HIST + tl - (CONV_WIDTH - 1):CONV_HIST + tl, :]
        sre_ref[0] = car_r[...].T[LANES - 1:LANES, :]
        sim_ref[0] = car_i[...].T[LANES - 1:LANES, :]


def _mixer_prompt(x, mod, wts, alpha):
    b, l, d = x.shape
    tl = TOKEN_BLOCK
    nc = l // tl
    dc = wts["w_pw"].shape[0]
    nstate = wts["lamb_r"].shape[0]
    names = ["w_in", "b_in", "w_dw", "b_dw", "cln_g", "cln_b", "w_pw", "b_pw", "bre", "bim", "cc",
             "einv_r", "einv_i", "epow_r", "epow_i", "lamb_r", "lamb_i", "dskip",
             "w_sv", "b_sv", "w_sg", "b_sg", "w_out", "b_out", "ln1_g", "ln1_b",
             "wr_hi", "wr_lo", "b_r"]
    consts = [wts[n] for n in names]
    tok = lambda bi, ci: (bi, ci, 0)
    seq = lambda bi, ci: (bi, 0, 0)
    return pl.pallas_call(
        functools.partial(_mixer_prompt_kernel, alpha=alpha),
        grid=(b, nc),
        in_specs=[pl.BlockSpec((1, tl, d), tok), pl.BlockSpec((1, 1, mod.shape[-1]), seq)]
                 + [_const_spec(a.shape) for a in consts],
        out_specs=[pl.BlockSpec((1, tl, d), tok), pl.BlockSpec((1, tl, d), tok),
                   pl.BlockSpec((1, tl, LANES), tok),
                   pl.BlockSpec((1, 1, 1, LANES), lambda bi, ci: (bi, ci, 0, 0)),
                   pl.BlockSpec((1, CONV_WIDTH - 1, dc), seq),
                   pl.BlockSpec((1, 1, nstate), seq), pl.BlockSpec((1, 1, nstate), seq)],
        out_shape=[jax.ShapeDtypeStruct((b, l, d), _F32), jax.ShapeDtypeStruct((b, l, d), _BF16),
                   jax.ShapeDtypeStruct((b, l, LANES), _F32),
                   jax.ShapeDtypeStruct((b, nc, 1, LANES), _F32),
                   jax.ShapeDtypeStruct((b, CONV_WIDTH - 1, dc), _F32),
                   jax.ShapeDtypeStruct((b, 1, nstate), _F32),
                   jax.ShapeDtypeStruct((b, 1, nstate), _F32)],
        scratch_shapes=[pltpu.VMEM((CONV_HIST + tl, dc), _F32),
                        pltpu.VMEM((SUBLANES - 1, tl + CONV_HIST - SUBLANES, dc), _F32),
                        pltpu.VMEM((nstate, LANES), _F32), pltpu.VMEM((nstate, LANES), _F32)],
        compiler_params=_params(("arbitrary", "arbitrary")),
        name="mixer_prompt",
    )(x, mod, *consts)


def _mixer_sample_kernel(x_ref, mod_ref, hist_ref, h0r_ref, h0i_ref,
                         w_in, b_in, w_dw, b_dw, cln_g, cln_b, w_pw, b_pw,
                         bre, bim, cc, lrow_r, lrow_i, dskip,
                         w_sv, b_sv, w_sg, b_sg, w_out, b_out, ln1_g, ln1_b, wr_hi, wr_lo, b_r,
                         x1_ref, h2_ref, route_ref, cnt_ref, cstate_ref, sre_ref, sim_ref, *, alpha):
    steps, nb, d = x_ref.shape
    dc = w_pw.shape[0]
    dsm = dskip.shape[1]
    nj = bre.shape[0]
    rows = bre.shape[1]
    nhist = CONV_WIDTH - 1
    x = x_ref[...].reshape(steps * nb, d)
    mod = jnp.concatenate([mod_ref[...]] * steps, axis=0)
    sh1, sc1, g1 = mod[:, 0:d], mod[:, d:2 * d], mod[:, 2 * d:3 * d]
    sh2, sc2 = mod[:, 3 * d:4 * d], mod[:, 4 * d:5 * d]
    hb = _bf(x * (1.0 + sc1) + sh1)

    u = _in_proj(hb, w_in, b_in, 0, dc) * _sigmoid(_in_proj(hb, w_in, b_in, dc, 2 * dc))
    us = [u[t * nb:(t + 1) * nb, :] for t in range(steps)]
    vs = []
    for t in range(steps):
        acc = jnp.broadcast_to(b_dw[...], (nb, dc))
        for i in range(t, nhist):
            acc = acc + w_dw[i - t:i - t + 1, :] * hist_ref[i]
        for s in range(t + 1):
            acc = acc + w_dw[nhist + s - t:nhist + s - t + 1, :] * us[s]
        vs.append(acc)
    for i in range(nhist):
        src = i + steps
        cstate_ref[i] = hist_ref[src] if src < nhist else us[src - nhist]
    conv_out = _conv_tail(jnp.concatenate(vs, axis=0), cln_g, cln_b, w_pw, b_pw)

    zs = _in_proj(hb, w_in, b_in, 2 * dc, 2 * dc + dsm)
    zsb = _bf(zs)
    h_r, h_i = h0r_ref[...], h0i_ref[...]
    lr, li = lrow_r[...], lrow_i[...]
    ys = []
    for t in range(steps):
        zt = zsb[t * nb:(t + 1) * nb, :]
        bu_r = jnp.concatenate([_dot_nt(zt[:, j * LANES:(j + 1) * LANES], bre[j]) for j in range(nj)], axis=1)
        bu_i = jnp.concatenate([_dot_nt(zt[:, j * LANES:(j + 1) * LANES], bim[j]) for j in range(nj)], axis=1)
        h_r, h_i = lr * h_r - li * h_i + bu_r, lr * h_i + li * h_r + bu_i
        hrb, hib = _bf(h_r), _bf(h_i)
        ys.append(jnp.concatenate(
            [_dot_nt(jnp.concatenate([hrb[:, j * rows:(j + 1) * rows], hib[:, j * rows:(j + 1) * rows]], axis=1),
                     cc[j]) for j in range(nj)], axis=1))
    sre_ref[...] = h_r
    sim_ref[...] = h_i
    ssm_out = _ssm_tail(jnp.concatenate(ys, axis=0), zs, dskip, w_sv, b_sv, w_sg, b_sg)

    zg = _in_proj(hb, w_in, b_in, 2 * dc + dsm, 2 * dc + dsm + 2 * d)
    x1, h2, route = _mixer_tail(x, zg, conv_out, ssm_out, g1, sh2, sc2, alpha,
                                w_out, b_out, ln1_g, ln1_b, wr_hi, wr_lo, b_r)
    x1_ref[...] = x1
    h2_ref[...] = _bf(h2)
    route_ref[...] = route
    for blk in range(cnt_ref.shape[0]):
        cnt_ref[blk] = _block_counts(route[blk * TOKEN_BLOCK:(blk + 1) * TOKEN_BLOCK, :])


def _mixer_sample(x_tm, mod, hist_tm, h0r, h0i, wts, alpha):
    steps, nb, d = x_tm.shape
    n = steps * nb
    dc = wts["w_pw"].shape[0]
    nstate = h0r.shape[1]
    names = ["w_in", "b_in", "w_dw", "b_dw", "cln_g", "cln_b", "w_pw", "b_pw", "bre", "bim", "cc",
             "lrow_r", "lrow_i", "dskip", "w_sv", "b_sv", "w_sg", "b_sg", "w_out", "b_out",
             "ln1_g", "ln1_b", "wr_hi", "wr_lo", "b_r"]
    args = [x_tm, mod, hist_tm, h0r, h0i] + [wts[k] for k in names]
    return pl.pallas_call(
        functools.partial(_mixer_sample_kernel, alpha=alpha),
        grid=(1,),
        in_specs=[_const_spec(a.shape) for a in args],
        out_specs=[_const_spec((n, d)), _const_spec((n, d)), _const_spec((n, LANES)),
                   _const_spec((n // TOKEN_BLOCK, 1, LANES)),
                   _const_spec((CONV_WIDTH - 1, nb, dc)),
                   _const_spec((nb, nstate)), _const_spec((nb, nstate))],
        out_shape=[jax.ShapeDtypeStruct((n, d), _F32), jax.ShapeDtypeStruct((n, d), _BF16),
                   jax.ShapeDtypeStruct((n, LANES), _F32),
                   jax.ShapeDtypeStruct((n // TOKEN_BLOCK, 1, LANES), _F32),
                   jax.ShapeDtypeStruct((CONV_WIDTH - 1, nb, dc), _F32),
                   jax.ShapeDtypeStruct((nb, nstate), _F32), jax.ShapeDtypeStruct((nb, nstate), _F32)],
        compiler_params=_params(("arbitrary",)),
        name="mixer_sample",
    )(*args)


def _round_up(x, m):
    return (x + m - 1) // m * m


def _plan_sizes(n_tokens):
    nb = n_tokens // TOKEN_BLOCK
    r_max = _round_up(TOP_K * TOKEN_BLOCK + 2 * N_EXPERTS * (SEG_PAD - 1), 2 * LANES)
    rows_max = n_tokens * TOP_K + N_EXPERTS * (EXPERT_TILE - 1)
    nt_max = -(-rows_max // EXPERT_TILE)
    return nb, r_max, nt_max


def _routing_plan(cnt, nt_max):
    nb, ne = cnt.shape
    before = jnp.cumsum(cnt, axis=0) - cnt
    tot = jnp.sum(cnt, axis=0)
    cin = before % SEG_PAD
    last = (jnp.arange(nb, dtype=jnp.int32) == nb - 1)[:, None]
    active = (cnt > 0) | (last & (cin > 0))
    seg = jnp.where(active, _round_up(cin + cnt, SEG_PAD), 0)
    loc = jnp.cumsum(seg, axis=1) - seg
    totpad = _round_up(tot, EXPERT_TILE)
    eend = jnp.cumsum(totpad)
    estart = eend - totpad
    goff = estart[None, :] + before - cin
    nfull = jnp.where(last, seg, jnp.where(active, (cin + cnt) // SEG_PAD * SEG_PAD, 0))
    cout = jnp.where(last | ~active, 0, (cin + cnt) % SEG_PAD)
    n_tiles = (eend[-1] // EXPERT_TILE).astype(jnp.int32)
    tiles = jnp.minimum(jnp.arange(nt_max, dtype=jnp.int32), n_tiles - 1)
    te = jnp.sum(((eend // EXPERT_TILE)[None, :] <= tiles[:, None]).astype(jnp.int32), axis=1)
    te = jnp.minimum(te, ne - 1)
    tot16 = _round_up(tot, SEG_PAD)
    of_tile = te[:, None] == jnp.arange(ne, dtype=jnp.int32)[None, :]
    valid = jnp.sum(jnp.where(of_tile, (estart + tot16)[None, :], 0), axis=1) - tiles * EXPERT_TILE
    short = (valid <= EXPERT_TILE // 2).astype(jnp.int32)
    return dict(seg=seg, loc=loc, goff=goff, nfull=nfull, cin=jnp.where(active, cin, 0), cout=cout, short=short,
                used=jnp.sum(seg, axis=1).astype(jnp.int32), sent=jnp.sum(nfull, axis=1).astype(jnp.int32),
                n_tiles=n_tiles.reshape(1), te=te,
                npad=(totpad - tot16).astype(jnp.int32), pad_base=(estart + tot16).astype(jnp.int32))


def _dest_columns(route, loc_row):
    n = route.shape[0]
    lane_i = lax.broadcasted_iota(jnp.int32, (n, LANES), 1)
    lane = lane_i.astype(_F32)
    member = _bf(_members(route))
    strict = jnp.where(lax.broadcasted_iota(jnp.int32, (n, n), 1)
                       < lax.broadcasted_iota(jnp.int32, (n, n), 0), 1.0, 0.0).astype(_BF16)
    dest = _dot(strict, member) + loc_row
    d4 = jnp.zeros((n, LANES), _F32)
    for k in range(TOP_K):
        idx = route[:, ROUTE_IDX + k:ROUTE_IDX + k + 1]
        dk = jnp.sum(jnp.where(lane == idx, dest, 0.0), axis=-1, keepdims=True)
        d4 = d4 + jnp.where(lane_i == k, dk, 0.0)
    return d4


def _typical_rows(r_max):
    return min(r_max, _round_up(TOP_K * TOKEN_BLOCK + N_EXPERTS * SEG_PAD, 2 * LANES))


def _dispatch_kernel(nfull, sloc, goff, cin, cout, used, sent, npad, pad_base, hp_ref, hs_ref, rp_ref, rs_ref, loc_ref,
                     xs_hbm, d4_ref, buf, stage, zbuf, sem, zsem, *, nbp):
    i = pl.program_id(0)
    nblk = pl.num_programs(0)
    slot = lax.rem(i, 2)
    tb = hp_ref.shape[0]
    r_max = buf.shape[1]
    is_p = i < nbp
    h = jnp.where(is_p, hp_ref[...], hs_ref[...])
    route = jnp.where(is_p, rp_ref[...], rs_ref[...])
    d4 = _dest_columns(route, loc_ref[0])
    d4_ref[...] = d4
    d4t = d4.T

    def compact(rows):
        row = lax.broadcasted_iota(jnp.int32, (rows, tb), 0).astype(_F32)
        p = jnp.zeros((rows, tb), _F32)
        for k in range(TOP_K):
            p = p + jnp.where(row == d4t[k:k + 1, :], 1.0, 0.0)
        buf[slot, 0:rows, :] = _bf(_dot(_bf(p), h))

    r_typ = _typical_rows(r_max)

    @pl.when(i == 0)
    def _():
        buf[...] = jnp.zeros(buf.shape, buf.dtype)
        stage[...] = jnp.zeros(stage.shape, stage.dtype)

    @pl.when(used[i] <= r_typ)
    def _():
        compact(r_typ)

    @pl.when(used[i] > r_typ)
    def _():
        compact(r_max)

    def seg_copy(blk, s, e):
        n = pl.multiple_of(nfull[blk * N_EXPERTS + e], SEG_PAD)
        src = pl.multiple_of(sloc[blk * N_EXPERTS + e], SEG_PAD)
        dst = pl.multiple_of(goff[blk * N_EXPERTS + e], SEG_PAD)
        return pltpu.make_async_copy(buf.at[s, pl.ds(src, n), :], xs_hbm.at[pl.ds(dst, n), :], sem.at[s])

    def pad_copy(e):
        n = pl.multiple_of(npad[e], SEG_PAD)
        dst = pl.multiple_of(pad_base[e], SEG_PAD)
        return pltpu.make_async_copy(zbuf.at[pl.ds(0, n), :], xs_hbm.at[pl.ds(dst, n), :], zsem.at[0])

    def for_segments(blk, fn):
        for e in range(N_EXPERTS):
            @pl.when(nfull[blk * N_EXPERTS + e] > 0)
            def _(e=e):
                fn(e)

    for e in range(N_EXPERTS):
        first = pl.ds(pl.multiple_of(sloc[i * N_EXPERTS + e], SEG_PAD), SEG_PAD)
        carried = jnp.where(cin[i * N_EXPERTS + e] > 0, stage[e], jnp.zeros_like(stage[e]))
        buf[slot, first, :] = buf[slot, first, :] + carried

    for_segments(i, lambda e: seg_copy(i, slot, e).start())

    for e in range(N_EXPERTS):
        rest = pl.multiple_of(sloc[i * N_EXPERTS + e] + nfull[i * N_EXPERTS + e], SEG_PAD)
        stage[e] = jnp.where(cout[i * N_EXPERTS + e] > 0, buf[slot, pl.ds(rest, SEG_PAD), :], stage[e])

    def wait_block(blk, s):
        @pl.when(sent[blk] > 0)
        def _():
            n = pl.multiple_of(sent[blk], SEG_PAD)
            pltpu.make_async_copy(buf.at[s, pl.ds(0, n), :], xs_hbm.at[pl.ds(0, n), :], sem.at[s]).wait()

    @pl.when(i > 0)
    def _():
        wait_block(i - 1, 1 - slot)

    @pl.when(i == nblk - 1)
    def _():
        zbuf[...] = jnp.zeros(zbuf.shape, zbuf.dtype)
        for e in range(N_EXPERTS):
            @pl.when(npad[e] > 0)
            def _(e=e):
                pad_copy(e).start()
        for e in range(N_EXPERTS):
            @pl.when(npad[e] > 0)
            def _(e=e):
                pad_copy(e).wait()
        wait_block(i, slot)


def _dispatch(plan, h2p, h2s, rp, rs, r_max, nt_max):
    tb = TOKEN_BLOCK
    d = h2p.shape[1]
    nbp, nbs = h2p.shape[0] // tb, h2s.shape[0] // tb
    nb = nbp + nbs
    first_row = (plan["loc"] + plan["cin"]).astype(_F32)
    loc = jnp.pad(first_row, ((0, 0), (0, LANES - N_EXPERTS))).reshape(nb, 1, LANES)
    pidx = lambda i, *_: (jnp.minimum(i, nbp - 1), 0)
    sidx = lambda i, *_: (jnp.maximum(i - nbp, 0), 0)
    grid_spec = pltpu.PrefetchScalarGridSpec(
        num_scalar_prefetch=9,
        grid=(nb,),
        in_specs=[pl.BlockSpec((tb, d), pidx), pl.BlockSpec((tb, d), sidx),
                  pl.BlockSpec((tb, LANES), pidx), pl.BlockSpec((tb, LANES), sidx),
                  pl.BlockSpec((1, 1, LANES), lambda i, *_: (i, 0, 0))],
        out_specs=[pl.BlockSpec(memory_space=pl.ANY), pl.BlockSpec((tb, LANES), lambda i, *_: (i, 0))],
        scratch_shapes=[pltpu.VMEM((2, r_max, d), _BF16), pltpu.VMEM((N_EXPERTS, SEG_PAD, d), _BF16),
                        pltpu.VMEM((EXPERT_TILE, d), _BF16),
                        pltpu.SemaphoreType.DMA((2,)), pltpu.SemaphoreType.DMA((1,))])
    flat = lambda a: a.astype(jnp.int32).reshape(-1)
    return pl.pallas_call(
        functools.partial(_dispatch_kernel, nbp=nbp),
        grid_spec=grid_spec,
        out_shape=[jax.ShapeDtypeStruct((nt_max * EXPERT_TILE, d), _BF16),
                   jax.ShapeDtypeStruct((nb * tb, LANES), _F32)],
        compiler_params=_params(("arbitrary",)),
        name="dispatch",
    )(flat(plan["nfull"]), flat(plan["loc"]), flat(plan["goff"]), flat(plan["cin"]), flat(plan["cout"]),
      plan["used"], plan["sent"], plan["npad"], plan["pad_base"], h2p, h2s, rp, rs, loc)


def _expert_kernel(te, n_tiles, short, x_ref, w1_ref, b1_ref, w2_ref, b2_ref, y_ref, w1b, w2b):
    i = pl.program_id(0)
    tm = x_ref.shape[0]
    dff = w2_ref.shape[1]

    def ffn(rows):
        gu = _dot(x_ref[0:rows, :], w1b[...]) + b1_ref[0]
        g = jnp.minimum(gu[:, :dff], SWIGLU_LIMIT)
        up = jnp.clip(gu[:, dff:], -SWIGLU_LIMIT, SWIGLU_LIMIT)
        act = g * _sigmoid(SWIGLU_ALPHA * g) * (up + 1.0)
        y_ref[0:rows, :] = _bf(_dot(_bf(act), w2b[...]) + b2_ref[0])

    @pl.when(i < n_tiles[0])
    def _():
        prev = te[jnp.maximum(i - 1, 0)]

        @pl.when((i == 0) | (te[i] != prev))
        def _():
            w1b[...] = _bf(w1_ref[0])
            w2b[...] = _bf(w2_ref[0])

        @pl.when(short[i] == 0)
        def _():
            ffn(tm)

        @pl.when(short[i] != 0)
        def _():
            ffn(tm // 2)


def _experts(plan, xs, w1, b1, w2, b2, nt_max):
    ne, d, dff2 = w1.shape
    dff = w2.shape[1]
    tm = EXPERT_TILE
    tile = lambda i, te, nt, short: (jnp.minimum(i, nt[0] - 1), 0)
    wsel = lambda i, te, nt, short: (te[i], 0, 0)
    grid_spec = pltpu.PrefetchScalarGridSpec(
        num_scalar_prefetch=3,
        grid=(nt_max,),
        in_specs=[pl.BlockSpec((tm, d), tile),
                  pl.BlockSpec((1, d, dff2), wsel), pl.BlockSpec((1, 1, dff2), wsel),
                  pl.BlockSpec((1, dff, d), wsel), pl.BlockSpec((1, 1, d), wsel)],
        out_specs=pl.BlockSpec((tm, d), tile),
        scratch_shapes=[pltpu.VMEM((d, dff2), _BF16), pltpu.VMEM((dff, d), _BF16)])
    return pl.pallas_call(
        _expert_kernel,
        grid_spec=grid_spec,
        out_shape=jax.ShapeDtypeStruct(xs.shape, _BF16),
        compiler_params=_params(("arbitrary",)),
        name="experts",
    )(plan["te"], plan["n_tiles"], plan["short"], xs, w1, b1.reshape(ne, 1, dff2), w2, b2.reshape(ne, 1, d))


def _combine_kernel(seg, sloc, goff, used, rp_ref, rs_ref, d4_ref, x1p_ref, x1s_ref, g2p_ref, g2s_ref,
                    ln2_g, ln2_b, ys_hbm, yp_ref, ysm_ref, buf, fbuf, sem, *, nbp, alpha):
    i = pl.program_id(0)
    nblk = pl.num_programs(0)
    slot = lax.rem(i, 2)
    tb = rp_ref.shape[0]
    r_max = buf.shape[1]

    def seg_copy(blk, s, e):
        n = pl.multiple_of(seg[blk * N_EXPERTS + e], SEG_PAD)
        src = pl.multiple_of(goff[blk * N_EXPERTS + e], SEG_PAD)
        dst = pl.multiple_of(sloc[blk * N_EXPERTS + e], SEG_PAD)
        return pltpu.make_async_copy(ys_hbm.at[pl.ds(src, n), :], buf.at[s, pl.ds(dst, n), :], sem.at[s])

    def for_segments(blk, fn):
        for e in range(N_EXPERTS):
            @pl.when(seg[blk * N_EXPERTS + e] > 0)
            def _(e=e):
                fn(e)

    @pl.when(i == 0)
    def _():
        buf[...] = jnp.zeros(buf.shape, buf.dtype)
        for_segments(0, lambda e: seg_copy(0, 0, e).start())

    @pl.when(i + 1 < nblk)
    def _():
        for_segments(i + 1, lambda e: seg_copy(i + 1, 1 - slot, e).start())

    @pl.when(used[i] > 0)
    def _():
        n = pl.multiple_of(used[i], SEG_PAD)
        pltpu.make_async_copy(ys_hbm.at[pl.ds(0, n), :], buf.at[slot, pl.ds(0, n), :], sem.at[slot]).wait()

    is_p = i < nbp
    route = jnp.where(is_p, rp_ref[...], rs_ref[...])
    d4 = d4_ref[...]

    def gather(rows):
        col = lax.broadcasted_iota(jnp.int32, (tb, rows), 1).astype(_F32)
        pg = jnp.zeros((tb, rows), _F32)
        for k in range(TOP_K):
            gate = route[:, ROUTE_GATE + k:ROUTE_GATE + k + 1]
            pg = pg + jnp.where(col == d4[:, k:k + 1], gate, 0.0)
        fbuf[...] = _dot(_bf(pg), buf[slot, 0:rows, :])

    r_typ = _typical_rows(r_max)

    @pl.when(used[i] <= r_typ)
    def _():
        gather(r_typ)

    @pl.when(used[i] > r_typ)
    def _():
        gather(r_max)

    f = fbuf[...]
    x1 = jnp.where(is_p, x1p_ref[...], x1s_ref[...])
    g2s = jnp.concatenate([g2s_ref[...]] * (tb // g2s_ref.shape[0]), axis=0)
    g2 = jnp.where(is_p, jnp.broadcast_to(g2p_ref[0], g2s.shape), g2s)
    y = _layer_norm(alpha * x1 + g2 * f, ln2_g[...], ln2_b[...])

    @pl.when(is_p)
    def _():
        yp_ref[...] = y

    @pl.when(jnp.logical_not(is_p))
    def _():
        ysm_ref[...] = y


def _combine(plan, ys, rp, rs, d4, x1p, x1s, modp, mods, ln2_g, ln2_b, r_max, alpha, blocks_per_seq):
    tb = TOKEN_BLOCK
    d = x1p.shape[1]
    nbp, nbs = x1p.shape[0] // tb, x1s.shape[0] // tb
    nb = nbp + nbs
    pidx = lambda i, *_: (jnp.minimum(i, nbp - 1), 0)
    sidx = lambda i, *_: (jnp.maximum(i - nbp, 0), 0)
    g2_lane_block = 5
    flat = lambda a: a.astype(jnp.int32).reshape(-1)
    grid_spec = pltpu.PrefetchScalarGridSpec(
        num_scalar_prefetch=4,
        grid=(nb,),
        in_specs=[pl.BlockSpec((tb, LANES), pidx), pl.BlockSpec((tb, LANES), sidx),
                  pl.BlockSpec((tb, LANES), lambda i, *_: (i, 0)),
                  pl.BlockSpec((tb, d), pidx), pl.BlockSpec((tb, d), sidx),
                  pl.BlockSpec((1, 1, d), lambda i, *_: (jnp.minimum(i, nbp - 1) // blocks_per_seq, 0,
                                                         g2_lane_block)),
                  pl.BlockSpec((mods.shape[0], d), lambda i, *_: (0, g2_lane_block)),
                  pl.BlockSpec((1, d), lambda i, *_: (0, 0)), pl.BlockSpec((1, d), lambda i, *_: (0, 0)),
                  pl.BlockSpec(memory_space=pl.ANY)],
        out_specs=[pl.BlockSpec((tb, d), pidx), pl.BlockSpec((tb, d), sidx)],
        scratch_shapes=[pltpu.VMEM((2, r_max, d), _BF16), pltpu.VMEM((tb, d), _F32),
                        pltpu.SemaphoreType.DMA((2,))])
    return pl.pallas_call(
        functools.partial(_combine_kernel, nbp=nbp, alpha=alpha),
        grid_spec=grid_spec,
        out_shape=[jax.ShapeDtypeStruct(x1p.shape, _F32), jax.ShapeDtypeStruct(x1s.shape, _F32)],
        compiler_params=_params(("arbitrary",)),
        name="combine",
    )(flat(plan["seg"]), flat(plan["loc"]), flat(plan["goff"]), plan["used"],
      rp, rs, d4, x1p, x1s, modp, mods, ln2_g, ln2_b, ys)


def _complex_powers(zr, zi, n):
    low = 16
    if n <= low or n % low:
        return _bit_powers(zr, zi, n)
    lr, li = _bit_powers(zr, zi, low)
    sr, si = zr, zi
    for _ in range(4):
        sr, si = sr * sr - si * si, 2.0 * sr * si
    hr, hi = _bit_powers(sr, si, n // low)
    pr = hr[:, :, None] * lr[:, None, :] - hi[:, :, None] * li[:, None, :]
    pi = hr[:, :, None] * li[:, None, :] + hi[:, :, None] * lr[:, None, :]
    return pr.reshape(-1, n), pi.reshape(-1, n)


def _bit_powers(zr, zi, n):
    k = jnp.arange(n, dtype=jnp.int32)[None, :]
    pr = jnp.ones((zr.shape[0], n), _F32)
    pi = jnp.zeros((zr.shape[0], n), _F32)
    sr, si = zr[:, None], zi[:, None]
    bit = 1
    while bit < n:
        on = (k & bit) != 0
        mr, mi = jnp.where(on, sr, 1.0), jnp.where(on, si, 0.0)
        pr, pi = pr * mr - pi * mi, pr * mi + pi * mr
        sr, si = sr * sr - si * si, 2.0 * sr * si
        bit *= 2
    return pr, pi


def _block_diag(m):
    g, a, b = m.shape
    gb = GROUPS_PER_BLOCK
    m = m.reshape(g // gb, gb, a, 1, b)
    eye = jnp.eye(gb, dtype=m.dtype).reshape(1, gb, 1, gb, 1)
    return (m * eye).reshape(g // gb, gb * a, gb * b)


def _layer_weights(p, q):
    d = p["w_in"].shape[0]
    row = lambda v: v.reshape(1, -1).astype(_F32)
    lam_re, lam_im = p["lam_re"].astype(_F32), p["lam_im"].astype(_F32)
    dt = jnp.exp(p["log_dt"].astype(_F32))[:, None]
    mag = jnp.exp(lam_re * dt)
    lbr, lbi = mag * jnp.cos(lam_im * dt), mag * jnp.sin(lam_im * dt)
    den = lam_re * lam_re + lam_im * lam_im
    nr, ni = lbr - 1.0, lbi
    fr, fi = (nr * lam_re + ni * lam_im) / den, (ni * lam_re - nr * lam_im) / den
    b_re, b_im = p["b_re"].astype(_F32), p["b_im"].astype(_F32)
    bbr = fr[..., None] * b_re - fi[..., None] * b_im
    bbi = fr[..., None] * b_im + fi[..., None] * b_re
    nstate = lam_re.size
    mod2 = lbr * lbr + lbi * lbi
    flat = lambda v: v.reshape(nstate)
    epr, epi = _complex_powers(flat(lbr), flat(lbi), q)
    eir, eii = _complex_powers(flat(lbr / mod2), flat(-lbi / mod2), q)
    table = lambda t: t
    bcast = lambda v: jnp.broadcast_to(v.reshape(nstate, 1), (nstate, LANES))
    c_re, c_im = p["c_re"].astype(_F32), p["c_im"].astype(_F32)
    w_r = p["w_router"].astype(_F32)
    w_r = jnp.pad(w_r, ((0, 0), (0, LANES - w_r.shape[1])))
    wr_hi = _bf(w_r)
    return dict(
        w_in=_bf(p["w_in"]), b_in=row(p["b_in"]), w_dw=p["w_dw"].astype(_F32), b_dw=row(p["b_dw"]),
        cln_g=row(p["conv_ln_g"]), cln_b=row(p["conv_ln_b"]), w_pw=_bf(p["w_pw"]), b_pw=row(p["b_pw"]),
        bre=_bf(_block_diag(bbr)), bim=_bf(_block_diag(bbi)),
        cc=_bf(jnp.concatenate([_block_diag(c_re), _block_diag(-c_im)], axis=2)),
        einv_r=table(eir), einv_i=table(eii), epow_r=table(epr), epow_i=table(epi),
        lamb_r=bcast(lbr), lamb_i=bcast(lbi), lrow_r=lbr.reshape(1, nstate), lrow_i=lbi.reshape(1, nstate),
        dskip=row(p["d_skip"]),
        w_sv=_bf(p["w_sv"]), b_sv=row(p["b_sv"]), w_sg=_bf(p["w_sg"]), b_sg=row(p["b_sg"]),
        w_out=_bf(p["w_out"]), b_out=row(p["b_out"]), ln1_g=row(p["ln1_g"]), ln1_b=row(p["ln1_b"]),
        wr_hi=wr_hi, wr_lo=_bf(w_r - wr_hi.astype(_F32)),
        b_r=p["b_router"].astype(_F32).reshape(-1, 1),
        ln2_g=row(p["ln2_g"]), ln2_b=row(p["ln2_b"]))


def _layer(xp, xs_tm, c_all, hist_tm, h0r, h0i, p, alpha):
    b, l, d = xp.shape
    steps, nbs, _ = xs_tm.shape
    tb = TOKEN_BLOCK
    assert l % tb == 0 and l >= CONV_WIDTH - 1 and (steps * nbs) % tb == 0 and tb % nbs == 0
    wts = _layer_weights(p, tb)
    mod = _ada(c_all, p["w_ada"].astype(_F32), p["b_ada"].astype(_F32))
    modp, mods = mod[:b].reshape(b, 1, -1), mod[b:]

    x1p, h2p, rp, cntp, conv_p, sre_p, sim_p = _mixer_prompt(xp, modp, wts, alpha)
    x1s, h2s, rs, cnts, conv_s, sre_s, sim_s = _mixer_sample(xs_tm, mods, hist_tm, h0r, h0i, wts, alpha)

    n_tok = b * l + steps * nbs
    nb, r_max, nt_max = _plan_sizes(n_tok)
    cnt = jnp.concatenate([cntp.reshape(-1, LANES), cnts.reshape(-1, LANES)], axis=0)[:, :N_EXPERTS]
    plan = _routing_plan(cnt.astype(jnp.int32), nt_max)

    flat = lambda a: a.reshape(b * l, a.shape[-1])
    xs_sorted, d4 = _dispatch(plan, flat(h2p), h2s, flat(rp), rs, r_max, nt_max)
    ys_sorted = _experts(plan, xs_sorted, p["w1"], p["b1"], p["w2"], p["b2"], nt_max)
    yp, ysm = _combine(plan, ys_sorted, flat(rp), rs, d4, flat(x1p), x1s, modp, mods,
                       wts["ln2_g"], wts["ln2_b"], r_max, alpha, l // tb)
    return (yp.reshape(b, l, d), ysm.reshape(steps, nbs, d), conv_p, sre_p[:, 0], sim_p[:, 0],
            conv_s, sre_s, sim_s)


def kernel(x_prompt, x_sample, state_conv, state_ssm_re, state_ssm_im, c_prompt, c_sample, w_ada, b_ada, w_in, b_in, w_dw, b_dw, conv_ln_g, conv_ln_b, w_pw, b_pw, lam_re, lam_im, log_dt, b_re, b_im, c_re, c_im, d_skip, w_sv, b_sv, w_sg, b_sg, w_out, b_out, ln1_g, ln1_b, w_router, b_router, w1, b1, w2, b2, ln2_g, ln2_b):
    stacked = dict(w_ada=w_ada, b_ada=b_ada, w_in=w_in, b_in=b_in, w_dw=w_dw, b_dw=b_dw,
                   conv_ln_g=conv_ln_g, conv_ln_b=conv_ln_b, w_pw=w_pw, b_pw=b_pw, lam_re=lam_re,
                   lam_im=lam_im, log_dt=log_dt, b_re=b_re, b_im=b_im, c_re=c_re, c_im=c_im,
                   d_skip=d_skip, w_sv=w_sv, b_sv=b_sv, w_sg=w_sg, b_sg=b_sg, w_out=w_out, b_out=b_out,
                   ln1_g=ln1_g, ln1_b=ln1_b, w_router=w_router, b_router=b_router, w1=w1, b1=b1,
                   w2=w2, b2=b2, ln2_g=ln2_g, ln2_b=ln2_b)
    depth = w_ada.shape[0]
    alpha = (2 * depth) ** 0.25
    b = x_prompt.shape[0]
    nbs = x_sample.shape[0]
    g, s = state_ssm_re.shape[2], state_ssm_re.shape[3]
    xp = x_prompt
    xs_tm = jnp.transpose(x_sample, (1, 0, 2))
    c_all = jnp.concatenate([c_prompt, c_sample], axis=0)
    conv_ps, re_ps, im_ps, conv_ss, re_ss, im_ss = [], [], [], [], [], []
    for layer in range(depth):
        p = {k: v[layer] for k, v in stacked.items()}
        hist_tm = jnp.transpose(state_conv[layer], (1, 0, 2))
        h0r = state_ssm_re[layer].reshape(nbs, g * s)
        h0i = state_ssm_im[layer].reshape(nbs, g * s)
        xp, xs_tm, conv_p, sre_p, sim_p, conv_s, sre_s, sim_s = _layer(
            xp, xs_tm, c_all, hist_tm, h0r, h0i, p, alpha)
        conv_ps.append(conv_p.astype(state_conv.dtype))
        re_ps.append(sre_p.reshape(b, g, s))
        im_ps.append(sim_p.reshape(b, g, s))
        conv_ss.append(jnp.transpose(conv_s, (1, 0, 2)).astype(state_conv.dtype))
        re_ss.append(sre_s.reshape(nbs, g, s))
        im_ss.append(sim_s.reshape(nbs, g, s))
    return (xp, jnp.transpose(xs_tm, (1, 0, 2)), jnp.stack(conv_ps), jnp.stack(re_ps), jnp.stack(im_ps),
            jnp.stack(conv_ss), jnp.stack(re_ss), jnp.stack(im_ss))
```

```python
import functools

import jax
import jax.numpy as jnp
from jax import lax
from jax.experimental import pallas as pl
from jax.experimental.pallas import tpu as pltpu

CONV_WIDTH = 31
SSM_GROUP = 16
SSM_STATE = 64
N_EXPERTS = 32
TOP_K = 4
SWIGLU_LIMIT = 7.0
SWIGLU_ALPHA = 1.702
LN_EPS = 1e-5

LANES = 128
SUBLANES = 8
TOKEN_BLOCK = 256
EXPERT_TILE = 512
SEG_PAD = 16
GROUPS_PER_BLOCK = LANES // SSM_GROUP
CONV_HIST = 32
CONV_ROWS = 32
ROUTE_IDX = N_EXPERTS
ROUTE_GATE = N_EXPERTS + TOP_K
VMEM_LIMIT = 56 * 1024 * 1024

_F32 = jnp.float32
_BF16 = jnp.bfloat16


def _bf(x):
    return x.astype(_BF16)


def _dot(a, b):
    return jnp.dot(a, b, preferred_element_type=_F32)


def _dot_nt(a, b):
    return lax.dot_general(a, b, (((1,), (1,)), ((), ())), preferred_element_type=_F32)


def _split(x):
    hi = _bf(x)
    lo = _bf(x - hi.astype(_F32))
    return hi, lo


def _sigmoid(x):
    return 1.0 / (1.0 + jnp.exp(-x))


def _gelu_tanh(x):
    return 0.5 * x * (1.0 + jnp.tanh(0.7978845608028654 * (x + 0.044715 * (x * x * x))))


def _layer_norm(x, g, b):
    mu = jnp.mean(x, axis=-1, keepdims=True)
    xc = x - mu
    var = jnp.mean(xc * xc, axis=-1, keepdims=True)
    return xc * lax.rsqrt(var + LN_EPS) * g + b


def _const_spec(shape):
    nd = len(shape)
    return pl.BlockSpec(shape, lambda *_: (0,) * nd)


def _params(sem):
    return pltpu.CompilerParams(dimension_semantics=sem, vmem_limit_bytes=VMEM_LIMIT)


def _ada_kernel(c_ref, w_ref, b_ref, o_ref):
    c = c_ref[...]
    s_hi, s_lo = _split(c * _sigmoid(c))
    w_hi, w_lo = _split(w_ref[...])
    o_ref[...] = _dot(s_hi, w_hi) + _dot(s_lo, w_hi) + _dot(s_hi, w_lo) + b_ref[...]


def _ada(c, w, b):
    n, d = c.shape
    cols = w.shape[1]
    tn = d
    return pl.pallas_call(
        _ada_kernel,
        grid=(cols // tn,),
        in_specs=[pl.BlockSpec((n, d), lambda j: (0, 0)),
                  pl.BlockSpec((d, tn), lambda j: (0, j)),
                  pl.BlockSpec((1, tn), lambda j: (0, j))],
        out_specs=pl.BlockSpec((n, tn), lambda j: (0, j)),
        out_shape=jax.ShapeDtypeStruct((n, cols), _F32),
        compiler_params=_params(("arbitrary",)),
        name="ada",
    )(c, w, b.reshape(1, cols))


def _tiled(fn, tile, *args, vecs=()):
    n, m = args[0].shape
    tr, tc = min(tile[0], n), min(tile[1], m)
    rows_out = None
    for r0 in range(0, n, tr):
        cols_out = None
        for c0 in range(0, m, tc):
            res = fn(*[a[r0:r0 + tr, c0:c0 + tc] for a in args], *[v[:, c0:c0 + tc] for v in vecs])
            res = res if isinstance(res, tuple) else (res,)
            if cols_out is None:
                cols_out = [[] for _ in res]
            for acc, v in zip(cols_out, res):
                acc.append(v)
        row_vals = [c[0] if len(c) == 1 else jnp.concatenate(c, axis=1) for c in cols_out]
        if rows_out is None:
            rows_out = [[] for _ in row_vals]
        for acc, v in zip(rows_out, row_vals):
            acc.append(v)
    outs = [r[0] if len(r) == 1 else jnp.concatenate(r, axis=0) for r in rows_out]
    return outs[0] if len(outs) == 1 else tuple(outs)


def _in_proj(hb, w_in, b_in, lo, hi):
    return _dot(hb, w_in[:, lo:hi]) + b_in[:, lo:hi]


def _conv_tail(v, cln_g, cln_b, w_pw, b_pw):
    g, b = cln_g[...], cln_b[...]

    def norm_swish(vb):
        vb = _layer_norm(vb, g, b)
        return vb * _sigmoid(vb)
    v = _tiled(norm_swish, (SUBLANES, v.shape[1]), v)
    return _dot(_bf(v), w_pw[...]) + b_pw[...]


def _ssm_tail(y, zs, dskip, w_sv, b_sv, w_sg, b_sg):
    yg = _tiled(lambda yb, zb, db: _bf(_gelu_tanh(yb + db * zb)), (2 * SUBLANES, 4 * LANES), y, zs, vecs=(dskip,))
    sv, sg = _dot(yg, w_sv[...]), _dot(yg, w_sg[...])
    return _tiled(lambda a, b, bv, bg: (a + bv) * _sigmoid(b + bg), (SUBLANES, 4 * LANES),
                  sv, sg, vecs=(b_sv, b_sg))


def _route(h2, wr_hi, wr_lo, b_r):
    n = h2.shape[0]
    ne = b_r.shape[0]
    h_hi, h_lo = _split(h2)
    logits = (_dot(h_hi, wr_hi[...]) + _dot(h_lo, wr_hi[...]) + _dot(h_hi, wr_lo[...])).T[:ne, :] + b_r[...]
    row = lax.broadcasted_iota(jnp.int32, (ne, n), 0).astype(_F32)
    neg = jnp.float32(-jnp.inf)
    cur = logits
    vals, idxs = [], []
    for _ in range(TOP_K):
        m = jnp.max(cur, axis=0, keepdims=True)
        idx = jnp.min(jnp.where(cur == m, row, float(ne)), axis=0, keepdims=True)
        vals.append(m)
        idxs.append(idx)
        cur = jnp.where(row == idx, neg, cur)
    es = [jnp.exp(v - vals[0]) for v in vals]
    tot = es[0]
    for e in es[1:]:
        tot = tot + e
    inv = 1.0 / tot
    selected = jnp.zeros((ne, n), _F32)
    row8 = lax.broadcasted_iota(jnp.int32, (2 * TOP_K, n), 0)
    extra = jnp.zeros((2 * TOP_K, n), _F32)
    for k in range(TOP_K):
        gate = es[k] * inv
        selected = selected + jnp.where(row == idxs[k], 1.0, 0.0)
        extra = extra + jnp.where(row8 == k, idxs[k], 0.0) + jnp.where(row8 == TOP_K + k, gate, 0.0)
    rest = jnp.zeros((LANES - ne - 2 * TOP_K, n), _F32)
    return jnp.concatenate([selected, extra, rest], axis=0).T


def _members(route):
    lane = lax.broadcasted_iota(jnp.int32, route.shape, 1)
    return jnp.where(lane < N_EXPERTS, route, 0.0)


def _block_counts(route):
    return jnp.sum(_members(route), axis=0, keepdims=True)


def _mixer_tail(x, zg, conv_out, ssm_out, g1, sh2, sc2, alpha, w_out, b_out,
                ln1_g, ln1_b, wr_hi, wr_lo, b_r):
    d = x.shape[1]
    merged = _tiled(lambda gc, gs, co, so: _bf(_sigmoid(gc) * co + _sigmoid(gs) * so),
                    (2 * SUBLANES, 4 * LANES), zg[:, :d], zg[:, d:], conv_out, ssm_out)
    m = _dot(merged, w_out[...])

    def norm_mod(xb, mb, gb, scb, shb, bo, lg, lb):
        x1b = _layer_norm(alpha * xb + gb * (mb + bo), lg, lb)
        return x1b, x1b * (1.0 + scb) + shb
    mods = (g1, sc2, sh2)
    per_seq = g1.shape[0] == 1
    x1, h2 = _tiled(norm_mod, (SUBLANES, d), x, m, *(() if per_seq else mods),
                    vecs=(mods if per_seq else ()) + (b_out, ln1_g, ln1_b))
    return x1, h2, _route(h2, wr_hi, wr_lo, b_r)


def _mixer_prompt_kernel(x_ref, mod_ref, w_in, b_in, w_dw, b_dw, cln_g, cln_b, w_pw, b_pw,
                         bre, bim, cc, einv_r, einv_i, epow_r, epow_i, lamb_r, lamb_i, dskip,
                         w_sv, b_sv, w_sg, b_sg, w_out, b_out, ln1_g, ln1_b, wr_hi, wr_lo, b_r,
                         x1_ref, h2_ref, route_ref, cnt_ref, cstate_ref, sre_ref, sim_ref,
                         ubuf, ush, car_r, car_i, *, alpha):
    c = pl.program_id(1)
    last = pl.num_programs(1) - 1
    tl, d = x_ref.shape[1], x_ref.shape[2]
    dc = w_pw.shape[0]
    dsm = dskip.shape[1]
    nj = bre.shape[0]
    rows = bre.shape[1]

    @pl.when(c == 0)
    def _():
        ubuf[0:CONV_HIST, :] = jnp.zeros((CONV_HIST, dc), _F32)
        car_r[...] = jnp.zeros(car_r.shape, _F32)
        car_i[...] = jnp.zeros(car_i.shape, _F32)

    x = x_ref[0]
    mod = mod_ref[0]
    sh1, sc1, g1 = mod[:, 0:d], mod[:, d:2 * d], mod[:, 2 * d:3 * d]
    sh2, sc2 = mod[:, 3 * d:4 * d], mod[:, 4 * d:5 * d]
    hb = _bf(x * (1.0 + sc1) + sh1)

    u = _in_proj(hb, w_in, b_in, 0, dc) * _sigmoid(_in_proj(hb, w_in, b_in, dc, 2 * dc))
    ubuf[CONV_HIST:CONV_HIST + tl, :] = u
    span = ush.shape[1]
    for r in range(1, SUBLANES):
        ush[r - 1] = ubuf[r:r + span, :]

    n_blocks = tl // CONV_ROWS
    gate_lo = 2 * dc + dsm
    gate_w = 2 * d // n_blocks
    vblocks, zg_cols = [], []
    for blk in range(n_blocks):
        r0 = blk * CONV_ROWS
        acc = jnp.broadcast_to(b_dw[...], (CONV_ROWS, dc))
        for k in range(CONV_WIDTH):
            q, r = divmod(CONV_HIST - (CONV_WIDTH - 1) + k, SUBLANES)
            rows_k = slice(r0 + q * SUBLANES, r0 + q * SUBLANES + CONV_ROWS)
            tap = ubuf[rows_k, :] if r == 0 else ush[r - 1, rows_k, :]
            acc = acc + w_dw[k:k + 1, :] * tap
        vblocks.append(acc)
        zg_cols.append(_in_proj(hb, w_in, b_in, gate_lo + blk * gate_w, gate_lo + (blk + 1) * gate_w))
    ubuf[0:CONV_HIST, :] = ubuf[tl:tl + CONV_HIST, :]
    conv_out = _conv_tail(jnp.concatenate(vblocks, axis=0), cln_g, cln_b, w_pw, b_pw)

    zs = _in_proj(hb, w_in, b_in, 2 * dc, 2 * dc + dsm)
    zst = _bf(zs.T)
    tri = jnp.where(lax.broadcasted_iota(jnp.int32, (tl, tl), 0)
                    <= lax.broadcasted_iota(jnp.int32, (tl, tl), 1), 1.0, 0.0).astype(_BF16)
    def project(j):
        zj = zst[j * LANES:(j + 1) * LANES, :]
        return _dot(bre[j], zj), _dot(bim[j], zj)

    def scale_in(j, bu):
        r = slice(j * rows, (j + 1) * rows)
        eir, eii = einv_r[r, :], einv_i[r, :]
        return _bf(bu[0] * eir - bu[1] * eii), _bf(bu[0] * eii + bu[1] * eir)

    def prefix(v):
        return _dot(v[0], tri), _dot(v[1], tri)

    def scale_out(j, cum):
        r = slice(j * rows, (j + 1) * rows)
        cr = jnp.broadcast_to(car_r[r, LANES - 1:LANES], (rows, LANES))
        ci = jnp.broadcast_to(car_i[r, LANES - 1:LANES], (rows, LANES))
        lr, li = lamb_r[r, :], lamb_i[r, :]
        cum_r = cum[0] + jnp.concatenate([lr * cr - li * ci] * (tl // LANES), axis=1)
        cum_i = cum[1] + jnp.concatenate([lr * ci + li * cr] * (tl // LANES), axis=1)
        epr, epi = epow_r[r, :], epow_i[r, :]
        h_r = cum_r * epr - cum_i * epi
        h_i = cum_r * epi + cum_i * epr
        car_r[r, :] = h_r[:, tl - LANES:tl]
        car_i[r, :] = h_i[:, tl - LANES:tl]
        return jnp.concatenate([_bf(h_r), _bf(h_i)], axis=0)

    bu, v, cum, hcat, yts = {}, {}, {}, {}, {}
    for t in range(nj + 4):
        if t < nj:
            bu[t] = project(t)
        if 0 <= t - 2 < nj:
            cum[t - 2] = prefix(v.pop(t - 2))
        if 0 <= t - 4 < nj:
            yts[t - 4] = _dot(cc[t - 4], hcat.pop(t - 4))
        if 0 <= t - 1 < nj:
            v[t - 1] = scale_in(t - 1, bu.pop(t - 1))
        if 0 <= t - 3 < nj:
            hcat[t - 3] = scale_out(t - 3, cum.pop(t - 3))
    ssm_out = _ssm_tail(jnp.concatenate([yts[j] for j in range(nj)], axis=0).T, zs, dskip,
                        w_sv, b_sv, w_sg, b_sg)

    x1, h2, route = _mixer_tail(x, jnp.concatenate(zg_cols, axis=1), conv_out, ssm_out, g1, sh2, sc2, alpha,
                                w_out, b_out, ln1_g, ln1_b, wr_hi, wr_lo, b_r)
    x1_ref[0] = x1
    h2_ref[0] = _bf(h2)
    route_ref[0] = route
    cnt_ref[0, 0] = _block_counts(route)

    @pl.when(c == last)
    def _():
        cstate_ref[0] = ubuf[CONV_HIST + tl - (CONV_WIDTH - 1):CONV_HIST + tl, :]
        sre_ref[0] = car_r[...].T[LANES - 1:LANES, :]
        sim_ref[0] = car_i[...].T[LANES - 1:LANES, :]


def _mixer_prompt(x, mod, wts, alpha):
    b, l, d = x.shape
    tl = TOKEN_BLOCK
    nc = l // tl
    dc = wts["w_pw"].shape[0]
    nstate = wts["lamb_r"].shape[0]
    names = ["w_in", "b_in", "w_dw", "b_dw", "cln_g", "cln_b", "w_pw", "b_pw", "bre", "bim", "cc",
             "einv_r", "einv_i", "epow_r", "epow_i", "lamb_r", "lamb_i", "dskip",
             "w_sv", "b_sv", "w_sg", "b_sg", "w_out", "b_out", "ln1_g", "ln1_b",
             "wr_hi", "wr_lo", "b_r"]
    consts = [wts[n] for n in names]
    tok = lambda bi, ci: (bi, ci, 0)
    seq = lambda bi, ci: (bi, 0, 0)
    return pl.pallas_call(
        functools.partial(_mixer_prompt_kernel, alpha=alpha),
        grid=(b, nc),
        in_specs=[pl.BlockSpec((1, tl, d), tok), pl.BlockSpec((1, 1, mod.shape[-1]), seq)]
                 + [_const_spec(a.shape) for a in consts],
        out_specs=[pl.BlockSpec((1, tl, d), tok), pl.BlockSpec((1, tl, d), tok),
                   pl.BlockSpec((1, tl, LANES), tok),
                   pl.BlockSpec((1, 1, 1, LANES), lambda bi, ci: (bi, ci, 0, 0)),
                   pl.BlockSpec((1, CONV_WIDTH - 1, dc), seq),
                   pl.BlockSpec((1, 1, nstate), seq), pl.BlockSpec((1, 1, nstate), seq)],
        out_shape=[jax.ShapeDtypeStruct((b, l, d), _F32), jax.ShapeDtypeStruct((b, l, d), _BF16),
                   jax.ShapeDtypeStruct((b, l, LANES), _F32),
                   jax.ShapeDtypeStruct((b, nc, 1, LANES), _F32),
                   jax.ShapeDtypeStruct((b, CONV_WIDTH - 1, dc), _F32),
                   jax.ShapeDtypeStruct((b, 1, nstate), _F32),
                   jax.ShapeDtypeStruct((b, 1, nstate), _F32)],
        scratch_shapes=[pltpu.VMEM((CONV_HIST + tl, dc), _F32),
                        pltpu.VMEM((SUBLANES - 1, tl + CONV_HIST - SUBLANES, dc), _F32),
                        pltpu.VMEM((nstate, LANES), _F32), pltpu.VMEM((nstate, LANES), _F32)],
        compiler_params=_params(("arbitrary", "arbitrary")),
        name="mixer_prompt",
    )(x, mod, *consts)


def _mixer_sample_kernel(x_ref, mod_ref, hist_ref, h0r_ref, h0i_ref,
                         w_in, b_in, w_dw, b_dw, cln_g, cln_b, w_pw, b_pw,
                         bre, bim, cc, lrow_r, lrow_i, dskip,
                         w_sv, b_sv, w_sg, b_sg, w_out, b_out, ln1_g, ln1_b, wr_hi, wr_lo, b_r,
                         x1_ref, h2_ref, route_ref, cnt_ref, cstate_ref, sre_ref, sim_ref, *, alpha):
    steps, nb, d = x_ref.shape
    dc = w_pw.shape[0]
    dsm = dskip.shape[1]
    nj = bre.shape[0]
    rows = bre.shape[1]
    nhist = CONV_WIDTH - 1
    x = x_ref[...].reshape(steps * nb, d)
    mod = jnp.concatenate([mod_ref[...]] * steps, axis=0)
    sh1, sc1, g1 = mod[:, 0:d], mod[:, d:2 * d], mod[:, 2 * d:3 * d]
    sh2, sc2 = mod[:, 3 * d:4 * d], mod[:, 4 * d:5 * d]
    hb = _bf(x * (1.0 + sc1) + sh1)

    u = _in_proj(hb, w_in, b_in, 0, dc) * _sigmoid(_in_proj(hb, w_in, b_in, dc, 2 * dc))
    us = [u[t * nb:(t + 1) * nb, :] for t in range(steps)]
    vs = []
    for t in range(steps):
        acc = jnp.broadcast_to(b_dw[...], (nb, dc))
        for i in range(t, nhist):
            acc = acc + w_dw[i - t:i - t + 1, :] * hist_ref[i]
        for s in range(t + 1):
            acc = acc + w_dw[nhist + s - t:nhist + s - t + 1, :] * us[s]
        vs.append(acc)
    for i in range(nhist):
        src = i + steps
        cstate_ref[i] = hist_ref[src] if src < nhist else us[src - nhist]
    conv_out = _conv_tail(jnp.concatenate(vs, axis=0), cln_g, cln_b, w_pw, b_pw)

    zs = _in_proj(hb, w_in, b_in, 2 * dc, 2 * dc + dsm)
    zsb = _bf(zs)
    h_r, h_i = h0r_ref[...], h0i_ref[...]
    lr, li = lrow_r[...], lrow_i[...]
    ys = []
    for t in range(steps):
        zt = zsb[t * nb:(t + 1) * nb, :]
        bu_r = jnp.concatenate([_dot_nt(zt[:, j * LANES:(j + 1) * LANES], bre[j]) for j in range(nj)], axis=1)
        bu_i = jnp.concatenate([_dot_nt(zt[:, j * LANES:(j + 1) * LANES], bim[j]) for j in range(nj)], axis=1)
        h_r, h_i = lr * h_r - li * h_i + bu_r, lr * h_i + li * h_r + bu_i
        hrb, hib = _bf(h_r), _bf(h_i)
        ys.append(jnp.concatenate(
            [_dot_nt(jnp.concatenate([hrb[:, j * rows:(j + 1) * rows], hib[:, j * rows:(j + 1) * rows]], axis=1),
                     cc[j]) for j in range(nj)], axis=1))
    sre_ref[...] = h_r
    sim_ref[...] = h_i
    ssm_out = _ssm_tail(jnp.concatenate(ys, axis=0), zs, dskip, w_sv, b_sv, w_sg, b_sg)

    zg = _in_proj(hb, w_in, b_in, 2 * dc + dsm, 2 * dc + dsm + 2 * d)
    x1, h2, route = _mixer_tail(x, zg, conv_out, ssm_out, g1, sh2, sc2, alpha,
                                w_out, b_out, ln1_g, ln1_b, wr_hi, wr_lo, b_r)
    x1_ref[...] = x1
    h2_ref[...] = _bf(h2)
    route_ref[...] = route
    for blk in range(cnt_ref.shape[0]):
        cnt_ref[blk] = _block_counts(route[blk * TOKEN_BLOCK:(blk + 1) * TOKEN_BLOCK, :])


def _mixer_sample(x_tm, mod, hist_tm, h0r, h0i, wts, alpha):
    steps, nb, d = x_tm.shape
    n = steps * nb
    dc = wts["w_pw"].shape[0]
    nstate = h0r.shape[1]
    names = ["w_in", "b_in", "w_dw", "b_dw", "cln_g", "cln_b", "w_pw", "b_pw", "bre", "bim", "cc",
             "lrow_r", "lrow_i", "dskip", "w_sv", "b_sv", "w_sg", "b_sg", "w_out", "b_out",
             "ln1_g", "ln1_b", "wr_hi", "wr_lo", "b_r"]
    args = [x_tm, mod, hist_tm, h0r, h0i] + [wts[k] for k in names]
    return pl.pallas_call(
        functools.partial(_mixer_sample_kernel, alpha=alpha),
        grid=(1,),
        in_specs=[_const_spec(a.shape) for a in args],
        out_specs=[_const_spec((n, d)), _const_spec((n, d)), _const_spec((n, LANES)),
                   _const_spec((n // TOKEN_BLOCK, 1, LANES)),
                   _const_spec((CONV_WIDTH - 1, nb, dc)),
                   _const_spec((nb, nstate)), _const_spec((nb, nstate))],
        out_shape=[jax.ShapeDtypeStruct((n, d), _F32), jax.ShapeDtypeStruct((n, d), _BF16),
                   jax.ShapeDtypeStruct((n, LANES), _F32),
                   jax.ShapeDtypeStruct((n // TOKEN_BLOCK, 1, LANES), _F32),
                   jax.ShapeDtypeStruct((CONV_WIDTH - 1, nb, dc), _F32),
                   jax.ShapeDtypeStruct((nb, nstate), _F32), jax.ShapeDtypeStruct((nb, nstate), _F32)],
        compiler_params=_params(("arbitrary",)),
        name="mixer_sample",
    )(*args)


def _round_up(x, m):
    return (x + m - 1) // m * m


def _plan_sizes(n_tokens):
    nb = n_tokens // TOKEN_BLOCK
    r_max = _round_up(TOP_K * TOKEN_BLOCK + 2 * N_EXPERTS * (SEG_PAD - 1), 2 * LANES)
    rows_max = n_tokens * TOP_K + N_EXPERTS * (EXPERT_TILE - 1)
    nt_max = -(-rows_max // EXPERT_TILE)
    return nb, r_max, nt_max


def _routing_plan(cnt, nt_max):
    nb, ne = cnt.shape
    before = jnp.cumsum(cnt, axis=0) - cnt
    tot = jnp.sum(cnt, axis=0)
    cin = before % SEG_PAD
    last = (jnp.arange(nb, dtype=jnp.int32) == nb - 1)[:, None]
    active = (cnt > 0) | (last & (cin > 0))
    seg = jnp.where(active, _round_up(cin + cnt, SEG_PAD), 0)
    loc = jnp.cumsum(seg, axis=1) - seg
    totpad = _round_up(tot, EXPERT_TILE)
    eend = jnp.cumsum(totpad)
    estart = eend - totpad
    goff = estart[None, :] + before - cin
    nfull = jnp.where(last, seg, jnp.where(active, (cin + cnt) // SEG_PAD * SEG_PAD, 0))
    cout = jnp.where(last | ~active, 0, (cin + cnt) % SEG_PAD)
    n_tiles = (eend[-1] // EXPERT_TILE).astype(jnp.int32)
    tiles = jnp.minimum(jnp.arange(nt_max, dtype=jnp.int32), n_tiles - 1)
    te = jnp.sum(((eend // EXPERT_TILE)[None, :] <= tiles[:, None]).astype(jnp.int32), axis=1)
    te = jnp.minimum(te, ne - 1)
    tot16 = _round_up(tot, SEG_PAD)
    return dict(seg=seg, loc=loc, goff=goff, nfull=nfull, cin=jnp.where(active, cin, 0), cout=cout,
                used=jnp.sum(seg, axis=1).astype(jnp.int32), sent=jnp.sum(nfull, axis=1).astype(jnp.int32),
                n_tiles=n_tiles.reshape(1), te=te,
                npad=(totpad - tot16).astype(jnp.int32), pad_base=(estart + tot16).astype(jnp.int32))


def _dest_columns(route, loc_row):
    n = route.shape[0]
    lane_i = lax.broadcasted_iota(jnp.int32, (n, LANES), 1)
    lane = lane_i.astype(_F32)
    member = _bf(_members(route))
    strict = jnp.where(lax.broadcasted_iota(jnp.int32, (n, n), 1)
                       < lax.broadcasted_iota(jnp.int32, (n, n), 0), 1.0, 0.0).astype(_BF16)
    dest = _dot(strict, member) + loc_row
    d4 = jnp.zeros((n, LANES), _F32)
    for k in range(TOP_K):
        idx = route[:, ROUTE_IDX + k:ROUTE_IDX + k + 1]
        dk = jnp.sum(jnp.where(lane == idx, dest, 0.0), axis=-1, keepdims=True)
        d4 = d4 + jnp.where(lane_i == k, dk, 0.0)
    return d4


def _typical_rows(r_max):
    return min(r_max, _round_up(TOP_K * TOKEN_BLOCK + N_EXPERTS * SEG_PAD, 2 * LANES))


def _dispatch_kernel(nfull, sloc, goff, cin, cout, used, sent, npad, pad_base, hp_ref, hs_ref, rp_ref, rs_ref, loc_ref,
                     xs_hbm, d4_ref, buf, stage, zbuf, sem, zsem, *, nbp):
    i = pl.program_id(0)
    nblk = pl.num_programs(0)
    slot = lax.rem(i, 2)
    tb = hp_ref.shape[0]
    r_max = buf.shape[1]
    is_p = i < nbp
    h = jnp.where(is_p, hp_ref[...], hs_ref[...])
    route = jnp.where(is_p, rp_ref[...], rs_ref[...])
    d4 = _dest_columns(route, loc_ref[0])
    d4_ref[...] = d4
    d4t = d4.T

    def compact(rows):
        row = lax.broadcasted_iota(jnp.int32, (rows, tb), 0).astype(_F32)
        p = jnp.zeros((rows, tb), _F32)
        for k in range(TOP_K):
            p = jnp.where(row == d4t[k:k + 1, :], 1.0, p)
        buf[slot, 0:rows, :] = _bf(_dot(_bf(p), h))

    r_typ = _typical_rows(r_max)

    @pl.when(i == 0)
    def _():
        buf[...] = jnp.zeros(buf.shape, buf.dtype)
        stage[...] = jnp.zeros(stage.shape, stage.dtype)

    @pl.when(used[i] <= r_typ)
    def _():
        compact(r_typ)

    @pl.when(used[i] > r_typ)
    def _():
        compact(r_max)

    def seg_copy(blk, s, e):
        n = pl.multiple_of(nfull[blk * N_EXPERTS + e], SEG_PAD)
        src = pl.multiple_of(sloc[blk * N_EXPERTS + e], SEG_PAD)
        dst = pl.multiple_of(goff[blk * N_EXPERTS + e], SEG_PAD)
        return pltpu.make_async_copy(buf.at[s, pl.ds(src, n), :], xs_hbm.at[pl.ds(dst, n), :], sem.at[s])

    def pad_copy(e):
        n = pl.multiple_of(npad[e], SEG_PAD)
        dst = pl.multiple_of(pad_base[e], SEG_PAD)
        return pltpu.make_async_copy(zbuf.at[pl.ds(0, n), :], xs_hbm.at[pl.ds(dst, n), :], zsem.at[0])

    def for_segments(blk, fn):
        for e in range(N_EXPERTS):
            @pl.when(nfull[blk * N_EXPERTS + e] > 0)
            def _(e=e):
                fn(e)

    for e in range(N_EXPERTS):
        first = pl.ds(pl.multiple_of(sloc[i * N_EXPERTS + e], SEG_PAD), SEG_PAD)
        carried = jnp.where(cin[i * N_EXPERTS + e] > 0, stage[e], jnp.zeros_like(stage[e]))
        buf[slot, first, :] = buf[slot, first, :] + carried

    for_segments(i, lambda e: seg_copy(i, slot, e).start())

    for e in range(N_EXPERTS):
        rest = pl.multiple_of(sloc[i * N_EXPERTS + e] + nfull[i * N_EXPERTS + e], SEG_PAD)
        stage[e] = jnp.where(cout[i * N_EXPERTS + e] > 0, buf[slot, pl.ds(rest, SEG_PAD), :], stage[e])

    def wait_block(blk, s):
        @pl.when(sent[blk] > 0)
        def _():
            n = pl.multiple_of(sent[blk], SEG_PAD)
            pltpu.make_async_copy(buf.at[s, pl.ds(0, n), :], xs_hbm.at[pl.ds(0, n), :], sem.at[s]).wait()

    @pl.when(i > 0)
    def _():
        wait_block(i - 1, 1 - slot)

    @pl.when(i == nblk - 1)
    def _():
        zbuf[...] = jnp.zeros(zbuf.shape, zbuf.dtype)
        for e in range(N_EXPERTS):
            @pl.when(npad[e] > 0)
            def _(e=e):
                pad_copy(e).start()
        for e in range(N_EXPERTS):
            @pl.when(npad[e] > 0)
            def _(e=e):
                pad_copy(e).wait()
        wait_block(i, slot)


def _dispatch(plan, h2p, h2s, rp, rs, r_max, nt_max):
    tb = TOKEN_BLOCK
    d = h2p.shape[1]
    nbp, nbs = h2p.shape[0] // tb, h2s.shape[0] // tb
    nb = nbp + nbs
    first_row = (plan["loc"] + plan["cin"]).astype(_F32)
    loc = jnp.pad(first_row, ((0, 0), (0, LANES - N_EXPERTS))).reshape(nb, 1, LANES)
    pidx = lambda i, *_: (jnp.minimum(i, nbp - 1), 0)
    sidx = lambda i, *_: (jnp.maximum(i - nbp, 0), 0)
    grid_spec = pltpu.PrefetchScalarGridSpec(
        num_scalar_prefetch=9,
        grid=(nb,),
        in_specs=[pl.BlockSpec((tb, d), pidx), pl.BlockSpec((tb, d), sidx),
                  pl.BlockSpec((tb, LANES), pidx), pl.BlockSpec((tb, LANES), sidx),
                  pl.BlockSpec((1, 1, LANES), lambda i, *_: (i, 0, 0))],
        out_specs=[pl.BlockSpec(memory_space=pl.ANY), pl.BlockSpec((tb, LANES), lambda i, *_: (i, 0))],
        scratch_shapes=[pltpu.VMEM((2, r_max, d), _BF16), pltpu.VMEM((N_EXPERTS, SEG_PAD, d), _BF16),
                        pltpu.VMEM((EXPERT_TILE, d), _BF16),
                        pltpu.SemaphoreType.DMA((2,)), pltpu.SemaphoreType.DMA((1,))])
    flat = lambda a: a.astype(jnp.int32).reshape(-1)
    return pl.pallas_call(
        functools.partial(_dispatch_kernel, nbp=nbp),
        grid_spec=grid_spec,
        out_shape=[jax.ShapeDtypeStruct((nt_max * EXPERT_TILE, d), _BF16),
                   jax.ShapeDtypeStruct((nb * tb, LANES), _F32)],
        compiler_params=_params(("arbitrary",)),
        name="dispatch",
    )(flat(plan["nfull"]), flat(plan["loc"]), flat(plan["goff"]), flat(plan["cin"]), flat(plan["cout"]),
      plan["used"], plan["sent"], plan["npad"], plan["pad_base"], h2p, h2s, rp, rs, loc)


def _expert_kernel(te, n_tiles, x_ref, w1_ref, b1_ref, w2_ref, b2_ref, y_ref, w1b, w2b):
    i = pl.program_id(0)
    dff = w2_ref.shape[1]

    @pl.when(i < n_tiles[0])
    def _():
        prev = te[jnp.maximum(i - 1, 0)]

        @pl.when((i == 0) | (te[i] != prev))
        def _():
            w1b[...] = _bf(w1_ref[0])
            w2b[...] = _bf(w2_ref[0])

        gu = _dot(x_ref[...], w1b[...]) + b1_ref[0]
        g = jnp.minimum(gu[:, :dff], SWIGLU_LIMIT)
        up = jnp.clip(gu[:, dff:], -SWIGLU_LIMIT, SWIGLU_LIMIT)
        act = g * _sigmoid(SWIGLU_ALPHA * g) * (up + 1.0)
        y_ref[...] = _bf(_dot(_bf(act), w2b[...]) + b2_ref[0])


def _experts(plan, xs, w1, b1, w2, b2, nt_max):
    ne, d, dff2 = w1.shape
    dff = w2.shape[1]
    tm = EXPERT_TILE
    tile = lambda i, te, nt: (jnp.minimum(i, nt[0] - 1), 0)
    wsel = lambda i, te, nt: (te[i], 0, 0)
    grid_spec = pltpu.PrefetchScalarGridSpec(
        num_scalar_prefetch=2,
        grid=(nt_max,),
        in_specs=[pl.BlockSpec((tm, d), tile),
                  pl.BlockSpec((1, d, dff2), wsel), pl.BlockSpec((1, 1, dff2), wsel),
                  pl.BlockSpec((1, dff, d), wsel), pl.BlockSpec((1, 1, d), wsel)],
        out_specs=pl.BlockSpec((tm, d), tile),
        scratch_shapes=[pltpu.VMEM((d, dff2), _BF16), pltpu.VMEM((dff, d), _BF16)])
    return pl.pallas_call(
        _expert_kernel,
        grid_spec=grid_spec,
        out_shape=jax.ShapeDtypeStruct(xs.shape, _BF16),
        compiler_params=_params(("arbitrary",)),
        name="experts",
    )(plan["te"], plan["n_tiles"], xs, w1, b1.reshape(ne, 1, dff2), w2, b2.reshape(ne, 1, d))


def _combine_kernel(seg, sloc, goff, used, rp_ref, rs_ref, d4_ref, x1p_ref, x1s_ref, g2p_ref, g2s_ref,
                    ln2_g, ln2_b, ys_hbm, yp_ref, ysm_ref, buf, fbuf, sem, *, nbp, alpha):
    i = pl.program_id(0)
    nblk = pl.num_programs(0)
    slot = lax.rem(i, 2)
    tb = rp_ref.shape[0]
    r_max = buf.shape[1]

    def seg_copy(blk, s, e):
        n = pl.multiple_of(seg[blk * N_EXPERTS + e], SEG_PAD)
        src = pl.multiple_of(goff[blk * N_EXPERTS + e], SEG_PAD)
        dst = pl.multiple_of(sloc[blk * N_EXPERTS + e], SEG_PAD)
        return pltpu.make_async_copy(ys_hbm.at[pl.ds(src, n), :], buf.at[s, pl.ds(dst, n), :], sem.at[s])

    def for_segments(blk, fn):
        for e in range(N_EXPERTS):
            @pl.when(seg[blk * N_EXPERTS + e] > 0)
            def _(e=e):
                fn(e)

    @pl.when(i == 0)
    def _():
        buf[...] = jnp.zeros(buf.shape, buf.dtype)
        for_segments(0, lambda e: seg_copy(0, 0, e).start())

    @pl.when(i + 1 < nblk)
    def _():
        for_segments(i + 1, lambda e: seg_copy(i + 1, 1 - slot, e).start())

    @pl.when(used[i] > 0)
    def _():
        n = pl.multiple_of(used[i], SEG_PAD)
        pltpu.make_async_copy(ys_hbm.at[pl.ds(0, n), :], buf.at[slot, pl.ds(0, n), :], sem.at[slot]).wait()

    is_p = i < nbp
    route = jnp.where(is_p, rp_ref[...], rs_ref[...])
    d4 = d4_ref[...]

    def gather(rows):
        col = lax.broadcasted_iota(jnp.int32, (tb, rows), 1).astype(_F32)
        pg = jnp.zeros((tb, rows), _F32)
        for k in range(TOP_K):
            gate = route[:, ROUTE_GATE + k:ROUTE_GATE + k + 1]
            pg = jnp.where(col == d4[:, k:k + 1], gate, pg)
        fbuf[...] = _dot(_bf(pg), buf[slot, 0:rows, :])

    r_typ = _typical_rows(r_max)

    @pl.when(used[i] <= r_typ)
    def _():
        gather(r_typ)

    @pl.when(used[i] > r_typ)
    def _():
        gather(r_max)

    f = fbuf[...]
    x1 = jnp.where(is_p, x1p_ref[...], x1s_ref[...])
    g2s = jnp.concatenate([g2s_ref[...]] * (tb // g2s_ref.shape[0]), axis=0)
    g2 = jnp.where(is_p, jnp.broadcast_to(g2p_ref[0], g2s.shape), g2s)
    y = _layer_norm(alpha * x1 + g2 * f, ln2_g[...], ln2_b[...])

    @pl.when(is_p)
    def _():
        yp_ref[...] = y

    @pl.when(jnp.logical_not(is_p))
    def _():
        ysm_ref[...] = y


def _combine(plan, ys, rp, rs, d4, x1p, x1s, modp, mods, ln2_g, ln2_b, r_max, alpha, blocks_per_seq):
    tb = TOKEN_BLOCK
    d = x1p.shape[1]
    nbp, nbs = x1p.shape[0] // tb, x1s.shape[0] // tb
    nb = nbp + nbs
    pidx = lambda i, *_: (jnp.minimum(i, nbp - 1), 0)
    sidx = lambda i, *_: (jnp.maximum(i - nbp, 0), 0)
    g2_lane_block = 5
    flat = lambda a: a.astype(jnp.int32).reshape(-1)
    grid_spec = pltpu.PrefetchScalarGridSpec(
        num_scalar_prefetch=4,
        grid=(nb,),
        in_specs=[pl.BlockSpec((tb, LANES), pidx), pl.BlockSpec((tb, LANES), sidx),
                  pl.BlockSpec((tb, LANES), lambda i, *_: (i, 0)),
                  pl.BlockSpec((tb, d), pidx), pl.BlockSpec((tb, d), sidx),
                  pl.BlockSpec((1, 1, d), lambda i, *_: (jnp.minimum(i, nbp - 1) // blocks_per_seq, 0,
                                                         g2_lane_block)),
                  pl.BlockSpec((mods.shape[0], d), lambda i, *_: (0, g2_lane_block)),
                  pl.BlockSpec((1, d), lambda i, *_: (0, 0)), pl.BlockSpec((1, d), lambda i, *_: (0, 0)),
                  pl.BlockSpec(memory_space=pl.ANY)],
        out_specs=[pl.BlockSpec((tb, d), pidx), pl.BlockSpec((tb, d), sidx)],
        scratch_shapes=[pltpu.VMEM((2, r_max, d), _BF16), pltpu.VMEM((tb, d), _F32),
                        pltpu.SemaphoreType.DMA((2,))])
    return pl.pallas_call(
        functools.partial(_combine_kernel, nbp=nbp, alpha=alpha),
        grid_spec=grid_spec,
        out_shape=[jax.ShapeDtypeStruct(x1p.shape, _F32), jax.ShapeDtypeStruct(x1s.shape, _F32)],
        compiler_params=_params(("arbitrary",)),
        name="combine",
    )(flat(plan["seg"]), flat(plan["loc"]), flat(plan["goff"]), plan["used"],
      rp, rs, d4, x1p, x1s, modp, mods, ln2_g, ln2_b, ys)


def _complex_powers(zr, zi, n):
    low_bits = 4
    low = 1 << low_bits
    if n <= low or n % low:
        return _bit_powers(zr, zi, n)
    lr, li = _bit_powers(zr, zi, low)
    sr, si = zr, zi
    for _ in range(low_bits):
        sr, si = sr * sr - si * si, 2.0 * sr * si
    hr, hi = _bit_powers(sr, si, n // low)
    pr = hr[:, :, None] * lr[:, None, :] - hi[:, :, None] * li[:, None, :]
    pi = hr[:, :, None] * li[:, None, :] + hi[:, :, None] * lr[:, None, :]
    return pr.reshape(-1, n), pi.reshape(-1, n)


def _bit_powers(zr, zi, n):
    k = jnp.arange(n, dtype=jnp.int32)[None, :]
    pr = jnp.ones((zr.shape[0], n), _F32)
    pi = jnp.zeros((zr.shape[0], n), _F32)
    sr, si = zr[:, None], zi[:, None]
    bit = 1
    while bit < n:
        on = (k & bit) != 0
        mr, mi = jnp.where(on, sr, 1.0), jnp.where(on, si, 0.0)
        pr, pi = pr * mr - pi * mi, pr * mi + pi * mr
        sr, si = sr * sr - si * si, 2.0 * sr * si
        bit *= 2
    return pr, pi


def _block_diag(m):
    g, a, b = m.shape
    gb = GROUPS_PER_BLOCK
    m = m.reshape(g // gb, gb, a, 1, b)
    eye = jnp.eye(gb, dtype=m.dtype).reshape(1, gb, 1, gb, 1)
    return (m * eye).reshape(g // gb, gb * a, gb * b)


def _layer_weights(p, q):
    d = p["w_in"].shape[0]
    row = lambda v: v.reshape(1, -1).astype(_F32)
    lam_re, lam_im = p["lam_re"].astype(_F32), p["lam_im"].astype(_F32)
    dt = jnp.exp(p["log_dt"].astype(_F32))[:, None]
    mag = jnp.exp(lam_re * dt)
    lbr, lbi = mag * jnp.cos(lam_im * dt), mag * jnp.sin(lam_im * dt)
    den = lam_re * lam_re + lam_im * lam_im
    nr, ni = lbr - 1.0, lbi
    fr, fi = (nr * lam_re + ni * lam_im) / den, (ni * lam_re - nr * lam_im) / den
    b_re, b_im = p["b_re"].astype(_F32), p["b_im"].astype(_F32)
    bbr = fr[..., None] * b_re - fi[..., None] * b_im
    bbi = fr[..., None] * b_im + fi[..., None] * b_re
    nstate = lam_re.size
    mod2 = lbr * lbr + lbi * lbi
    flat = lambda v: v.reshape(nstate)
    epr, epi = _complex_powers(flat(lbr), flat(lbi), q)
    eir, eii = _complex_powers(flat(lbr / mod2), flat(-lbi / mod2), q)
    table = lambda t: t
    bcast = lambda v: jnp.broadcast_to(v.reshape(nstate, 1), (nstate, LANES))
    c_re, c_im = p["c_re"].astype(_F32), p["c_im"].astype(_F32)
    w_r = p["w_router"].astype(_F32)
    w_r = jnp.pad(w_r, ((0, 0), (0, LANES - w_r.shape[1])))
    wr_hi = _bf(w_r)
    return dict(
        w_in=_bf(p["w_in"]), b_in=row(p["b_in"]), w_dw=p["w_dw"].astype(_F32), b_dw=row(p["b_dw"]),
        cln_g=row(p["conv_ln_g"]), cln_b=row(p["conv_ln_b"]), w_pw=_bf(p["w_pw"]), b_pw=row(p["b_pw"]),
        bre=_bf(_block_diag(bbr)), bim=_bf(_block_diag(bbi)),
        cc=_bf(jnp.concatenate([_block_diag(c_re), _block_diag(-c_im)], axis=2)),
        einv_r=table(eir), einv_i=table(eii), epow_r=table(epr), epow_i=table(epi),
        lamb_r=bcast(lbr), lamb_i=bcast(lbi), lrow_r=lbr.reshape(1, nstate), lrow_i=lbi.reshape(1, nstate),
        dskip=row(p["d_skip"]),
        w_sv=_bf(p["w_sv"]), b_sv=row(p["b_sv"]), w_sg=_bf(p["w_sg"]), b_sg=row(p["b_sg"]),
        w_out=_bf(p["w_out"]), b_out=row(p["b_out"]), ln1_g=row(p["ln1_g"]), ln1_b=row(p["ln1_b"]),
        wr_hi=wr_hi, wr_lo=_bf(w_r - wr_hi.astype(_F32)),
        b_r=p["b_router"].astype(_F32).reshape(-1, 1),
        ln2_g=row(p["ln2_g"]), ln2_b=row(p["ln2_b"]))


def _layer(xp, xs_tm, c_all, hist_tm, h0r, h0i, p, alpha):
    b, l, d = xp.shape
    steps, nbs, _ = xs_tm.shape
    tb = TOKEN_BLOCK
    assert l % tb == 0 and l >= CONV_WIDTH - 1 and (steps * nbs) % tb == 0 and tb % nbs == 0
    wts = _layer_weights(p, tb)
    mod = _ada(c_all, p["w_ada"].astype(_F32), p["b_ada"].astype(_F32))
    modp, mods = mod[:b].reshape(b, 1, -1), mod[b:]

    x1p, h2p, rp, cntp, conv_p, sre_p, sim_p = _mixer_prompt(xp, modp, wts, alpha)
    x1s, h2s, rs, cnts, conv_s, sre_s, sim_s = _mixer_sample(xs_tm, mods, hist_tm, h0r, h0i, wts, alpha)

    n_tok = b * l + steps * nbs
    nb, r_max, nt_max = _plan_sizes(n_tok)
    cnt = jnp.concatenate([cntp.reshape(-1, LANES), cnts.reshape(-1, LANES)], axis=0)[:, :N_EXPERTS]
    plan = _routing_plan(cnt.astype(jnp.int32), nt_max)

    flat = lambda a: a.reshape(b * l, a.shape[-1])
    xs_sorted, d4 = _dispatch(plan, flat(h2p), h2s, flat(rp), rs, r_max, nt_max)
    ys_sorted = _experts(plan, xs_sorted, p["w1"], p["b1"], p["w2"], p["b2"], nt_max)
    yp, ysm = _combine(plan, ys_sorted, flat(rp), rs, d4, flat(x1p), x1s, modp, mods,
                       wts["ln2_g"], wts["ln2_b"], r_max, alpha, l // tb)
    return (yp.reshape(b, l, d), ysm.reshape(steps, nbs, d), conv_p, sre_p[:, 0], sim_p[:, 0],
            conv_s, sre_s, sim_s)


def kernel(x_prompt, x_sample, state_conv, state_ssm_re, state_ssm_im, c_prompt, c_sample, w_ada, b_ada, w_in, b_in, w_dw, b_dw, conv_ln_g, conv_ln_b, w_pw, b_pw, lam_re, lam_im, log_dt, b_re, b_im, c_re, c_im, d_skip, w_sv, b_sv, w_sg, b_sg, w_out, b_out, ln1_g, ln1_b, w_router, b_router, w1, b1, w2, b2, ln2_g, ln2_b):
    stacked = dict(w_ada=w_ada, b_ada=b_ada, w_in=w_in, b_in=b_in, w_dw=w_dw, b_dw=b_dw,
                   conv_ln_g=conv_ln_g, conv_ln_b=conv_ln_b, w_pw=w_pw, b_pw=b_pw, lam_re=lam_re,
                   lam_im=lam_im, log_dt=log_dt, b_re=b_re, b_im=b_im, c_re=c_re, c_im=c_im,
                   d_skip=d_skip, w_sv=w_sv, b_sv=b_sv, w_sg=w_sg, b_sg=b_sg, w_out=w_out, b_out=b_out,
                   ln1_g=ln1_g, ln1_b=ln1_b, w_router=w_router, b_router=b_router, w1=w1, b1=b1,
                   w2=w2, b2=b2, ln2_g=ln2_g, ln2_b=ln2_b)
    depth = w_ada.shape[0]
    alpha = (2 * depth) ** 0.25
    b = x_prompt.shape[0]
    nbs = x_sample.shape[0]
    g, s = state_ssm_re.shape[2], state_ssm_re.shape[3]
    xp = x_prompt
    xs_tm = jnp.transpose(x_sample, (1, 0, 2))
    c_all = jnp.concatenate([c_prompt, c_sample], axis=0)
    conv_ps, re_ps, im_ps, conv_ss, re_ss, im_ss = [], [], [], [], [], []
    for layer in range(depth):
        p = {k: v[layer] for k, v in stacked.items()}
        hist_tm = jnp.transpose(state_conv[layer], (1, 0, 2))
        h0r = state_ssm_re[layer].reshape(nbs, g * s)
        h0i = state_ssm_im[layer].reshape(nbs, g * s)
        xp, xs_tm, conv_p, sre_p, sim_p, conv_s, sre_s, sim_s = _layer(
            xp, xs_tm, c_all, hist_tm, h0r, h0i, p, alpha)
        conv_ps.append(conv_p.astype(state_conv.dtype))
        re_ps.append(sre_p.reshape(b, g, s))
        im_ps.append(sim_p.reshape(b, g, s))
        conv_ss.append(jnp.transpose(conv_s, (1, 0, 2)).astype(state_conv.dtype))
        re_ss.append(sre_s.reshape(nbs, g, s))
        im_ss.append(sim_s.reshape(nbs, g, s))
    return (xp, jnp.transpose(xs_tm, (1, 0, 2)), jnp.stack(conv_ps), jnp.stack(re_ps), jnp.stack(im_ps),
            jnp.stack(conv_ss), jnp.stack(re_ss), jnp.stack(im_ss))
```

```python
import functools

import jax
import jax.numpy as jnp
from jax import lax
from jax.experimental import pallas as pl
from jax.experimental.pallas import tpu as pltpu

CONV_WIDTH = 31
SSM_GROUP = 16
SSM_STATE = 64
N_EXPERTS = 32
TOP_K = 4
SWIGLU_LIMIT = 7.0
SWIGLU_ALPHA = 1.702
LN_EPS = 1e-5

LANES = 128
SUBLANES = 8
TOKEN_BLOCK = 256
EXPERT_TILE = 512
SEG_PAD = 16
GROUPS_PER_BLOCK = LANES // SSM_GROUP
CONV_HIST = 32
CONV_ROWS = 32
ROUTE_IDX = N_EXPERTS
ROUTE_GATE = N_EXPERTS + TOP_K
VMEM_LIMIT = 56 * 1024 * 1024

_F32 = jnp.float32
_BF16 = jnp.bfloat16


def _bf(x):
    return x.astype(_BF16)


def _dot(a, b):
    return jnp.dot(a, b, preferred_element_type=_F32)


def _dot_nt(a, b):
    return lax.dot_general(a, b, (((1,), (1,)), ((), ())), preferred_element_type=_F32)


def _split(x):
    hi = _bf(x)
    lo = _bf(x - hi.astype(_F32))
    return hi, lo


def _sigmoid(x):
    return 1.0 / (1.0 + jnp.exp(-x))


def _gelu_tanh(x):
    return 0.5 * x * (1.0 + jnp.tanh(0.7978845608028654 * (x + 0.044715 * (x * x * x))))


def _layer_norm(x, g, b):
    mu = jnp.mean(x, axis=-1, keepdims=True)
    xc = x - mu
    var = jnp.mean(xc * xc, axis=-1, keepdims=True)
    return xc * lax.rsqrt(var + LN_EPS) * g + b


def _const_spec(shape):
    nd = len(shape)
    return pl.BlockSpec(shape, lambda *_: (0,) * nd)


def _params(sem):
    return pltpu.CompilerParams(dimension_semantics=sem, vmem_limit_bytes=VMEM_LIMIT)


def _ada_kernel(c_ref, w_ref, b_ref, o_ref):
    c = c_ref[...]
    s_hi, s_lo = _split(c * _sigmoid(c))
    w_hi, w_lo = _split(w_ref[...])
    o_ref[...] = _dot(s_hi, w_hi) + _dot(s_lo, w_hi) + _dot(s_hi, w_lo) + b_ref[...]


def _ada(c, w, b):
    n, d = c.shape
    cols = w.shape[1]
    tn = d
    return pl.pallas_call(
        _ada_kernel,
        grid=(cols // tn,),
        in_specs=[pl.BlockSpec((n, d), lambda j: (0, 0)),
                  pl.BlockSpec((d, tn), lambda j: (0, j)),
                  pl.BlockSpec((1, tn), lambda j: (0, j))],
        out_specs=pl.BlockSpec((n, tn), lambda j: (0, j)),
        out_shape=jax.ShapeDtypeStruct((n, cols), _F32),
        compiler_params=_params(("arbitrary",)),
        name="ada",
    )(c, w, b.reshape(1, cols))


def _in_proj(hb, w_in, b_in, lo, hi):
    return _dot(hb, w_in[:, lo:hi]) + b_in[:, lo:hi]


def _conv_tail(v, cln_g, cln_b, w_pw, b_pw):
    v = _layer_norm(v, cln_g[...], cln_b[...])
    v = v * _sigmoid(v)
    return _dot(_bf(v), w_pw[...]) + b_pw[...]


def _ssm_tail(y, zs, dskip, w_sv, b_sv, w_sg, b_sg):
    yg = _bf(_gelu_tanh(y + dskip[...] * zs))
    return (_dot(yg, w_sv[...]) + b_sv[...]) * _sigmoid(_dot(yg, w_sg[...]) + b_sg[...])


def _route(h2, wr_hi, wr_lo, b_r):
    n = h2.shape[0]
    ne = b_r.shape[0]
    h_hi, h_lo = _split(h2)
    logits = (_dot(h_hi, wr_hi[...]) + _dot(h_lo, wr_hi[...]) + _dot(h_hi, wr_lo[...])).T[:ne, :] + b_r[...]
    row = lax.broadcasted_iota(jnp.int32, (ne, n), 0).astype(_F32)
    neg = jnp.float32(-jnp.inf)
    cur = logits
    vals, idxs = [], []
    for _ in range(TOP_K):
        m = jnp.max(cur, axis=0, keepdims=True)
        idx = jnp.min(jnp.where(cur == m, row, float(ne)), axis=0, keepdims=True)
        vals.append(m)
        idxs.append(idx)
        cur = jnp.where(row == idx, neg, cur)
    es = [jnp.exp(v - vals[0]) for v in vals]
    tot = es[0]
    for e in es[1:]:
        tot = tot + e
    inv = 1.0 / tot
    selected = jnp.zeros((ne, n), _F32)
    row8 = lax.broadcasted_iota(jnp.int32, (2 * TOP_K, n), 0)
    extra = jnp.zeros((2 * TOP_K, n), _F32)
    for k in range(TOP_K):
        gate = es[k] * inv
        selected = selected + jnp.where(row == idxs[k], 1.0, 0.0)
        extra = extra + jnp.where(row8 == k, idxs[k], 0.0) + jnp.where(row8 == TOP_K + k, gate, 0.0)
    rest = jnp.zeros((LANES - ne - 2 * TOP_K, n), _F32)
    return jnp.concatenate([selected, extra, rest], axis=0).T


def _members(route):
    lane = lax.broadcasted_iota(jnp.int32, route.shape, 1)
    return jnp.where(lane < N_EXPERTS, route, 0.0)


def _block_counts(route):
    return jnp.sum(_members(route), axis=0, keepdims=True)


def _mixer_tail(x, zg, conv_out, ssm_out, g1, sh2, sc2, alpha, w_out, b_out,
                ln1_g, ln1_b, wr_hi, wr_lo, b_r):
    d = x.shape[1]
    merged = _sigmoid(zg[:, :d]) * conv_out + _sigmoid(zg[:, d:]) * ssm_out
    m = _dot(_bf(merged), w_out[...]) + b_out[...]
    x1 = _layer_norm(alpha * x + g1 * m, ln1_g[...], ln1_b[...])
    h2 = x1 * (1.0 + sc2) + sh2
    return x1, h2, _route(h2, wr_hi, wr_lo, b_r)


def _mixer_prompt_kernel(x_ref, mod_ref, w_in, b_in, w_dw, b_dw, cln_g, cln_b, w_pw, b_pw,
                         bre, bim, cc, einv_r, einv_i, epow_r, epow_i, lamb_r, lamb_i, dskip,
                         w_sv, b_sv, w_sg, b_sg, w_out, b_out, ln1_g, ln1_b, wr_hi, wr_lo, b_r,
                         x1_ref, h2_ref, route_ref, cnt_ref, cstate_ref, sre_ref, sim_ref,
                         ubuf, ush, car_r, car_i, *, alpha):
    c = pl.program_id(1)
    last = pl.num_programs(1) - 1
    tl, d = x_ref.shape[1], x_ref.shape[2]
    dc = w_pw.shape[0]
    dsm = dskip.shape[1]
    nj = bre.shape[0]
    rows = bre.shape[1]

    @pl.when(c == 0)
    def _():
        ubuf[0:CONV_HIST, :] = jnp.zeros((CONV_HIST, dc), _F32)
        car_r[...] = jnp.zeros(car_r.shape, _F32)
        car_i[...] = jnp.zeros(car_i.shape, _F32)

    x = x_ref[0]
    mod = mod_ref[0]
    sh1, sc1, g1 = mod[:, 0:d], mod[:, d:2 * d], mod[:, 2 * d:3 * d]
    sh2, sc2 = mod[:, 3 * d:4 * d], mod[:, 4 * d:5 * d]
    hb = _bf(x * (1.0 + sc1) + sh1)

    u = _in_proj(hb, w_in, b_in, 0, dc) * _sigmoid(_in_proj(hb, w_in, b_in, dc, 2 * dc))
    ubuf[CONV_HIST:CONV_HIST + tl, :] = u
    span = ush.shape[1]
    for r in range(1, SUBLANES):
        ush[r - 1] = ubuf[r:r + span, :]

    vblocks = []
    for r0 in range(0, tl, CONV_ROWS):
        acc = jnp.broadcast_to(b_dw[...], (CONV_ROWS, dc))
        for k in range(CONV_WIDTH):
            q, r = divmod(CONV_HIST - (CONV_WIDTH - 1) + k, SUBLANES)
            rows_k = slice(r0 + q * SUBLANES, r0 + q * SUBLANES + CONV_ROWS)
            tap = ubuf[rows_k, :] if r == 0 else ush[r - 1, rows_k, :]
            acc = acc + w_dw[k:k + 1, :] * tap
        vblocks.append(acc)
    ubuf[0:CONV_HIST, :] = ubuf[tl:tl + CONV_HIST, :]
    zg = _in_proj(hb, w_in, b_in, 2 * dc + dsm, 2 * dc + dsm + 2 * d)
    conv_out = _conv_tail(jnp.concatenate(vblocks, axis=0), cln_g, cln_b, w_pw, b_pw)

    zs = _in_proj(hb, w_in, b_in, 2 * dc, 2 * dc + dsm)
    zst = _bf(zs.T)
    tri = jnp.where(lax.broadcasted_iota(jnp.int32, (tl, tl), 0)
                    <= lax.broadcasted_iota(jnp.int32, (tl, tl), 1), 1.0, 0.0).astype(_BF16)
    def project(j):
        zj = zst[j * LANES:(j + 1) * LANES, :]
        return _dot(bre[j], zj), _dot(bim[j], zj)

    def scale_in(j, bu):
        r = slice(j * rows, (j + 1) * rows)
        eir, eii = einv_r[r, :], einv_i[r, :]
        return _bf(bu[0] * eir - bu[1] * eii), _bf(bu[0] * eii + bu[1] * eir)

    def prefix(v):
        return _dot(v[0], tri), _dot(v[1], tri)

    def scale_out(j, cum):
        r = slice(j * rows, (j + 1) * rows)
        cr = jnp.broadcast_to(car_r[r, LANES - 1:LANES], (rows, LANES))
        ci = jnp.broadcast_to(car_i[r, LANES - 1:LANES], (rows, LANES))
        lr, li = lamb_r[r, :], lamb_i[r, :]
        cum_r = cum[0] + jnp.concatenate([lr * cr - li * ci] * (tl // LANES), axis=1)
        cum_i = cum[1] + jnp.concatenate([lr * ci + li * cr] * (tl // LANES), axis=1)
        epr, epi = epow_r[r, :], epow_i[r, :]
        h_r = cum_r * epr - cum_i * epi
        h_i = cum_r * epi + cum_i * epr
        car_r[r, :] = h_r[:, tl - LANES:tl]
        car_i[r, :] = h_i[:, tl - LANES:tl]
        return jnp.concatenate([_bf(h_r), _bf(h_i)], axis=0)

    bu, v, cum, hcat, yts = {}, {}, {}, {}, {}
    for t in range(nj + 4):
        if t < nj:
            bu[t] = project(t)
        if 0 <= t - 2 < nj:
            cum[t - 2] = prefix(v.pop(t - 2))
        if 0 <= t - 4 < nj:
            yts[t - 4] = _dot(cc[t - 4], hcat.pop(t - 4))
        if 0 <= t - 1 < nj:
            v[t - 1] = scale_in(t - 1, bu.pop(t - 1))
        if 0 <= t - 3 < nj:
            hcat[t - 3] = scale_out(t - 3, cum.pop(t - 3))
    ssm_out = _ssm_tail(jnp.concatenate([yts[j] for j in range(nj)], axis=0).T, zs, dskip,
                        w_sv, b_sv, w_sg, b_sg)

    x1, h2, route = _mixer_tail(x, zg, conv_out, ssm_out, g1, sh2, sc2, alpha,
                                w_out, b_out, ln1_g, ln1_b, wr_hi, wr_lo, b_r)
    x1_ref[0] = x1
    h2_ref[0] = _bf(h2)
    route_ref[0] = route
    cnt_ref[0, 0] = _block_counts(route)

    @pl.when(c == last)
    def _():
        cstate_ref[0] = ubuf[CONV_HIST + tl - (CONV_WIDTH - 1):CONV_HIST + tl, :]
        sre_ref[0] = car_r[...].T[LANES - 1:LANES, :]
        sim_ref[0] = car_i[...].T[LANES - 1:LANES, :]


def _mixer_prompt(x, mod, wts, alpha):
    b, l, d = x.shape
    tl = TOKEN_BLOCK
    nc = l // tl
    dc = wts["w_pw"].shape[0]
    nstate = wts["lamb_r"].shape[0]
    names = ["w_in", "b_in", "w_dw", "b_dw", "cln_g", "cln_b", "w_pw", "b_pw", "bre", "bim", "cc",
             "einv_r", "einv_i", "epow_r", "epow_i", "lamb_r", "lamb_i", "dskip",
             "w_sv", "b_sv", "w_sg", "b_sg", "w_out", "b_out", "ln1_g", "ln1_b",
             "wr_hi", "wr_lo", "b_r"]
    consts = [wts[n] for n in names]
    tok = lambda bi, ci: (bi, ci, 0)
    seq = lambda bi, ci: (bi, 0, 0)
    return pl.pallas_call(
        functools.partial(_mixer_prompt_kernel, alpha=alpha),
        grid=(b, nc),
        in_specs=[pl.BlockSpec((1, tl, d), tok), pl.BlockSpec((1, 1, mod.shape[-1]), seq)]
                 + [_const_spec(a.shape) for a in consts],
        out_specs=[pl.BlockSpec((1, tl, d), tok), pl.BlockSpec((1, tl, d), tok),
                   pl.BlockSpec((1, tl, LANES), tok),
                   pl.BlockSpec((1, 1, 1, LANES), lambda bi, ci: (bi, ci, 0, 0)),
                   pl.BlockSpec((1, CONV_WIDTH - 1, dc), seq),
                   pl.BlockSpec((1, 1, nstate), seq), pl.BlockSpec((1, 1, nstate), seq)],
        out_shape=[jax.ShapeDtypeStruct((b, l, d), _F32), jax.ShapeDtypeStruct((b, l, d), _BF16),
                   jax.ShapeDtypeStruct((b, l, LANES), _F32),
                   jax.ShapeDtypeStruct((b, nc, 1, LANES), _F32),
                   jax.ShapeDtypeStruct((b, CONV_WIDTH - 1, dc), _F32),
                   jax.ShapeDtypeStruct((b, 1, nstate), _F32),
                   jax.ShapeDtypeStruct((b, 1, nstate), _F32)],
        scratch_shapes=[pltpu.VMEM((CONV_HIST + tl, dc), _F32),
                        pltpu.VMEM((SUBLANES - 1, tl + CONV_HIST - SUBLANES, dc), _F32),
                        pltpu.VMEM((nstate, LANES), _F32), pltpu.VMEM((nstate, LANES), _F32)],
        compiler_params=_params(("arbitrary", "arbitrary")),
        name="mixer_prompt",
    )(x, mod, *consts)


def _mixer_sample_kernel(x_ref, mod_ref, hist_ref, h0r_ref, h0i_ref,
                         w_in, b_in, w_dw, b_dw, cln_g, cln_b, w_pw, b_pw,
                         bre, bim, cc, lrow_r, lrow_i, dskip,
                         w_sv, b_sv, w_sg, b_sg, w_out, b_out, ln1_g, ln1_b, wr_hi, wr_lo, b_r,
                         x1_ref, h2_ref, route_ref, cnt_ref, cstate_ref, sre_ref, sim_ref, *, alpha):
    steps, nb, d = x_ref.shape
    dc = w_pw.shape[0]
    dsm = dskip.shape[1]
    nj = bre.shape[0]
    rows = bre.shape[1]
    nhist = CONV_WIDTH - 1
    x = x_ref[...].reshape(steps * nb, d)
    mod = jnp.concatenate([mod_ref[...]] * steps, axis=0)
    sh1, sc1, g1 = mod[:, 0:d], mod[:, d:2 * d], mod[:, 2 * d:3 * d]
    sh2, sc2 = mod[:, 3 * d:4 * d], mod[:, 4 * d:5 * d]
    hb = _bf(x * (1.0 + sc1) + sh1)

    u = _in_proj(hb, w_in, b_in, 0, dc) * _sigmoid(_in_proj(hb, w_in, b_in, dc, 2 * dc))
    us = [u[t * nb:(t + 1) * nb, :] for t in range(steps)]
    vs = []
    for t in range(steps):
        acc = jnp.broadcast_to(b_dw[...], (nb, dc))
        for i in range(t, nhist):
            acc = acc + w_dw[i - t:i - t + 1, :] * hist_ref[i]
        for s in range(t + 1):
            acc = acc + w_dw[nhist + s - t:nhist + s - t + 1, :] * us[s]
        vs.append(acc)
    for i in range(nhist):
        src = i + steps
        cstate_ref[i] = hist_ref[src] if src < nhist else us[src - nhist]
    conv_out = _conv_tail(jnp.concatenate(vs, axis=0), cln_g, cln_b, w_pw, b_pw)

    zs = _in_proj(hb, w_in, b_in, 2 * dc, 2 * dc + dsm)
    zsb = _bf(zs)
    h_r, h_i = h0r_ref[...], h0i_ref[...]
    lr, li = lrow_r[...], lrow_i[...]
    ys = []
    for t in range(steps):
        zt = zsb[t * nb:(t + 1) * nb, :]
        bu_r = jnp.concatenate([_dot_nt(zt[:, j * LANES:(j + 1) * LANES], bre[j]) for j in range(nj)], axis=1)
        bu_i = jnp.concatenate([_dot_nt(zt[:, j * LANES:(j + 1) * LANES], bim[j]) for j in range(nj)], axis=1)
        h_r, h_i = lr * h_r - li * h_i + bu_r, lr * h_i + li * h_r + bu_i
        hrb, hib = _bf(h_r), _bf(h_i)
        ys.append(jnp.concatenate(
            [_dot_nt(jnp.concatenate([hrb[:, j * rows:(j + 1) * rows], hib[:, j * rows:(j + 1) * rows]], axis=1),
                     cc[j]) for j in range(nj)], axis=1))
    sre_ref[...] = h_r
    sim_ref[...] = h_i
    ssm_out = _ssm_tail(jnp.concatenate(ys, axis=0), zs, dskip, w_sv, b_sv, w_sg, b_sg)

    zg = _in_proj(hb, w_in, b_in, 2 * dc + dsm, 2 * dc + dsm + 2 * d)
    x1, h2, route = _mixer_tail(x, zg, conv_out, ssm_out, g1, sh2, sc2, alpha,
                                w_out, b_out, ln1_g, ln1_b, wr_hi, wr_lo, b_r)
    x1_ref[...] = x1
    h2_ref[...] = _bf(h2)
    route_ref[...] = route
    for blk in range(cnt_ref.shape[0]):
        cnt_ref[blk] = _block_counts(route[blk * TOKEN_BLOCK:(blk + 1) * TOKEN_BLOCK, :])


def _mixer_sample(x_tm, mod, hist_tm, h0r, h0i, wts, alpha):
    steps, nb, d = x_tm.shape
    n = steps * nb
    dc = wts["w_pw"].shape[0]
    nstate = h0r.shape[1]
    names = ["w_in", "b_in", "w_dw", "b_dw", "cln_g", "cln_b", "w_pw", "b_pw", "bre", "bim", "cc",
             "lrow_r", "lrow_i", "dskip", "w_sv", "b_sv", "w_sg", "b_sg", "w_out", "b_out",
             "ln1_g", "ln1_b", "wr_hi", "wr_lo", "b_r"]
    args = [x_tm, mod, hist_tm, h0r, h0i] + [wts[k] for k in names]
    return pl.pallas_call(
        functools.partial(_mixer_sample_kernel, alpha=alpha),
        grid=(1,),
        in_specs=[_const_spec(a.shape) for a in args],
        out_specs=[_const_spec((n, d)), _const_spec((n, d)), _const_spec((n, LANES)),
                   _const_spec((n // TOKEN_BLOCK, 1, LANES)),
                   _const_spec((CONV_WIDTH - 1, nb, dc)),
                   _const_spec((nb, nstate)), _const_spec((nb, nstate))],
        out_shape=[jax.ShapeDtypeStruct((n, d), _F32), jax.ShapeDtypeStruct((n, d), _BF16),
                   jax.ShapeDtypeStruct((n, LANES), _F32),
                   jax.ShapeDtypeStruct((n // TOKEN_BLOCK, 1, LANES), _F32),
                   jax.ShapeDtypeStruct((CONV_WIDTH - 1, nb, dc), _F32),
                   jax.ShapeDtypeStruct((nb, nstate), _F32), jax.ShapeDtypeStruct((nb, nstate), _F32)],
        compiler_params=_params(("arbitrary",)),
        name="mixer_sample",
    )(*args)


def _round_up(x, m):
    return (x + m - 1) // m * m


def _plan_sizes(n_tokens):
    nb = n_tokens // TOKEN_BLOCK
    r_max = _round_up(TOP_K * TOKEN_BLOCK + 2 * N_EXPERTS * (SEG_PAD - 1), 2 * LANES)
    rows_max = n_tokens * TOP_K + N_EXPERTS * (EXPERT_TILE - 1)
    nt_max = -(-rows_max // EXPERT_TILE)
    return nb, r_max, nt_max


def _routing_plan(cnt, nt_max):
    nb, ne = cnt.shape
    before = jnp.cumsum(cnt, axis=0) - cnt
    tot = jnp.sum(cnt, axis=0)
    cin = before % SEG_PAD
    last = (jnp.arange(nb, dtype=jnp.int32) == nb - 1)[:, None]
    active = (cnt > 0) | (last & (cin > 0))
    seg = jnp.where(active, _round_up(cin + cnt, SEG_PAD), 0)
    loc = jnp.cumsum(seg, axis=1) - seg
    totpad = _round_up(tot, EXPERT_TILE)
    eend = jnp.cumsum(totpad)
    estart = eend - totpad
    goff = estart[None, :] + before - cin
    nfull = jnp.where(last, seg, jnp.where(active, (cin + cnt) // SEG_PAD * SEG_PAD, 0))
    cout = jnp.where(last | ~active, 0, (cin + cnt) % SEG_PAD)
    n_tiles = (eend[-1] // EXPERT_TILE).astype(jnp.int32)
    tiles = jnp.minimum(jnp.arange(nt_max, dtype=jnp.int32), n_tiles - 1)
    te = jnp.sum(((eend // EXPERT_TILE)[None, :] <= tiles[:, None]).astype(jnp.int32), axis=1)
    te = jnp.minimum(te, ne - 1)
    tot16 = _round_up(tot, SEG_PAD)
    return dict(seg=seg, loc=loc, goff=goff, nfull=nfull, cin=jnp.where(active, cin, 0), cout=cout,
                used=jnp.sum(seg, axis=1).astype(jnp.int32), sent=jnp.sum(nfull, axis=1).astype(jnp.int32),
                n_tiles=n_tiles.reshape(1), te=te,
                npad=(totpad - tot16).astype(jnp.int32), pad_base=(estart + tot16).astype(jnp.int32))


def _dest_columns(route, loc_row):
    n = route.shape[0]
    lane_i = lax.broadcasted_iota(jnp.int32, (n, LANES), 1)
    lane = lane_i.astype(_F32)
    member = _bf(_members(route))
    strict = jnp.where(lax.broadcasted_iota(jnp.int32, (n, n), 1)
                       < lax.broadcasted_iota(jnp.int32, (n, n), 0), 1.0, 0.0).astype(_BF16)
    dest = _dot(strict, member) + loc_row
    d4 = jnp.zeros((n, LANES), _F32)
    for k in range(TOP_K):
        idx = route[:, ROUTE_IDX + k:ROUTE_IDX + k + 1]
        dk = jnp.sum(jnp.where(lane == idx, dest, 0.0), axis=-1, keepdims=True)
        d4 = d4 + jnp.where(lane_i == k, dk, 0.0)
    return d4


def _typical_rows(r_max):
    return min(r_max, _round_up(TOP_K * TOKEN_BLOCK + N_EXPERTS * SEG_PAD, 2 * LANES))


def _dispatch_kernel(nfull, sloc, goff, cin, cout, used, sent, npad, pad_base, hp_ref, hs_ref, rp_ref, rs_ref, loc_ref,
                     xs_hbm, d4_ref, buf, stage, zbuf, sem, zsem, *, nbp):
    i = pl.program_id(0)
    nblk = pl.num_programs(0)
    slot = lax.rem(i, 2)
    tb = hp_ref.shape[0]
    r_max = buf.shape[1]
    is_p = i < nbp
    h = jnp.where(is_p, hp_ref[...], hs_ref[...])
    route = jnp.where(is_p, rp_ref[...], rs_ref[...])
    d4 = _dest_columns(route, loc_ref[0])
    d4_ref[...] = d4
    d4t = d4.T

    def compact(rows):
        row = lax.broadcasted_iota(jnp.int32, (rows, tb), 0).astype(_F32)
        p = jnp.zeros((rows, tb), _F32)
        for k in range(TOP_K):
            p = jnp.where(row == d4t[k:k + 1, :], 1.0, p)
        buf[slot, 0:rows, :] = _bf(_dot(_bf(p), h))

    r_typ = _typical_rows(r_max)

    @pl.when(i == 0)
    def _():
        buf[...] = jnp.zeros(buf.shape, buf.dtype)
        stage[...] = jnp.zeros(stage.shape, stage.dtype)

    @pl.when(used[i] <= r_typ)
    def _():
        compact(r_typ)

    @pl.when(used[i] > r_typ)
    def _():
        compact(r_max)

    def seg_copy(blk, s, e):
        n = pl.multiple_of(nfull[blk * N_EXPERTS + e], SEG_PAD)
        src = pl.multiple_of(sloc[blk * N_EXPERTS + e], SEG_PAD)
        dst = pl.multiple_of(goff[blk * N_EXPERTS + e], SEG_PAD)
        return pltpu.make_async_copy(buf.at[s, pl.ds(src, n), :], xs_hbm.at[pl.ds(dst, n), :], sem.at[s])

    def pad_copy(e):
        n = pl.multiple_of(npad[e], SEG_PAD)
        dst = pl.multiple_of(pad_base[e], SEG_PAD)
        return pltpu.make_async_copy(zbuf.at[pl.ds(0, n), :], xs_hbm.at[pl.ds(dst, n), :], zsem.at[0])

    def for_segments(blk, fn):
        for e in range(N_EXPERTS):
            @pl.when(nfull[blk * N_EXPERTS + e] > 0)
            def _(e=e):
                fn(e)

    for e in range(N_EXPERTS):
        first = pl.ds(pl.multiple_of(sloc[i * N_EXPERTS + e], SEG_PAD), SEG_PAD)
        carried = jnp.where(cin[i * N_EXPERTS + e] > 0, stage[e], jnp.zeros_like(stage[e]))
        buf[slot, first, :] = buf[slot, first, :] + carried

    for_segments(i, lambda e: seg_copy(i, slot, e).start())

    for e in range(N_EXPERTS):
        rest = pl.multiple_of(sloc[i * N_EXPERTS + e] + nfull[i * N_EXPERTS + e], SEG_PAD)
        stage[e] = jnp.where(cout[i * N_EXPERTS + e] > 0, buf[slot, pl.ds(rest, SEG_PAD), :], stage[e])

    def wait_block(blk, s):
        @pl.when(sent[blk] > 0)
        def _():
            n = pl.multiple_of(sent[blk], SEG_PAD)
            pltpu.make_async_copy(buf.at[s, pl.ds(0, n), :], xs_hbm.at[pl.ds(0, n), :], sem.at[s]).wait()

    @pl.when(i > 0)
    def _():
        wait_block(i - 1, 1 - slot)

    @pl.when(i == nblk - 1)
    def _():
        zbuf[...] = jnp.zeros(zbuf.shape, zbuf.dtype)
        for e in range(N_EXPERTS):
            @pl.when(npad[e] > 0)
            def _(e=e):
                pad_copy(e).start()
        for e in range(N_EXPERTS):
            @pl.when(npad[e] > 0)
            def _(e=e):
                pad_copy(e).wait()
        wait_block(i, slot)


def _dispatch(plan, h2p, h2s, rp, rs, r_max, nt_max):
    tb = TOKEN_BLOCK
    d = h2p.shape[1]
    nbp, nbs = h2p.shape[0] // tb, h2s.shape[0] // tb
    nb = nbp + nbs
    first_row = (plan["loc"] + plan["cin"]).astype(_F32)
    loc = jnp.pad(first_row, ((0, 0), (0, LANES - N_EXPERTS))).reshape(nb, 1, LANES)
    pidx = lambda i, *_: (jnp.minimum(i, nbp - 1), 0)
    sidx = lambda i, *_: (jnp.maximum(i - nbp, 0), 0)
    grid_spec = pltpu.PrefetchScalarGridSpec(
        num_scalar_prefetch=9,
        grid=(nb,),
        in_specs=[pl.BlockSpec((tb, d), pidx), pl.BlockSpec((tb, d), sidx),
                  pl.BlockSpec((tb, LANES), pidx), pl.BlockSpec((tb, LANES), sidx),
                  pl.BlockSpec((1, 1, LANES), lambda i, *_: (i, 0, 0))],
        out_specs=[pl.BlockSpec(memory_space=pl.ANY), pl.BlockSpec((tb, LANES), lambda i, *_: (i, 0))],
        scratch_shapes=[pltpu.VMEM((2, r_max, d), _BF16), pltpu.VMEM((N_EXPERTS, SEG_PAD, d), _BF16),
                        pltpu.VMEM((EXPERT_TILE, d), _BF16),
                        pltpu.SemaphoreType.DMA((2,)), pltpu.SemaphoreType.DMA((1,))])
    flat = lambda a: a.astype(jnp.int32).reshape(-1)
    return pl.pallas_call(
        functools.partial(_dispatch_kernel, nbp=nbp),
        grid_spec=grid_spec,
        out_shape=[jax.ShapeDtypeStruct((nt_max * EXPERT_TILE, d), _BF16),
                   jax.ShapeDtypeStruct((nb * tb, LANES), _F32)],
        compiler_params=_params(("arbitrary",)),
        name="dispatch",
    )(flat(plan["nfull"]), flat(plan["loc"]), flat(plan["goff"]), flat(plan["cin"]), flat(plan["cout"]),
      plan["used"], plan["sent"], plan["npad"], plan["pad_base"], h2p, h2s, rp, rs, loc)


def _expert_kernel(te, n_tiles, x_ref, w1_ref, b1_ref, w2_ref, b2_ref, y_ref, w1b, w2b):
    i = pl.program_id(0)
    dff = w2_ref.shape[1]

    @pl.when(i < n_tiles[0])
    def _():
        prev = te[jnp.maximum(i - 1, 0)]

        @pl.when((i == 0) | (te[i] != prev))
        def _():
            w1b[...] = _bf(w1_ref[0])
            w2b[...] = _bf(w2_ref[0])

        gu = _dot(x_ref[...], w1b[...]) + b1_ref[0]
        g = jnp.minimum(gu[:, :dff], SWIGLU_LIMIT)
        up = jnp.clip(gu[:, dff:], -SWIGLU_LIMIT, SWIGLU_LIMIT)
        act = g * _sigmoid(SWIGLU_ALPHA * g) * (up + 1.0)
        y_ref[...] = _bf(_dot(_bf(act), w2b[...]) + b2_ref[0])


def _experts(plan, xs, w1, b1, w2, b2, nt_max):
    ne, d, dff2 = w1.shape
    dff = w2.shape[1]
    tm = EXPERT_TILE
    tile = lambda i, te, nt: (jnp.minimum(i, nt[0] - 1), 0)
    wsel = lambda i, te, nt: (te[i], 0, 0)
    grid_spec = pltpu.PrefetchScalarGridSpec(
        num_scalar_prefetch=2,
        grid=(nt_max,),
        in_specs=[pl.BlockSpec((tm, d), tile),
                  pl.BlockSpec((1, d, dff2), wsel), pl.BlockSpec((1, 1, dff2), wsel),
                  pl.BlockSpec((1, dff, d), wsel), pl.BlockSpec((1, 1, d), wsel)],
        out_specs=pl.BlockSpec((tm, d), tile),
        scratch_shapes=[pltpu.VMEM((d, dff2), _BF16), pltpu.VMEM((dff, d), _BF16)])
    return pl.pallas_call(
        _expert_kernel,
        grid_spec=grid_spec,
        out_shape=jax.ShapeDtypeStruct(xs.shape, _BF16),
        compiler_params=_params(("arbitrary",)),
        name="experts",
    )(plan["te"], plan["n_tiles"], xs, w1, b1.reshape(ne, 1, dff2), w2, b2.reshape(ne, 1, d))


def _combine_kernel(seg, sloc, goff, used, rp_ref, rs_ref, d4_ref, x1p_ref, x1s_ref, g2p_ref, g2s_ref,
                    ln2_g, ln2_b, ys_hbm, yp_ref, ysm_ref, buf, fbuf, sem, *, nbp, alpha):
    i = pl.program_id(0)
    nblk = pl.num_programs(0)
    slot = lax.rem(i, 2)
    tb = rp_ref.shape[0]
    r_max = buf.shape[1]

    def seg_copy(blk, s, e):
        n = pl.multiple_of(seg[blk * N_EXPERTS + e], SEG_PAD)
        src = pl.multiple_of(goff[blk * N_EXPERTS + e], SEG_PAD)
        dst = pl.multiple_of(sloc[blk * N_EXPERTS + e], SEG_PAD)
        return pltpu.make_async_copy(ys_hbm.at[pl.ds(src, n), :], buf.at[s, pl.ds(dst, n), :], sem.at[s])

    def for_segments(blk, fn):
        for e in range(N_EXPERTS):
            @pl.when(seg[blk * N_EXPERTS + e] > 0)
            def _(e=e):
                fn(e)

    @pl.when(i == 0)
    def _():
        buf[...] = jnp.zeros(buf.shape, buf.dtype)
        for_segments(0, lambda e: seg_copy(0, 0, e).start())

    @pl.when(i + 1 < nblk)
    def _():
        for_segments(i + 1, lambda e: seg_copy(i + 1, 1 - slot, e).start())

    @pl.when(used[i] > 0)
    def _():
        n = pl.multiple_of(used[i], SEG_PAD)
        pltpu.make_async_copy(ys_hbm.at[pl.ds(0, n), :], buf.at[slot, pl.ds(0, n), :], sem.at[slot]).wait()

    is_p = i < nbp
    route = jnp.where(is_p, rp_ref[...], rs_ref[...])
    d4 = d4_ref[...]

    def gather(rows):
        col = lax.broadcasted_iota(jnp.int32, (tb, rows), 1).astype(_F32)
        pg = jnp.zeros((tb, rows), _F32)
        for k in range(TOP_K):
            gate = route[:, ROUTE_GATE + k:ROUTE_GATE + k + 1]
            pg = jnp.where(col == d4[:, k:k + 1], gate, pg)
        fbuf[...] = _dot(_bf(pg), buf[slot, 0:rows, :])

    r_typ = _typical_rows(r_max)

    @pl.when(used[i] <= r_typ)
    def _():
        gather(r_typ)

    @pl.when(used[i] > r_typ)
    def _():
        gather(r_max)

    f = fbuf[...]
    x1 = jnp.where(is_p, x1p_ref[...], x1s_ref[...])
    g2s = jnp.concatenate([g2s_ref[...]] * (tb // g2s_ref.shape[0]), axis=0)
    g2 = jnp.where(is_p, jnp.broadcast_to(g2p_ref[0], g2s.shape), g2s)
    y = _layer_norm(alpha * x1 + g2 * f, ln2_g[...], ln2_b[...])

    @pl.when(is_p)
    def _():
        yp_ref[...] = y

    @pl.when(jnp.logical_not(is_p))
    def _():
        ysm_ref[...] = y


def _combine(plan, ys, rp, rs, d4, x1p, x1s, modp, mods, ln2_g, ln2_b, r_max, alpha, blocks_per_seq):
    tb = TOKEN_BLOCK
    d = x1p.shape[1]
    nbp, nbs = x1p.shape[0] // tb, x1s.shape[0] // tb
    nb = nbp + nbs
    pidx = lambda i, *_: (jnp.minimum(i, nbp - 1), 0)
    sidx = lambda i, *_: (jnp.maximum(i - nbp, 0), 0)
    g2_lane_block = 5
    flat = lambda a: a.astype(jnp.int32).reshape(-1)
    grid_spec = pltpu.PrefetchScalarGridSpec(
        num_scalar_prefetch=4,
        grid=(nb,),
        in_specs=[pl.BlockSpec((tb, LANES), pidx), pl.BlockSpec((tb, LANES), sidx),
                  pl.BlockSpec((tb, LANES), lambda i, *_: (i, 0)),
                  pl.BlockSpec((tb, d), pidx), pl.BlockSpec((tb, d), sidx),
                  pl.BlockSpec((1, 1, d), lambda i, *_: (jnp.minimum(i, nbp - 1) // blocks_per_seq, 0,
                                                         g2_lane_block)),
                  pl.BlockSpec((mods.shape[0], d), lambda i, *_: (0, g2_lane_block)),
                  pl.BlockSpec((1, d), lambda i, *_: (0, 0)), pl.BlockSpec((1, d), lambda i, *_: (0, 0)),
                  pl.BlockSpec(memory_space=pl.ANY)],
        out_specs=[pl.BlockSpec((tb, d), pidx), pl.BlockSpec((tb, d), sidx)],
        scratch_shapes=[pltpu.VMEM((2, r_max, d), _BF16), pltpu.VMEM((tb, d), _F32),
                        pltpu.SemaphoreType.DMA((2,))])
    return pl.pallas_call(
        functools.partial(_combine_kernel, nbp=nbp, alpha=alpha),
        grid_spec=grid_spec,
        out_shape=[jax.ShapeDtypeStruct(x1p.shape, _F32), jax.ShapeDtypeStruct(x1s.shape, _F32)],
        compiler_params=_params(("arbitrary",)),
        name="combine",
    )(flat(plan["seg"]), flat(plan["loc"]), flat(plan["goff"]), plan["used"],
      rp, rs, d4, x1p, x1s, modp, mods, ln2_g, ln2_b, ys)


def _complex_powers(zr, zi, n):
    low_bits = 4
    low = 1 << low_bits
    if n <= low or n % low:
        return _bit_powers(zr, zi, n)
    lr, li = _bit_powers(zr, zi, low)
    sr, si = zr, zi
    for _ in range(low_bits):
        sr, si = sr * sr - si * si, 2.0 * sr * si
    hr, hi = _bit_powers(sr, si, n // low)
    pr = hr[:, :, None] * lr[:, None, :] - hi[:, :, None] * li[:, None, :]
    pi = hr[:, :, None] * li[:, None, :] + hi[:, :, None] * lr[:, None, :]
    return pr.reshape(-1, n), pi.reshape(-1, n)


def _bit_powers(zr, zi, n):
    k = jnp.arange(n, dtype=jnp.int32)[None, :]
    pr = jnp.ones((zr.shape[0], n), _F32)
    pi = jnp.zeros((zr.shape[0], n), _F32)
    sr, si = zr[:, None], zi[:, None]
    bit = 1
    while bit < n:
        on = (k & bit) != 0
        mr, mi = jnp.where(on, sr, 1.0), jnp.where(on, si, 0.0)
        pr, pi = pr * mr - pi * mi, pr * mi + pi * mr
        sr, si = sr * sr - si * si, 2.0 * sr * si
        bit *= 2
    return pr, pi


def _block_diag(m):
    g, a, b = m.shape
    gb = GROUPS_PER_BLOCK
    m = m.reshape(g // gb, gb, a, 1, b)
    eye = jnp.eye(gb, dtype=m.dtype).reshape(1, gb, 1, gb, 1)
    return (m * eye).reshape(g // gb, gb * a, gb * b)


def _layer_weights(p, q):
    d = p["w_in"].shape[0]
    row = lambda v: v.reshape(1, -1).astype(_F32)
    lam_re, lam_im = p["lam_re"].astype(_F32), p["lam_im"].astype(_F32)
    dt = jnp.exp(p["log_dt"].astype(_F32))[:, None]
    mag = jnp.exp(lam_re * dt)
    lbr, lbi = mag * jnp.cos(lam_im * dt), mag * jnp.sin(lam_im * dt)
    den = lam_re * lam_re + lam_im * lam_im
    nr, ni = lbr - 1.0, lbi
    fr, fi = (nr * lam_re + ni * lam_im) / den, (ni * lam_re - nr * lam_im) / den
    b_re, b_im = p["b_re"].astype(_F32), p["b_im"].astype(_F32)
    bbr = fr[..., None] * b_re - fi[..., None] * b_im
    bbi = fr[..., None] * b_im + fi[..., None] * b_re
    nstate = lam_re.size
    mod2 = lbr * lbr + lbi * lbi
    flat = lambda v: v.reshape(nstate)
    epr, epi = _complex_powers(flat(lbr), flat(lbi), q)
    eir, eii = _complex_powers(flat(lbr / mod2), flat(-lbi / mod2), q)
    table = lambda t: t
    bcast = lambda v: jnp.broadcast_to(v.reshape(nstate, 1), (nstate, LANES))
    c_re, c_im = p["c_re"].astype(_F32), p["c_im"].astype(_F32)
    w_r = p["w_router"].astype(_F32)
    w_r = jnp.pad(w_r, ((0, 0), (0, LANES - w_r.shape[1])))
    wr_hi = _bf(w_r)
    return dict(
        w_in=_bf(p["w_in"]), b_in=row(p["b_in"]), w_dw=p["w_dw"].astype(_F32), b_dw=row(p["b_dw"]),
        cln_g=row(p["conv_ln_g"]), cln_b=row(p["conv_ln_b"]), w_pw=_bf(p["w_pw"]), b_pw=row(p["b_pw"]),
        bre=_bf(_block_diag(bbr)), bim=_bf(_block_diag(bbi)),
        cc=_bf(jnp.concatenate([_block_diag(c_re), _block_diag(-c_im)], axis=2)),
        einv_r=table(eir), einv_i=table(eii), epow_r=table(epr), epow_i=table(epi),
        lamb_r=bcast(lbr), lamb_i=bcast(lbi), lrow_r=lbr.reshape(1, nstate), lrow_i=lbi.reshape(1, nstate),
        dskip=row(p["d_skip"]),
        w_sv=_bf(p["w_sv"]), b_sv=row(p["b_sv"]), w_sg=_bf(p["w_sg"]), b_sg=row(p["b_sg"]),
        w_out=_bf(p["w_out"]), b_out=row(p["b_out"]), ln1_g=row(p["ln1_g"]), ln1_b=row(p["ln1_b"]),
        wr_hi=wr_hi, wr_lo=_bf(w_r - wr_hi.astype(_F32)),
        b_r=p["b_router"].astype(_F32).reshape(-1, 1),
        ln2_g=row(p["ln2_g"]), ln2_b=row(p["ln2_b"]))


def _layer(xp, xs_tm, c_all, hist_tm, h0r, h0i, p, alpha):
    b, l, d = xp.shape
    steps, nbs, _ = xs_tm.shape
    tb = TOKEN_BLOCK
    assert l % tb == 0 and l >= CONV_WIDTH - 1 and (steps * nbs) % tb == 0 and tb % nbs == 0
    wts = _layer_weights(p, tb)
    mod = _ada(c_all, p["w_ada"].astype(_F32), p["b_ada"].astype(_F32))
    modp, mods = mod[:b].reshape(b, 1, -1), mod[b:]

    x1p, h2p, rp, cntp, conv_p, sre_p, sim_p = _mixer_prompt(xp, modp, wts, alpha)
    x1s, h2s, rs, cnts, conv_s, sre_s, sim_s = _mixer_sample(xs_tm, mods, hist_tm, h0r, h0i, wts, alpha)

    n_tok = b * l + steps * nbs
    nb, r_max, nt_max = _plan_sizes(n_tok)
    cnt = jnp.concatenate([cntp.reshape(-1, LANES), cnts.reshape(-1, LANES)], axis=0)[:, :N_EXPERTS]
    plan = _routing_plan(cnt.astype(jnp.int32), nt_max)

    flat = lambda a: a.reshape(b * l, a.shape[-1])
    xs_sorted, d4 = _dispatch(plan, flat(h2p), h2s, flat(rp), rs, r_max, nt_max)
    ys_sorted = _experts(plan, xs_sorted, p["w1"], p["b1"], p["w2"], p["b2"], nt_max)
    yp, ysm = _combine(plan, ys_sorted, flat(rp), rs, d4, flat(x1p), x1s, modp, mods,
                       wts["ln2_g"], wts["ln2_b"], r_max, alpha, l // tb)
    return (yp.reshape(b, l, d), ysm.reshape(steps, nbs, d), conv_p, sre_p[:, 0], sim_p[:, 0],
            conv_s, sre_s, sim_s)


def kernel(x_prompt, x_sample, state_conv, state_ssm_re, state_ssm_im, c_prompt, c_sample, w_ada, b_ada, w_in, b_in, w_dw, b_dw, conv_ln_g, conv_ln_b, w_pw, b_pw, lam_re, lam_im, log_dt, b_re, b_im, c_re, c_im, d_skip, w_sv, b_sv, w_sg, b_sg, w_out, b_out, ln1_g, ln1_b, w_router, b_router, w1, b1, w2, b2, ln2_g, ln2_b):
    stacked = dict(w_ada=w_ada, b_ada=b_ada, w_in=w_in, b_in=b_in, w_dw=w_dw, b_dw=b_dw,
                   conv_ln_g=conv_ln_g, conv_ln_b=conv_ln_b, w_pw=w_pw, b_pw=b_pw, lam_re=lam_re,
                   lam_im=lam_im, log_dt=log_dt, b_re=b_re, b_im=b_im, c_re=c_re, c_im=c_im,
                   d_skip=d_skip, w_sv=w_sv, b_sv=b_sv, w_sg=w_sg, b_sg=b_sg, w_out=w_out, b_out=b_out,
                   ln1_g=ln1_g, ln1_b=ln1_b, w_router=w_router, b_router=b_router, w1=w1, b1=b1,
                   w2=w2, b2=b2, ln2_g=ln2_g, ln2_b=ln2_b)
    depth = w_ada.shape[0]
    alpha = (2 * depth) ** 0.25
    b = x_prompt.shape[0]
    nbs = x_sample.shape[0]
    g, s = state_ssm_re.shape[2], state_ssm_re.shape[3]
    xp = x_prompt
    xs_tm = jnp.transpose(x_sample, (1, 0, 2))
    c_all = jnp.concatenate([c_prompt, c_sample], axis=0)
    conv_ps, re_ps, im_ps, conv_ss, re_ss, im_ss = [], [], [], [], [], []
    for layer in range(depth):
        p = {k: v[layer] for k, v in stacked.items()}
        hist_tm = jnp.transpose(state_conv[layer], (1, 0, 2))
        h0r = state_ssm_re[layer].reshape(nbs, g * s)
        h0i = state_ssm_im[layer].reshape(nbs, g * s)
        xp, xs_tm, conv_p, sre_p, sim_p, conv_s, sre_s, sim_s = _layer(
            xp, xs_tm, c_all, hist_tm, h0r, h0i, p, alpha)
        conv_ps.append(conv_p.astype(state_conv.dtype))
        re_ps.append(sre_p.reshape(b, g, s))
        im_ps.append(sim_p.reshape(b, g, s))
        conv_ss.append(jnp.transpose(conv_s, (1, 0, 2)).astype(state_conv.dtype))
        re_ss.append(sre_s.reshape(nbs, g, s))
        im_ss.append(sim_s.reshape(nbs, g, s))
    return (xp, jnp.transpose(xs_tm, (1, 0, 2)), jnp.stack(conv_ps), jnp.stack(re_ps), jnp.stack(im_ps),
            jnp.stack(conv_ss), jnp.stack(re_ss), jnp.stack(im_ss))
```

```python
import functools

import jax
import jax.numpy as jnp
from jax import lax
from jax.experimental import pallas as pl
from jax.experimental.pallas import tpu as pltpu

CONV_WIDTH = 31
SSM_GROUP = 16
N_EXPERTS = 32
TOP_K = 4
SWIGLU_LIMIT = 7.0
SWIGLU_ALPHA = 1.702
LN_EPS = 1e-5

LANES = 128
SUBLANES = 8
TOKEN_BLOCK = 256
EXPERT_TILE = 512
SEG_PAD = 16
GROUPS_PER_BLOCK = LANES // SSM_GROUP
CONV_HIST = 32
CONV_ROWS = 32
ROUTE_GATE = N_EXPERTS + TOP_K
VMEM_LIMIT = 56 * 1024 * 1024

_F32 = jnp.float32
_BF16 = jnp.bfloat16


def _bf(x):
    return x.astype(_BF16)


def _dot(a, b):
    return jnp.dot(a, b, preferred_element_type=_F32)


def _dot_nt(a, b):
    return lax.dot_general(a, b, (((1,), (1,)), ((), ())), preferred_element_type=_F32)


def _split(x):
    hi = _bf(x)
    lo = _bf(x - hi.astype(_F32))
    return hi, lo


def _sigmoid(x):
    return 1.0 / (1.0 + jnp.exp(-x))


def _gelu_tanh(x):
    return 0.5 * x * (1.0 + jnp.tanh(0.7978845608028654 * (x + 0.044715 * (x * x * x))))


def _layer_norm(x, g, b):
    mu = jnp.mean(x, axis=-1, keepdims=True)
    xc = x - mu
    var = jnp.mean(xc * xc, axis=-1, keepdims=True)
    return xc * lax.rsqrt(var + LN_EPS) * g + b


def _const_spec(shape):
    nd = len(shape)
    return pl.BlockSpec(shape, lambda *_: (0,) * nd)


def _params(sem):
    return pltpu.CompilerParams(dimension_semantics=sem, vmem_limit_bytes=VMEM_LIMIT)


def _ada_kernel(c_ref, w_ref, b_ref, o_ref):
    c = c_ref[...]
    s_hi, s_lo = _split(c * _sigmoid(c))
    w_hi, w_lo = _split(w_ref[...])
    o_ref[...] = _dot(s_hi, w_hi) + _dot(s_lo, w_hi) + _dot(s_hi, w_lo) + b_ref[...]


def _ada(c, w, b):
    n, d = c.shape
    cols = w.shape[1]
    tn = d
    return pl.pallas_call(
        _ada_kernel,
        grid=(cols // tn,),
        in_specs=[pl.BlockSpec((n, d), lambda j: (0, 0)),
                  pl.BlockSpec((d, tn), lambda j: (0, j)),
                  pl.BlockSpec((1, tn), lambda j: (0, j))],
        out_specs=pl.BlockSpec((n, tn), lambda j: (0, j)),
        out_shape=jax.ShapeDtypeStruct((n, cols), _F32),
        compiler_params=_params(("arbitrary",)),
        name="ada",
    )(c, w, b.reshape(1, cols))


def _in_proj(hb, w_in, b_in, lo, hi):
    return _dot(hb, w_in[:, lo:hi]) + b_in[:, lo:hi]


def _conv_tail(v, cln_g, cln_b, w_pw, b_pw):
    v = _layer_norm(v, cln_g[...], cln_b[...])
    v = v * _sigmoid(v)
    return _dot(_bf(v), w_pw[...]) + b_pw[...]


def _ssm_tail(y, zs, dskip, w_sv, b_sv, w_sg, b_sg):
    yg = _bf(_gelu_tanh(y + dskip[...] * zs))
    return (_dot(yg, w_sv[...]) + b_sv[...]) * _sigmoid(_dot(yg, w_sg[...]) + b_sg[...])


def _route(h2, wr_hi, wr_lo, b_r):
    n = h2.shape[0]
    ne = b_r.shape[0]
    h_hi, h_lo = _split(h2)
    logits = (_dot(h_hi, wr_hi[...]) + _dot(h_lo, wr_hi[...]) + _dot(h_hi, wr_lo[...])).T[:ne, :] + b_r[...]
    row = lax.broadcasted_iota(jnp.int32, (ne, n), 0).astype(_F32)
    neg = jnp.float32(-jnp.inf)
    cur = logits
    vals, idxs = [], []
    for _ in range(TOP_K):
        m = jnp.max(cur, axis=0, keepdims=True)
        idx = jnp.min(jnp.where(cur == m, row, float(ne)), axis=0, keepdims=True)
        vals.append(m)
        idxs.append(idx)
        cur = jnp.where(row == idx, neg, cur)
    es = [jnp.exp(v - vals[0]) for v in vals]
    tot = es[0]
    for e in es[1:]:
        tot = tot + e
    inv = 1.0 / tot
    selected = jnp.zeros((ne, n), _F32)
    row8 = lax.broadcasted_iota(jnp.int32, (2 * TOP_K, n), 0)
    extra = jnp.zeros((2 * TOP_K, n), _F32)
    for k in range(TOP_K):
        selected = jnp.where(row == idxs[k], 1.0, selected)
        extra = jnp.where(row8 == TOP_K + k, es[k] * inv, extra)
    rest = jnp.zeros((LANES - ne - 2 * TOP_K, n), _F32)
    tb = min(n, TOKEN_BLOCK)
    earlier = jnp.where(lax.broadcasted_iota(jnp.int32, (tb, tb), 0)
                        < lax.broadcasted_iota(jnp.int32, (tb, tb), 1), 1.0, 0.0).astype(_BF16)
    rank = jnp.concatenate([_dot(_bf(selected[:, c0:c0 + tb]), earlier) for c0 in range(0, n, tb)], axis=1)
    sched = jnp.zeros((2 * TOP_K, n), _F32)
    for k in range(TOP_K):
        rank_k = jnp.sum(jnp.where(row == idxs[k], rank, 0.0), axis=0, keepdims=True)
        sched = jnp.where(row8 == k, idxs[k], jnp.where(row8 == TOP_K + k, rank_k, sched))
    return jnp.concatenate([selected, extra, rest], axis=0).T, sched


def _members(route):
    lane = lax.broadcasted_iota(jnp.int32, route.shape, 1)
    return jnp.where(lane < N_EXPERTS, route, 0.0)


def _block_counts(route):
    return jnp.sum(_members(route), axis=0, keepdims=True)


def _mixer_tail(x, zg, conv_out, ssm_out, g1, sh2, sc2, alpha, w_out, b_out,
                ln1_g, ln1_b, wr_hi, wr_lo, b_r):
    d = x.shape[1]
    merged = _sigmoid(zg[:, :d]) * conv_out + _sigmoid(zg[:, d:]) * ssm_out
    m = _dot(_bf(merged), w_out[...]) + b_out[...]
    x1 = _layer_norm(alpha * x + g1 * m, ln1_g[...], ln1_b[...])
    h2 = x1 * (1.0 + sc2) + sh2
    return (x1, h2) + _route(h2, wr_hi, wr_lo, b_r)


def _mixer_prompt_kernel(x_ref, mod_ref, w_in, b_in, w_dw, b_dw, cln_g, cln_b, w_pw, b_pw,
                         bre, bim, cc, einv_r, einv_i, epow_r, epow_i, lamb_r, lamb_i, dskip,
                         w_sv, b_sv, w_sg, b_sg, w_out, b_out, ln1_g, ln1_b, wr_hi, wr_lo, b_r,
                         x1_ref, h2_ref, route_ref, sched_ref, cnt_ref, cstate_ref, sre_ref, sim_ref,
                         ubuf, ush, car_r, car_i, *, alpha):
    c = pl.program_id(1)
    last = pl.num_programs(1) - 1
    tl, d = x_ref.shape[1], x_ref.shape[2]
    dc = w_pw.shape[0]
    dsm = dskip.shape[1]
    nj = bre.shape[0]
    rows = bre.shape[1]

    @pl.when(c == 0)
    def _():
        ubuf[0:CONV_HIST, :] = jnp.zeros((CONV_HIST, dc), _F32)
        car_r[...] = jnp.zeros(car_r.shape, _F32)
        car_i[...] = jnp.zeros(car_i.shape, _F32)

    x = x_ref[0]
    mod = mod_ref[0]
    sh1, sc1, g1 = mod[:, 0:d], mod[:, d:2 * d], mod[:, 2 * d:3 * d]
    sh2, sc2 = mod[:, 3 * d:4 * d], mod[:, 4 * d:5 * d]
    hb = _bf(x * (1.0 + sc1) + sh1)

    u = _in_proj(hb, w_in, b_in, 0, dc) * _sigmoid(_in_proj(hb, w_in, b_in, dc, 2 * dc))
    ubuf[CONV_HIST:CONV_HIST + tl, :] = u
    span = ush.shape[1]
    for r in range(1, SUBLANES):
        ush[r - 1] = ubuf[r:r + span, :]

    vblocks = []
    for r0 in range(0, tl, CONV_ROWS):
        acc = jnp.broadcast_to(b_dw[...], (CONV_ROWS, dc))
        for k in range(CONV_WIDTH):
            q, r = divmod(CONV_HIST - (CONV_WIDTH - 1) + k, SUBLANES)
            rows_k = slice(r0 + q * SUBLANES, r0 + q * SUBLANES + CONV_ROWS)
            tap = ubuf[rows_k, :] if r == 0 else ush[r - 1, rows_k, :]
            acc = acc + w_dw[k:k + 1, :] * tap
        vblocks.append(acc)
    ubuf[0:CONV_HIST, :] = ubuf[tl:tl + CONV_HIST, :]
    zg = _in_proj(hb, w_in, b_in, 2 * dc + dsm, 2 * dc + dsm + 2 * d)
    conv_out = _conv_tail(jnp.concatenate(vblocks, axis=0), cln_g, cln_b, w_pw, b_pw)

    zs = _in_proj(hb, w_in, b_in, 2 * dc, 2 * dc + dsm)
    zst = _bf(zs.T)
    tri = jnp.where(lax.broadcasted_iota(jnp.int32, (tl, tl), 0)
                    <= lax.broadcasted_iota(jnp.int32, (tl, tl), 1), 1.0, 0.0).astype(_BF16)
    def project(j):
        zj = zst[j * LANES:(j + 1) * LANES, :]
        return _dot(bre[j], zj), _dot(bim[j], zj)

    def scale_in(j, bu):
        r = slice(j * rows, (j + 1) * rows)
        eir, eii = einv_r[r, :], einv_i[r, :]
        return _bf(bu[0] * eir - bu[1] * eii), _bf(bu[0] * eii + bu[1] * eir)

    def prefix(v):
        return _dot(v[0], tri), _dot(v[1], tri)

    def scale_out(j, cum):
        r = slice(j * rows, (j + 1) * rows)
        cr = jnp.broadcast_to(car_r[r, LANES - 1:LANES], (rows, LANES))
        ci = jnp.broadcast_to(car_i[r, LANES - 1:LANES], (rows, LANES))
        lr, li = lamb_r[r, :], lamb_i[r, :]
        cum_r = cum[0] + jnp.concatenate([lr * cr - li * ci] * (tl // LANES), axis=1)
        cum_i = cum[1] + jnp.concatenate([lr * ci + li * cr] * (tl // LANES), axis=1)
        epr, epi = epow_r[r, :], epow_i[r, :]
        h_r = cum_r * epr - cum_i * epi
        h_i = cum_r * epi + cum_i * epr
        car_r[r, :] = h_r[:, tl - LANES:tl]
        car_i[r, :] = h_i[:, tl - LANES:tl]
        return jnp.concatenate([_bf(h_r), _bf(h_i)], axis=0)

    bu, v, cum, hcat, yts = {}, {}, {}, {}, {}
    for t in range(nj + 4):
        if t < nj:
            bu[t] = project(t)
        if 0 <= t - 2 < nj:
            cum[t - 2] = prefix(v.pop(t - 2))
        if 0 <= t - 4 < nj:
            yts[t - 4] = _dot(cc[t - 4], hcat.pop(t - 4))
        if 0 <= t - 1 < nj:
            v[t - 1] = scale_in(t - 1, bu.pop(t - 1))
        if 0 <= t - 3 < nj:
            hcat[t - 3] = scale_out(t - 3, cum.pop(t - 3))
    ssm_out = _ssm_tail(jnp.concatenate([yts[j] for j in range(nj)], axis=0).T, zs, dskip,
                        w_sv, b_sv, w_sg, b_sg)

    x1, h2, route, sched = _mixer_tail(x, zg, conv_out, ssm_out, g1, sh2, sc2, alpha,
                                w_out, b_out, ln1_g, ln1_b, wr_hi, wr_lo, b_r)
    x1_ref[0] = x1
    h2_ref[0] = _bf(h2)
    route_ref[0] = route
    sched_ref[0] = sched
    cnt_ref[0, 0] = _block_counts(route)

    @pl.when(c == last)
    def _():
        cstate_ref[0] = ubuf[CONV_HIST + tl - (CONV_WIDTH - 1):CONV_HIST + tl, :]
        sre_ref[0] = car_r[...].T[LANES - 1:LANES, :]
        sim_ref[0] = car_i[...].T[LANES - 1:LANES, :]


def _mixer_prompt(x, mod, wts, alpha):
    b, l, d = x.shape
    tl = TOKEN_BLOCK
    nc = l // tl
    dc = wts["w_pw"].shape[0]
    nstate = wts["lamb_r"].shape[0]
    names = ["w_in", "b_in", "w_dw", "b_dw", "cln_g", "cln_b", "w_pw", "b_pw", "bre", "bim", "cc",
             "einv_r", "einv_i", "epow_r", "epow_i", "lamb_r", "lamb_i", "dskip",
             "w_sv", "b_sv", "w_sg", "b_sg", "w_out", "b_out", "ln1_g", "ln1_b",
             "wr_hi", "wr_lo", "b_r"]
    consts = [wts[n] for n in names]
    tok = lambda bi, ci: (bi, ci, 0)
    seq = lambda bi, ci: (bi, 0, 0)
    return pl.pallas_call(
        functools.partial(_mixer_prompt_kernel, alpha=alpha),
        grid=(b, nc),
        in_specs=[pl.BlockSpec((1, tl, d), tok), pl.BlockSpec((1, 1, mod.shape[-1]), seq)]
                 + [_const_spec(a.shape) for a in consts],
        out_specs=[pl.BlockSpec((1, tl, d), tok), pl.BlockSpec((1, tl, d), tok),
                   pl.BlockSpec((1, tl, LANES), tok),
                   pl.BlockSpec((1, 2 * TOP_K, tl), lambda bi, ci: (bi, 0, ci)),
                   pl.BlockSpec((1, 1, 1, LANES), lambda bi, ci: (bi, ci, 0, 0)),
                   pl.BlockSpec((1, CONV_WIDTH - 1, dc), seq),
                   pl.BlockSpec((1, 1, nstate), seq), pl.BlockSpec((1, 1, nstate), seq)],
        out_shape=[jax.ShapeDtypeStruct((b, l, d), _F32), jax.ShapeDtypeStruct((b, l, d), _BF16),
                   jax.ShapeDtypeStruct((b, l, LANES), _F32),
                   jax.ShapeDtypeStruct((b, 2 * TOP_K, l), _F32),
                   jax.ShapeDtypeStruct((b, nc, 1, LANES), _F32),
                   jax.ShapeDtypeStruct((b, CONV_WIDTH - 1, dc), _F32),
                   jax.ShapeDtypeStruct((b, 1, nstate), _F32),
                   jax.ShapeDtypeStruct((b, 1, nstate), _F32)],
        scratch_shapes=[pltpu.VMEM((CONV_HIST + tl, dc), _F32),
                        pltpu.VMEM((SUBLANES - 1, tl + CONV_HIST - SUBLANES, dc), _F32),
                        pltpu.VMEM((nstate, LANES), _F32), pltpu.VMEM((nstate, LANES), _F32)],
        compiler_params=_params(("arbitrary", "arbitrary")),
        name="mixer_prompt",
    )(x, mod, *consts)


def _mixer_sample_kernel(x_ref, mod_ref, hist_ref, h0r_ref, h0i_ref,
                         w_in, b_in, w_dw, b_dw, cln_g, cln_b, w_pw, b_pw,
                         bre, bim, cc, lrow_r, lrow_i, dskip,
                         w_sv, b_sv, w_sg, b_sg, w_out, b_out, ln1_g, ln1_b, wr_hi, wr_lo, b_r,
                         x1_ref, h2_ref, route_ref, sched_ref, cnt_ref, cstate_ref, sre_ref, sim_ref, *, alpha):
    steps, nb, d = x_ref.shape
    dc = w_pw.shape[0]
    dsm = dskip.shape[1]
    nj = bre.shape[0]
    rows = bre.shape[1]
    nhist = CONV_WIDTH - 1
    x = x_ref[...].reshape(steps * nb, d)
    mod = jnp.concatenate([mod_ref[...]] * steps, axis=0)
    sh1, sc1, g1 = mod[:, 0:d], mod[:, d:2 * d], mod[:, 2 * d:3 * d]
    sh2, sc2 = mod[:, 3 * d:4 * d], mod[:, 4 * d:5 * d]
    hb = _bf(x * (1.0 + sc1) + sh1)

    u = _in_proj(hb, w_in, b_in, 0, dc) * _sigmoid(_in_proj(hb, w_in, b_in, dc, 2 * dc))
    us = [u[t * nb:(t + 1) * nb, :] for t in range(steps)]
    vs = []
    for t in range(steps):
        acc = jnp.broadcast_to(b_dw[...], (nb, dc))
        for i in range(t, nhist):
            acc = acc + w_dw[i - t:i - t + 1, :] * hist_ref[i]
        for s in range(t + 1):
            acc = acc + w_dw[nhist + s - t:nhist + s - t + 1, :] * us[s]
        vs.append(acc)
    for i in range(nhist):
        src = i + steps
        cstate_ref[i] = hist_ref[src] if src < nhist else us[src - nhist]
    conv_out = _conv_tail(jnp.concatenate(vs, axis=0), cln_g, cln_b, w_pw, b_pw)

    zs = _in_proj(hb, w_in, b_in, 2 * dc, 2 * dc + dsm)
    zsb = _bf(zs)
    h_r, h_i = h0r_ref[...], h0i_ref[...]
    lr, li = lrow_r[...], lrow_i[...]
    ys = []
    for t in range(steps):
        zt = zsb[t * nb:(t + 1) * nb, :]
        bu_r = jnp.concatenate([_dot_nt(zt[:, j * LANES:(j + 1) * LANES], bre[j]) for j in range(nj)], axis=1)
        bu_i = jnp.concatenate([_dot_nt(zt[:, j * LANES:(j + 1) * LANES], bim[j]) for j in range(nj)], axis=1)
        h_r, h_i = lr * h_r - li * h_i + bu_r, lr * h_i + li * h_r + bu_i
        hrb, hib = _bf(h_r), _bf(h_i)
        ys.append(jnp.concatenate(
            [_dot_nt(jnp.concatenate([hrb[:, j * rows:(j + 1) * rows], hib[:, j * rows:(j + 1) * rows]], axis=1),
                     cc[j]) for j in range(nj)], axis=1))
    sre_ref[...] = h_r
    sim_ref[...] = h_i
    ssm_out = _ssm_tail(jnp.concatenate(ys, axis=0), zs, dskip, w_sv, b_sv, w_sg, b_sg)

    zg = _in_proj(hb, w_in, b_in, 2 * dc + dsm, 2 * dc + dsm + 2 * d)
    x1, h2, route, sched = _mixer_tail(x, zg, conv_out, ssm_out, g1, sh2, sc2, alpha,
                                w_out, b_out, ln1_g, ln1_b, wr_hi, wr_lo, b_r)
    x1_ref[...] = x1
    h2_ref[...] = _bf(h2)
    route_ref[...] = route
    sched_ref[...] = sched
    for blk in range(cnt_ref.shape[0]):
        cnt_ref[blk] = _block_counts(route[blk * TOKEN_BLOCK:(blk + 1) * TOKEN_BLOCK, :])


def _mixer_sample(x_tm, mod, hist_tm, h0r, h0i, wts, alpha):
    steps, nb, d = x_tm.shape
    n = steps * nb
    dc = wts["w_pw"].shape[0]
    nstate = h0r.shape[1]
    names = ["w_in", "b_in", "w_dw", "b_dw", "cln_g", "cln_b", "w_pw", "b_pw", "bre", "bim", "cc",
             "lrow_r", "lrow_i", "dskip", "w_sv", "b_sv", "w_sg", "b_sg", "w_out", "b_out",
             "ln1_g", "ln1_b", "wr_hi", "wr_lo", "b_r"]
    args = [x_tm, mod, hist_tm, h0r, h0i] + [wts[k] for k in names]
    return pl.pallas_call(
        functools.partial(_mixer_sample_kernel, alpha=alpha),
        grid=(1,),
        in_specs=[_const_spec(a.shape) for a in args],
        out_specs=[_const_spec((n, d)), _const_spec((n, d)), _const_spec((n, LANES)),
                   _const_spec((2 * TOP_K, n)),
                   _const_spec((n // TOKEN_BLOCK, 1, LANES)),
                   _const_spec((CONV_WIDTH - 1, nb, dc)),
                   _const_spec((nb, nstate)), _const_spec((nb, nstate))],
        out_shape=[jax.ShapeDtypeStruct((n, d), _F32), jax.ShapeDtypeStruct((n, d), _BF16),
                   jax.ShapeDtypeStruct((n, LANES), _F32),
                   jax.ShapeDtypeStruct((2 * TOP_K, n), _F32),
                   jax.ShapeDtypeStruct((n // TOKEN_BLOCK, 1, LANES), _F32),
                   jax.ShapeDtypeStruct((CONV_WIDTH - 1, nb, dc), _F32),
                   jax.ShapeDtypeStruct((nb, nstate), _F32), jax.ShapeDtypeStruct((nb, nstate), _F32)],
        compiler_params=_params(("arbitrary",)),
        name="mixer_sample",
    )(*args)


def _round_up(x, m):
    return (x + m - 1) // m * m


def _plan_sizes(n_tokens):
    nb = n_tokens // TOKEN_BLOCK
    r_max = _round_up(TOP_K * TOKEN_BLOCK + 2 * N_EXPERTS * (SEG_PAD - 1), 2 * LANES)
    rows_max = n_tokens * TOP_K + N_EXPERTS * (EXPERT_TILE - 1)
    nt_max = -(-rows_max // EXPERT_TILE)
    return nb, r_max, nt_max


def _routing_plan(cnt, nt_max):
    nb, ne = cnt.shape
    before = jnp.cumsum(cnt, axis=0) - cnt
    tot = jnp.sum(cnt, axis=0)
    cin = before % SEG_PAD
    last = (jnp.arange(nb, dtype=jnp.int32) == nb - 1)[:, None]
    active = (cnt > 0) | (last & (cin > 0))
    seg = jnp.where(active, _round_up(cin + cnt, SEG_PAD), 0)
    loc = jnp.cumsum(seg, axis=1) - seg
    totpad = _round_up(tot, EXPERT_TILE)
    eend = jnp.cumsum(totpad)
    estart = eend - totpad
    goff = estart[None, :] + before - cin
    nfull = jnp.where(last, seg, jnp.where(active, (cin + cnt) // SEG_PAD * SEG_PAD, 0))
    cout = jnp.where(last | ~active, 0, (cin + cnt) % SEG_PAD)
    n_tiles = (eend[-1] // EXPERT_TILE).astype(jnp.int32)
    tiles = jnp.minimum(jnp.arange(nt_max, dtype=jnp.int32), n_tiles - 1)
    te = jnp.sum(((eend // EXPERT_TILE)[None, :] <= tiles[:, None]).astype(jnp.int32), axis=1)
    te = jnp.minimum(te, ne - 1)
    tot16 = _round_up(tot, SEG_PAD)
    return dict(seg=seg, loc=loc, goff=goff, nfull=nfull, cin=jnp.where(active, cin, 0), cout=cout,
                used=jnp.sum(seg, axis=1).astype(jnp.int32), sent=jnp.sum(nfull, axis=1).astype(jnp.int32),
                n_tiles=n_tiles.reshape(1), te=te,
                npad=(totpad - tot16).astype(jnp.int32), pad_base=(estart + tot16).astype(jnp.int32))


def _dest_rows(sched, first):
    row = lax.broadcasted_iota(jnp.int32, first.shape, 0).astype(_F32)
    return [sched[TOP_K + k:TOP_K + k + 1, :]
            + jnp.sum(jnp.where(row == sched[k:k + 1, :], first, 0.0), axis=0, keepdims=True)
            for k in range(TOP_K)]


def _typical_rows(r_max):
    return min(r_max, _round_up(TOP_K * TOKEN_BLOCK + N_EXPERTS * SEG_PAD, 2 * LANES))


def _dispatch_kernel(nfull, sloc, goff, cin, cout, used, sent, npad, pad_base, hp_ref, hs_ref, sp_ref, ss_ref,
                     first_ref, xs_hbm, d4_ref, buf, stage, zbuf, sem, zsem, *, nbp):
    i = pl.program_id(0)
    nblk = pl.num_programs(0)
    slot = lax.rem(i, 2)
    tb = hp_ref.shape[0]
    r_max = buf.shape[1]
    is_p = i < nbp
    h = jnp.where(is_p, hp_ref[...], hs_ref[...])
    sched = jnp.where(is_p, sp_ref[0], ss_ref[...])
    d4t = _dest_rows(sched, jnp.concatenate([first_ref[0]] * (tb // LANES), axis=1))
    row8 = lax.broadcasted_iota(jnp.int32, (SUBLANES, tb), 0)
    d4t8 = jnp.zeros((SUBLANES, tb), _F32)
    for k in range(TOP_K):
        d4t8 = jnp.where(row8 == k, d4t[k], d4t8)
    d4_ref[...] = jnp.concatenate([d4t8, jnp.zeros((LANES - SUBLANES, tb), _F32)], axis=0).T

    def compact(rows):
        row = lax.broadcasted_iota(jnp.int32, (rows, tb), 0).astype(_F32)
        p = jnp.zeros((rows, tb), _F32)
        for k in range(TOP_K):
            p = jnp.where(row == d4t[k], 1.0, p)
        buf[slot, 0:rows, :] = _bf(_dot(_bf(p), h))

    r_typ = _typical_rows(r_max)

    @pl.when(i == 0)
    def _():
        buf[...] = jnp.zeros(buf.shape, buf.dtype)
        stage[...] = jnp.zeros(stage.shape, stage.dtype)

    @pl.when(used[i] <= r_typ)
    def _():
        compact(r_typ)

    @pl.when(used[i] > r_typ)
    def _():
        compact(r_max)

    def seg_copy(blk, s, e):
        n = pl.multiple_of(nfull[blk * N_EXPERTS + e], SEG_PAD)
        src = pl.multiple_of(sloc[blk * N_EXPERTS + e], SEG_PAD)
        dst = pl.multiple_of(goff[blk * N_EXPERTS + e], SEG_PAD)
        return pltpu.make_async_copy(buf.at[s, pl.ds(src, n), :], xs_hbm.at[pl.ds(dst, n), :], sem.at[s])

    def pad_copy(e):
        n = pl.multiple_of(npad[e], SEG_PAD)
        dst = pl.multiple_of(pad_base[e], SEG_PAD)
        return pltpu.make_async_copy(zbuf.at[pl.ds(0, n), :], xs_hbm.at[pl.ds(dst, n), :], zsem.at[0])

    def for_segments(blk, fn):
        for e in range(N_EXPERTS):
            @pl.when(nfull[blk * N_EXPERTS + e] > 0)
            def _(e=e):
                fn(e)

    for e in range(N_EXPERTS):
        first = pl.ds(pl.multiple_of(sloc[i * N_EXPERTS + e], SEG_PAD), SEG_PAD)
        carried = jnp.where(cin[i * N_EXPERTS + e] > 0, stage[e], jnp.zeros_like(stage[e]))
        buf[slot, first, :] = buf[slot, first, :] + carried

    for_segments(i, lambda e: seg_copy(i, slot, e).start())

    for e in range(N_EXPERTS):
        rest = pl.multiple_of(sloc[i * N_EXPERTS + e] + nfull[i * N_EXPERTS + e], SEG_PAD)
        stage[e] = jnp.where(cout[i * N_EXPERTS + e] > 0, buf[slot, pl.ds(rest, SEG_PAD), :], stage[e])

    def wait_block(blk, s):
        @pl.when(sent[blk] > 0)
        def _():
            n = pl.multiple_of(sent[blk], SEG_PAD)
            pltpu.make_async_copy(buf.at[s, pl.ds(0, n), :], xs_hbm.at[pl.ds(0, n), :], sem.at[s]).wait()

    @pl.when(i > 0)
    def _():
        wait_block(i - 1, 1 - slot)

    @pl.when(i == nblk - 1)
    def _():
        zbuf[...] = jnp.zeros(zbuf.shape, zbuf.dtype)
        for e in range(N_EXPERTS):
            @pl.when(npad[e] > 0)
            def _(e=e):
                pad_copy(e).start()
        for e in range(N_EXPERTS):
            @pl.when(npad[e] > 0)
            def _(e=e):
                pad_copy(e).wait()
        wait_block(i, slot)


def _dispatch(plan, h2p, h2s, schedp, scheds, r_max, nt_max, blocks_per_seq):
    tb = TOKEN_BLOCK
    d = h2p.shape[1]
    nbp, nbs = h2p.shape[0] // tb, h2s.shape[0] // tb
    nb = nbp + nbs
    first_row = (plan["loc"] + plan["cin"]).astype(_F32)
    first = jnp.broadcast_to(first_row[:, :, None], (nb, N_EXPERTS, LANES))
    pidx = lambda i, *_: (jnp.minimum(i, nbp - 1), 0)
    sidx = lambda i, *_: (jnp.maximum(i - nbp, 0), 0)
    grid_spec = pltpu.PrefetchScalarGridSpec(
        num_scalar_prefetch=9,
        grid=(nb,),
        in_specs=[pl.BlockSpec((tb, d), pidx), pl.BlockSpec((tb, d), sidx),
                  pl.BlockSpec((1, 2 * TOP_K, tb), lambda i, *_: (jnp.minimum(i, nbp - 1) // blocks_per_seq, 0,
                                                                   jnp.minimum(i, nbp - 1) % blocks_per_seq)),
                  pl.BlockSpec((2 * TOP_K, tb), lambda i, *_: (0, jnp.maximum(i - nbp, 0))),
                  pl.BlockSpec((1, N_EXPERTS, LANES), lambda i, *_: (i, 0, 0))],
        out_specs=[pl.BlockSpec(memory_space=pl.ANY), pl.BlockSpec((tb, LANES), lambda i, *_: (i, 0))],
        scratch_shapes=[pltpu.VMEM((2, r_max, d), _BF16), pltpu.VMEM((N_EXPERTS, SEG_PAD, d), _BF16),
                        pltpu.VMEM((EXPERT_TILE, d), _BF16),
                        pltpu.SemaphoreType.DMA((2,)), pltpu.SemaphoreType.DMA((1,))])
    flat = lambda a: a.astype(jnp.int32).reshape(-1)
    return pl.pallas_call(
        functools.partial(_dispatch_kernel, nbp=nbp),
        grid_spec=grid_spec,
        out_shape=[jax.ShapeDtypeStruct((nt_max * EXPERT_TILE, d), _BF16),
                   jax.ShapeDtypeStruct((nb * tb, LANES), _F32)],
        compiler_params=_params(("arbitrary",)),
        name="dispatch",
    )(flat(plan["nfull"]), flat(plan["loc"]), flat(plan["goff"]), flat(plan["cin"]), flat(plan["cout"]),
      plan["used"], plan["sent"], plan["npad"], plan["pad_base"], h2p, h2s, schedp, scheds, first)


def _expert_kernel(te, n_tiles, x_ref, w1_ref, b1_ref, w2_ref, b2_ref, y_ref, w1b, w2b):
    i = pl.program_id(0)
    dff = w2_ref.shape[1]

    @pl.when(i < n_tiles[0])
    def _():
        prev = te[jnp.maximum(i - 1, 0)]

        @pl.when((i == 0) | (te[i] != prev))
        def _():
            w1b[...] = _bf(w1_ref[0])
            w2b[...] = _bf(w2_ref[0])

        gu = _dot(x_ref[...], w1b[...]) + b1_ref[0]
        g = jnp.minimum(gu[:, :dff], SWIGLU_LIMIT)
        up = jnp.clip(gu[:, dff:], -SWIGLU_LIMIT, SWIGLU_LIMIT)
        act = g * _sigmoid(SWIGLU_ALPHA * g) * (up + 1.0)
        y_ref[...] = _bf(_dot(_bf(act), w2b[...]) + b2_ref[0])


def _experts(plan, xs, w1, b1, w2, b2, nt_max):
    ne, d, dff2 = w1.shape
    dff = w2.shape[1]
    tm = EXPERT_TILE
    tile = lambda i, te, nt: (jnp.minimum(i, nt[0] - 1), 0)
    wsel = lambda i, te, nt: (te[i], 0, 0)
    grid_spec = pltpu.PrefetchScalarGridSpec(
        num_scalar_prefetch=2,
        grid=(nt_max,),
        in_specs=[pl.BlockSpec((tm, d), tile),
                  pl.BlockSpec((1, d, dff2), wsel), pl.BlockSpec((1, 1, dff2), wsel),
                  pl.BlockSpec((1, dff, d), wsel), pl.BlockSpec((1, 1, d), wsel)],
        out_specs=pl.BlockSpec((tm, d), tile),
        scratch_shapes=[pltpu.VMEM((d, dff2), _BF16), pltpu.VMEM((dff, d), _BF16)])
    return pl.pallas_call(
        _expert_kernel,
        grid_spec=grid_spec,
        out_shape=jax.ShapeDtypeStruct(xs.shape, _BF16),
        compiler_params=_params(("arbitrary",)),
        name="experts",
    )(plan["te"], plan["n_tiles"], xs, w1, b1.reshape(ne, 1, dff2), w2, b2.reshape(ne, 1, d))


def _combine_kernel(seg, sloc, goff, used, rp_ref, rs_ref, d4_ref, x1p_ref, x1s_ref, g2p_ref, g2s_ref,
                    ln2_g, ln2_b, ys_hbm, yp_ref, ysm_ref, buf, fbuf, sem, *, nbp, alpha):
    i = pl.program_id(0)
    nblk = pl.num_programs(0)
    slot = lax.rem(i, 2)
    tb = rp_ref.shape[0]
    r_max = buf.shape[1]

    def seg_copy(blk, s, e):
        n = pl.multiple_of(seg[blk * N_EXPERTS + e], SEG_PAD)
        src = pl.multiple_of(goff[blk * N_EXPERTS + e], SEG_PAD)
        dst = pl.multiple_of(sloc[blk * N_EXPERTS + e], SEG_PAD)
        return pltpu.make_async_copy(ys_hbm.at[pl.ds(src, n), :], buf.at[s, pl.ds(dst, n), :], sem.at[s])

    def for_segments(blk, fn):
        for e in range(N_EXPERTS):
            @pl.when(seg[blk * N_EXPERTS + e] > 0)
            def _(e=e):
                fn(e)

    @pl.when(i == 0)
    def _():
        buf[...] = jnp.zeros(buf.shape, buf.dtype)
        for_segments(0, lambda e: seg_copy(0, 0, e).start())

    @pl.when(i + 1 < nblk)
    def _():
        for_segments(i + 1, lambda e: seg_copy(i + 1, 1 - slot, e).start())

    @pl.when(used[i] > 0)
    def _():
        n = pl.multiple_of(used[i], SEG_PAD)
        pltpu.make_async_copy(ys_hbm.at[pl.ds(0, n), :], buf.at[slot, pl.ds(0, n), :], sem.at[slot]).wait()

    is_p = i < nbp
    route = jnp.where(is_p, rp_ref[...], rs_ref[...])
    d4 = d4_ref[...]

    def gather(rows):
        col = lax.broadcasted_iota(jnp.int32, (tb, rows), 1).astype(_F32)
        pg = jnp.zeros((tb, rows), _F32)
        for k in range(TOP_K):
            gate = route[:, ROUTE_GATE + k:ROUTE_GATE + k + 1]
            pg = jnp.where(col == d4[:, k:k + 1], gate, pg)
        fbuf[...] = _dot(_bf(pg), buf[slot, 0:rows, :])

    r_typ = _typical_rows(r_max)

    @pl.when(used[i] <= r_typ)
    def _():
        gather(r_typ)

    @pl.when(used[i] > r_typ)
    def _():
        gather(r_max)

    f = fbuf[...]
    x1 = jnp.where(is_p, x1p_ref[...], x1s_ref[...])
    g2s = jnp.concatenate([g2s_ref[...]] * (tb // g2s_ref.shape[0]), axis=0)
    g2 = jnp.where(is_p, jnp.broadcast_to(g2p_ref[0], g2s.shape), g2s)
    y = _layer_norm(alpha * x1 + g2 * f, ln2_g[...], ln2_b[...])

    @pl.when(is_p)
    def _():
        yp_ref[...] = y

    @pl.when(jnp.logical_not(is_p))
    def _():
        ysm_ref[...] = y


def _combine(plan, ys, rp, rs, d4, x1p, x1s, modp, mods, ln2_g, ln2_b, r_max, alpha, blocks_per_seq):
    tb = TOKEN_BLOCK
    d = x1p.shape[1]
    nbp, nbs = x1p.shape[0] // tb, x1s.shape[0] // tb
    nb = nbp + nbs
    pidx = lambda i, *_: (jnp.minimum(i, nbp - 1), 0)
    sidx = lambda i, *_: (jnp.maximum(i - nbp, 0), 0)
    g2_lane_block = 5
    flat = lambda a: a.astype(jnp.int32).reshape(-1)
    grid_spec = pltpu.PrefetchScalarGridSpec(
        num_scalar_prefetch=4,
        grid=(nb,),
        in_specs=[pl.BlockSpec((tb, LANES), pidx), pl.BlockSpec((tb, LANES), sidx),
                  pl.BlockSpec((tb, LANES), lambda i, *_: (i, 0)),
                  pl.BlockSpec((tb, d), pidx), pl.BlockSpec((tb, d), sidx),
                  pl.BlockSpec((1, 1, d), lambda i, *_: (jnp.minimum(i, nbp - 1) // blocks_per_seq, 0,
                                                         g2_lane_block)),
                  pl.BlockSpec((mods.shape[0], d), lambda i, *_: (0, g2_lane_block)),
                  pl.BlockSpec((1, d), lambda i, *_: (0, 0)), pl.BlockSpec((1, d), lambda i, *_: (0, 0)),
                  pl.BlockSpec(memory_space=pl.ANY)],
        out_specs=[pl.BlockSpec((tb, d), pidx), pl.BlockSpec((tb, d), sidx)],
        scratch_shapes=[pltpu.VMEM((2, r_max, d), _BF16), pltpu.VMEM((tb, d), _F32),
                        pltpu.SemaphoreType.DMA((2,))])
    return pl.pallas_call(
        functools.partial(_combine_kernel, nbp=nbp, alpha=alpha),
        grid_spec=grid_spec,
        out_shape=[jax.ShapeDtypeStruct(x1p.shape, _F32), jax.ShapeDtypeStruct(x1s.shape, _F32)],
        compiler_params=_params(("arbitrary",)),
        name="combine",
    )(flat(plan["seg"]), flat(plan["loc"]), flat(plan["goff"]), plan["used"],
      rp, rs, d4, x1p, x1s, modp, mods, ln2_g, ln2_b, ys)


def _complex_powers(zr, zi, n):
    low_bits = 4
    low = 1 << low_bits
    if n <= low or n % low:
        return _bit_powers(zr, zi, n)
    lr, li = _bit_powers(zr, zi, low)
    sr, si = zr, zi
    for _ in range(low_bits):
        sr, si = sr * sr - si * si, 2.0 * sr * si
    hr, hi = _bit_powers(sr, si, n // low)
    pr = hr[:, :, None] * lr[:, None, :] - hi[:, :, None] * li[:, None, :]
    pi = hr[:, :, None] * li[:, None, :] + hi[:, :, None] * lr[:, None, :]
    return pr.reshape(-1, n), pi.reshape(-1, n)


def _bit_powers(zr, zi, n):
    k = jnp.arange(n, dtype=jnp.int32)[None, :]
    pr = jnp.ones((zr.shape[0], n), _F32)
    pi = jnp.zeros((zr.shape[0], n), _F32)
    sr, si = zr[:, None], zi[:, None]
    bit = 1
    while bit < n:
        on = (k & bit) != 0
        mr, mi = jnp.where(on, sr, 1.0), jnp.where(on, si, 0.0)
        pr, pi = pr * mr - pi * mi, pr * mi + pi * mr
        sr, si = sr * sr - si * si, 2.0 * sr * si
        bit *= 2
    return pr, pi


def _block_diag(m):
    g, a, b = m.shape
    gb = GROUPS_PER_BLOCK
    m = m.reshape(g // gb, gb, a, 1, b)
    eye = jnp.eye(gb, dtype=m.dtype).reshape(1, gb, 1, gb, 1)
    return (m * eye).reshape(g // gb, gb * a, gb * b)


def _layer_weights(p, q):
    d = p["w_in"].shape[0]
    row = lambda v: v.reshape(1, -1).astype(_F32)
    lam_re, lam_im = p["lam_re"].astype(_F32), p["lam_im"].astype(_F32)
    dt = jnp.exp(p["log_dt"].astype(_F32))[:, None]
    mag = jnp.exp(lam_re * dt)
    lbr, lbi = mag * jnp.cos(lam_im * dt), mag * jnp.sin(lam_im * dt)
    den = lam_re * lam_re + lam_im * lam_im
    nr, ni = lbr - 1.0, lbi
    fr, fi = (nr * lam_re + ni * lam_im) / den, (ni * lam_re - nr * lam_im) / den
    b_re, b_im = p["b_re"].astype(_F32), p["b_im"].astype(_F32)
    bbr = fr[..., None] * b_re - fi[..., None] * b_im
    bbi = fr[..., None] * b_im + fi[..., None] * b_re
    nstate = lam_re.size
    mod2 = lbr * lbr + lbi * lbi
    flat = lambda v: v.reshape(nstate)
    epr, epi = _complex_powers(flat(lbr), flat(lbi), q)
    eir, eii = _complex_powers(flat(lbr / mod2), flat(-lbi / mod2), q)
    table = lambda t: t
    bcast = lambda v: jnp.broadcast_to(v.reshape(nstate, 1), (nstate, LANES))
    c_re, c_im = p["c_re"].astype(_F32), p["c_im"].astype(_F32)
    w_r = p["w_router"].astype(_F32)
    w_r = jnp.pad(w_r, ((0, 0), (0, LANES - w_r.shape[1])))
    wr_hi = _bf(w_r)
    return dict(
        w_in=_bf(p["w_in"]), b_in=row(p["b_in"]), w_dw=p["w_dw"].astype(_F32), b_dw=row(p["b_dw"]),
        cln_g=row(p["conv_ln_g"]), cln_b=row(p["conv_ln_b"]), w_pw=_bf(p["w_pw"]), b_pw=row(p["b_pw"]),
        bre=_bf(_block_diag(bbr)), bim=_bf(_block_diag(bbi)),
        cc=_bf(jnp.concatenate([_block_diag(c_re), _block_diag(-c_im)], axis=2)),
        einv_r=table(eir), einv_i=table(eii), epow_r=table(epr), epow_i=table(epi),
        lamb_r=bcast(lbr), lamb_i=bcast(lbi), lrow_r=lbr.reshape(1, nstate), lrow_i=lbi.reshape(1, nstate),
        dskip=row(p["d_skip"]),
        w_sv=_bf(p["w_sv"]), b_sv=row(p["b_sv"]), w_sg=_bf(p["w_sg"]), b_sg=row(p["b_sg"]),
        w_out=_bf(p["w_out"]), b_out=row(p["b_out"]), ln1_g=row(p["ln1_g"]), ln1_b=row(p["ln1_b"]),
        wr_hi=wr_hi, wr_lo=_bf(w_r - wr_hi.astype(_F32)),
        b_r=p["b_router"].astype(_F32).reshape(-1, 1),
        ln2_g=row(p["ln2_g"]), ln2_b=row(p["ln2_b"]))


def _layer(xp, xs_tm, c_all, hist_tm, h0r, h0i, p, alpha):
    b, l, d = xp.shape
    steps, nbs, _ = xs_tm.shape
    tb = TOKEN_BLOCK
    assert l % tb == 0 and l >= CONV_WIDTH - 1 and (steps * nbs) % tb == 0 and tb % nbs == 0
    wts = _layer_weights(p, tb)
    mod = _ada(c_all, p["w_ada"].astype(_F32), p["b_ada"].astype(_F32))
    modp, mods = mod[:b].reshape(b, 1, -1), mod[b:]

    x1p, h2p, rp, schedp, cntp, conv_p, sre_p, sim_p = _mixer_prompt(xp, modp, wts, alpha)
    x1s, h2s, rs, scheds, cnts, conv_s, sre_s, sim_s = _mixer_sample(xs_tm, mods, hist_tm, h0r, h0i, wts, alpha)

    n_tok = b * l + steps * nbs
    nb, r_max, nt_max = _plan_sizes(n_tok)
    cnt = jnp.concatenate([cntp.reshape(-1, LANES), cnts.reshape(-1, LANES)], axis=0)[:, :N_EXPERTS]
    plan = _routing_plan(cnt.astype(jnp.int32), nt_max)

    flat = lambda a: a.reshape(b * l, a.shape[-1])
    xs_sorted, d4 = _dispatch(plan, flat(h2p), h2s, schedp, scheds, r_max, nt_max, l // tb)
    ys_sorted = _experts(plan, xs_sorted, p["w1"], p["b1"], p["w2"], p["b2"], nt_max)
    yp, ysm = _combine(plan, ys_sorted, flat(rp), rs, d4, flat(x1p), x1s, modp, mods,
                       wts["ln2_g"], wts["ln2_b"], r_max, alpha, l // tb)
    return (yp.reshape(b, l, d), ysm.reshape(steps, nbs, d), conv_p, sre_p[:, 0], sim_p[:, 0],
            conv_s, sre_s, sim_s)


def kernel(x_prompt, x_sample, state_conv, state_ssm_re, state_ssm_im, c_prompt, c_sample, w_ada, b_ada, w_in, b_in, w_dw, b_dw, conv_ln_g, conv_ln_b, w_pw, b_pw, lam_re, lam_im, log_dt, b_re, b_im, c_re, c_im, d_skip, w_sv, b_sv, w_sg, b_sg, w_out, b_out, ln1_g, ln1_b, w_router, b_router, w1, b1, w2, b2, ln2_g, ln2_b):
    stacked = dict(w_ada=w_ada, b_ada=b_ada, w_in=w_in, b_in=b_in, w_dw=w_dw, b_dw=b_dw,
                   conv_ln_g=conv_ln_g, conv_ln_b=conv_ln_b, w_pw=w_pw, b_pw=b_pw, lam_re=lam_re,
                   lam_im=lam_im, log_dt=log_dt, b_re=b_re, b_im=b_im, c_re=c_re, c_im=c_im,
                   d_skip=d_skip, w_sv=w_sv, b_sv=b_sv, w_sg=w_sg, b_sg=b_sg, w_out=w_out, b_out=b_out,
                   ln1_g=ln1_g, ln1_b=ln1_b, w_router=w_router, b_router=b_router, w1=w1, b1=b1,
                   w2=w2, b2=b2, ln2_g=ln2_g, ln2_b=ln2_b)
    depth = w_ada.shape[0]
    alpha = (2 * depth) ** 0.25
    b = x_prompt.shape[0]
    nbs = x_sample.shape[0]
    g, s = state_ssm_re.shape[2], state_ssm_re.shape[3]
    xp = x_prompt
    xs_tm = jnp.transpose(x_sample, (1, 0, 2))
    c_all = jnp.concatenate([c_prompt, c_sample], axis=0)
    conv_ps, re_ps, im_ps, conv_ss, re_ss, im_ss = [], [], [], [], [], []
    for layer in range(depth):
        p = {k: v[layer] for k, v in stacked.items()}
        hist_tm = jnp.transpose(state_conv[layer], (1, 0, 2))
        h0r = state_ssm_re[layer].reshape(nbs, g * s)
        h0i = state_ssm_im[layer].reshape(nbs, g * s)
        xp, xs_tm, conv_p, sre_p, sim_p, conv_s, sre_s, sim_s = _layer(
            xp, xs_tm, c_all, hist_tm, h0r, h0i, p, alpha)
        conv_ps.append(conv_p.astype(state_conv.dtype))
        re_ps.append(sre_p.reshape(b, g, s))
        im_ps.append(sim_p.reshape(b, g, s))
        conv_ss.append(jnp.transpose(conv_s, (1, 0, 2)).astype(state_conv.dtype))
        re_ss.append(sre_s.reshape(nbs, g, s))
        im_ss.append(sim_s.reshape(nbs, g, s))
    return (xp, jnp.transpose(xs_tm, (1, 0, 2)), jnp.stack(conv_ps), jnp.stack(re_ps), jnp.stack(im_ps),
            jnp.stack(conv_ss), jnp.stack(re_ss), jnp.stack(im_ss))
```

```python
import functools

import jax
import jax.numpy as jnp
from jax import lax
from jax.experimental import pallas as pl
from jax.experimental.pallas import tpu as pltpu

CONV_WIDTH = 31
SSM_GROUP = 16
N_EXPERTS = 32
TOP_K = 4
SWIGLU_LIMIT = 7.0
SWIGLU_ALPHA = 1.702
LN_EPS = 1e-5

LANES = 128
SUBLANES = 8
TOKEN_BLOCK = 256
EXPERT_TILE = 512
SEG_PAD = 16
GROUPS_PER_BLOCK = LANES // SSM_GROUP
CONV_HIST = 32
CONV_ROWS = 32
ROUTE_GATE = N_EXPERTS + TOP_K
VMEM_LIMIT = 56 * 1024 * 1024

_F32 = jnp.float32
_BF16 = jnp.bfloat16


def _bf(x):
    return x.astype(_BF16)


def _dot(a, b):
    return jnp.dot(a, b, preferred_element_type=_F32)


def _dot_nt(a, b):
    return lax.dot_general(a, b, (((1,), (1,)), ((), ())), preferred_element_type=_F32)


def _split(x):
    hi = _bf(x)
    lo = _bf(x - hi.astype(_F32))
    return hi, lo


def _sigmoid(x):
    return 1.0 / (1.0 + jnp.exp(-x))


def _gelu_tanh(x):
    return 0.5 * x * (1.0 + jnp.tanh(0.7978845608028654 * (x + 0.044715 * (x * x * x))))


def _layer_norm(x, g, b):
    mu = jnp.mean(x, axis=-1, keepdims=True)
    xc = x - mu
    var = jnp.mean(xc * xc, axis=-1, keepdims=True)
    return xc * lax.rsqrt(var + LN_EPS) * g + b


def _const_spec(shape):
    nd = len(shape)
    return pl.BlockSpec(shape, lambda *_: (0,) * nd, pipeline_mode=pl.Buffered(1))


def _params(sem):
    return pltpu.CompilerParams(dimension_semantics=sem, vmem_limit_bytes=VMEM_LIMIT)


def _ada_kernel(c_ref, w_ref, b_ref, o_ref):
    c = c_ref[...]
    s_hi, s_lo = _split(c * _sigmoid(c))
    w_hi, w_lo = _split(w_ref[...])
    o_ref[...] = _dot(s_hi, w_hi) + _dot(s_lo, w_hi) + _dot(s_hi, w_lo) + b_ref[...]


def _ada(c, w, b):
    n, d = c.shape
    cols = w.shape[1]
    tn = d
    return pl.pallas_call(
        _ada_kernel,
        grid=(cols // tn,),
        in_specs=[pl.BlockSpec((n, d), lambda j: (0, 0)),
                  pl.BlockSpec((d, tn), lambda j: (0, j)),
                  pl.BlockSpec((1, tn), lambda j: (0, j))],
        out_specs=pl.BlockSpec((n, tn), lambda j: (0, j)),
        out_shape=jax.ShapeDtypeStruct((n, cols), _F32),
        compiler_params=_params(("arbitrary",)),
        name="ada",
    )(c, w, b.reshape(1, cols))


def _in_proj(hb, w_in, b_in, lo, hi):
    return _dot(hb, w_in[:, lo:hi]) + b_in[:, lo:hi]


def _conv_tail(v, cln_g, cln_b, w_pw, b_pw):
    v = _layer_norm(v, cln_g[...], cln_b[...])
    v = v * _sigmoid(v)
    return _dot(_bf(v), w_pw[...]) + b_pw[...]


def _ssm_tail(y, zs, dskip, w_sv, b_sv, w_sg, b_sg):
    yg = _bf(_gelu_tanh(y + dskip[...] * zs))
    return (_dot(yg, w_sv[...]) + b_sv[...]) * _sigmoid(_dot(yg, w_sg[...]) + b_sg[...])


def _route(h2, wr_hi, wr_lo, b_r):
    n = h2.shape[0]
    ne = b_r.shape[0]
    h_hi, h_lo = _split(h2)
    logits = (_dot(h_hi, wr_hi[...]) + _dot(h_lo, wr_hi[...]) + _dot(h_hi, wr_lo[...])).T[:ne, :] + b_r[...]
    row = lax.broadcasted_iota(jnp.int32, (ne, n), 0).astype(_F32)
    neg = jnp.float32(-jnp.inf)
    cur = logits
    vals, idxs = [], []
    for _ in range(TOP_K):
        m = jnp.max(cur, axis=0, keepdims=True)
        idx = jnp.min(jnp.where(cur == m, row, float(ne)), axis=0, keepdims=True)
        vals.append(m)
        idxs.append(idx)
        cur = jnp.where(row == idx, neg, cur)
    es = [jnp.exp(v - vals[0]) for v in vals]
    tot = es[0]
    for e in es[1:]:
        tot = tot + e
    inv = 1.0 / tot
    selected = jnp.zeros((ne, n), _F32)
    row8 = lax.broadcasted_iota(jnp.int32, (2 * TOP_K, n), 0)
    extra = jnp.zeros((2 * TOP_K, n), _F32)
    for k in range(TOP_K):
        selected = jnp.where(row == idxs[k], 1.0, selected)
        extra = jnp.where(row8 == TOP_K + k, es[k] * inv, extra)
    rest = jnp.zeros((LANES - ne - 2 * TOP_K, n), _F32)
    tb = min(n, TOKEN_BLOCK)
    earlier = jnp.where(lax.broadcasted_iota(jnp.int32, (tb, tb), 0)
                        < lax.broadcasted_iota(jnp.int32, (tb, tb), 1), 1.0, 0.0).astype(_BF16)
    rank = jnp.concatenate([_dot(_bf(selected[:, c0:c0 + tb]), earlier) for c0 in range(0, n, tb)], axis=1)
    sched = jnp.zeros((2 * TOP_K, n), _F32)
    for k in range(TOP_K):
        rank_k = jnp.sum(jnp.where(row == idxs[k], rank, 0.0), axis=0, keepdims=True)
        sched = jnp.where(row8 == k, idxs[k], jnp.where(row8 == TOP_K + k, rank_k, sched))
    return jnp.concatenate([selected, extra, rest], axis=0).T, sched


def _members(route):
    lane = lax.broadcasted_iota(jnp.int32, route.shape, 1)
    return jnp.where(lane < N_EXPERTS, route, 0.0)


def _block_counts(route):
    return jnp.sum(_members(route), axis=0, keepdims=True)


def _mixer_tail(x, zg, conv_out, ssm_out, g1, sh2, sc2, alpha, w_out, b_out,
                ln1_g, ln1_b, wr_hi, wr_lo, b_r):
    d = x.shape[1]
    merged = _sigmoid(zg[:, :d]) * conv_out + _sigmoid(zg[:, d:]) * ssm_out
    m = _dot(_bf(merged), w_out[...]) + b_out[...]
    x1 = _layer_norm(alpha * x + g1 * m, ln1_g[...], ln1_b[...])
    h2 = x1 * (1.0 + sc2) + sh2
    return (x1, h2) + _route(h2, wr_hi, wr_lo, b_r)


def _mixer_prompt_kernel(x_ref, mod_ref, w_in, b_in, w_dw, b_dw, cln_g, cln_b, w_pw, b_pw,
                         bre, bim, cc, einv_r, einv_i, epow_r, epow_i, lamb_r, lamb_i, dskip,
                         w_sv, b_sv, w_sg, b_sg, w_out, b_out, ln1_g, ln1_b, wr_hi, wr_lo, b_r,
                         x1_ref, h2_ref, route_ref, sched_ref, cnt_ref, cstate_ref, sre_ref, sim_ref,
                         ubuf, ush, car_r, car_i, *, alpha):
    c = pl.program_id(1)
    last = pl.num_programs(1) - 1
    tl, d = x_ref.shape[1], x_ref.shape[2]
    dc = w_pw.shape[0]
    dsm = dskip.shape[1]
    nj = bre.shape[0]
    rows = bre.shape[1]

    @pl.when(c == 0)
    def _():
        ubuf[0:CONV_HIST, :] = jnp.zeros((CONV_HIST, dc), _F32)
        car_r[...] = jnp.zeros(car_r.shape, _F32)
        car_i[...] = jnp.zeros(car_i.shape, _F32)

    x = x_ref[0]
    mod = mod_ref[0]
    sh1, sc1, g1 = mod[:, 0:d], mod[:, d:2 * d], mod[:, 2 * d:3 * d]
    sh2, sc2 = mod[:, 3 * d:4 * d], mod[:, 4 * d:5 * d]
    hb = _bf(x * (1.0 + sc1) + sh1)

    u = _in_proj(hb, w_in, b_in, 0, dc) * _sigmoid(_in_proj(hb, w_in, b_in, dc, 2 * dc))
    ubuf[CONV_HIST:CONV_HIST + tl, :] = u
    span = ush.shape[1]
    for r in range(1, SUBLANES):
        ush[r - 1] = ubuf[r:r + span, :]

    vblocks = []
    for r0 in range(0, tl, CONV_ROWS):
        acc = jnp.broadcast_to(b_dw[...], (CONV_ROWS, dc))
        for k in range(CONV_WIDTH):
            q, r = divmod(CONV_HIST - (CONV_WIDTH - 1) + k, SUBLANES)
            rows_k = slice(r0 + q * SUBLANES, r0 + q * SUBLANES + CONV_ROWS)
            tap = ubuf[rows_k, :] if r == 0 else ush[r - 1, rows_k, :]
            acc = acc + w_dw[k:k + 1, :] * tap
        vblocks.append(acc)
    ubuf[0:CONV_HIST, :] = ubuf[tl:tl + CONV_HIST, :]
    zg = _in_proj(hb, w_in, b_in, 2 * dc + dsm, 2 * dc + dsm + 2 * d)
    conv_out = _conv_tail(jnp.concatenate(vblocks, axis=0), cln_g, cln_b, w_pw, b_pw)

    zs = _in_proj(hb, w_in, b_in, 2 * dc, 2 * dc + dsm)
    zst = _bf(zs.T)
    tri = jnp.where(lax.broadcasted_iota(jnp.int32, (tl, tl), 0)
                    <= lax.broadcasted_iota(jnp.int32, (tl, tl), 1), 1.0, 0.0).astype(_BF16)
    def project(j):
        zj = zst[j * LANES:(j + 1) * LANES, :]
        return _dot(bre[j], zj), _dot(bim[j], zj)

    def scale_in(j, bu):
        r = slice(j * rows, (j + 1) * rows)
        eir, eii = einv_r[r, :], einv_i[r, :]
        return _bf(bu[0] * eir - bu[1] * eii), _bf(bu[0] * eii + bu[1] * eir)

    def prefix(v):
        return _dot(v[0], tri), _dot(v[1], tri)

    def scale_out(j, cum):
        r = slice(j * rows, (j + 1) * rows)
        cr = jnp.broadcast_to(car_r[r, LANES - 1:LANES], (rows, LANES))
        ci = jnp.broadcast_to(car_i[r, LANES - 1:LANES], (rows, LANES))
        lr, li = lamb_r[r, :], lamb_i[r, :]
        cum_r = cum[0] + jnp.concatenate([lr * cr - li * ci] * (tl // LANES), axis=1)
        cum_i = cum[1] + jnp.concatenate([lr * ci + li * cr] * (tl // LANES), axis=1)
        epr, epi = epow_r[r, :], epow_i[r, :]
        h_r = cum_r * epr - cum_i * epi
        h_i = cum_r * epi + cum_i * epr
        car_r[r, :] = h_r[:, tl - LANES:tl]
        car_i[r, :] = h_i[:, tl - LANES:tl]
        return jnp.concatenate([_bf(h_r), _bf(h_i)], axis=0)

    bu, v, cum, hcat, yts = {}, {}, {}, {}, {}
    for t in range(nj + 4):
        if t < nj:
            bu[t] = project(t)
        if 0 <= t - 2 < nj:
            cum[t - 2] = prefix(v.pop(t - 2))
        if 0 <= t - 4 < nj:
            yts[t - 4] = _dot(cc[t - 4], hcat.pop(t - 4))
        if 0 <= t - 1 < nj:
            v[t - 1] = scale_in(t - 1, bu.pop(t - 1))
        if 0 <= t - 3 < nj:
            hcat[t - 3] = scale_out(t - 3, cum.pop(t - 3))
    ssm_out = _ssm_tail(jnp.concatenate([yts[j] for j in range(nj)], axis=0).T, zs, dskip,
                        w_sv, b_sv, w_sg, b_sg)

    x1, h2, route, sched = _mixer_tail(x, zg, conv_out, ssm_out, g1, sh2, sc2, alpha,
                                w_out, b_out, ln1_g, ln1_b, wr_hi, wr_lo, b_r)
    x1_ref[0] = x1
    h2_ref[0] = _bf(h2)
    route_ref[0] = route
    sched_ref[0] = sched
    cnt_ref[0, 0] = _block_counts(route)

    @pl.when(c == last)
    def _():
        cstate_ref[0] = ubuf[CONV_HIST + tl - (CONV_WIDTH - 1):CONV_HIST + tl, :]
        sre_ref[0] = car_r[...].T[LANES - 1:LANES, :]
        sim_ref[0] = car_i[...].T[LANES - 1:LANES, :]


def _mixer_prompt(x, mod, wts, alpha):
    b, l, d = x.shape
    tl = TOKEN_BLOCK
    nc = l // tl
    dc = wts["w_pw"].shape[0]
    nstate = wts["lamb_r"].shape[0]
    names = ["w_in", "b_in", "w_dw", "b_dw", "cln_g", "cln_b", "w_pw", "b_pw", "bre", "bim", "cc",
             "einv_r", "einv_i", "epow_r", "epow_i", "lamb_r", "lamb_i", "dskip",
             "w_sv", "b_sv", "w_sg", "b_sg", "w_out", "b_out", "ln1_g", "ln1_b",
             "wr_hi", "wr_lo", "b_r"]
    consts = [wts[n] for n in names]
    tok = lambda bi, ci: (bi, ci, 0)
    seq = lambda bi, ci: (bi, 0, 0)
    return pl.pallas_call(
        functools.partial(_mixer_prompt_kernel, alpha=alpha),
        grid=(b, nc),
        in_specs=[pl.BlockSpec((1, tl, d), tok), pl.BlockSpec((1, 1, mod.shape[-1]), seq)]
                 + [_const_spec(a.shape) for a in consts],
        out_specs=[pl.BlockSpec((1, tl, d), tok), pl.BlockSpec((1, tl, d), tok),
                   pl.BlockSpec((1, tl, LANES), tok),
                   pl.BlockSpec((1, 2 * TOP_K, tl), lambda bi, ci: (bi, 0, ci)),
                   pl.BlockSpec((1, 1, 1, LANES), lambda bi, ci: (bi, ci, 0, 0)),
                   pl.BlockSpec((1, CONV_WIDTH - 1, dc), seq),
                   pl.BlockSpec((1, 1, nstate), seq), pl.BlockSpec((1, 1, nstate), seq)],
        out_shape=[jax.ShapeDtypeStruct((b, l, d), _F32), jax.ShapeDtypeStruct((b, l, d), _BF16),
                   jax.ShapeDtypeStruct((b, l, LANES), _F32),
                   jax.ShapeDtypeStruct((b, 2 * TOP_K, l), _F32),
                   jax.ShapeDtypeStruct((b, nc, 1, LANES), _F32),
                   jax.ShapeDtypeStruct((b, CONV_WIDTH - 1, dc), _F32),
                   jax.ShapeDtypeStruct((b, 1, nstate), _F32),
                   jax.ShapeDtypeStruct((b, 1, nstate), _F32)],
        scratch_shapes=[pltpu.VMEM((CONV_HIST + tl, dc), _F32),
                        pltpu.VMEM((SUBLANES - 1, tl + CONV_HIST - SUBLANES, dc), _F32),
                        pltpu.VMEM((nstate, LANES), _F32), pltpu.VMEM((nstate, LANES), _F32)],
        compiler_params=_params(("arbitrary", "arbitrary")),
        name="mixer_prompt",
    )(x, mod, *consts)


def _mixer_sample_kernel(x_ref, mod_ref, hist_ref, h0r_ref, h0i_ref,
                         w_in, b_in, w_dw, b_dw, cln_g, cln_b, w_pw, b_pw,
                         bre, bim, cc, lrow_r, lrow_i, dskip,
                         w_sv, b_sv, w_sg, b_sg, w_out, b_out, ln1_g, ln1_b, wr_hi, wr_lo, b_r,
                         x1_ref, h2_ref, route_ref, sched_ref, cnt_ref, cstate_ref, sre_ref, sim_ref, *, alpha):
    steps, nb, d = x_ref.shape
    dc = w_pw.shape[0]
    dsm = dskip.shape[1]
    nj = bre.shape[0]
    rows = bre.shape[1]
    nhist = CONV_WIDTH - 1
    x = x_ref[...].reshape(steps * nb, d)
    mod = jnp.concatenate([mod_ref[...]] * steps, axis=0)
    sh1, sc1, g1 = mod[:, 0:d], mod[:, d:2 * d], mod[:, 2 * d:3 * d]
    sh2, sc2 = mod[:, 3 * d:4 * d], mod[:, 4 * d:5 * d]
    hb = _bf(x * (1.0 + sc1) + sh1)

    u = _in_proj(hb, w_in, b_in, 0, dc) * _sigmoid(_in_proj(hb, w_in, b_in, dc, 2 * dc))
    us = [u[t * nb:(t + 1) * nb, :] for t in range(steps)]
    vs = []
    for t in range(steps):
        acc = jnp.broadcast_to(b_dw[...], (nb, dc))
        for i in range(t, nhist):
            acc = acc + w_dw[i - t:i - t + 1, :] * hist_ref[i]
        for s in range(t + 1):
            acc = acc + w_dw[nhist + s - t:nhist + s - t + 1, :] * us[s]
        vs.append(acc)
    for i in range(nhist):
        src = i + steps
        cstate_ref[i] = hist_ref[src] if src < nhist else us[src - nhist]
    conv_out = _conv_tail(jnp.concatenate(vs, axis=0), cln_g, cln_b, w_pw, b_pw)

    zs = _in_proj(hb, w_in, b_in, 2 * dc, 2 * dc + dsm)
    zsb = _bf(zs)
    h_r, h_i = h0r_ref[...], h0i_ref[...]
    lr, li = lrow_r[...], lrow_i[...]
    ys = []
    for t in range(steps):
        zt = zsb[t * nb:(t + 1) * nb, :]
        bu_r = jnp.concatenate([_dot_nt(zt[:, j * LANES:(j + 1) * LANES], bre[j]) for j in range(nj)], axis=1)
        bu_i = jnp.concatenate([_dot_nt(zt[:, j * LANES:(j + 1) * LANES], bim[j]) for j in range(nj)], axis=1)
        h_r, h_i = lr * h_r - li * h_i + bu_r, lr * h_i + li * h_r + bu_i
        hrb, hib = _bf(h_r), _bf(h_i)
        ys.append(jnp.concatenate(
            [_dot_nt(jnp.concatenate([hrb[:, j * rows:(j + 1) * rows], hib[:, j * rows:(j + 1) * rows]], axis=1),
                     cc[j]) for j in range(nj)], axis=1))
    sre_ref[...] = h_r
    sim_ref[...] = h_i
    ssm_out = _ssm_tail(jnp.concatenate(ys, axis=0), zs, dskip, w_sv, b_sv, w_sg, b_sg)

    zg = _in_proj(hb, w_in, b_in, 2 * dc + dsm, 2 * dc + dsm + 2 * d)
    x1, h2, route, sched = _mixer_tail(x, zg, conv_out, ssm_out, g1, sh2, sc2, alpha,
                                w_out, b_out, ln1_g, ln1_b, wr_hi, wr_lo, b_r)
    x1_ref[...] = x1
    h2_ref[...] = _bf(h2)
    route_ref[...] = route
    sched_ref[...] = sched
    for blk in range(cnt_ref.shape[0]):
        cnt_ref[blk] = _block_counts(route[blk * TOKEN_BLOCK:(blk + 1) * TOKEN_BLOCK, :])


def _mixer_sample(x_tm, mod, hist_tm, h0r, h0i, wts, alpha):
    steps, nb, d = x_tm.shape
    n = steps * nb
    dc = wts["w_pw"].shape[0]
    nstate = h0r.shape[1]
    names = ["w_in", "b_in", "w_dw", "b_dw", "cln_g", "cln_b", "w_pw", "b_pw", "bre", "bim", "cc",
             "lrow_r", "lrow_i", "dskip", "w_sv", "b_sv", "w_sg", "b_sg", "w_out", "b_out",
             "ln1_g", "ln1_b", "wr_hi", "wr_lo", "b_r"]
    args = [x_tm, mod, hist_tm, h0r, h0i] + [wts[k] for k in names]
    return pl.pallas_call(
        functools.partial(_mixer_sample_kernel, alpha=alpha),
        grid=(1,),
        in_specs=[_const_spec(a.shape) for a in args],
        out_specs=[_const_spec((n, d)), _const_spec((n, d)), _const_spec((n, LANES)),
                   _const_spec((2 * TOP_K, n)),
                   _const_spec((n // TOKEN_BLOCK, 1, LANES)),
                   _const_spec((CONV_WIDTH - 1, nb, dc)),
                   _const_spec((nb, nstate)), _const_spec((nb, nstate))],
        out_shape=[jax.ShapeDtypeStruct((n, d), _F32), jax.ShapeDtypeStruct((n, d), _BF16),
                   jax.ShapeDtypeStruct((n, LANES), _F32),
                   jax.ShapeDtypeStruct((2 * TOP_K, n), _F32),
                   jax.ShapeDtypeStruct((n // TOKEN_BLOCK, 1, LANES), _F32),
                   jax.ShapeDtypeStruct((CONV_WIDTH - 1, nb, dc), _F32),
                   jax.ShapeDtypeStruct((nb, nstate), _F32), jax.ShapeDtypeStruct((nb, nstate), _F32)],
        compiler_params=_params(("arbitrary",)),
        name="mixer_sample",
    )(*args)


def _round_up(x, m):
    return (x + m - 1) // m * m


def _plan_sizes(n_tokens):
    nb = n_tokens // TOKEN_BLOCK
    r_max = _round_up(TOP_K * TOKEN_BLOCK + 2 * N_EXPERTS * (SEG_PAD - 1), 2 * LANES)
    rows_max = n_tokens * TOP_K + N_EXPERTS * (EXPERT_TILE - 1)
    nt_max = -(-rows_max // EXPERT_TILE)
    return nb, r_max, nt_max


def _routing_plan(cnt, nt_max):
    nb, ne = cnt.shape
    before = jnp.cumsum(cnt, axis=0) - cnt
    tot = jnp.sum(cnt, axis=0)
    cin = before % SEG_PAD
    last = (jnp.arange(nb, dtype=jnp.int32) == nb - 1)[:, None]
    active = (cnt > 0) | (last & (cin > 0))
    seg = jnp.where(active, _round_up(cin + cnt, SEG_PAD), 0)
    loc = jnp.cumsum(seg, axis=1) - seg
    totpad = _round_up(tot, EXPERT_TILE)
    eend = jnp.cumsum(totpad)
    estart = eend - totpad
    goff = estart[None, :] + before - cin
    nfull = jnp.where(last, seg, jnp.where(active, (cin + cnt) // SEG_PAD * SEG_PAD, 0))
    cout = jnp.where(last | ~active, 0, (cin + cnt) % SEG_PAD)
    n_tiles = (eend[-1] // EXPERT_TILE).astype(jnp.int32)
    tiles = jnp.minimum(jnp.arange(nt_max, dtype=jnp.int32), n_tiles - 1)
    te = jnp.sum(((eend // EXPERT_TILE)[None, :] <= tiles[:, None]).astype(jnp.int32), axis=1)
    te = jnp.minimum(te, ne - 1)
    tot16 = _round_up(tot, SEG_PAD)
    return dict(seg=seg, loc=loc, goff=goff, nfull=nfull, cin=jnp.where(active, cin, 0), cout=cout,
                used=jnp.sum(seg, axis=1).astype(jnp.int32), sent=jnp.sum(nfull, axis=1).astype(jnp.int32),
                n_tiles=n_tiles.reshape(1), te=te,
                npad=(totpad - tot16).astype(jnp.int32), pad_base=(estart + tot16).astype(jnp.int32))


def _dest_rows(sched, first):
    row = lax.broadcasted_iota(jnp.int32, first.shape, 0).astype(_F32)
    return [sched[TOP_K + k:TOP_K + k + 1, :]
            + jnp.sum(jnp.where(row == sched[k:k + 1, :], first, 0.0), axis=0, keepdims=True)
            for k in range(TOP_K)]


def _typical_rows(r_max):
    return min(r_max, _round_up(TOP_K * TOKEN_BLOCK + N_EXPERTS * SEG_PAD, 2 * LANES))


def _dispatch_kernel(nfull, sloc, goff, cin, cout, used, sent, npad, pad_base, hp_ref, hs_ref, sp_ref, ss_ref,
                     first_ref, xs_hbm, d4_ref, buf, stage, zbuf, sem, zsem, *, nbp):
    i = pl.program_id(0)
    nblk = pl.num_programs(0)
    slot = lax.rem(i, 2)
    tb = hp_ref.shape[0]
    r_max = buf.shape[1]
    is_p = i < nbp
    h = jnp.where(is_p, hp_ref[...], hs_ref[...])
    sched = jnp.where(is_p, sp_ref[0], ss_ref[...])
    d4t = _dest_rows(sched, jnp.concatenate([first_ref[0]] * (tb // LANES), axis=1))
    row8 = lax.broadcasted_iota(jnp.int32, (SUBLANES, tb), 0)
    d4t8 = jnp.zeros((SUBLANES, tb), _F32)
    for k in range(TOP_K):
        d4t8 = jnp.where(row8 == k, d4t[k], d4t8)
    d4_ref[...] = jnp.concatenate([d4t8, jnp.zeros((LANES - SUBLANES, tb), _F32)], axis=0).T

    def compact(rows):
        row = lax.broadcasted_iota(jnp.int32, (rows, tb), 0).astype(_F32)
        p = jnp.zeros((rows, tb), _F32)
        for k in range(TOP_K):
            p = jnp.where(row == d4t[k], 1.0, p)
        buf[slot, 0:rows, :] = _bf(_dot(_bf(p), h))

    r_typ = _typical_rows(r_max)

    @pl.when(i == 0)
    def _():
        buf[...] = jnp.zeros(buf.shape, buf.dtype)
        stage[...] = jnp.zeros(stage.shape, stage.dtype)

    @pl.when(used[i] <= r_typ)
    def _():
        compact(r_typ)

    @pl.when(used[i] > r_typ)
    def _():
        compact(r_max)

    def seg_copy(blk, s, e):
        n = pl.multiple_of(nfull[blk * N_EXPERTS + e], SEG_PAD)
        src = pl.multiple_of(sloc[blk * N_EXPERTS + e], SEG_PAD)
        dst = pl.multiple_of(goff[blk * N_EXPERTS + e], SEG_PAD)
        return pltpu.make_async_copy(buf.at[s, pl.ds(src, n), :], xs_hbm.at[pl.ds(dst, n), :], sem.at[s])

    def pad_copy(e):
        n = pl.multiple_of(npad[e], SEG_PAD)
        dst = pl.multiple_of(pad_base[e], SEG_PAD)
        return pltpu.make_async_copy(zbuf.at[pl.ds(0, n), :], xs_hbm.at[pl.ds(dst, n), :], zsem.at[0])

    def for_segments(blk, fn):
        for e in range(N_EXPERTS):
            @pl.when(nfull[blk * N_EXPERTS + e] > 0)
            def _(e=e):
                fn(e)

    for e in range(N_EXPERTS):
        first = pl.ds(pl.multiple_of(sloc[i * N_EXPERTS + e], SEG_PAD), SEG_PAD)
        carried = jnp.where(cin[i * N_EXPERTS + e] > 0, stage[e], jnp.zeros_like(stage[e]))
        buf[slot, first, :] = buf[slot, first, :] + carried

    for_segments(i, lambda e: seg_copy(i, slot, e).start())

    for e in range(N_EXPERTS):
        rest = pl.multiple_of(sloc[i * N_EXPERTS + e] + nfull[i * N_EXPERTS + e], SEG_PAD)
        stage[e] = jnp.where(cout[i * N_EXPERTS + e] > 0, buf[slot, pl.ds(rest, SEG_PAD), :], stage[e])

    def wait_block(blk, s):
        @pl.when(sent[blk] > 0)
        def _():
            n = pl.multiple_of(sent[blk], SEG_PAD)
            pltpu.make_async_copy(buf.at[s, pl.ds(0, n), :], xs_hbm.at[pl.ds(0, n), :], sem.at[s]).wait()

    @pl.when(i > 0)
    def _():
        wait_block(i - 1, 1 - slot)

    @pl.when(i == nblk - 1)
    def _():
        zbuf[...] = jnp.zeros(zbuf.shape, zbuf.dtype)
        for e in range(N_EXPERTS):
            @pl.when(npad[e] > 0)
            def _(e=e):
                pad_copy(e).start()
        for e in range(N_EXPERTS):
            @pl.when(npad[e] > 0)
            def _(e=e):
                pad_copy(e).wait()
        wait_block(i, slot)


def _dispatch(plan, h2p, h2s, schedp, scheds, r_max, nt_max, blocks_per_seq):
    tb = TOKEN_BLOCK
    d = h2p.shape[1]
    nbp, nbs = h2p.shape[0] // tb, h2s.shape[0] // tb
    nb = nbp + nbs
    first_row = (plan["loc"] + plan["cin"]).astype(_F32)
    first = jnp.broadcast_to(first_row[:, :, None], (nb, N_EXPERTS, LANES))
    pidx = lambda i, *_: (jnp.minimum(i, nbp - 1), 0)
    sidx = lambda i, *_: (jnp.maximum(i - nbp, 0), 0)
    grid_spec = pltpu.PrefetchScalarGridSpec(
        num_scalar_prefetch=9,
        grid=(nb,),
        in_specs=[pl.BlockSpec((tb, d), pidx), pl.BlockSpec((tb, d), sidx),
                  pl.BlockSpec((1, 2 * TOP_K, tb), lambda i, *_: (jnp.minimum(i, nbp - 1) // blocks_per_seq, 0,
                                                                   jnp.minimum(i, nbp - 1) % blocks_per_seq)),
                  pl.BlockSpec((2 * TOP_K, tb), lambda i, *_: (0, jnp.maximum(i - nbp, 0))),
                  pl.BlockSpec((1, N_EXPERTS, LANES), lambda i, *_: (i, 0, 0))],
        out_specs=[pl.BlockSpec(memory_space=pl.ANY), pl.BlockSpec((tb, LANES), lambda i, *_: (i, 0))],
        scratch_shapes=[pltpu.VMEM((2, r_max, d), _BF16), pltpu.VMEM((N_EXPERTS, SEG_PAD, d), _BF16),
                        pltpu.VMEM((EXPERT_TILE, d), _BF16),
                        pltpu.SemaphoreType.DMA((2,)), pltpu.SemaphoreType.DMA((1,))])
    flat = lambda a: a.astype(jnp.int32).reshape(-1)
    return pl.pallas_call(
        functools.partial(_dispatch_kernel, nbp=nbp),
        grid_spec=grid_spec,
        out_shape=[jax.ShapeDtypeStruct((nt_max * EXPERT_TILE, d), _BF16),
                   jax.ShapeDtypeStruct((nb * tb, LANES), _F32)],
        compiler_params=_params(("arbitrary",)),
        name="dispatch",
    )(flat(plan["nfull"]), flat(plan["loc"]), flat(plan["goff"]), flat(plan["cin"]), flat(plan["cout"]),
      plan["used"], plan["sent"], plan["npad"], plan["pad_base"], h2p, h2s, schedp, scheds, first)


def _expert_kernel(te, n_tiles, x_ref, w1_ref, b1_ref, w2_ref, b2_ref, y_ref, w1b, w2b):
    i = pl.program_id(0)
    dff = w2_ref.shape[1]

    @pl.when(i < n_tiles[0])
    def _():
        prev = te[jnp.maximum(i - 1, 0)]

        @pl.when((i == 0) | (te[i] != prev))
        def _():
            w1b[...] = _bf(w1_ref[0])
            w2b[...] = _bf(w2_ref[0])

        gu = _dot(x_ref[...], w1b[...]) + b1_ref[0]
        g = jnp.minimum(gu[:, :dff], SWIGLU_LIMIT)
        up = jnp.clip(gu[:, dff:], -SWIGLU_LIMIT, SWIGLU_LIMIT)
        act = g * _sigmoid(SWIGLU_ALPHA * g) * (up + 1.0)
        y_ref[...] = _bf(_dot(_bf(act), w2b[...]) + b2_ref[0])


def _experts(plan, xs, w1, b1, w2, b2, nt_max):
    ne, d, dff2 = w1.shape
    dff = w2.shape[1]
    tm = EXPERT_TILE
    tile = lambda i, te, nt: (jnp.minimum(i, nt[0] - 1), 0)
    wsel = lambda i, te, nt: (te[i], 0, 0)
    grid_spec = pltpu.PrefetchScalarGridSpec(
        num_scalar_prefetch=2,
        grid=(nt_max,),
        in_specs=[pl.BlockSpec((tm, d), tile),
                  pl.BlockSpec((1, d, dff2), wsel), pl.BlockSpec((1, 1, dff2), wsel),
                  pl.BlockSpec((1, dff, d), wsel), pl.BlockSpec((1, 1, d), wsel)],
        out_specs=pl.BlockSpec((tm, d), tile),
        scratch_shapes=[pltpu.VMEM((d, dff2), _BF16), pltpu.VMEM((dff, d), _BF16)])
    return pl.pallas_call(
        _expert_kernel,
        grid_spec=grid_spec,
        out_shape=jax.ShapeDtypeStruct(xs.shape, _BF16),
        compiler_params=_params(("arbitrary",)),
        name="experts",
    )(plan["te"], plan["n_tiles"], xs, w1, b1.reshape(ne, 1, dff2), w2, b2.reshape(ne, 1, d))


def _combine_kernel(seg, sloc, goff, used, rp_ref, rs_ref, d4_ref, x1p_ref, x1s_ref, g2p_ref, g2s_ref,
                    ln2_g, ln2_b, ys_hbm, yp_ref, ysm_ref, buf, fbuf, sem, *, nbp, alpha):
    i = pl.program_id(0)
    nblk = pl.num_programs(0)
    slot = lax.rem(i, 2)
    tb = rp_ref.shape[0]
    r_max = buf.shape[1]

    def seg_copy(blk, s, e):
        n = pl.multiple_of(seg[blk * N_EXPERTS + e], SEG_PAD)
        src = pl.multiple_of(goff[blk * N_EXPERTS + e], SEG_PAD)
        dst = pl.multiple_of(sloc[blk * N_EXPERTS + e], SEG_PAD)
        return pltpu.make_async_copy(ys_hbm.at[pl.ds(src, n), :], buf.at[s, pl.ds(dst, n), :], sem.at[s])

    def for_segments(blk, fn):
        for e in range(N_EXPERTS):
            @pl.when(seg[blk * N_EXPERTS + e] > 0)
            def _(e=e):
                fn(e)

    @pl.when(i == 0)
    def _():
        buf[...] = jnp.zeros(buf.shape, buf.dtype)
        for_segments(0, lambda e: seg_copy(0, 0, e).start())

    @pl.when(i + 1 < nblk)
    def _():
        for_segments(i + 1, lambda e: seg_copy(i + 1, 1 - slot, e).start())

    @pl.when(used[i] > 0)
    def _():
        n = pl.multiple_of(used[i], SEG_PAD)
        pltpu.make_async_copy(ys_hbm.at[pl.ds(0, n), :], buf.at[slot, pl.ds(0, n), :], sem.at[slot]).wait()

    is_p = i < nbp
    route = jnp.where(is_p, rp_ref[...], rs_ref[...])
    d4 = d4_ref[...]

    def gather(rows):
        col = lax.broadcasted_iota(jnp.int32, (tb, rows), 1).astype(_F32)
        pg = jnp.zeros((tb, rows), _F32)
        for k in range(TOP_K):
            gate = route[:, ROUTE_GATE + k:ROUTE_GATE + k + 1]
            pg = jnp.where(col == d4[:, k:k + 1], gate, pg)
        fbuf[...] = _dot(_bf(pg), buf[slot, 0:rows, :])

    r_typ = _typical_rows(r_max)

    @pl.when(used[i] <= r_typ)
    def _():
        gather(r_typ)

    @pl.when(used[i] > r_typ)
    def _():
        gather(r_max)

    f = fbuf[...]
    x1 = jnp.where(is_p, x1p_ref[...], x1s_ref[...])
    g2s = jnp.concatenate([g2s_ref[...]] * (tb // g2s_ref.shape[0]), axis=0)
    g2 = jnp.where(is_p, jnp.broadcast_to(g2p_ref[0], g2s.shape), g2s)
    y = _layer_norm(alpha * x1 + g2 * f, ln2_g[...], ln2_b[...])

    @pl.when(is_p)
    def _():
        yp_ref[...] = y

    @pl.when(jnp.logical_not(is_p))
    def _():
        ysm_ref[...] = y


def _combine(plan, ys, rp, rs, d4, x1p, x1s, modp, mods, ln2_g, ln2_b, r_max, alpha, blocks_per_seq):
    tb = TOKEN_BLOCK
    d = x1p.shape[1]
    nbp, nbs = x1p.shape[0] // tb, x1s.shape[0] // tb
    nb = nbp + nbs
    pidx = lambda i, *_: (jnp.minimum(i, nbp - 1), 0)
    sidx = lambda i, *_: (jnp.maximum(i - nbp, 0), 0)
    g2_lane_block = 5
    flat = lambda a: a.astype(jnp.int32).reshape(-1)
    grid_spec = pltpu.PrefetchScalarGridSpec(
        num_scalar_prefetch=4,
        grid=(nb,),
        in_specs=[pl.BlockSpec((tb, LANES), pidx), pl.BlockSpec((tb, LANES), sidx),
                  pl.BlockSpec((tb, LANES), lambda i, *_: (i, 0)),
                  pl.BlockSpec((tb, d), pidx), pl.BlockSpec((tb, d), sidx),
                  pl.BlockSpec((1, 1, d), lambda i, *_: (jnp.minimum(i, nbp - 1) // blocks_per_seq, 0,
                                                         g2_lane_block)),
                  pl.BlockSpec((mods.shape[0], d), lambda i, *_: (0, g2_lane_block)),
                  pl.BlockSpec((1, d), lambda i, *_: (0, 0)), pl.BlockSpec((1, d), lambda i, *_: (0, 0)),
                  pl.BlockSpec(memory_space=pl.ANY)],
        out_specs=[pl.BlockSpec((tb, d), pidx), pl.BlockSpec((tb, d), sidx)],
        scratch_shapes=[pltpu.VMEM((2, r_max, d), _BF16), pltpu.VMEM((tb, d), _F32),
                        pltpu.SemaphoreType.DMA((2,))])
    return pl.pallas_call(
        functools.partial(_combine_kernel, nbp=nbp, alpha=alpha),
        grid_spec=grid_spec,
        out_shape=[jax.ShapeDtypeStruct(x1p.shape, _F32), jax.ShapeDtypeStruct(x1s.shape, _F32)],
        compiler_params=_params(("arbitrary",)),
        name="combine",
    )(flat(plan["seg"]), flat(plan["loc"]), flat(plan["goff"]), plan["used"],
      rp, rs, d4, x1p, x1s, modp, mods, ln2_g, ln2_b, ys)


def _complex_powers(zr, zi, n):
    low_bits = 4
    low = 1 << low_bits
    if n <= low or n % low:
        return _bit_powers(zr, zi, n)
    lr, li = _bit_powers(zr, zi, low)
    sr, si = zr, zi
    for _ in range(low_bits):
        sr, si = sr * sr - si * si, 2.0 * sr * si
    hr, hi = _bit_powers(sr, si, n // low)
    pr = hr[:, :, None] * lr[:, None, :] - hi[:, :, None] * li[:, None, :]
    pi = hr[:, :, None] * li[:, None, :] + hi[:, :, None] * lr[:, None, :]
    return pr.reshape(-1, n), pi.reshape(-1, n)


def _bit_powers(zr, zi, n):
    k = jnp.arange(n, dtype=jnp.int32)[None, :]
    pr = jnp.ones((zr.shape[0], n), _F32)
    pi = jnp.zeros((zr.shape[0], n), _F32)
    sr, si = zr[:, None], zi[:, None]
    bit = 1
    while bit < n:
        on = (k & bit) != 0
        mr, mi = jnp.where(on, sr, 1.0), jnp.where(on, si, 0.0)
        pr, pi = pr * mr - pi * mi, pr * mi + pi * mr
        sr, si = sr * sr - si * si, 2.0 * sr * si
        bit *= 2
    return pr, pi


def _block_diag(m):
    g, a, b = m.shape
    gb = GROUPS_PER_BLOCK
    m = m.reshape(g // gb, gb, a, 1, b)
    eye = jnp.eye(gb, dtype=m.dtype).reshape(1, gb, 1, gb, 1)
    return (m * eye).reshape(g // gb, gb * a, gb * b)


def _layer_weights(p, q):
    d = p["w_in"].shape[0]
    row = lambda v: v.reshape(1, -1).astype(_F32)
    lam_re, lam_im = p["lam_re"].astype(_F32), p["lam_im"].astype(_F32)
    dt = jnp.exp(p["log_dt"].astype(_F32))[:, None]
    mag = jnp.exp(lam_re * dt)
    lbr, lbi = mag * jnp.cos(lam_im * dt), mag * jnp.sin(lam_im * dt)
    den = lam_re * lam_re + lam_im * lam_im
    nr, ni = lbr - 1.0, lbi
    fr, fi = (nr * lam_re + ni * lam_im) / den, (ni * lam_re - nr * lam_im) / den
    b_re, b_im = p["b_re"].astype(_F32), p["b_im"].astype(_F32)
    bbr = fr[..., None] * b_re - fi[..., None] * b_im
    bbi = fr[..., None] * b_im + fi[..., None] * b_re
    nstate = lam_re.size
    mod2 = lbr * lbr + lbi * lbi
    flat = lambda v: v.reshape(nstate)
    epr, epi = _complex_powers(flat(lbr), flat(lbi), q)
    eir, eii = _complex_powers(flat(lbr / mod2), flat(-lbi / mod2), q)
    table = lambda t: t
    bcast = lambda v: jnp.broadcast_to(v.reshape(nstate, 1), (nstate, LANES))
    c_re, c_im = p["c_re"].astype(_F32), p["c_im"].astype(_F32)
    w_r = p["w_router"].astype(_F32)
    w_r = jnp.pad(w_r, ((0, 0), (0, LANES - w_r.shape[1])))
    wr_hi = _bf(w_r)
    return dict(
        w_in=_bf(p["w_in"]), b_in=row(p["b_in"]), w_dw=p["w_dw"].astype(_F32), b_dw=row(p["b_dw"]),
        cln_g=row(p["conv_ln_g"]), cln_b=row(p["conv_ln_b"]), w_pw=_bf(p["w_pw"]), b_pw=row(p["b_pw"]),
        bre=_bf(_block_diag(bbr)), bim=_bf(_block_diag(bbi)),
        cc=_bf(jnp.concatenate([_block_diag(c_re), _block_diag(-c_im)], axis=2)),
        einv_r=table(eir), einv_i=table(eii), epow_r=table(epr), epow_i=table(epi),
        lamb_r=bcast(lbr), lamb_i=bcast(lbi), lrow_r=lbr.reshape(1, nstate), lrow_i=lbi.reshape(1, nstate),
        dskip=row(p["d_skip"]),
        w_sv=_bf(p["w_sv"]), b_sv=row(p["b_sv"]), w_sg=_bf(p["w_sg"]), b_sg=row(p["b_sg"]),
        w_out=_bf(p["w_out"]), b_out=row(p["b_out"]), ln1_g=row(p["ln1_g"]), ln1_b=row(p["ln1_b"]),
        wr_hi=wr_hi, wr_lo=_bf(w_r - wr_hi.astype(_F32)),
        b_r=p["b_router"].astype(_F32).reshape(-1, 1),
        ln2_g=row(p["ln2_g"]), ln2_b=row(p["ln2_b"]))


def _layer(xp, xs_tm, c_all, hist_tm, h0r, h0i, p, alpha):
    b, l, d = xp.shape
    steps, nbs, _ = xs_tm.shape
    tb = TOKEN_BLOCK
    assert l % tb == 0 and l >= CONV_WIDTH - 1 and (steps * nbs) % tb == 0 and tb % nbs == 0
    wts = _layer_weights(p, tb)
    mod = _ada(c_all, p["w_ada"].astype(_F32), p["b_ada"].astype(_F32))
    modp, mods = mod[:b].reshape(b, 1, -1), mod[b:]

    x1p, h2p, rp, schedp, cntp, conv_p, sre_p, sim_p = _mixer_prompt(xp, modp, wts, alpha)
    x1s, h2s, rs, scheds, cnts, conv_s, sre_s, sim_s = _mixer_sample(xs_tm, mods, hist_tm, h0r, h0i, wts, alpha)

    n_tok = b * l + steps * nbs
    nb, r_max, nt_max = _plan_sizes(n_tok)
    cnt = jnp.concatenate([cntp.reshape(-1, LANES), cnts.reshape(-1, LANES)], axis=0)[:, :N_EXPERTS]
    plan = _routing_plan(cnt.astype(jnp.int32), nt_max)

    flat = lambda a: a.reshape(b * l, a.shape[-1])
    xs_sorted, d4 = _dispatch(plan, flat(h2p), h2s, schedp, scheds, r_max, nt_max, l // tb)
    ys_sorted = _experts(plan, xs_sorted, p["w1"], p["b1"], p["w2"], p["b2"], nt_max)
    yp, ysm = _combine(plan, ys_sorted, flat(rp), rs, d4, flat(x1p), x1s, modp, mods,
                       wts["ln2_g"], wts["ln2_b"], r_max, alpha, l // tb)
    return (yp.reshape(b, l, d), ysm.reshape(steps, nbs, d), conv_p, sre_p[:, 0], sim_p[:, 0],
            conv_s, sre_s, sim_s)


def kernel(x_prompt, x_sample, state_conv, state_ssm_re, state_ssm_im, c_prompt, c_sample, w_ada, b_ada, w_in, b_in, w_dw, b_dw, conv_ln_g, conv_ln_b, w_pw, b_pw, lam_re, lam_im, log_dt, b_re, b_im, c_re, c_im, d_skip, w_sv, b_sv, w_sg, b_sg, w_out, b_out, ln1_g, ln1_b, w_router, b_router, w1, b1, w2, b2, ln2_g, ln2_b):
    stacked = dict(w_ada=w_ada, b_ada=b_ada, w_in=w_in, b_in=b_in, w_dw=w_dw, b_dw=b_dw,
                   conv_ln_g=conv_ln_g, conv_ln_b=conv_ln_b, w_pw=w_pw, b_pw=b_pw, lam_re=lam_re,
                   lam_im=lam_im, log_dt=log_dt, b_re=b_re, b_im=b_im, c_re=c_re, c_im=c_im,
                   d_skip=d_skip, w_sv=w_sv, b_sv=b_sv, w_sg=w_sg, b_sg=b_sg, w_out=w_out, b_out=b_out,
                   ln1_g=ln1_g, ln1_b=ln1_b, w_router=w_router, b_router=b_router, w1=w1, b1=b1,
                   w2=w2, b2=b2, ln2_g=ln2_g, ln2_b=ln2_b)
    depth = w_ada.shape[0]
    alpha = (2 * depth) ** 0.25
    b = x_prompt.shape[0]
    nbs = x_sample.shape[0]
    g, s = state_ssm_re.shape[2], state_ssm_re.shape[3]
    xp = x_prompt
    xs_tm = jnp.transpose(x_sample, (1, 0, 2))
    c_all = jnp.concatenate([c_prompt, c_sample], axis=0)
    conv_ps, re_ps, im_ps, conv_ss, re_ss, im_ss = [], [], [], [], [], []
    for layer in range(depth):
        p = {k: v[layer] for k, v in stacked.items()}
        hist_tm = jnp.transpose(state_conv[layer], (1, 0, 2))
        h0r = state_ssm_re[layer].reshape(nbs, g * s)
        h0i = state_ssm_im[layer].reshape(nbs, g * s)
        xp, xs_tm, conv_p, sre_p, sim_p, conv_s, sre_s, sim_s = _layer(
            xp, xs_tm, c_all, hist_tm, h0r, h0i, p, alpha)
        conv_ps.append(conv_p.astype(state_conv.dtype))
        re_ps.append(sre_p.reshape(b, g, s))
        im_ps.append(sim_p.reshape(b, g, s))
        conv_ss.append(jnp.transpose(conv_s, (1, 0, 2)).astype(state_conv.dtype))
        re_ss.append(sre_s.reshape(nbs, g, s))
        im_ss.append(sim_s.reshape(nbs, g, s))
    return (xp, jnp.transpose(xs_tm, (1, 0, 2)), jnp.stack(conv_ps), jnp.stack(re_ps), jnp.stack(im_ps),
            jnp.stack(conv_ss), jnp.stack(re_ss), jnp.stack(im_ss))
```
